```python
import jax, jax.numpy as jnp
from jax import lax
import numpy as np

D_MODEL = 1024
BATCH = 2
SEQ = 8192
DEPTH = 2

GRID_W = 64
CTX_LEN = 256
NA_HEADS = 8
NA_HEAD_DIM = 64
NA_ROWS = 8
NA_COLS = 16
NA_QCB = 32
NA_KCB = NA_QCB + NA_COLS
HG_HEADS = 4
HG_DK = 128
HG_DV = 128
HG_CHUNK = 64
LOG_FLOOR = 1e-30
WA_HEADS = 8
WA_KV_HEADS = 2
WA_HEAD_DIM = 64
WA_WINDOW = 128
WA_BLOCK = 128
ROPE_THETA = 10000.0
N_EXPERTS = 16
N_GROUPS = 4
EXPERTS_PER_GROUP = N_EXPERTS // N_GROUPS
TOP_K = 2
EXPERT_FF = 512
MOE_BLOCK = 128
N_BRANCHES = 3
NA_W = NA_HEADS * NA_HEAD_DIM
HG_KW = HG_HEADS * HG_DK
HG_VW = HG_HEADS * HG_DV
WA_QW = WA_HEADS * WA_HEAD_DIM
WA_KVW = WA_KV_HEADS * WA_HEAD_DIM
IN_SPLITS = (NA_W, NA_W, NA_W, HG_KW, HG_KW, HG_KW, HG_VW, HG_VW, WA_QW, WA_KVW, WA_KVW, N_BRANCHES * D_MODEL)
IN_COLS = sum(IN_SPLITS)
NEG_INF = -1e30
RMS_EPS = 1e-6

kernel_name = 'hybrid_natten_hgrn2_swa_moe_dit'


def rms_norm(x, w):
    xf = x.astype(jnp.float32)
    y = xf * lax.rsqrt(jnp.mean(xf * xf, axis=-1, keepdims=True) + RMS_EPS)
    return (y * w.astype(jnp.float32)).astype(x.dtype)


def modulation(cond, w, b):
    return jnp.split(jax.nn.silu(cond) @ w + b, 6, axis=-1)


def modulate(h, shift, scale):
    return h * (1 + scale) + shift


def rope_2d(x, row_pos, col_pos):
    half = x.shape[-1] // 2
    quarter = half // 2
    inv_freq = ROPE_THETA ** (-jnp.arange(quarter, dtype=jnp.float32) / quarter)

    def rotate(xp, pos):
        ang = pos[:, None] * inv_freq[None, :]
        cos = jnp.cos(ang)[None, :, None, :]
        sin = jnp.sin(ang)[None, :, None, :]
        x1 = xp[..., :quarter].astype(jnp.float32)
        x2 = xp[..., quarter:].astype(jnp.float32)
        return jnp.concatenate([x1 * cos - x2 * sin, x2 * cos + x1 * sin], axis=-1)

    out = jnp.concatenate([rotate(x[..., :half], row_pos), rotate(x[..., half:], col_pos)], axis=-1)
    return out.astype(x.dtype)


def context_attention(q, k, v, sink):
    B, L, H, Dh = q.shape
    Hkv = k.shape[2]
    G = H // Hkv
    qg = q.reshape(B, L, Hkv, G, Dh)
    s = jnp.einsum('blkgd,bmkd->bkglm', qg, k).astype(jnp.float32) * (Dh ** -0.5)
    if sink is not None:
        s_sink = jnp.broadcast_to(sink.astype(jnp.float32).reshape(1, Hkv, G, 1, 1), s.shape[:-1] + (1,))
        s = jnp.concatenate([s, s_sink], axis=-1)
    p = jax.nn.softmax(s, axis=-1)[..., :L].astype(v.dtype)
    o = jnp.einsum('bkglm,bmkd->blkgd', p, v)
    return o.reshape(B, L, H * Dh)


def neighbourhood_attention(q, k, v, kc, vc, rpb):
    B, T, H, Dh = q.shape
    rows = T // GRID_W
    kr = min(NA_ROWS, rows)
    ncb = GRID_W // NA_QCB
    r = np.arange(rows)
    row_idx = np.clip(r - kr // 2, 0, rows - kr)[:, None] + np.arange(kr)[None, :]
    qcol = np.arange(GRID_W).reshape(ncb, NA_QCB)
    col_lo = np.clip(qcol - NA_COLS // 2, 0, GRID_W - NA_COLS)
    blk_lo = np.minimum(col_lo[:, 0], GRID_W - NA_KCB)
    col_idx = blk_lo[:, None] + np.arange(NA_KCB)[None, :]
    key_col = col_idx[:, None, :]
    col_ok = (key_col >= col_lo[..., None]) & (key_col < col_lo[..., None] + NA_COLS)
    d_row = row_idx - r[:, None] + NA_ROWS - 1
    d_col = np.clip(key_col - qcol[..., None] + NA_COLS - 1, 0, 2 * NA_COLS - 2)
    bias = rpb[:, d_row[:, None, None, :, None], d_col[None, :, :, None, :]]

    kg = k.reshape(B, rows, GRID_W, H, Dh)
    vg = v.reshape(B, rows, GRID_W, H, Dh)
    gi_r = row_idx[:, :, None, None]
    gi_c = col_idx[None, None, :, :]
    k_nb = kg[:, gi_r, gi_c]
    v_nb = vg[:, gi_r, gi_c]
    qb = q.reshape(B, rows, ncb, NA_QCB, H, Dh)
    scale = Dh ** -0.5
    s_nb = jnp.einsum('brjqhd,brajmhd->bhrjqam', qb, k_nb).astype(jnp.float32) * scale + bias[None].astype(jnp.float32)
    s_nb = jnp.where(col_ok[:, :, None, :], s_nb, NEG_INF)
    n_nb = kr * NA_KCB
    s_nb = s_nb.reshape(B, H, rows, ncb, NA_QCB, n_nb)
    s_ctx = jnp.einsum('brjqhd,blhd->bhrjql', qb, kc).astype(jnp.float32) * scale
    p = jax.nn.softmax(jnp.concatenate([s_nb, s_ctx], axis=-1), axis=-1).astype(v.dtype)
    p_nb = p[..., :n_nb].reshape(B, H, rows, ncb, NA_QCB, kr, NA_KCB)
    o = (jnp.einsum('bhrjqam,brajmhd->brjqhd', p_nb, v_nb)
         + jnp.einsum('bhrjql,blhd->brjqhd', p[..., n_nb:], vc))
    return o.reshape(B, T, H * Dh)


def window_attention(q, k, v, kc, vc, sink):
    B, T, H, Dh = q.shape
    Hkv = k.shape[2]
    G = H // Hkv
    L = kc.shape[1]
    nb = T // WA_BLOCK
    qb = q.reshape(B, nb, WA_BLOCK, Hkv, G, Dh)

    def band(a):
        ap = jnp.pad(a, ((0, 0), (WA_BLOCK, WA_BLOCK), (0, 0), (0, 0))).reshape(B, nb + 2, WA_BLOCK, Hkv, Dh)
        return jnp.concatenate([ap[:, :-2], ap[:, 1:-1], ap[:, 2:]], axis=2)

    kb = band(k)
    vb = band(v)
    blk = np.arange(nb)[:, None, None] * WA_BLOCK
    qpos = blk + np.arange(WA_BLOCK)[None, :, None]
    kpos = blk - WA_BLOCK + np.arange(3 * WA_BLOCK)[None, None, :]
    ok = (np.abs(kpos - qpos) <= WA_WINDOW) & (kpos >= 0) & (kpos < T)
    scale = Dh ** -0.5
    s_loc = jnp.einsum('bnqkgd,bnmkd->bkgnqm', qb, kb).astype(jnp.float32) * scale
    s_loc = jnp.where(ok, s_loc, NEG_INF)
    s_ctx = jnp.einsum('bnqkgd,blkd->bkgnql', qb, kc).astype(jnp.float32) * scale
    s_sink = jnp.broadcast_to(sink.astype(jnp.float32).reshape(1, Hkv, G, 1, 1, 1), s_ctx.shape[:-1] + (1,))
    p = jax.nn.softmax(jnp.concatenate([s_loc, s_ctx, s_sink], axis=-1), axis=-1).astype(v.dtype)
    nloc = 3 * WA_BLOCK
    o = (jnp.einsum('bkgnqm,bnmkd->bnqkgd', p[..., :nloc], vb)
         + jnp.einsum('bkgnql,blkd->bnqkgd', p[..., nloc:nloc + L], vc))
    return o.reshape(B, T, H * Dh)


def gla_chunk_scan(q, k, v, log_f, s0):
    B, T, H, dk = q.shape
    dv = v.shape[-1]
    C = HG_CHUNK
    n = T // C

    def chunks(a):
        return a.reshape(B, n, C, H, a.shape[-1]).transpose(1, 0, 3, 2, 4)

    lower_tri = np.tril(np.ones((C, C), dtype=bool))[:, :, None]

    def step(S, inp):
        qc, kc, vc, gc = inp
        b = jnp.cumsum(gc, axis=2)
        diff = b[:, :, :, None, :] - b[:, :, None, :, :]
        decay = jnp.exp(jnp.where(lower_tri, diff, NEG_INF))
        att = jnp.einsum('bhik,bhijk,bhjk->bhij', qc, decay, kc)
        o = jnp.einsum('bhij,bhjv->bhiv', att, vc) + jnp.einsum('bhik,bhkv->bhiv', qc * jnp.exp(b), S)
        b_last = b[:, :, -1:, :]
        S_new = (jnp.exp(b_last[:, :, 0, :])[..., None] * S
                 + jnp.einsum('bhjk,bhjv->bhkv', kc * jnp.exp(b_last - b), vc))
        return S_new, o

    s_final, o = lax.scan(step, s0, (chunks(q), chunks(k), chunks(v), chunks(log_f)))
    return o.transpose(1, 0, 3, 2, 4).reshape(B, T, H, dv), s_final


def hgrn_lower_bounds(hg_lower):
    p = jax.nn.softmax(hg_lower.astype(jnp.float32), axis=1)
    return jnp.cumsum(p, axis=1) - p[:, :1]


def hgrn2_mixer(q, f_fwd, f_bwd, i, g, lb_fwd, lb_bwd, norm_w, n_ctx):
    B, N = q.shape[:2]
    qf = jax.nn.silu(q.astype(jnp.float32))
    vf = i.astype(jnp.float32)
    s0 = jnp.zeros((B, HG_HEADS, HG_DK, HG_DV), jnp.float32)
    outs = []
    for f_pre, lb, reverse in ((f_fwd, lb_fwd, False), (f_bwd, lb_bwd, True)):
        lbh = lb.reshape(HG_HEADS, HG_DK)
        f = lbh + (1.0 - lbh) * jax.nn.sigmoid(f_pre.astype(jnp.float32))
        k = 1.0 - f
        log_f = jnp.log(jnp.maximum(f, LOG_FLOOR))
        orient = (lambda a: jnp.flip(a, axis=1)) if reverse else (lambda a: a)
        seqs = (qf, k, vf, log_f)
        o_c, s_c = gla_chunk_scan(*[orient(a[:, :n_ctx]) for a in seqs], s0)
        o_l, _ = gla_chunk_scan(*[orient(a[:, n_ctx:]) for a in seqs], s_c)
        outs.append(jnp.concatenate([orient(o_c), orient(o_l)], axis=1))
    o = rms_norm(outs[0] + outs[1], norm_w) * jax.nn.silu(g.astype(jnp.float32))
    return o.reshape(B, N, HG_VW).astype(q.dtype)


def token_mixers(h, n_ctx, need_ctx, w_in, na_qn, na_kn, na_rpb, lb_fwd, lb_bwd, hg_norm,
                 wa_qn, wa_kn, wa_sink, w_pa, w_pb, w_pc, w_out, row_pos, col_pos):
    B, N, D = h.shape
    L = n_ctx
    cuts = np.cumsum(IN_SPLITS)[:-1].tolist()
    qa, ka, va, qh, fh_f, fh_b, ih, gh, qw, kw, vw, gates = jnp.split(h @ w_in, cuts, axis=-1)

    def heads(a, n):
        return a.reshape(B, N, n, -1)

    qa = rms_norm(heads(qa, NA_HEADS), na_qn)
    ka = rms_norm(heads(ka, NA_HEADS), na_kn)
    va = heads(va, NA_HEADS)
    y_a = neighbourhood_attention(qa[:, L:], ka[:, L:], va[:, L:], ka[:, :L], va[:, :L], na_rpb)
    y_b = hgrn2_mixer(heads(qh, HG_HEADS), heads(fh_f, HG_HEADS), heads(fh_b, HG_HEADS),
                      heads(ih, HG_HEADS), heads(gh, HG_HEADS), lb_fwd, lb_bwd, hg_norm, L)
    qw = rms_norm(heads(qw, WA_HEADS), wa_qn)
    kw = rms_norm(heads(kw, WA_KV_HEADS), wa_kn)
    vw = heads(vw, WA_KV_HEADS)
    y_c = window_attention(rope_2d(qw[:, L:], row_pos, col_pos), rope_2d(kw[:, L:], row_pos, col_pos),
                           vw[:, L:], kw[:, :L], vw[:, :L], wa_sink)
    if need_ctx:
        y_a = jnp.concatenate([context_attention(qa[:, :L], ka[:, :L], va[:, :L], None), y_a], axis=1)
        y_c = jnp.concatenate([context_attention(qw[:, :L], kw[:, :L], vw[:, :L], wa_sink), y_c], axis=1)
    else:
        y_b = y_b[:, L:]
        gates = gates[:, L:]
    g_a, g_b, g_c = jnp.split(jax.nn.sigmoid(gates), N_BRANCHES, axis=-1)
    merged = g_a * (y_a @ w_pa) + g_b * (y_b @ w_pb) + g_c * (y_c @ w_pc)
    return merged @ w_out


def moe_ffn(h, w_router, b_router, w_gate, w_up, w_down):
    N, D = h.shape
    probs = jax.nn.softmax((h @ w_router).astype(jnp.float32), axis=-1)
    sel = (probs + b_router.astype(jnp.float32)).reshape(N, N_GROUPS, EXPERTS_PER_GROUP)
    grp_score = lax.top_k(sel, TOP_K)[0].sum(-1)
    g_idx = jnp.argmax(grp_score, axis=-1)
    in_grp = jnp.take_along_axis(sel, g_idx[:, None, None], axis=1)[:, 0]
    _, loc = lax.top_k(in_grp, TOP_K)
    e_idx = g_idx[:, None] * EXPERTS_PER_GROUP + loc
    wts = jnp.take_along_axis(probs, e_idx, axis=1)
    wts = wts / jnp.sum(wts, axis=-1, keepdims=True)
    A = N * TOP_K
    flat_e = e_idx.reshape(A)
    flat_tok = jnp.repeat(jnp.arange(N), TOP_K)
    flat_w = wts.reshape(A)
    order = jnp.argsort(flat_e)
    se, st, sw = flat_e[order], flat_tok[order], flat_w[order]
    counts = jnp.bincount(flat_e, length=N_EXPERTS)
    padded = ((counts + MOE_BLOCK - 1) // MOE_BLOCK) * MOE_BLOCK
    pend = jnp.cumsum(padded)
    pstart = pend - padded
    start = jnp.cumsum(counts) - counts
    dest = pstart[se] + jnp.arange(A) - start[se]
    nblk = -(-A // MOE_BLOCK) + N_EXPERTS
    buf_tok = jnp.zeros((nblk * MOE_BLOCK,), jnp.int32).at[dest].set(st)
    blk_e = jnp.minimum(jnp.searchsorted(pend, jnp.arange(nblk) * MOE_BLOCK, side='right'), N_EXPERTS - 1)
    xb = h[buf_tok].reshape(nblk, MOE_BLOCK, D)

    def expert_block(args):
        xblk, e = args
        return (jax.nn.silu(xblk @ w_gate[e]) * (xblk @ w_up[e])) @ w_down[e]

    yb = lax.map(expert_block, (xb, blk_e)).reshape(nblk * MOE_BLOCK, D)
    y = yb[dest] * sw[:, None].astype(h.dtype)
    return jax.ops.segment_sum(y, st, num_segments=N)


def hybrid_layer(x, ctx, c, c_ctx, need_ctx, w_ada, b_ada, norm1, norm2, w_in, na_qn, na_kn, na_rpb,
                 lb_fwd, lb_bwd, hg_norm, wa_qn, wa_kn, wa_sink, w_pa, w_pb, w_pc, w_out,
                 w_router, b_router, w_gate, w_up, w_down, row_pos, col_pos):
    B, T, D = x.shape
    L = ctx.shape[1]
    m = modulation(c, w_ada, b_ada)
    mc = modulation(c_ctx, w_ada, b_ada)
    h = jnp.concatenate([modulate(rms_norm(ctx, norm1), mc[0], mc[1]),
                         modulate(rms_norm(x, norm1), m[0][:, None], m[1][:, None])], axis=1)
    mix = token_mixers(h, L, need_ctx, w_in, na_qn, na_kn, na_rpb, lb_fwd, lb_bwd, hg_norm,
                       wa_qn, wa_kn, wa_sink, w_pa, w_pb, w_pc, w_out, row_pos, col_pos)
    if need_ctx:
        ctx = ctx + mc[2] * mix[:, :L]
        x = x + m[2][:, None] * mix[:, L:]
        hc = modulate(rms_norm(ctx, norm2), mc[3], mc[4])
        hx = modulate(rms_norm(x, norm2), m[3][:, None], m[4][:, None])
        y = moe_ffn(jnp.concatenate([hc, hx], axis=1).reshape(-1, D), w_router, b_router,
                    w_gate, w_up, w_down).reshape(B, L + T, D)
        ctx = ctx + mc[5] * y[:, :L]
        x = x + m[5][:, None] * y[:, L:]
    else:
        x = x + m[2][:, None] * mix
        hx = modulate(rms_norm(x, norm2), m[3][:, None], m[4][:, None])
        y = moe_ffn(hx.reshape(-1, D), w_router, b_router, w_gate, w_up, w_down).reshape(B, T, D)
        x = x + m[5][:, None] * y
    return x, ctx


def setup_inputs(seed: int = 0) -> dict:
    key = jax.random.key(seed)
    ks = jax.random.split(key, 26)
    f32 = jnp.float32
    D = D_MODEL

    def nrm(k, shape, scale):
        return jax.random.normal(k, shape, f32) * scale

    return {
        'x': nrm(ks[0], (BATCH, SEQ, D), 1.0),
        'c': nrm(ks[1], (BATCH, D), 1.0),
        'ctx': nrm(ks[2], (BATCH, CTX_LEN, D), 1.0),
        'c_ctx': nrm(ks[3], (D,), 1.0),
        'w_ada': nrm(ks[4], (DEPTH, D, 6 * D), 0.5 * D ** -0.5),
        'b_ada': nrm(ks[5], (DEPTH, 6 * D), 0.01),
        'norm1': 1.0 + nrm(ks[6], (DEPTH, D), 0.01),
        'norm2': 1.0 + nrm(ks[7], (DEPTH, D), 0.01),
        'w_in': nrm(ks[8], (DEPTH, D, IN_COLS), D ** -0.5),
        'na_q_norm': 1.0 + nrm(ks[9], (DEPTH, NA_HEAD_DIM), 0.01),
        'na_k_norm': 1.0 + nrm(ks[10], (DEPTH, NA_HEAD_DIM), 0.01),
        'na_rpb': nrm(ks[11], (DEPTH, NA_HEADS, 2 * NA_ROWS - 1, 2 * NA_COLS - 1), 0.2),
        'hg_lower': nrm(ks[12], (2, DEPTH, HG_KW), 1.0),
        'hg_norm': 1.0 + nrm(ks[13], (DEPTH, HG_DV), 0.01),
        'wa_q_norm': 1.0 + nrm(ks[14], (DEPTH, WA_HEAD_DIM), 0.01),
        'wa_k_norm': 1.0 + nrm(ks[15], (DEPTH, WA_HEAD_DIM), 0.01),
        'wa_sink': nrm(ks[16], (DEPTH, WA_HEADS), 1.0),
        'w_pa': nrm(ks[17], (DEPTH, NA_W, D), NA_W ** -0.5),
        'w_pb': nrm(ks[18], (DEPTH, HG_VW, D), HG_VW ** -0.5),
        'w_pc': nrm(ks[19], (DEPTH, WA_QW, D), WA_QW ** -0.5),
        'w_out': nrm(ks[20], (DEPTH, D, D), D ** -0.5),
        'w_router': nrm(ks[21], (D, N_EXPERTS), D ** -0.5),
        'b_router': nrm(ks[22], (N_EXPERTS,), 0.01),
        'w_gate': nrm(ks[23], (DEPTH, N_EXPERTS, D, EXPERT_FF), D ** -0.5),
        'w_up': nrm(ks[24], (DEPTH, N_EXPERTS, D, EXPERT_FF), D ** -0.5),
        'w_down': nrm(ks[25], (DEPTH, N_EXPERTS, EXPERT_FF, D), EXPERT_FF ** -0.5),
    }


def reference(x, c, ctx, c_ctx, w_ada, b_ada, norm1, norm2, w_in, na_q_norm, na_k_norm, na_rpb, hg_lower,
              hg_norm, wa_q_norm, wa_k_norm, wa_sink, w_pa, w_pb, w_pc, w_out, w_router, b_router,
              w_gate, w_up, w_down):
    T = x.shape[1]
    pos = jnp.arange(T)
    row_pos = (pos // GRID_W).astype(jnp.float32)
    col_pos = (pos % GRID_W).astype(jnp.float32)
    lb = hgrn_lower_bounds(hg_lower)
    for l in range(DEPTH):
        x, ctx = hybrid_layer(x, ctx, c, c_ctx, l < DEPTH - 1, w_ada[l], b_ada[l], norm1[l], norm2[l], w_in[l],
                              na_q_norm[l], na_k_norm[l], na_rpb[l], lb[0, l], lb[1, l], hg_norm[l],
                              wa_q_norm[l], wa_k_norm[l], wa_sink[l], w_pa[l], w_pb[l], w_pc[l], w_out[l],
                              w_router, b_router, w_gate[l], w_up[l], w_down[l], row_pos, col_pos)
    return x
```

```python
import functools

import numpy as np
import jax
import jax.numpy as jnp
from jax import lax
from jax.experimental import pallas as pl
from jax.experimental.pallas import tpu as pltpu

F32 = jnp.float32
BF16 = jnp.bfloat16
I32 = jnp.int32

GRID_W = 64
NA_HEADS = 8
NA_ROWS = 8
NA_COLS = 16
HEAD_DIM = 64
HG_HEADS = 4
HG_DK = 128
LOG_FLOOR = 1e-30
WA_HEADS = 8
WA_KV_HEADS = 2
WA_WINDOW = 128
ROPE_THETA = 10000.0
N_EXPERTS = 16
N_GROUPS = 4
EXPERTS_PER_GROUP = N_EXPERTS // N_GROUPS
N_BRANCHES = 3
NEG_INF = -1e30
RMS_EPS = 1e-6

LANES = 128
SUBLANES = 8
VMEM_LIMIT = 56 * 1024 * 1024

TOK_BLK = 256
MOE_BLK = 256


def _cparams(n_axes):
    return pltpu.CompilerParams(
        dimension_semantics=("arbitrary",) * n_axes, vmem_limit_bytes=VMEM_LIMIT)


def _sigmoid(x):
    return 1.0 / (1.0 + jnp.exp(-x))


def _silu(x):
    return x * _sigmoid(x)


def _dot(a, b):
    return jnp.dot(a.astype(BF16), b.astype(BF16), preferred_element_type=F32)


def _dot_nt(a, b):
    return lax.dot_general(a.astype(BF16), b.astype(BF16), (((1,), (1,)), ((), ())),
                           preferred_element_type=F32)


def _dot_tn(a, b):
    return lax.dot_general(a.astype(BF16), b.astype(BF16), (((0,), (0,)), ((), ())),
                           preferred_element_type=F32)


def _hi_lo(a):
    hi = a.astype(BF16)
    lo = (a - hi.astype(F32)).astype(BF16)
    return hi, lo


def _dot3(a, b):
    ah, al = _hi_lo(a)
    bh, bl = _hi_lo(b)
    d = functools.partial(jnp.dot, preferred_element_type=F32)
    return d(ah, bh) + (d(ah, bl) + d(al, bh))


def _dot3_nt(a, b):
    ah, al = _hi_lo(a)
    bh, bl = _hi_lo(b)
    d = functools.partial(lax.dot_general, dimension_numbers=(((1,), (1,)), ((), ())),
                          preferred_element_type=F32)
    return d(ah, bh) + (d(ah, bl) + d(al, bh))


def _rms(x, w):
    return x * lax.rsqrt(jnp.mean(x * x, axis=-1, keepdims=True) + RMS_EPS) * w


def _iota(shape, dim):
    return lax.broadcasted_iota(I32, shape, dim)


def _ada_kernel(cond_ref, w_ref, b_ref, o_ref):
    o_ref[0] = _dot3(_silu(cond_ref[...]), w_ref[0]) + b_ref[0]


def _ada(cond, w_ada, b_ada):
    depth, d, d6 = w_ada.shape
    rows = cond.shape[0]
    tn = d6 // 4
    return pl.pallas_call(
        _ada_kernel,
        grid=(depth, d6 // tn),
        in_specs=[pl.BlockSpec((rows, d), lambda l, j: (0, 0)),
                  pl.BlockSpec((1, d, tn), lambda l, j: (l, 0, j)),
                  pl.BlockSpec((1, 1, tn), lambda l, j: (l, 0, j))],
        out_specs=pl.BlockSpec((1, rows, tn), lambda l, j: (l, 0, j)),
        out_shape=jax.ShapeDtypeStruct((depth, rows, d6), F32),
        compiler_params=_cparams(2), name="ada",
    )(cond, w_ada, b_ada.reshape(depth, 1, d6))


def _mod_rows(mod_ref, which, b, row0, n_ctx, ctx_row, tm, d):
    lat = mod_ref[pl.ds(b, 1), which * d:(which + 1) * d]
    cx = mod_ref[ctx_row:ctx_row + 1, which * d:(which + 1) * d]
    row = row0 + _iota((tm, d), 0)
    return jnp.where(row < n_ctx, cx, lat)


def _inproj_kernel(x_ref, mod_ref, nw_ref, w_ref, o_ref, h_ref, *, n_ctx, ctx_row):
    b, i, j = pl.program_id(0), pl.program_id(1), pl.program_id(2)
    tm, d = h_ref.shape

    @pl.when(j == 0)
    def _():
        h = _rms(x_ref[0], nw_ref[...])
        shift = _mod_rows(mod_ref, 0, b, i * tm, n_ctx, ctx_row, tm, d)
        scale = _mod_rows(mod_ref, 1, b, i * tm, n_ctx, ctx_row, tm, d)
        h_ref[...] = (h * (1.0 + scale) + shift).astype(BF16)

    o_ref[0] = jnp.dot(h_ref[...], w_ref[...], preferred_element_type=F32)


def _inproj(xa, mod_l, norm_w, w_in, n_ctx, ctx_row):
    bsz, n, d = xa.shape
    cols = w_in.shape[1]
    tm = 768 if n % 768 == 0 else TOK_BLK
    tn = 1024
    return pl.pallas_call(
        functools.partial(_inproj_kernel, n_ctx=n_ctx, ctx_row=ctx_row),
        grid=(bsz, n // tm, cols // tn),
        in_specs=[pl.BlockSpec((1, tm, d), lambda b, i, j: (b, i, 0)),
                  pl.BlockSpec(mod_l.shape, lambda b, i, j: (0, 0)),
                  pl.BlockSpec((1, d), lambda b, i, j: (0, 0)),
                  pl.BlockSpec((d, tn), lambda b, i, j: (0, j))],
        out_specs=pl.BlockSpec((1, tm, tn), lambda b, i, j: (b, i, j)),
        out_shape=jax.ShapeDtypeStruct((bsz, n, cols), F32),
        scratch_shapes=[pltpu.VMEM((tm, d), BF16)],
        compiler_params=_cparams(3), name="inproj",
    )(xa, mod_l, norm_w.reshape(1, d), w_in)


def _pair_norm(x, w):
    lane = _iota(x.shape, 1)
    lo = lane < HEAD_DIM
    sq = x * x
    s_lo = jnp.sum(jnp.where(lo, sq, 0.0), axis=-1, keepdims=True)
    s_hi = jnp.sum(jnp.where(lo, 0.0, sq), axis=-1, keepdims=True)
    ms = jnp.where(lo, s_lo, s_hi) * (1.0 / HEAD_DIM)
    return x * lax.rsqrt(ms + RMS_EPS) * w


def _softmax_av(s_loc, s_ctx, v_loc, v_ctx, sink):
    m = jnp.maximum(jnp.max(s_loc, axis=-1, keepdims=True), jnp.max(s_ctx, axis=-1, keepdims=True))
    if sink is not None:
        m = jnp.maximum(m, sink)
    p_loc = jnp.exp(s_loc - m)
    p_ctx = jnp.exp(s_ctx - m)
    den = jnp.sum(p_loc, axis=-1, keepdims=True) + jnp.sum(p_ctx, axis=-1, keepdims=True)
    if sink is not None:
        den = den + jnp.exp(sink - m)
    return (_dot(p_loc, v_loc) + _dot(p_ctx, v_ctx)) / den


NA_QROWS = TOK_BLK // GRID_W
NA_KROWS = 3 * NA_QROWS
N_RPB_COLS = 2 * NA_COLS - 1
N_RPB_ROWS = 2 * NA_ROWS - 1
NA_VARIANTS = 4


def _na_row_valid(variant, a, k):
    if variant == 0:
        return False
    if variant == 1:
        return NA_QROWS <= k < NA_QROWS + NA_ROWS
    if variant == 3:
        return k < NA_ROWS
    return a <= k < a + NA_ROWS


def _na_bias_kernel(rpb_ref, o_ref):
    h = pl.program_id(0)
    qc = _iota((GRID_W, GRID_W), 0)
    kc = _iota((GRID_W, GRID_W), 1)
    col_lo = jnp.clip(qc - NA_COLS // 2, 0, GRID_W - NA_COLS)
    col_ok = (kc >= col_lo) & (kc < col_lo + NA_COLS)
    d_col = kc - qc + (NA_COLS - 1)
    masked = jnp.full((GRID_W, GRID_W), NEG_INF, F32)
    blocks = []
    for dr in range(N_RPB_ROWS):
        acc = jnp.zeros((GRID_W, GRID_W), F32)
        for dc in range(N_RPB_COLS):
            val = rpb_ref[(h * N_RPB_ROWS + dr) * N_RPB_COLS + dc]
            acc = jnp.where(d_col == dc, val, acc)
        blocks.append(jnp.where(col_ok, acc, NEG_INF))
    for variant in range(NA_VARIANTS):
        for a in range(NA_QROWS):
            for k in range(NA_KROWS):
                dr = k - NA_QROWS - a + NA_ROWS - 1
                ok = _na_row_valid(variant, a, k) and 0 <= dr < N_RPB_ROWS
                o_ref[variant, 0, a * GRID_W:(a + 1) * GRID_W, k * GRID_W:(k + 1) * GRID_W] = (
                    blocks[dr] if ok else masked)


def _na_bias(rpb):
    heads = rpb.shape[0]
    return pl.pallas_call(
        _na_bias_kernel,
        grid=(heads,),
        in_specs=[pl.BlockSpec(memory_space=pltpu.SMEM)],
        out_specs=pl.BlockSpec((NA_VARIANTS, 1, TOK_BLK, 3 * TOK_BLK), lambda h: (0, h, 0, 0)),
        out_shape=jax.ShapeDtypeStruct((NA_VARIANTS, heads, TOK_BLK, 3 * TOK_BLK), F32),
        compiler_params=_cparams(1), name="na_bias",
    )(rpb.reshape(-1))


def _na_kernel(q_ref, kp_ref, kc_ref, kn_ref, vp_ref, vc_ref, vn_ref, kx_ref, vx_ref,
               bias_ref, qn_ref, kn_w_ref, o_ref):
    scale = HEAD_DIM ** -0.5
    q = _pair_norm(q_ref[0], qn_ref[...])
    k_loc = _pair_norm(jnp.concatenate([kp_ref[0], kc_ref[0], kn_ref[0]], axis=0), kn_w_ref[...])
    k_ctx = _pair_norm(kx_ref[0], kn_w_ref[...])
    v_loc = jnp.concatenate([vp_ref[0], vc_ref[0], vn_ref[0]], axis=0)
    v_ctx = vx_ref[0]
    lo = _iota(q.shape, 1) < HEAD_DIM
    outs = []
    for h in range(2):
        qh = jnp.where(lo if h == 0 else ~lo, q, 0.0)
        s_loc = _dot_nt(qh, k_loc) * scale + bias_ref[0, h]
        s_ctx = _dot_nt(qh, k_ctx) * scale
        outs.append(_softmax_av(s_loc, s_ctx, v_loc, v_ctx, None))
    o_ref[0] = jnp.where(lo, outs[0], outs[1])


def _na_attention(proj, bias, qn_w, kn_w, col_q, col_k, col_v):
    bsz, n, _ = proj.shape
    nblk = n // TOK_BLK
    last = nblk - 1
    cq, ck, cv = col_q // LANES, col_k // LANES, col_v // LANES

    def blk(col, shift):
        def index(hp, b, i):
            return (b, jnp.clip(i + shift, 0, last), col + hp)
        return pl.BlockSpec((1, TOK_BLK, LANES), index)

    def ctx_blk(col):
        return pl.BlockSpec((1, TOK_BLK, LANES), lambda hp, b, i: (b, 0, col + hp))

    def variant(hp, b, i):
        v = jnp.where(i == 0, 0, jnp.where(i == 1, 1, jnp.where(i == last, 3, 2)))
        return (v, hp, 0, 0)

    vec = pl.BlockSpec((1, LANES), lambda hp, b, i: (0, 0))
    return pl.pallas_call(
        _na_kernel,
        grid=(NA_HEADS // 2, bsz, nblk),
        in_specs=[blk(cq, 0), blk(ck, -1), blk(ck, 0), blk(ck, 1), blk(cv, -1), blk(cv, 0), blk(cv, 1),
                  ctx_blk(ck), ctx_blk(cv),
                  pl.BlockSpec((1, 2, TOK_BLK, 3 * TOK_BLK), variant), vec, vec],
        out_specs=pl.BlockSpec((1, TOK_BLK, LANES), lambda hp, b, i: (b, i, hp)),
        out_shape=jax.ShapeDtypeStruct((bsz, n, NA_HEADS * HEAD_DIM), F32),
        compiler_params=_cparams(3), name="na_attn",
    )(proj, proj, proj, proj, proj, proj, proj, proj, proj, bias,
      jnp.tile(qn_w, 2).reshape(1, LANES), jnp.tile(kn_w, 2).reshape(1, LANES))


def _rope(x, cos, sin_signed):
    lane = _iota(x.shape, 1)
    first = (lane & (HEAD_DIM // 2 - 1)) < HEAD_DIM // 4
    quarter = HEAD_DIM // 4
    partner = jnp.where(first, pltpu.roll(x, LANES - quarter, 1), pltpu.roll(x, quarter, 1))
    return x * cos + partner * sin_signed


def _wa_kernel(sink_ref, q_ref, kp_ref, kc_ref, kn_ref, vp_ref, vc_ref, vn_ref, kx_ref, vx_ref,
               cq_ref, sq_ref, cp_ref, sp_ref, cn_ref, sn_ref, qn_ref, kn_w_ref, o_ref, *, last):
    i = pl.program_id(1)
    scale = HEAD_DIM ** -0.5
    tq = q_ref.shape[1]
    cos_loc = jnp.concatenate([cp_ref[...], cq_ref[...], cn_ref[...]], axis=0)
    sin_loc = jnp.concatenate([sp_ref[...], sq_ref[...], sn_ref[...]], axis=0)
    k_loc = _pair_norm(jnp.concatenate([kp_ref[0], kc_ref[0], kn_ref[0]], axis=0), kn_w_ref[...])
    k_loc = _rope(k_loc, cos_loc, sin_loc)
    k_ctx = _pair_norm(kx_ref[0], kn_w_ref[...])
    v_loc = jnp.concatenate([vp_ref[0], vc_ref[0], vn_ref[0]], axis=0)
    v_ctx = vx_ref[0]
    half = LANES // 2
    k_sw = (pltpu.roll(k_loc, half, 1), pltpu.roll(k_ctx, half, 1),
            pltpu.roll(v_loc, half, 1), pltpu.roll(v_ctx, half, 1))
    k_id = (k_loc, k_ctx, v_loc, v_ctx)

    qi = _iota((tq, 3 * tq), 0)
    kj = _iota((tq, 3 * tq), 1)
    rel = kj - tq - qi
    lo_col = jnp.where(i >= 2, 0, tq)
    hi_col = jnp.where(i >= 1, jnp.where(i < last, 3 * tq, 2 * tq), 0)
    ok = (rel >= -WA_WINDOW) & (rel <= WA_WINDOW) & (kj >= lo_col) & (kj < hi_col)
    mask = jnp.where(ok, 0.0, NEG_INF)

    lo = _iota((tq, LANES), 1) < HEAD_DIM
    group = WA_HEADS // WA_KV_HEADS
    for p in range(WA_HEADS // 2):
        q = _pair_norm(q_ref[0, :, p * LANES:(p + 1) * LANES], qn_ref[...])
        q = _rope(q, cq_ref[...], sq_ref[...])
        outs = []
        for hh in range(2):
            h = 2 * p + hh
            kv = h // group
            kl, kx, vl, vx = k_id if kv == hh else k_sw
            qh = jnp.where(lo if hh == 0 else ~lo, q, 0.0)
            s_loc = _dot_nt(qh, kl) * scale + mask
            s_ctx = _dot_nt(qh, kx) * scale
            outs.append(_softmax_av(s_loc, s_ctx, vl, vx, sink_ref[h]))
        o_ref[0, :, p * LANES:(p + 1) * LANES] = jnp.where(lo, outs[0], outs[1])


def _wa_attention(proj, sink, qn_w, kn_w, cos_t, sin_t, col_q, col_k, col_v):
    bsz, n, _ = proj.shape
    nblk = n // TOK_BLK
    last = nblk - 1
    qw = WA_HEADS * HEAD_DIM
    cq, ck, cv = col_q // qw, col_k // LANES, col_v // LANES

    def blk(col, shift):
        return pl.BlockSpec((1, TOK_BLK, LANES),
                            lambda b, i: (b, jnp.clip(i + shift, 0, last), col))

    def tab(shift):
        return pl.BlockSpec((TOK_BLK, LANES), lambda b, i: (jnp.clip(i + shift, 0, last), 0))

    def ctx_blk(col):
        return pl.BlockSpec((1, TOK_BLK, LANES), lambda b, i: (b, 0, col))

    vec = pl.BlockSpec((1, LANES), lambda b, i: (0, 0))
    return pl.pallas_call(
        functools.partial(_wa_kernel, last=last),
        grid=(bsz, nblk),
        in_specs=[pl.BlockSpec(memory_space=pltpu.SMEM),
                  pl.BlockSpec((1, TOK_BLK, qw), lambda b, i: (b, i, cq)),
                  blk(ck, -1), blk(ck, 0), blk(ck, 1), blk(cv, -1), blk(cv, 0), blk(cv, 1),
                  ctx_blk(ck), ctx_blk(cv),
                  tab(0), tab(0), tab(-1), tab(-1), tab(1), tab(1), vec, vec],
        out_specs=pl.BlockSpec((1, TOK_BLK, qw), lambda b, i: (b, i, 0)),
        out_shape=jax.ShapeDtypeStruct((bsz, n, qw), F32),
        compiler_params=_cparams(2), name="wa_attn",
    )(sink, proj, proj, proj, proj, proj, proj, proj, proj, proj,
      cos_t, sin_t, cos_t, sin_t, cos_t, sin_t,
      jnp.tile(qn_w, 2).reshape(1, LANES), jnp.tile(kn_w, 2).reshape(1, LANES))


def _rope_tables(n_ctx, t):
    quarter = HEAD_DIM // 4
    inv_freq = ROPE_THETA ** (-jnp.arange(quarter, dtype=F32) / quarter)
    pos = jnp.arange(t)
    lane = np.arange(LANES)
    in_head = lane % HEAD_DIM
    use_col = in_head >= HEAD_DIM // 2
    second = (in_head % (HEAD_DIM // 2)) >= quarter
    freq = inv_freq[in_head % quarter]
    p = jnp.where(use_col[None, :], (pos % GRID_W)[:, None], (pos // GRID_W)[:, None]).astype(F32)
    ang = p * freq[None, :]
    cos = jnp.cos(ang)
    sin = jnp.where(second[None, :], jnp.sin(ang), -jnp.sin(ang))
    cos = jnp.concatenate([jnp.ones((n_ctx, LANES), F32), cos], axis=0)
    sin = jnp.concatenate([jnp.zeros((n_ctx, LANES), F32), sin], axis=0)
    return cos, sin


HG_SUB = SUBLANES


def _hgrn_kernel(q_ref, f_ref, v_ref, hl_ref, *rest, layer, reverse, final):
    if final:
        g_ref, prev_ref, nw_ref, o_ref, st_ref = rest
    else:
        o_ref, st_ref = rest
    c = q_ref.shape[1]

    @pl.when(pl.program_id(2) == 0)
    def _():
        st_ref[...] = jnp.zeros_like(st_ref)

    a = hl_ref[0]
    e = jnp.exp(a - jnp.max(a, axis=0, keepdims=True))
    pr = e / jnp.sum(e, axis=0, keepdims=True)
    lb = jnp.zeros((1, HG_DK), F32)
    for j in range(1, layer + 1):
        lb = lb + pr[j:j + 1]

    qs = _silu(q_ref[0])
    f = lb + (1.0 - lb) * _sigmoid(f_ref[0])
    kk = 1.0 - f
    g = jnp.log(jnp.maximum(f, LOG_FLOOR))
    v = v_ref[0]
    row = _iota((c, HG_DK), 0)

    def shifted(x, d):
        return pltpu.roll(x, (c - d) if reverse else d, 0)

    def has_earlier(d, width):
        r = row & (width - 1)
        return (r <= width - 1 - d) if reverse else (r >= d)

    b = g
    d = 1
    while d < c:
        b = b + jnp.where(has_earlier(d, c), shifted(b, d), 0.0)
        d *= 2

    ri = _iota((c, c), 0)
    ci = _iota((c, c), 1)
    att = jnp.zeros((c, c), F32)
    hs = c // 2
    while hs >= HG_SUB:
        blk = 2 * hs
        pieces = []
        for p in range(c // blk):
            mid = p * blk + hs
            r = mid if reverse else mid - 1
            pieces.append(jnp.broadcast_to(b[r:r + 1, :], (blk, HG_DK)))
        ref = pieces[0] if len(pieces) == 1 else jnp.concatenate(pieces, axis=0)
        decay = jnp.exp(-jnp.abs(b - ref))
        later_half = has_earlier(hs, blk)
        q_l = jnp.where(later_half, qs * decay, 0.0)
        k_l = jnp.where(later_half, 0.0, kk * decay)
        a_l = _dot_nt(q_l, k_l)
        if blk < c:
            sh = blk.bit_length() - 1
            a_l = jnp.where((ri >> sh) == (ci >> sh), a_l, 0.0)
        att = att + a_l
        hs //= 2

    o = _dot(att, v)
    o = o + jnp.sum(qs * kk, axis=-1, keepdims=True) * v
    for d in range(1, HG_SUB):
        dl = jnp.minimum(b - shifted(b, d), 0.0)
        term = jnp.where(has_earlier(d, HG_SUB), qs * shifted(kk, d) * jnp.exp(dl), 0.0)
        o = o + jnp.sum(term, axis=-1, keepdims=True) * shifted(v, d)

    st = st_ref[...]
    o = o + _dot_nt(qs * jnp.exp(b), st)
    b_end = b[0:1, :] if reverse else b[c - 1:c, :]
    st_ref[...] = st * jnp.exp(b_end) + _dot_tn(v, kk * jnp.exp(b_end - b))

    if final:
        tot = prev_ref[0] + o
        gate = g_ref[0]
        o_ref[0] = _rms(tot, nw_ref[...]) * _silu(gate)
    else:
        o_ref[0] = o


def _hgrn_pass(proj, hl, layer, reverse, col_q, col_f, col_v, final_args=None):
    bsz, n, _ = proj.shape
    nchunk = n // TOK_BLK
    cq, cf, cv = col_q // LANES, col_f // LANES, col_v // LANES

    def chunk(t):
        return jnp.where(t == 0, 0, nchunk - t) if reverse else t

    def blk(col):
        return pl.BlockSpec((1, TOK_BLK, LANES), lambda b, h, t: (b, chunk(t), col + h))

    yblk = pl.BlockSpec((1, TOK_BLK, LANES), lambda b, h, t: (b, chunk(t), h))
    in_specs = [blk(cq), blk(cf), blk(cv),
                pl.BlockSpec((1,) + hl.shape[1:2] + (LANES,), lambda b, h, t: (1 if reverse else 0, 0, h))]
    args = [proj, proj, proj, hl]
    final = final_args is not None
    if final:
        col_g, prev, norm_w = final_args
        in_specs += [blk(col_g // LANES), yblk, pl.BlockSpec((1, LANES), lambda b, h, t: (0, 0))]
        args += [proj, prev, norm_w.reshape(1, LANES)]
    return pl.pallas_call(
        functools.partial(_hgrn_kernel, layer=layer, reverse=reverse, final=final),
        grid=(bsz, HG_HEADS, nchunk),
        in_specs=in_specs,
        out_specs=yblk,
        out_shape=jax.ShapeDtypeStruct((bsz, n, HG_HEADS * HG_DK), F32),
        scratch_shapes=[pltpu.VMEM((HG_DK, HG_DK), F32)],
        compiler_params=_cparams(3), name="hgrn_bwd" if reverse else "hgrn_fwd",
    )(*args)


def _merge_kernel(x_ref, ya_ref, yb_ref, yc_ref, ga_ref, gb_ref, gc_ref, mod_ref,
                  wa_ref, wb_ref, wc_ref, wo_ref, o_ref, *, n_ctx, ctx_row):
    b, i = pl.program_id(0), pl.program_id(1)
    tm, d = x_ref.shape[1:]
    merged = (_sigmoid(ga_ref[0]) * _dot(ya_ref[0], wa_ref[...])
              + _sigmoid(gb_ref[0]) * _dot(yb_ref[0], wb_ref[...])
              + _sigmoid(gc_ref[0]) * _dot(yc_ref[0], wc_ref[...]))
    mix = _dot(merged, wo_ref[...])
    gate = _mod_rows(mod_ref, 2, b, i * tm, n_ctx, ctx_row, tm, d)
    o_ref[0] = x_ref[0] + gate * mix


def _merge(xa, ya, yb, yc, proj, mod_l, w_pa, w_pb, w_pc, w_out, col_gate, n_ctx, ctx_row):
    bsz, n, d = xa.shape
    tm = 384 if n % 384 == 0 else TOK_BLK
    g0 = col_gate // d
    tile = lambda w: pl.BlockSpec((1, tm, w), lambda b, i: (b, i, 0))
    gate = lambda k: pl.BlockSpec((1, tm, d), lambda b, i: (b, i, g0 + k))
    full = lambda w: pl.BlockSpec(w.shape, lambda b, i: (0, 0))
    return pl.pallas_call(
        functools.partial(_merge_kernel, n_ctx=n_ctx, ctx_row=ctx_row),
        grid=(bsz, n // tm),
        in_specs=[tile(d), tile(ya.shape[2]), tile(yb.shape[2]), tile(yc.shape[2]),
                  gate(0), gate(1), gate(2), full(mod_l),
                  full(w_pa), full(w_pb), full(w_pc), full(w_out)],
        out_specs=tile(d),
        out_shape=jax.ShapeDtypeStruct((bsz, n, d), F32),
        compiler_params=_cparams(2), name="merge",
    )(xa, ya, yb, yc, proj, proj, proj, mod_l, w_pa, w_pb, w_pc, w_out)


def _route_kernel(x_ref, mod_ref, nw_ref, wr_ref, br_ref, h_ref, idx_ref, wt_ref, cnt_ref,
                  carry_ref, *, n_ctx, ctx_row):
    b, i = pl.program_id(0), pl.program_id(1)
    tm, d = x_ref.shape[1:]

    @pl.when((b == 0) & (i == 0))
    def _():
        carry_ref[...] = jnp.zeros_like(carry_ref)

    h = _rms(x_ref[0], nw_ref[...])
    shift = _mod_rows(mod_ref, 3, b, i * tm, n_ctx, ctx_row, tm, d)
    scale = _mod_rows(mod_ref, 4, b, i * tm, n_ctx, ctx_row, tm, d)
    h = h * (1.0 + scale) + shift
    h_ref[0] = h

    logits = _dot3_nt(wr_ref[...], h)
    ex = jnp.exp(logits - jnp.max(logits, axis=0, keepdims=True))
    probs = ex / jnp.sum(ex, axis=0, keepdims=True)
    sel = probs + br_ref[...]

    def row(x, r):
        return x[r:r + 1, :]

    best = None
    g_idx = None
    for g in range(N_GROUPS):
        r0 = g * EXPERTS_PER_GROUP
        a0, a1, a2, a3 = (row(sel, r0 + j) for j in range(EXPERTS_PER_GROUP))
        hi1, lo1 = jnp.maximum(a0, a1), jnp.minimum(a0, a1)
        hi2, lo2 = jnp.maximum(a2, a3), jnp.minimum(a2, a3)
        score = jnp.maximum(hi1, hi2) + jnp.maximum(jnp.minimum(hi1, hi2), jnp.maximum(lo1, lo2))
        if g == 0:
            best, g_idx = score, jnp.zeros_like(score, dtype=I32)
        else:
            better = score > best
            best = jnp.where(better, score, best)
            g_idx = jnp.where(better, g, g_idx)

    def pick(x, j):
        out = row(x, j)
        for g in range(1, N_GROUPS):
            out = jnp.where(g_idx == g, row(x, g * EXPERTS_PER_GROUP + j), out)
        return out

    in_grp = [pick(sel, j) for j in range(EXPERTS_PER_GROUP)]
    in_prob = [pick(probs, j) for j in range(EXPERTS_PER_GROUP)]

    def first_argmax(vals, skip):
        bv, bi, bp = None, None, None
        for j in range(EXPERTS_PER_GROUP):
            v = vals[j] if skip is None else jnp.where(skip == j, -jnp.inf, vals[j])
            if bv is None:
                bv, bi, bp = v, jnp.zeros_like(g_idx), in_prob[0]
            else:
                better = v > bv
                bv = jnp.where(better, v, bv)
                bi = jnp.where(better, j, bi)
                bp = jnp.where(better, in_prob[j], bp)
        return bi, bp

    loc0, p0 = first_argmax(in_grp, None)
    loc1, p1 = first_argmax(in_grp, loc0)
    e0 = g_idx * EXPERTS_PER_GROUP + loc0
    e1 = g_idx * EXPERTS_PER_GROUP + loc1
    wsum = p0 + p1
    w0, w1 = p0 / wsum, p1 / wsum

    er = _iota((N_EXPERTS, tm), 0)
    hit0 = er == e0
    hit1 = er == e1
    hot = jnp.where(hit0 | hit1, 1.0, 0.0)
    upper = jnp.where(_iota((tm, tm), 0) < _iota((tm, tm), 1), 1.0, 0.0)
    before = _dot(hot, upper) + carry_ref[:, 0:1]
    r0 = jnp.sum(jnp.where(hit0, before, 0.0), axis=0, keepdims=True)
    r1 = jnp.sum(jnp.where(hit1, before, 0.0), axis=0, keepdims=True)
    carry_ref[...] = carry_ref[...] + jnp.sum(hot, axis=1, keepdims=True)
    cnt_ref[...] = carry_ref[...]

    idx_ref[...] = jnp.zeros_like(idx_ref)
    wt_ref[...] = jnp.zeros_like(wt_ref)
    for r, val in enumerate((e0, e1, r0.astype(I32), r1.astype(I32))):
        idx_ref[r:r + 1, :] = val
    for r, val in enumerate((w0, w1)):
        wt_ref[r:r + 1, :] = val


def _route(x1, mod_l, norm_w, w_router_t, b_router, n_ctx, ctx_row):
    bsz, n, d = x1.shape
    tm = TOK_BLK
    nt = n // tm
    tile = pl.BlockSpec((1, tm, d), lambda b, i: (b, i, 0))
    lane_tile = pl.BlockSpec((SUBLANES, tm), lambda b, i: (0, b * nt + i))
    full = lambda a: pl.BlockSpec(a.shape, lambda b, i: (0,) * a.ndim)
    br = b_router.reshape(N_EXPERTS, 1)
    nw = norm_w.reshape(1, d)
    return pl.pallas_call(
        functools.partial(_route_kernel, n_ctx=n_ctx, ctx_row=ctx_row),
        grid=(bsz, nt),
        in_specs=[tile, full(mod_l), full(nw), full(w_router_t), full(br)],
        out_specs=[tile, lane_tile, lane_tile, pl.BlockSpec((N_EXPERTS, LANES), lambda b, i: (0, 0))],
        out_shape=[jax.ShapeDtypeStruct((bsz, n, d), F32),
                   jax.ShapeDtypeStruct((SUBLANES, bsz * n), I32),
                   jax.ShapeDtypeStruct((SUBLANES, bsz * n), F32),
                   jax.ShapeDtypeStruct((N_EXPERTS, LANES), F32)],
        scratch_shapes=[pltpu.VMEM((N_EXPERTS, LANES), F32)],
        compiler_params=_cparams(2), name="route",
    )(x1, mod_l, nw, w_router_t, br)


def _dispatch_kernel(idx_ref, cnt_ref, h_ref, xs_in_ref, xs_ref, dest_ref, blk_e_ref,
                     start_ref, sem, *, tm, nblk):
    del xs_in_ref
    i = pl.program_id(0)

    @pl.when(i == 0)
    def _():
        def expert(e, end):
            start_ref[e] = end
            padded = ((cnt_ref[e] + MOE_BLK - 1) // MOE_BLK) * MOE_BLK
            return end + padded
        total = lax.fori_loop(0, N_EXPERTS, expert, 0)
        start_ref[N_EXPERTS] = total

        def block(j, carry):
            def count(e, acc):
                return acc + jnp.where(start_ref[e + 1] <= j * MOE_BLK, 1, 0)
            blk_e_ref[j] = jnp.minimum(lax.fori_loop(0, N_EXPERTS, count, 0), N_EXPERTS - 1)
            return carry
        lax.fori_loop(0, nblk, block, 0)
        blk_e_ref[nblk] = total // MOE_BLK

    def row_copy(r, dst):
        return pltpu.make_async_copy(h_ref.at[pl.ds(r, 1), :], xs_ref.at[pl.ds(dst, 1), :], sem)

    def issue(r, carry):
        d0 = start_ref[idx_ref[r]] + idx_ref[2 * tm + r]
        d1 = start_ref[idx_ref[tm + r]] + idx_ref[3 * tm + r]
        dest_ref[2 * r] = d0
        dest_ref[2 * r + 1] = d1
        row_copy(r, d0).start()
        row_copy(r, d1).start()
        return carry
    lax.fori_loop(0, tm, issue, 0)

    def drain(r, carry):
        row_copy(0, 0).wait()
        row_copy(0, 0).wait()
        return carry
    lax.fori_loop(0, tm, drain, 0)


def _dispatch(idx_tiles, counts, h_flat, nblk):
    ntok, d = h_flat.shape
    tm = TOK_BLK
    xs0 = jnp.zeros((nblk * MOE_BLK, d), F32)
    smem = lambda shape, index: pl.BlockSpec(shape, index, memory_space=pltpu.SMEM)
    return pl.pallas_call(
        functools.partial(_dispatch_kernel, tm=tm, nblk=nblk),
        grid=(ntok // tm,),
        in_specs=[smem((4 * tm,), lambda i: (i,)),
                  pl.BlockSpec(memory_space=pltpu.SMEM),
                  pl.BlockSpec((tm, d), lambda i: (i, 0)),
                  pl.BlockSpec(memory_space=pl.ANY)],
        out_specs=[pl.BlockSpec(memory_space=pl.ANY),
                   smem((2 * tm,), lambda i: (i,)),
                   pl.BlockSpec(memory_space=pltpu.SMEM)],
        out_shape=[jax.ShapeDtypeStruct((nblk * MOE_BLK, d), F32),
                   jax.ShapeDtypeStruct((2 * ntok,), I32),
                   jax.ShapeDtypeStruct((nblk + 1,), I32)],
        scratch_shapes=[pltpu.SMEM((N_EXPERTS + 1,), I32), pltpu.SemaphoreType.DMA(())],
        input_output_aliases={3: 0},
        compiler_params=_cparams(1), name="dispatch",
    )(idx_tiles, counts, h_flat, xs0)


def _expert_kernel(blk_e_ref, x_ref, wg_ref, wu_ref, wd_ref, o_ref, *, nblk):
    @pl.when(pl.program_id(0) < blk_e_ref[nblk])
    def _():
        x = x_ref[...].astype(BF16)
        hid = _silu(_dot(x, wg_ref[0])) * _dot(x, wu_ref[0])
        o_ref[...] = _dot(hid, wd_ref[0])

    @pl.when(pl.program_id(0) >= blk_e_ref[nblk])
    def _():
        o_ref[...] = jnp.zeros_like(o_ref)


def _experts(blk_e, xs, w_gate, w_up, w_down, nblk):
    d = xs.shape[1]
    ff = w_gate.shape[2]
    return pl.pallas_call(
        functools.partial(_expert_kernel, nblk=nblk),
        grid_spec=pltpu.PrefetchScalarGridSpec(
            num_scalar_prefetch=1, grid=(nblk,),
            in_specs=[pl.BlockSpec((MOE_BLK, d), lambda i, be: (i, 0)),
                      pl.BlockSpec((1, d, ff), lambda i, be: (be[i], 0, 0)),
                      pl.BlockSpec((1, d, ff), lambda i, be: (be[i], 0, 0)),
                      pl.BlockSpec((1, ff, d), lambda i, be: (be[i], 0, 0))],
            out_specs=pl.BlockSpec((MOE_BLK, d), lambda i, be: (i, 0))),
        out_shape=jax.ShapeDtypeStruct(xs.shape, F32),
        compiler_params=_cparams(1), name="experts",
    )(blk_e, xs, w_gate, w_up, w_down)


def _combine_kernel(dest_ref, x_ref, wt_ref, mod_ref, ys_ref, o_ref, g0_ref, g1_ref, sem,
                    *, n_ctx, ctx_row):
    b, i = pl.program_id(0), pl.program_id(1)
    tm, d = x_ref.shape[1:]

    def row_copy(src, r, buf):
        return pltpu.make_async_copy(ys_ref.at[pl.ds(src, 1), :], buf.at[pl.ds(r, 1), :], sem)

    def issue(r, carry):
        row_copy(dest_ref[2 * r], r, g0_ref).start()
        row_copy(dest_ref[2 * r + 1], r, g1_ref).start()
        return carry
    lax.fori_loop(0, tm, issue, 0)

    def drain(r, carry):
        row_copy(0, 0, g0_ref).wait()
        row_copy(0, 0, g1_ref).wait()
        return carry
    lax.fori_loop(0, tm, drain, 0)

    wt = wt_ref[...]
    y = wt[:, 0:1] * g0_ref[...] + wt[:, 1:2] * g1_ref[...]
    gate = _mod_rows(mod_ref, 5, b, i * tm, n_ctx, ctx_row, tm, d)
    o_ref[0] = x_ref[0] + gate * y


def _combine(dest, x1, wt_cols, mod_l, ys, n_ctx, ctx_row):
    bsz, n, d = x1.shape
    tm = TOK_BLK
    nt = n // tm
    return pl.pallas_call(
        functools.partial(_combine_kernel, n_ctx=n_ctx, ctx_row=ctx_row),
        grid=(bsz, nt),
        in_specs=[pl.BlockSpec((2 * tm,), lambda b, i: (b * nt + i,), memory_space=pltpu.SMEM),
                  pl.BlockSpec((1, tm, d), lambda b, i: (b, i, 0)),
                  pl.BlockSpec((tm, 2), lambda b, i: (b * nt + i, 0)),
                  pl.BlockSpec(mod_l.shape, lambda b, i: (0, 0)),
                  pl.BlockSpec(memory_space=pl.ANY)],
        out_specs=pl.BlockSpec((1, tm, d), lambda b, i: (b, i, 0)),
        out_shape=jax.ShapeDtypeStruct((bsz, n, d), F32),
        scratch_shapes=[pltpu.VMEM((tm, d), F32), pltpu.VMEM((tm, d), F32),
                        pltpu.SemaphoreType.DMA(())],
        compiler_params=_cparams(2), name="combine",
    )(dest, x1, wt_cols, mod_l, ys)


def _layer(xa, mod_l, layer, n_ctx, ctx_row, p, rope):
    bsz, n, d = xa.shape
    na_w = NA_HEADS * HEAD_DIM
    hg_w = HG_HEADS * HG_DK
    wa_qw = WA_HEADS * HEAD_DIM
    wa_kvw = WA_KV_HEADS * HEAD_DIM
    n_in = 3 * na_w + 5 * hg_w + wa_qw + 2 * wa_kvw
    col_gate = 0
    base = N_BRANCHES * d
    col_na = [base + k * na_w for k in range(3)]
    col_hg = [base + 3 * na_w + k * hg_w for k in range(5)]
    col_wq = base + 3 * na_w + 5 * hg_w
    col_wk = col_wq + wa_qw
    col_wv = col_wk + wa_kvw
    w_in = p["w_in"]
    pad = (-w_in.shape[1]) % 1024
    w_in_r = jnp.concatenate(
        [w_in[:, n_in:], w_in[:, :n_in], jnp.zeros((d, pad), w_in.dtype)], axis=1).astype(BF16)

    proj = _inproj(xa, mod_l, p["norm1"], w_in_r, n_ctx, ctx_row)

    ya = _na_attention(proj, _na_bias(p["na_rpb"]), p["na_q_norm"], p["na_k_norm"], *col_na)
    yc = _wa_attention(proj, p["wa_sink"], p["wa_q_norm"], p["wa_k_norm"], rope[0], rope[1],
                       col_wq, col_wk, col_wv)
    o_f = _hgrn_pass(proj, p["hg_lower"], layer, False, col_hg[0], col_hg[1], col_hg[3])
    yb = _hgrn_pass(proj, p["hg_lower"], layer, True, col_hg[0], col_hg[2], col_hg[3],
                    final_args=(col_hg[4], o_f, p["hg_norm"]))

    bf = lambda w: w.astype(BF16)
    x1 = _merge(xa, ya, yb, yc, proj, mod_l, bf(p["w_pa"]), bf(p["w_pb"]), bf(p["w_pc"]),
                bf(p["w_out"]), col_gate, n_ctx, ctx_row)

    h2, idx, wts, counts = _route(x1, mod_l, p["norm2"], p["w_router"].T, p["b_router"],
                                  n_ctx, ctx_row)
    ntok = bsz * n
    nblk = -(-2 * ntok // MOE_BLK) + N_EXPERTS
    nt = ntok // TOK_BLK
    idx_tiles = idx[:4].reshape(4, nt, TOK_BLK).transpose(1, 0, 2).reshape(-1)
    xs, dest, blk_e = _dispatch(idx_tiles, counts[:, 0].astype(I32), h2.reshape(ntok, d), nblk)
    ys = _experts(blk_e, xs, bf(p["w_gate"]), bf(p["w_up"]), bf(p["w_down"]), nblk)
    return _combine(dest, x1, wts[:2].T, mod_l, ys, n_ctx, ctx_row)


def kernel(x, c, ctx, c_ctx, w_ada, b_ada, norm1, norm2, w_in, na_q_norm, na_k_norm, na_rpb, hg_lower,
           hg_norm, wa_q_norm, wa_k_norm, wa_sink, w_pa, w_pb, w_pc, w_out, w_router, b_router,
           w_gate, w_up, w_down):
    bsz, t, d = x.shape
    n_ctx = ctx.shape[1]
    depth = w_ada.shape[0]
    assert n_ctx == TOK_BLK and t % TOK_BLK == 0 and t // TOK_BLK >= 3
    assert t % GRID_W == 0 and bsz + 1 <= SUBLANES

    xa = jnp.concatenate([ctx, x], axis=1)
    cond = jnp.concatenate([c, c_ctx[None], jnp.zeros((SUBLANES - bsz - 1, d), F32)], axis=0)
    mod = _ada(cond, w_ada, b_ada)
    rope = _rope_tables(n_ctx, t)
    for l in range(depth):
        p = dict(norm1=norm1[l], norm2=norm2[l], w_in=w_in[l], na_q_norm=na_q_norm[l],
                 na_k_norm=na_k_norm[l], na_rpb=na_rpb[l], hg_lower=hg_lower, hg_norm=hg_norm[l],
                 wa_q_norm=wa_q_norm[l], wa_k_norm=wa_k_norm[l], wa_sink=wa_sink[l],
                 w_pa=w_pa[l], w_pb=w_pb[l], w_pc=w_pc[l], w_out=w_out[l],
                 w_router=w_router, b_router=b_router, w_gate=w_gate[l], w_up=w_up[l],
                 w_down=w_down[l])
        xa = _layer(xa, mod[l], l, n_ctx, bsz, p, rope)
    return xa[:, n_ctx:]
```

```python
import functools

import numpy as np
import jax
import jax.numpy as jnp
from jax import lax
from jax.experimental import pallas as pl
from jax.experimental.pallas import tpu as pltpu

F32 = jnp.float32
BF16 = jnp.bfloat16
I32 = jnp.int32

GRID_W = 64
NA_HEADS = 8
NA_ROWS = 8
NA_COLS = 16
HEAD_DIM = 64
HG_HEADS = 4
HG_DK = 128
LOG_FLOOR = 1e-30
WA_HEADS = 8
WA_KV_HEADS = 2
WA_WINDOW = 128
ROPE_THETA = 10000.0
N_EXPERTS = 16
N_GROUPS = 4
EXPERTS_PER_GROUP = N_EXPERTS // N_GROUPS
N_BRANCHES = 3
NEG_INF = -1e30
RMS_EPS = 1e-6

LANES = 128
SUBLANES = 8
VMEM_LIMIT = 56 * 1024 * 1024

TOK_BLK = 256
MOE_BLK = 256
MOE_CHUNK = 32


def _cparams(n_axes):
    return pltpu.CompilerParams(
        dimension_semantics=("arbitrary",) * n_axes, vmem_limit_bytes=VMEM_LIMIT)


def _sigmoid(x):
    return 1.0 / (1.0 + jnp.exp(-x))


def _silu(x):
    return x * _sigmoid(x)


def _dot(a, b):
    return jnp.dot(a.astype(BF16), b.astype(BF16), preferred_element_type=F32)


def _dot_nt(a, b):
    return lax.dot_general(a.astype(BF16), b.astype(BF16), (((1,), (1,)), ((), ())),
                           preferred_element_type=F32)


def _dot_tn(a, b):
    return lax.dot_general(a.astype(BF16), b.astype(BF16), (((0,), (0,)), ((), ())),
                           preferred_element_type=F32)


def _hi_lo(a):
    hi = a.astype(BF16)
    lo = (a - hi.astype(F32)).astype(BF16)
    return hi, lo


def _dot3(a, b):
    ah, al = _hi_lo(a)
    bh, bl = _hi_lo(b)
    d = functools.partial(jnp.dot, preferred_element_type=F32)
    return d(ah, bh) + (d(ah, bl) + d(al, bh))


def _dot3_nt(a, b):
    ah, al = _hi_lo(a)
    bh, bl = _hi_lo(b)
    d = functools.partial(lax.dot_general, dimension_numbers=(((1,), (1,)), ((), ())),
                          preferred_element_type=F32)
    return d(ah, bh) + (d(ah, bl) + d(al, bh))


def _rms(x, w):
    return x * lax.rsqrt(jnp.mean(x * x, axis=-1, keepdims=True) + RMS_EPS) * w


def _iota(shape, dim):
    return lax.broadcasted_iota(I32, shape, dim)


def _ada_kernel(cond_ref, w_ref, b_ref, o_ref):
    o_ref[0] = _dot3(_silu(cond_ref[...]), w_ref[0]) + b_ref[0]


def _ada(cond, w_ada, b_ada):
    depth, d, d6 = w_ada.shape
    rows = cond.shape[0]
    tn = d6 // 4
    return pl.pallas_call(
        _ada_kernel,
        grid=(depth, d6 // tn),
        in_specs=[pl.BlockSpec((rows, d), lambda l, j: (0, 0)),
                  pl.BlockSpec((1, d, tn), lambda l, j: (l, 0, j)),
                  pl.BlockSpec((1, 1, tn), lambda l, j: (l, 0, j))],
        out_specs=pl.BlockSpec((1, rows, tn), lambda l, j: (l, 0, j)),
        out_shape=jax.ShapeDtypeStruct((depth, rows, d6), F32),
        compiler_params=_cparams(2), name="ada",
    )(cond, w_ada, b_ada.reshape(depth, 1, d6))


def _mod_rows(mod_ref, which, b, row0, n_ctx, ctx_row, tm, d):
    lat = mod_ref[pl.ds(b, 1), which * d:(which + 1) * d]
    cx = mod_ref[ctx_row:ctx_row + 1, which * d:(which + 1) * d]
    row = row0 + _iota((tm, d), 0)
    return jnp.where(row < n_ctx, cx, lat)


def _inproj_kernel(x_ref, mod_ref, nw_ref, w_ref, o_ref, h_ref, *, n_ctx, ctx_row):
    b, i, j = pl.program_id(0), pl.program_id(1), pl.program_id(2)
    tm, d = h_ref.shape

    @pl.when(j == 0)
    def _():
        h = _rms(x_ref[0], nw_ref[...])
        shift = _mod_rows(mod_ref, 0, b, i * tm, n_ctx, ctx_row, tm, d)
        scale = _mod_rows(mod_ref, 1, b, i * tm, n_ctx, ctx_row, tm, d)
        h_ref[...] = (h * (1.0 + scale) + shift).astype(BF16)

    o_ref[0] = jnp.dot(h_ref[...], w_ref[...], preferred_element_type=F32).astype(o_ref.dtype)


W_IN_COL_BLK = 256


def _take_cols_kernel(src_ref, w_ref, o_ref):
    del src_ref
    o_ref[...] = w_ref[...].astype(BF16)


def _take_cols(w_in, col_ranges):
    d = w_in.shape[0]
    blk = W_IN_COL_BLK
    src = []
    for start, stop in col_ranges:
        assert start % blk == 0 and stop % blk == 0
        src += list(range(start // blk, stop // blk))
    return pl.pallas_call(
        _take_cols_kernel,
        grid_spec=pltpu.PrefetchScalarGridSpec(
            num_scalar_prefetch=1, grid=(len(src),),
            in_specs=[pl.BlockSpec((d, blk), lambda j, src: (0, src[j]))],
            out_specs=pl.BlockSpec((d, blk), lambda j, src: (0, j))),
        out_shape=jax.ShapeDtypeStruct((d, len(src) * blk), BF16),
        compiler_params=_cparams(1), name="take_cols",
    )(jnp.asarray(src, I32), w_in)


def _inproj(xa, mod_l, norm_w, w, n_ctx, ctx_row, tn, out_dtype):
    bsz, n, d = xa.shape
    cols = w.shape[1]
    tm = 1408 if n % 1408 == 0 else TOK_BLK
    return pl.pallas_call(
        functools.partial(_inproj_kernel, n_ctx=n_ctx, ctx_row=ctx_row),
        grid=(bsz, n // tm, cols // tn),
        in_specs=[pl.BlockSpec((1, tm, d), lambda b, i, j: (b, i, 0)),
                  pl.BlockSpec(mod_l.shape, lambda b, i, j: (0, 0)),
                  pl.BlockSpec((1, d), lambda b, i, j: (0, 0)),
                  pl.BlockSpec((d, tn), lambda b, i, j: (0, j))],
        out_specs=pl.BlockSpec((1, tm, tn), lambda b, i, j: (b, i, j)),
        out_shape=jax.ShapeDtypeStruct((bsz, n, cols), out_dtype),
        scratch_shapes=[pltpu.VMEM((tm, d), BF16)],
        compiler_params=_cparams(3), name="inproj",
    )(xa, mod_l, norm_w.reshape(1, d), w)


def _pair_norm(x, w):
    lane = _iota(x.shape, 1)
    lo = lane < HEAD_DIM
    sq = x * x
    s_lo = jnp.sum(jnp.where(lo, sq, 0.0), axis=-1, keepdims=True)
    s_hi = jnp.sum(jnp.where(lo, 0.0, sq), axis=-1, keepdims=True)
    ms = jnp.where(lo, s_lo, s_hi) * (1.0 / HEAD_DIM)
    return x * lax.rsqrt(ms + RMS_EPS) * w


LOG2E = 1.4426950408889634


def _ones_beside(v, value_lanes):
    return jnp.where(value_lanes, v, 1.0).astype(BF16)


def _softmax_av(s_loc, s_ctx, v_loc, v_ctx, sink):
    m = jnp.maximum(jnp.max(s_loc, axis=-1, keepdims=True), jnp.max(s_ctx, axis=-1, keepdims=True))
    if sink is not None:
        m = jnp.maximum(m, sink)
    p_loc = jnp.exp2(s_loc - m).astype(BF16)
    p_ctx = jnp.exp2(s_ctx - m).astype(BF16)
    acc = (jnp.dot(p_loc, v_loc, preferred_element_type=F32)
           + jnp.dot(p_ctx, v_ctx, preferred_element_type=F32))
    den = pltpu.roll(acc, LANES // 2, 1)
    if sink is not None:
        den = den + jnp.exp2(sink - m)
    return acc / den


NA_QROWS = TOK_BLK // GRID_W
NA_KROWS = 3 * NA_QROWS
N_RPB_COLS = 2 * NA_COLS - 1
N_RPB_ROWS = 2 * NA_ROWS - 1
NA_VARIANTS = 4


def _na_row_valid(variant, a, k):
    if variant == 0:
        return False
    if variant == 1:
        return NA_QROWS <= k < NA_QROWS + NA_ROWS
    if variant == 3:
        return k < NA_ROWS
    return a <= k < a + NA_ROWS


def _na_bias_kernel(rpb_ref, o_ref):
    h = pl.program_id(0)
    qc = _iota((GRID_W, GRID_W), 0)
    kc = _iota((GRID_W, GRID_W), 1)
    col_lo = jnp.clip(qc - NA_COLS // 2, 0, GRID_W - NA_COLS)
    col_ok = (kc >= col_lo) & (kc < col_lo + NA_COLS)
    d_col = kc - qc + (NA_COLS - 1)
    masked = jnp.full((GRID_W, GRID_W), NEG_INF, F32)
    blocks = []
    for dr in range(N_RPB_ROWS):
        acc = jnp.zeros((GRID_W, GRID_W), F32)
        for dc in range(N_RPB_COLS):
            val = rpb_ref[(h * N_RPB_ROWS + dr) * N_RPB_COLS + dc]
            acc = jnp.where(d_col == dc, val, acc)
        blocks.append(jnp.where(col_ok, acc * LOG2E, NEG_INF))
    for variant in range(NA_VARIANTS):
        for a in range(NA_QROWS):
            for k in range(NA_KROWS):
                dr = k - NA_QROWS - a + NA_ROWS - 1
                ok = _na_row_valid(variant, a, k) and 0 <= dr < N_RPB_ROWS
                o_ref[variant, 0, a * GRID_W:(a + 1) * GRID_W, k * GRID_W:(k + 1) * GRID_W] = (
                    blocks[dr] if ok else masked)


def _na_bias(rpb):
    heads = rpb.shape[0]
    return pl.pallas_call(
        _na_bias_kernel,
        grid=(heads,),
        in_specs=[pl.BlockSpec(memory_space=pltpu.SMEM)],
        out_specs=pl.BlockSpec((NA_VARIANTS, 1, TOK_BLK, 3 * TOK_BLK), lambda h: (0, h, 0, 0)),
        out_shape=jax.ShapeDtypeStruct((NA_VARIANTS, heads, TOK_BLK, 3 * TOK_BLK), F32),
        compiler_params=_cparams(1), name="na_bias",
    )(rpb.reshape(-1))


def _na_kernel(q_ref, kp_ref, kc_ref, kn_ref, vp_ref, vc_ref, vn_ref, kx_ref, vx_ref,
               bias_ref, qn_ref, kn_w_ref, o_ref):
    tq = q_ref.shape[1]
    lo = _iota((tq, LANES), 1) < HEAD_DIM
    lo_k = _iota((3 * tq, LANES), 1) < HEAD_DIM
    qn_w = qn_ref[...] * (HEAD_DIM ** -0.5 * LOG2E)
    for p in range(NA_HEADS // 2):
        pair = slice(p * LANES, (p + 1) * LANES)
        f32 = lambda ref: ref[0, :, pair].astype(F32)
        q = _pair_norm(f32(q_ref), qn_w)
        k_loc = _pair_norm(jnp.concatenate([f32(kp_ref), f32(kc_ref), f32(kn_ref)], axis=0),
                           kn_w_ref[...]).astype(BF16)
        k_ctx = _pair_norm(f32(kx_ref), kn_w_ref[...]).astype(BF16)
        v_loc = jnp.concatenate([f32(vp_ref), f32(vc_ref), f32(vn_ref)], axis=0)
        v_ctx = f32(vx_ref)
        outs = []
        for h in range(2):
            mine = lo if h == 0 else ~lo
            mine_k = lo_k if h == 0 else ~lo_k
            qh = jnp.where(mine, q, 0.0).astype(BF16)
            s_loc = _dot_nt(qh, k_loc) + bias_ref[0, 2 * p + h]
            s_ctx = _dot_nt(qh, k_ctx)
            outs.append(_softmax_av(s_loc, s_ctx, _ones_beside(v_loc, mine_k),
                                    _ones_beside(v_ctx, mine), None))
        o_ref[0, :, pair] = jnp.where(lo, outs[0], outs[1])


def _na_attention(proj, bias, qn_w, kn_w, col_q, col_k, col_v):
    bsz, n, _ = proj.shape
    nblk = n // TOK_BLK
    last = nblk - 1
    width = NA_HEADS * HEAD_DIM
    cq, ck, cv = col_q // width, col_k // width, col_v // width

    def blk(col, shift):
        return pl.BlockSpec((1, TOK_BLK, width),
                            lambda b, i: (b, jnp.clip(i + shift, 0, last), col))

    def ctx_blk(col):
        return pl.BlockSpec((1, TOK_BLK, width), lambda b, i: (b, 0, col))

    def variant(b, i):
        return (jnp.where(i == 0, 0, jnp.where(i == 1, 1, jnp.where(i == last, 3, 2))), 0, 0, 0)

    vec = pl.BlockSpec((1, LANES), lambda b, i: (0, 0))
    return pl.pallas_call(
        _na_kernel,
        grid=(bsz, nblk),
        in_specs=[blk(cq, 0), blk(ck, -1), blk(ck, 0), blk(ck, 1), blk(cv, -1), blk(cv, 0), blk(cv, 1),
                  ctx_blk(ck), ctx_blk(cv),
                  pl.BlockSpec((1, NA_HEADS, TOK_BLK, 3 * TOK_BLK), variant), vec, vec],
        out_specs=pl.BlockSpec((1, TOK_BLK, width), lambda b, i: (b, i, 0)),
        out_shape=jax.ShapeDtypeStruct((bsz, n, width), F32),
        compiler_params=_cparams(2), name="na_attn",
    )(proj, proj, proj, proj, proj, proj, proj, proj, proj, bias,
      jnp.tile(qn_w, 2).reshape(1, LANES), jnp.tile(kn_w, 2).reshape(1, LANES))


def _rope(x, cos, sin_signed):
    lane = _iota(x.shape, 1)
    first = (lane & (HEAD_DIM // 2 - 1)) < HEAD_DIM // 4
    quarter = HEAD_DIM // 4
    partner = jnp.where(first, pltpu.roll(x, LANES - quarter, 1), pltpu.roll(x, quarter, 1))
    return x * cos + partner * sin_signed


def _wa_kernel(sink_ref, q_ref, kp_ref, kc_ref, kn_ref, vp_ref, vc_ref, vn_ref, kx_ref, vx_ref,
               cq_ref, sq_ref, cp_ref, sp_ref, cn_ref, sn_ref, qn_ref, kn_w_ref, o_ref, *, last):
    i = pl.program_id(1)
    tq = q_ref.shape[1]
    f32 = lambda ref: ref[0].astype(F32)
    cos_loc = jnp.concatenate([cp_ref[...], cq_ref[...], cn_ref[...]], axis=0)
    sin_loc = jnp.concatenate([sp_ref[...], sq_ref[...], sn_ref[...]], axis=0)
    k_loc = _pair_norm(jnp.concatenate([f32(kp_ref), f32(kc_ref), f32(kn_ref)], axis=0),
                       kn_w_ref[...])
    k_loc = _rope(k_loc, cos_loc, sin_loc)
    k_ctx = _pair_norm(f32(kx_ref), kn_w_ref[...])
    v_loc = jnp.concatenate([f32(vp_ref), f32(vc_ref), f32(vn_ref)], axis=0)
    v_ctx = f32(vx_ref)
    half = LANES // 2
    lo = _iota((tq, LANES), 1) < HEAD_DIM
    lo_k = _iota((3 * tq, LANES), 1) < HEAD_DIM
    operands = {}
    for hh in range(2):
        mine, mine_k = (lo, lo_k) if hh == 0 else (~lo, ~lo_k)
        for kv in range(WA_KV_HEADS):
            mv = (lambda x: x) if kv == hh else (lambda x: pltpu.roll(x, half, 1))
            operands[hh, kv] = (mv(k_loc).astype(BF16), mv(k_ctx).astype(BF16),
                                _ones_beside(mv(v_loc), mine_k), _ones_beside(mv(v_ctx), mine))
    qn_w = qn_ref[...] * (HEAD_DIM ** -0.5 * LOG2E)

    qi = _iota((tq, 3 * tq), 0)
    kj = _iota((tq, 3 * tq), 1)
    rel = kj - tq - qi
    lo_col = jnp.where(i >= 2, 0, tq)
    hi_col = jnp.where(i >= 1, jnp.where(i < last, 3 * tq, 2 * tq), 0)
    ok = (rel >= -WA_WINDOW) & (rel <= WA_WINDOW) & (kj >= lo_col) & (kj < hi_col)
    mask = jnp.where(ok, 0.0, NEG_INF)

    group = WA_HEADS // WA_KV_HEADS
    for p in range(WA_HEADS // 2):
        q = _pair_norm(q_ref[0, :, p * LANES:(p + 1) * LANES].astype(F32), qn_w)
        q = _rope(q, cq_ref[...], sq_ref[...])
        outs = []
        for hh in range(2):
            h = 2 * p + hh
            kl, kx, vl, vx = operands[hh, h // group]
            qh = jnp.where(lo if hh == 0 else ~lo, q, 0.0).astype(BF16)
            s_loc = _dot_nt(qh, kl) + mask
            s_ctx = _dot_nt(qh, kx)
            outs.append(_softmax_av(s_loc, s_ctx, vl, vx, sink_ref[h] * LOG2E))
        o_ref[0, :, p * LANES:(p + 1) * LANES] = jnp.where(lo, outs[0], outs[1])


def _wa_attention(proj, sink, qn_w, kn_w, cos_t, sin_t, col_q, col_k, col_v):
    bsz, n, _ = proj.shape
    nblk = n // TOK_BLK
    last = nblk - 1
    qw = WA_HEADS * HEAD_DIM
    cq, ck, cv = col_q // qw, col_k // LANES, col_v // LANES

    def blk(col, shift):
        return pl.BlockSpec((1, TOK_BLK, LANES),
                            lambda b, i: (b, jnp.clip(i + shift, 0, last), col))

    def tab(shift):
        return pl.BlockSpec((TOK_BLK, LANES), lambda b, i: (jnp.clip(i + shift, 0, last), 0))

    def ctx_blk(col):
        return pl.BlockSpec((1, TOK_BLK, LANES), lambda b, i: (b, 0, col))

    vec = pl.BlockSpec((1, LANES), lambda b, i: (0, 0))
    return pl.pallas_call(
        functools.partial(_wa_kernel, last=last),
        grid=(bsz, nblk),
        in_specs=[pl.BlockSpec(memory_space=pltpu.SMEM),
                  pl.BlockSpec((1, TOK_BLK, qw), lambda b, i: (b, i, cq)),
                  blk(ck, -1), blk(ck, 0), blk(ck, 1), blk(cv, -1), blk(cv, 0), blk(cv, 1),
                  ctx_blk(ck), ctx_blk(cv),
                  tab(0), tab(0), tab(-1), tab(-1), tab(1), tab(1), vec, vec],
        out_specs=pl.BlockSpec((1, TOK_BLK, qw), lambda b, i: (b, i, 0)),
        out_shape=jax.ShapeDtypeStruct((bsz, n, qw), F32),
        compiler_params=_cparams(2), name="wa_attn",
    )(sink, proj, proj, proj, proj, proj, proj, proj, proj, proj,
      cos_t, sin_t, cos_t, sin_t, cos_t, sin_t,
      jnp.tile(qn_w, 2).reshape(1, LANES), jnp.tile(kn_w, 2).reshape(1, LANES))


def _rope_tables(n_ctx, t):
    quarter = HEAD_DIM // 4
    inv_freq = ROPE_THETA ** (-jnp.arange(quarter, dtype=F32) / quarter)
    pos = jnp.arange(t)
    lane = np.arange(LANES)
    in_head = lane % HEAD_DIM
    use_col = in_head >= HEAD_DIM // 2
    second = (in_head % (HEAD_DIM // 2)) >= quarter
    freq = inv_freq[in_head % quarter]
    p = jnp.where(use_col[None, :], (pos % GRID_W)[:, None], (pos // GRID_W)[:, None]).astype(F32)
    ang = p * freq[None, :]
    cos = jnp.cos(ang)
    sin = jnp.where(second[None, :], jnp.sin(ang), -jnp.sin(ang))
    cos = jnp.concatenate([jnp.ones((n_ctx, LANES), F32), cos], axis=0)
    sin = jnp.concatenate([jnp.zeros((n_ctx, LANES), F32), sin], axis=0)
    return cos, sin


HG_SUB = SUBLANES


def _hgrn_kernel(q_ref, f_ref, v_ref, hl_ref, *rest, layer, reverse, final):
    if final:
        g_ref, prev_ref, nw_ref, o_ref, st_ref = rest
    else:
        o_ref, st_ref = rest
    c = q_ref.shape[1]

    @pl.when(pl.program_id(2) == 0)
    def _():
        st_ref[...] = jnp.zeros_like(st_ref)

    a = hl_ref[0]
    e = jnp.exp(a - jnp.max(a, axis=0, keepdims=True))
    pr = e / jnp.sum(e, axis=0, keepdims=True)
    lb = jnp.zeros((1, HG_DK), F32)
    for j in range(1, layer + 1):
        lb = lb + pr[j:j + 1]

    qs = _silu(q_ref[0])
    f = lb + (1.0 - lb) * _sigmoid(f_ref[0])
    kk = 1.0 - f
    g = jnp.log(jnp.maximum(f, LOG_FLOOR))
    v = v_ref[0]
    row = _iota((c, HG_DK), 0)

    def shifted(x, d):
        return pltpu.roll(x, (c - d) if reverse else d, 0)

    def has_earlier(d, width):
        r = row & (width - 1)
        return (r <= width - 1 - d) if reverse else (r >= d)

    b = g
    d = 1
    while d < c:
        b = b + jnp.where(has_earlier(d, c), shifted(b, d), 0.0)
        d *= 2

    ri = _iota((c, c), 0)
    ci = _iota((c, c), 1)
    att = jnp.zeros((c, c), F32)
    hs = c // 2
    while hs >= HG_SUB:
        blk = 2 * hs
        pieces = []
        for p in range(c // blk):
            mid = p * blk + hs
            r = mid if reverse else mid - 1
            pieces.append(jnp.broadcast_to(b[r:r + 1, :], (blk, HG_DK)))
        ref = pieces[0] if len(pieces) == 1 else jnp.concatenate(pieces, axis=0)
        decay = jnp.exp(-jnp.abs(b - ref))
        later_half = has_earlier(hs, blk)
        q_l = jnp.where(later_half, qs * decay, 0.0)
        k_l = jnp.where(later_half, 0.0, kk * decay)
        a_l = _dot_nt(q_l, k_l)
        if blk < c:
            sh = blk.bit_length() - 1
            a_l = jnp.where((ri >> sh) == (ci >> sh), a_l, 0.0)
        att = att + a_l
        hs //= 2

    o = _dot(att, v)
    o = o + jnp.sum(qs * kk, axis=-1, keepdims=True) * v
    for d in range(1, HG_SUB):
        dl = jnp.minimum(b - shifted(b, d), 0.0)
        term = jnp.where(has_earlier(d, HG_SUB), qs * shifted(kk, d) * jnp.exp(dl), 0.0)
        o = o + jnp.sum(term, axis=-1, keepdims=True) * shifted(v, d)

    st = st_ref[...]
    o = o + _dot_nt(qs * jnp.exp(b), st)
    b_end = b[0:1, :] if reverse else b[c - 1:c, :]
    st_ref[...] = st * jnp.exp(b_end) + _dot_tn(v, kk * jnp.exp(b_end - b))

    if final:
        tot = prev_ref[0] + o
        gate = g_ref[0]
        o_ref[0] = _rms(tot, nw_ref[...]) * _silu(gate)
    else:
        o_ref[0] = o


def _hgrn_pass(proj, hl, layer, reverse, col_q, col_f, col_v, final_args=None):
    bsz, n, _ = proj.shape
    nchunk = n // TOK_BLK
    cq, cf, cv = col_q // LANES, col_f // LANES, col_v // LANES

    def chunk(t):
        return jnp.where(t == 0, 0, nchunk - t) if reverse else t

    def blk(col):
        return pl.BlockSpec((1, TOK_BLK, LANES), lambda b, h, t: (b, chunk(t), col + h))

    yblk = pl.BlockSpec((1, TOK_BLK, LANES), lambda b, h, t: (b, chunk(t), h))
    in_specs = [blk(cq), blk(cf), blk(cv),
                pl.BlockSpec((1,) + hl.shape[1:2] + (LANES,), lambda b, h, t: (1 if reverse else 0, 0, h))]
    args = [proj, proj, proj, hl]
    final = final_args is not None
    if final:
        col_g, prev, norm_w = final_args
        in_specs += [blk(col_g // LANES), yblk, pl.BlockSpec((1, LANES), lambda b, h, t: (0, 0))]
        args += [proj, prev, norm_w.reshape(1, LANES)]
    return pl.pallas_call(
        functools.partial(_hgrn_kernel, layer=layer, reverse=reverse, final=final),
        grid=(bsz, HG_HEADS, nchunk),
        in_specs=in_specs,
        out_specs=yblk,
        out_shape=jax.ShapeDtypeStruct((bsz, n, HG_HEADS * HG_DK), F32),
        scratch_shapes=[pltpu.VMEM((HG_DK, HG_DK), F32)],
        compiler_params=_cparams(3), name="hgrn_bwd" if reverse else "hgrn_fwd",
    )(*args)


def _merge_kernel(x_ref, ya_ref, yb_ref, yc_ref, ga_ref, gb_ref, gc_ref, mod_ref,
                  wa_ref, wb_ref, wc_ref, wo_ref, o_ref, *, n_ctx, ctx_row):
    b, i = pl.program_id(0), pl.program_id(1)
    tm, d = x_ref.shape[1:]
    gate_of = lambda ref: _sigmoid(ref[0].astype(F32))
    merged = (gate_of(ga_ref) * _dot(ya_ref[0], wa_ref[...])
              + gate_of(gb_ref) * _dot(yb_ref[0], wb_ref[...])
              + gate_of(gc_ref) * _dot(yc_ref[0], wc_ref[...]))
    mix = _dot(merged, wo_ref[...])
    gate = _mod_rows(mod_ref, 2, b, i * tm, n_ctx, ctx_row, tm, d)
    o_ref[0] = x_ref[0] + gate * mix


def _merge(xa, ya, yb, yc, proj, mod_l, w_pa, w_pb, w_pc, w_out, col_gate, n_ctx, ctx_row):
    bsz, n, d = xa.shape
    tm = 384 if n % 384 == 0 else TOK_BLK
    g0 = col_gate // d
    tile = lambda w: pl.BlockSpec((1, tm, w), lambda b, i: (b, i, 0))
    gate = lambda k: pl.BlockSpec((1, tm, d), lambda b, i: (b, i, g0 + k))
    full = lambda w: pl.BlockSpec(w.shape, lambda b, i: (0, 0))
    return pl.pallas_call(
        functools.partial(_merge_kernel, n_ctx=n_ctx, ctx_row=ctx_row),
        grid=(bsz, n // tm),
        in_specs=[tile(d), tile(ya.shape[2]), tile(yb.shape[2]), tile(yc.shape[2]),
                  gate(0), gate(1), gate(2), full(mod_l),
                  full(w_pa), full(w_pb), full(w_pc), full(w_out)],
        out_specs=tile(d),
        out_shape=jax.ShapeDtypeStruct((bsz, n, d), F32),
        compiler_params=_cparams(2), name="merge",
    )(xa, ya, yb, yc, proj, proj, proj, mod_l, w_pa, w_pb, w_pc, w_out)


def _route_kernel(x_ref, mod_ref, nw_ref, wr_ref, br_ref, h_ref, idx_ref, wt_ref, cnt_ref,
                  meta_ref, carry_ref, *, n_ctx, ctx_row):
    b, i = pl.program_id(0), pl.program_id(1)
    tm, d = x_ref.shape[1:]

    @pl.when((b == 0) & (i == 0))
    def _():
        carry_ref[...] = jnp.zeros_like(carry_ref)

    h = _rms(x_ref[0], nw_ref[...])
    shift = _mod_rows(mod_ref, 3, b, i * tm, n_ctx, ctx_row, tm, d)
    scale = _mod_rows(mod_ref, 4, b, i * tm, n_ctx, ctx_row, tm, d)
    h = h * (1.0 + scale) + shift
    h_ref[0] = h

    logits = _dot3_nt(wr_ref[...], h)
    ex = jnp.exp(logits - jnp.max(logits, axis=0, keepdims=True))
    probs = ex / jnp.sum(ex, axis=0, keepdims=True)
    sel = probs + br_ref[...]

    def row(x, r):
        return x[r:r + 1, :]

    best = None
    g_idx = None
    for g in range(N_GROUPS):
        r0 = g * EXPERTS_PER_GROUP
        a0, a1, a2, a3 = (row(sel, r0 + j) for j in range(EXPERTS_PER_GROUP))
        hi1, lo1 = jnp.maximum(a0, a1), jnp.minimum(a0, a1)
        hi2, lo2 = jnp.maximum(a2, a3), jnp.minimum(a2, a3)
        score = jnp.maximum(hi1, hi2) + jnp.maximum(jnp.minimum(hi1, hi2), jnp.maximum(lo1, lo2))
        if g == 0:
            best, g_idx = score, jnp.zeros_like(score, dtype=I32)
        else:
            better = score > best
            best = jnp.where(better, score, best)
            g_idx = jnp.where(better, g, g_idx)

    def pick(x, j):
        out = row(x, j)
        for g in range(1, N_GROUPS):
            out = jnp.where(g_idx == g, row(x, g * EXPERTS_PER_GROUP + j), out)
        return out

    in_grp = [pick(sel, j) for j in range(EXPERTS_PER_GROUP)]
    in_prob = [pick(probs, j) for j in range(EXPERTS_PER_GROUP)]

    def first_argmax(vals, skip):
        bv, bi, bp = None, None, None
        for j in range(EXPERTS_PER_GROUP):
            v = vals[j] if skip is None else jnp.where(skip == j, -jnp.inf, vals[j])
            if bv is None:
                bv, bi, bp = v, jnp.zeros_like(g_idx), in_prob[0]
            else:
                better = v > bv
                bv = jnp.where(better, v, bv)
                bi = jnp.where(better, j, bi)
                bp = jnp.where(better, in_prob[j], bp)
        return bi, bp

    loc0, p0 = first_argmax(in_grp, None)
    loc1, p1 = first_argmax(in_grp, loc0)
    e0 = g_idx * EXPERTS_PER_GROUP + loc0
    e1 = g_idx * EXPERTS_PER_GROUP + loc1
    wsum = p0 + p1
    w0, w1 = p0 / wsum, p1 / wsum

    er = _iota((N_EXPERTS, tm), 0)
    hit0 = er == e0
    hit1 = er == e1
    hot = jnp.where(hit0 | hit1, 1.0, 0.0)
    upper = jnp.where(_iota((tm, tm), 0) < _iota((tm, tm), 1), 1.0, 0.0)
    before = _dot(hot, upper)
    cnt = jnp.sum(hot, axis=1, keepdims=True)
    e_col = _iota((N_EXPERTS, 1), 0)

    def slot_starts(align):
        padded = jnp.floor((cnt + (align - 1)) * (1.0 / align)) * align
        off = jnp.zeros((N_EXPERTS, 1), F32)
        for e in range(N_EXPERTS - 1):
            off = off + jnp.where(e_col > e, padded[e:e + 1, :], 0.0)
        return padded, off

    cnt_pad, off_d = slot_starts(SUBLANES)
    _, off_c = slot_starts(MOE_CHUNK)
    positions = []
    for off in (off_d, off_c):
        for hit in (hit0, hit1):
            positions.append(jnp.sum(jnp.where(hit, before + off, 0.0), axis=0, keepdims=True))

    lane = _iota((N_EXPERTS, LANES), 1)
    carry = carry_ref[...]
    meta = jnp.zeros((N_EXPERTS, LANES), F32)
    for field, val in enumerate((carry, cnt, off_d, off_c)):
        meta = jnp.where(lane == field, val, meta)
    meta_ref[0] = meta.astype(I32)
    carry_ref[...] = carry + cnt_pad
    cnt_ref[...] = carry + cnt_pad

    idx_ref[...] = jnp.zeros_like(idx_ref)
    wt_ref[...] = jnp.zeros_like(wt_ref)
    for r, val in enumerate(positions):
        idx_ref[r:r + 1, :] = val.astype(I32)
    for r, val in enumerate((w0, w1)):
        wt_ref[r:r + 1, :] = val


def _route(x1, mod_l, norm_w, w_router_t, b_router, n_ctx, ctx_row):
    bsz, n, d = x1.shape
    tm = TOK_BLK
    nt = n // tm
    tile = pl.BlockSpec((1, tm, d), lambda b, i: (b, i, 0))
    lane_tile = pl.BlockSpec((SUBLANES, tm), lambda b, i: (0, b * nt + i))
    full = lambda a: pl.BlockSpec(a.shape, lambda b, i: (0,) * a.ndim)
    br = b_router.reshape(N_EXPERTS, 1)
    nw = norm_w.reshape(1, d)
    return pl.pallas_call(
        functools.partial(_route_kernel, n_ctx=n_ctx, ctx_row=ctx_row),
        grid=(bsz, nt),
        in_specs=[tile, full(mod_l), full(nw), full(w_router_t), full(br)],
        out_specs=[tile, lane_tile, lane_tile,
                   pl.BlockSpec((N_EXPERTS, LANES), lambda b, i: (0, 0)),
                   pl.BlockSpec((1, N_EXPERTS, LANES), lambda b, i: (b * nt + i, 0, 0))],
        out_shape=[jax.ShapeDtypeStruct((bsz, n, d), F32),
                   jax.ShapeDtypeStruct((SUBLANES, bsz * n), I32),
                   jax.ShapeDtypeStruct((SUBLANES, bsz * n), F32),
                   jax.ShapeDtypeStruct((N_EXPERTS, LANES), F32),
                   jax.ShapeDtypeStruct((bsz * nt, N_EXPERTS, LANES), I32)],
        scratch_shapes=[pltpu.VMEM((N_EXPERTS, LANES), F32)],
        compiler_params=_cparams(2), name="route",
    )(x1, mod_l, nw, w_router_t, br)


def _round_up(x, m):
    return -(-x // m) * m


MOE_MAX_CHUNKS = TOK_BLK // MOE_CHUNK
MOE_SORT_ROWS = _round_up(2 * TOK_BLK + N_EXPERTS * (SUBLANES - 1) + MOE_CHUNK - 1, LANES)
MOE_GATHER_ROWS = _round_up(2 * TOK_BLK + N_EXPERTS * (MOE_CHUNK - 1), LANES)
META_FIELDS = 4


def _moe_blocks(n_assign, n_tiles):
    slack = N_EXPERTS * (n_tiles * (SUBLANES - 1) + MOE_CHUNK - 1)
    return -(-(n_assign + slack) // MOE_BLK) + N_EXPERTS


def _chunk_copies(meta_ref, start_ref, tile, slot_field, do, make_copy):
    def expert(e, carry):
        base = (tile * N_EXPERTS + e) * META_FIELDS
        before, rows, slot = meta_ref[base], meta_ref[base + 1], meta_ref[base + slot_field]
        for c in range(MOE_MAX_CHUNKS):
            @pl.when(rows > c * MOE_CHUNK)
            def _():
                do(make_copy(pl.multiple_of(slot + c * MOE_CHUNK, SUBLANES),
                             pl.multiple_of(start_ref[e] + before + c * MOE_CHUNK, SUBLANES)))
        return carry
    lax.fori_loop(0, N_EXPERTS, expert, 0)


def _dispatch_kernel(meta_ref, cnt_ref, pos_ref, h_ref, xs_in_ref, xs_ref, blk_e_ref, blk_on_ref,
                     start_ref, sort_ref, sem, *, nblk):
    del xs_in_ref
    t = pl.program_id(0)
    nt = pl.num_programs(0)
    tm = h_ref.shape[0]

    @pl.when(t == 0)
    def _():
        def expert(e, end):
            start_ref[e] = end
            return end + ((cnt_ref[e] + MOE_CHUNK - 1 + MOE_BLK - 1) // MOE_BLK) * MOE_BLK
        start_ref[N_EXPERTS] = lax.fori_loop(0, N_EXPERTS, expert, 0)

        def block(j, carry):
            def count(e, acc):
                return acc + jnp.where(start_ref[e + 1] <= j * MOE_BLK, 1, 0)
            e = jnp.minimum(lax.fori_loop(0, N_EXPERTS, count, 0), N_EXPERTS - 1)
            blk_e_ref[j] = e
            blk_on_ref[j] = jnp.where(j * MOE_BLK - start_ref[e] < cnt_ref[e], 1, 0)
            return carry
        lax.fori_loop(0, nblk, block, 0)

    r = _iota((MOE_SORT_ROWS, tm), 0)
    onehot = jnp.where((r == pos_ref[0:1, :]) | (r == pos_ref[1:2, :]), 1.0, 0.0)
    buf = t % 2
    sort_ref[buf] = _dot(onehot, h_ref[...])

    def copies(which):
        def make_copy(src_row, dst_row):
            return pltpu.make_async_copy(sort_ref.at[which, pl.ds(src_row, MOE_CHUNK), :],
                                         xs_ref.at[pl.ds(dst_row, MOE_CHUNK), :], sem.at[which])
        return make_copy

    @pl.when(t > 0)
    def _():
        _chunk_copies(meta_ref, start_ref, t - 1, 2, lambda cp: cp.wait(), copies(1 - buf))

    _chunk_copies(meta_ref, start_ref, t, 2, lambda cp: cp.start(), copies(buf))

    @pl.when(t == nt - 1)
    def _():
        _chunk_copies(meta_ref, start_ref, t, 2, lambda cp: cp.wait(), copies(buf))


def _dispatch(meta, counts, pos_rows, h_flat, nblk):
    ntok, d = h_flat.shape
    tm = TOK_BLK
    xs0 = jnp.zeros((nblk * MOE_BLK, d), F32)
    whole_smem = pl.BlockSpec(memory_space=pltpu.SMEM)
    return pl.pallas_call(
        functools.partial(_dispatch_kernel, nblk=nblk),
        grid=(ntok // tm,),
        in_specs=[whole_smem, whole_smem,
                  pl.BlockSpec((SUBLANES, tm), lambda i: (0, i)),
                  pl.BlockSpec((tm, d), lambda i: (i, 0)),
                  pl.BlockSpec(memory_space=pl.ANY)],
        out_specs=[pl.BlockSpec(memory_space=pl.ANY), whole_smem, whole_smem, whole_smem],
        out_shape=[jax.ShapeDtypeStruct((nblk * MOE_BLK, d), F32),
                   jax.ShapeDtypeStruct((nblk,), I32),
                   jax.ShapeDtypeStruct((nblk,), I32),
                   jax.ShapeDtypeStruct((N_EXPERTS + 1,), I32)],
        scratch_shapes=[pltpu.VMEM((2, MOE_SORT_ROWS, d), F32), pltpu.SemaphoreType.DMA((2,))],
        input_output_aliases={4: 0},
        compiler_params=_cparams(1), name="dispatch",
    )(meta, counts, pos_rows, h_flat, xs0)


def _expert_kernel(blk_e_ref, blk_on_ref, x_ref, wg_ref, wu_ref, wd_ref, o_ref, g_bf, u_bf, d_bf):
    i = pl.program_id(0)

    @pl.when((i == 0) | (blk_e_ref[i] != blk_e_ref[jnp.maximum(i - 1, 0)]))
    def _():
        g_bf[...] = wg_ref[0].astype(BF16)
        u_bf[...] = wu_ref[0].astype(BF16)
        d_bf[...] = wd_ref[0].astype(BF16)

    @pl.when(blk_on_ref[i] == 1)
    def _():
        x = x_ref[...].astype(BF16)
        hid = _silu(_dot(x, g_bf[...])) * _dot(x, u_bf[...])
        o_ref[...] = _dot(hid, d_bf[...])

    @pl.when(blk_on_ref[i] != 1)
    def _():
        o_ref[...] = jnp.zeros_like(o_ref)


def _experts(blk_e, blk_on, xs, w_gate, w_up, w_down):
    d = xs.shape[1]
    ff = w_gate.shape[2]
    nblk = blk_e.shape[0]
    return pl.pallas_call(
        _expert_kernel,
        grid_spec=pltpu.PrefetchScalarGridSpec(
            num_scalar_prefetch=2, grid=(nblk,),
            in_specs=[pl.BlockSpec((MOE_BLK, d), lambda i, be, on: (i, 0)),
                      pl.BlockSpec((1, d, ff), lambda i, be, on: (be[i], 0, 0)),
                      pl.BlockSpec((1, d, ff), lambda i, be, on: (be[i], 0, 0)),
                      pl.BlockSpec((1, ff, d), lambda i, be, on: (be[i], 0, 0))],
            out_specs=pl.BlockSpec((MOE_BLK, d), lambda i, be, on: (i, 0)),
            scratch_shapes=[pltpu.VMEM((d, ff), BF16), pltpu.VMEM((d, ff), BF16),
                            pltpu.VMEM((ff, d), BF16)]),
        out_shape=jax.ShapeDtypeStruct(xs.shape, F32),
        compiler_params=_cparams(1), name="experts",
    )(blk_e, blk_on, xs, w_gate, w_up, w_down)


def _combine_kernel(meta_ref, start_ref, x_ref, pos_ref, wt_ref, mod_ref, ys_ref, o_ref,
                    gath_ref, sem, *, n_ctx, ctx_row):
    b, i = pl.program_id(0), pl.program_id(1)
    nt = pl.num_programs(1)
    t = b * nt + i
    last = pl.num_programs(0) * nt - 1
    tm, d = x_ref.shape[1:]
    buf = t % 2

    def copies(which):
        def make_copy(slot_row, ys_row):
            return pltpu.make_async_copy(ys_ref.at[pl.ds(ys_row, MOE_CHUNK), :],
                                         gath_ref.at[which, pl.ds(slot_row, MOE_CHUNK), :],
                                         sem.at[which])
        return make_copy

    @pl.when(t == 0)
    def _():
        gath_ref[...] = jnp.zeros_like(gath_ref)
        _chunk_copies(meta_ref, start_ref, t, 3, lambda cp: cp.start(), copies(buf))

    @pl.when(t < last)
    def _():
        _chunk_copies(meta_ref, start_ref, t + 1, 3, lambda cp: cp.start(), copies(1 - buf))

    _chunk_copies(meta_ref, start_ref, t, 3, lambda cp: cp.wait(), copies(buf))

    col = _iota((tm, MOE_GATHER_ROWS), 1)
    pos = pos_ref[...]
    wt = wt_ref[...]
    sel = (jnp.where(col == pos[:, 0:1], wt[:, 0:1], 0.0)
           + jnp.where(col == pos[:, 1:2], wt[:, 1:2], 0.0))
    hi, lo = _hi_lo(sel)
    rows = gath_ref[buf].astype(BF16)
    y = (jnp.dot(hi, rows, preferred_element_type=F32)
         + jnp.dot(lo, rows, preferred_element_type=F32))
    gate = _mod_rows(mod_ref, 5, b, i * tm, n_ctx, ctx_row, tm, d)
    o_ref[0] = x_ref[0] + gate * y


def _combine(meta, start, x1, pos_cols, wt_cols, mod_l, ys, n_ctx, ctx_row):
    bsz, n, d = x1.shape
    tm = TOK_BLK
    nt = n // tm
    whole_smem = pl.BlockSpec(memory_space=pltpu.SMEM)
    cols = pl.BlockSpec((tm, 2), lambda b, i: (b * nt + i, 0))
    return pl.pallas_call(
        functools.partial(_combine_kernel, n_ctx=n_ctx, ctx_row=ctx_row),
        grid=(bsz, nt),
        in_specs=[whole_smem, whole_smem,
                  pl.BlockSpec((1, tm, d), lambda b, i: (b, i, 0)), cols, cols,
                  pl.BlockSpec(mod_l.shape, lambda b, i: (0, 0)),
                  pl.BlockSpec(memory_space=pl.ANY)],
        out_specs=pl.BlockSpec((1, tm, d), lambda b, i: (b, i, 0)),
        out_shape=jax.ShapeDtypeStruct((bsz, n, d), F32),
        scratch_shapes=[pltpu.VMEM((2, MOE_GATHER_ROWS, d), F32), pltpu.SemaphoreType.DMA((2,))],
        compiler_params=_cparams(2), name="combine",
    )(meta, start, x1, pos_cols, wt_cols, mod_l, ys)


def _layer(xa, mod_l, layer, n_ctx, ctx_row, p, rope):
    bsz, n, d = xa.shape
    na_w = NA_HEADS * HEAD_DIM
    hg_w = HG_HEADS * HG_DK
    wa_qw = WA_HEADS * HEAD_DIM
    wa_kvw = WA_KV_HEADS * HEAD_DIM
    hg0 = 3 * na_w
    wa0 = hg0 + 5 * hg_w
    gate0 = wa0 + wa_qw + 2 * wa_kvw
    n_cols = gate0 + N_BRANCHES * d
    assert p["w_in"].shape[1] == n_cols
    w_att = _take_cols(p["w_in"], [(gate0, n_cols), (0, hg0), (wa0, gate0)])
    w_hg = _take_cols(p["w_in"], [(hg0, wa0)])
    col_gate = 0
    col_na = [N_BRANCHES * d + k * na_w for k in range(3)]
    col_wq = N_BRANCHES * d + 3 * na_w
    col_wk = col_wq + wa_qw
    col_wv = col_wk + wa_kvw
    col_hg = [k * hg_w for k in range(5)]
    proj = _inproj(xa, mod_l, p["norm1"], w_att, n_ctx, ctx_row, 768, BF16)
    proj_hg = _inproj(xa, mod_l, p["norm1"], w_hg, n_ctx, ctx_row, 1280, F32)

    ya = _na_attention(proj, _na_bias(p["na_rpb"]), p["na_q_norm"], p["na_k_norm"], *col_na)
    yc = _wa_attention(proj, p["wa_sink"], p["wa_q_norm"], p["wa_k_norm"], rope[0], rope[1],
                       col_wq, col_wk, col_wv)
    o_f = _hgrn_pass(proj_hg, p["hg_lower"], layer, False, col_hg[0], col_hg[1], col_hg[3])
    yb = _hgrn_pass(proj_hg, p["hg_lower"], layer, True, col_hg[0], col_hg[2], col_hg[3],
                    final_args=(col_hg[4], o_f, p["hg_norm"]))

    bf = lambda w: w.astype(BF16)
    x1 = _merge(xa, ya, yb, yc, proj, mod_l, bf(p["w_pa"]), bf(p["w_pb"]), bf(p["w_pc"]),
                bf(p["w_out"]), col_gate, n_ctx, ctx_row)

    h2, idx, wts, counts, meta = _route(x1, mod_l, p["norm2"], p["w_router"].T, p["b_router"],
                                        n_ctx, ctx_row)
    ntok = bsz * n
    meta_flat = meta[:, :, :META_FIELDS].reshape(-1)
    xs, blk_e, blk_on, start = _dispatch(meta_flat, counts[:, 0].astype(I32), idx,
                                         h2.reshape(ntok, d),
                                         _moe_blocks(2 * ntok, ntok // TOK_BLK))
    ys = _experts(blk_e, blk_on, xs, p["w_gate"], p["w_up"], p["w_down"])
    return _combine(meta_flat, start, x1, idx[2:4].T, wts[:2].T, mod_l, ys, n_ctx, ctx_row)


def kernel(x, c, ctx, c_ctx, w_ada, b_ada, norm1, norm2, w_in, na_q_norm, na_k_norm, na_rpb, hg_lower,
           hg_norm, wa_q_norm, wa_k_norm, wa_sink, w_pa, w_pb, w_pc, w_out, w_router, b_router,
           w_gate, w_up, w_down):
    bsz, t, d = x.shape
    n_ctx = ctx.shape[1]
    depth = w_ada.shape[0]
    assert n_ctx == TOK_BLK and t % TOK_BLK == 0 and t // TOK_BLK >= 3
    assert t % GRID_W == 0 and bsz + 1 <= SUBLANES

    xa = jnp.concatenate([ctx, x], axis=1)
    cond = jnp.concatenate([c, c_ctx[None], jnp.zeros((SUBLANES - bsz - 1, d), F32)], axis=0)
    mod = _ada(cond, w_ada, b_ada)
    rope = _rope_tables(n_ctx, t)
    for l in range(depth):
        p = dict(norm1=norm1[l], norm2=norm2[l], w_in=w_in[l], na_q_norm=na_q_norm[l],
                 na_k_norm=na_k_norm[l], na_rpb=na_rpb[l], hg_lower=hg_lower, hg_norm=hg_norm[l],
                 wa_q_norm=wa_q_norm[l], wa_k_norm=wa_k_norm[l], wa_sink=wa_sink[l],
                 w_pa=w_pa[l], w_pb=w_pb[l], w_pc=w_pc[l], w_out=w_out[l],
                 w_router=w_router, b_router=b_router, w_gate=w_gate[l], w_up=w_up[l],
                 w_down=w_down[l])
        xa = _layer(xa, mod[l], l, n_ctx, bsz, p, rope)
    return xa[:, n_ctx:]
```

```python
import functools

import numpy as np
import jax
import jax.numpy as jnp
from jax import lax
from jax.experimental import pallas as pl
from jax.experimental.pallas import tpu as pltpu

F32 = jnp.float32
BF16 = jnp.bfloat16
I32 = jnp.int32

GRID_W = 64
NA_HEADS = 8
NA_ROWS = 8
NA_COLS = 16
HEAD_DIM = 64
HG_HEADS = 4
HG_DK = 128
LOG_FLOOR = 1e-30
WA_HEADS = 8
WA_KV_HEADS = 2
WA_WINDOW = 128
ROPE_THETA = 10000.0
N_EXPERTS = 16
N_GROUPS = 4
EXPERTS_PER_GROUP = N_EXPERTS // N_GROUPS
N_BRANCHES = 3
NEG_INF = -1e30
RMS_EPS = 1e-6

LANES = 128
SUBLANES = 8
VMEM_LIMIT = 56 * 1024 * 1024

TOK_BLK = 256
MOE_BLK = 512
MOE_CHUNK = 32
MOE_ZERO_ROWS = 256


def _cparams(n_axes):
    return pltpu.CompilerParams(
        dimension_semantics=("arbitrary",) * n_axes, vmem_limit_bytes=VMEM_LIMIT)


def _sigmoid(x):
    return 1.0 / (1.0 + jnp.exp(-x))


def _silu(x):
    return x * _sigmoid(x)


def _dot(a, b):
    return jnp.dot(a.astype(BF16), b.astype(BF16), preferred_element_type=F32)


def _dot_nt(a, b):
    return lax.dot_general(a.astype(BF16), b.astype(BF16), (((1,), (1,)), ((), ())),
                           preferred_element_type=F32)


def _dot_tn(a, b):
    return lax.dot_general(a.astype(BF16), b.astype(BF16), (((0,), (0,)), ((), ())),
                           preferred_element_type=F32)


def _hi_lo(a):
    hi = a.astype(BF16)
    lo = (a - hi.astype(F32)).astype(BF16)
    return hi, lo


def _dot3(a, b):
    ah, al = _hi_lo(a)
    bh, bl = _hi_lo(b)
    d = functools.partial(jnp.dot, preferred_element_type=F32)
    return d(ah, bh) + (d(ah, bl) + d(al, bh))


def _dot3_nt(a, b):
    ah, al = _hi_lo(a)
    bh, bl = _hi_lo(b)
    d = functools.partial(lax.dot_general, dimension_numbers=(((1,), (1,)), ((), ())),
                          preferred_element_type=F32)
    return d(ah, bh) + (d(ah, bl) + d(al, bh))


def _rms(x, w):
    return x * lax.rsqrt(jnp.mean(x * x, axis=-1, keepdims=True) + RMS_EPS) * w


def _iota(shape, dim):
    return lax.broadcasted_iota(I32, shape, dim)


def _ada_kernel(cond_ref, w_ref, b_ref, o_ref):
    o_ref[0] = _dot3(_silu(cond_ref[...]), w_ref[0]) + b_ref[0]


def _ada(cond, w_ada, b_ada):
    depth, d, d6 = w_ada.shape
    rows = cond.shape[0]
    tn = d6 // 4
    return pl.pallas_call(
        _ada_kernel,
        grid=(depth, d6 // tn),
        in_specs=[pl.BlockSpec((rows, d), lambda l, j: (0, 0)),
                  pl.BlockSpec((1, d, tn), lambda l, j: (l, 0, j)),
                  pl.BlockSpec((1, 1, tn), lambda l, j: (l, 0, j))],
        out_specs=pl.BlockSpec((1, rows, tn), lambda l, j: (l, 0, j)),
        out_shape=jax.ShapeDtypeStruct((depth, rows, d6), F32),
        compiler_params=_cparams(2), name="ada",
    )(cond, w_ada, b_ada.reshape(depth, 1, d6))


def _mod_rows(mod_ref, which, b, row0, n_ctx, ctx_row, tm, d):
    lat = mod_ref[pl.ds(b, 1), which * d:(which + 1) * d]
    cx = mod_ref[ctx_row:ctx_row + 1, which * d:(which + 1) * d]
    row = row0 + _iota((tm, d), 0)
    return jnp.where(row < n_ctx, cx, lat)


def _inproj_kernel(x_ref, mod_ref, nw_ref, w_ref, o_ref, h_ref, *, n_ctx, ctx_row):
    b, i, j = pl.program_id(0), pl.program_id(1), pl.program_id(2)
    tm, d = h_ref.shape[1:]

    @pl.when(j == 0)
    def _():
        h = _rms(x_ref[0], nw_ref[...])
        shift = _mod_rows(mod_ref, 0, b, i * tm, n_ctx, ctx_row, tm, d)
        scale = _mod_rows(mod_ref, 1, b, i * tm, n_ctx, ctx_row, tm, d)
        h_ref[0] = (h * (1.0 + scale) + shift).astype(BF16)

    o_ref[0] = jnp.dot(h_ref[0], w_ref[...], preferred_element_type=F32).astype(o_ref.dtype)


def _matmul_kernel(h_ref, w_ref, o_ref):
    o_ref[0] = jnp.dot(h_ref[0], w_ref[...], preferred_element_type=F32).astype(o_ref.dtype)


def _project(h, w, tn, out_dtype):
    bsz, n, d = h.shape
    cols = w.shape[1]
    tm = 1408 if n % 1408 == 0 else TOK_BLK
    return pl.pallas_call(
        _matmul_kernel,
        grid=(bsz, n // tm, cols // tn),
        in_specs=[pl.BlockSpec((1, tm, d), lambda b, i, j: (b, i, 0)),
                  pl.BlockSpec((d, tn), lambda b, i, j: (0, j))],
        out_specs=pl.BlockSpec((1, tm, tn), lambda b, i, j: (b, i, j)),
        out_shape=jax.ShapeDtypeStruct((bsz, n, cols), out_dtype),
        compiler_params=_cparams(3), name="project",
    )(h, w)


W_IN_COL_BLK = 256


def _take_cols_kernel(src_ref, w_ref, o_ref):
    del src_ref
    o_ref[...] = w_ref[0].astype(BF16)


def _take_cols(w_in, layer, col_ranges):
    d = w_in.shape[1]
    blk = W_IN_COL_BLK
    src = []
    for start, stop in col_ranges:
        assert start % blk == 0 and stop % blk == 0
        src += list(range(start // blk, stop // blk))
    return pl.pallas_call(
        _take_cols_kernel,
        grid_spec=pltpu.PrefetchScalarGridSpec(
            num_scalar_prefetch=1, grid=(len(src),),
            in_specs=[pl.BlockSpec((1, d, blk), lambda j, src: (layer, 0, src[j]))],
            out_specs=pl.BlockSpec((d, blk), lambda j, src: (0, j))),
        out_shape=jax.ShapeDtypeStruct((d, len(src) * blk), BF16),
        compiler_params=_cparams(1), name="take_cols",
    )(jnp.asarray(src, I32), w_in)


def _inproj(xa, mod_l, norm_w, w, n_ctx, ctx_row, tn, out_dtype):
    bsz, n, d = xa.shape
    cols = w.shape[1]
    tm = 1408 if n % 1408 == 0 else TOK_BLK
    return pl.pallas_call(
        functools.partial(_inproj_kernel, n_ctx=n_ctx, ctx_row=ctx_row),
        grid=(bsz, n // tm, cols // tn),
        in_specs=[pl.BlockSpec((1, tm, d), lambda b, i, j: (b, i, 0)),
                  pl.BlockSpec(mod_l.shape, lambda b, i, j: (0, 0)),
                  pl.BlockSpec((1, d), lambda b, i, j: (0, 0)),
                  pl.BlockSpec((d, tn), lambda b, i, j: (0, j))],
        out_specs=[pl.BlockSpec((1, tm, tn), lambda b, i, j: (b, i, j)),
                   pl.BlockSpec((1, tm, d), lambda b, i, j: (b, i, 0))],
        out_shape=[jax.ShapeDtypeStruct((bsz, n, cols), out_dtype),
                   jax.ShapeDtypeStruct((bsz, n, d), BF16)],
        compiler_params=_cparams(3), name="inproj",
    )(xa, mod_l, norm_w.reshape(1, d), w)


def _pair_norm(x, w):
    lane = _iota(x.shape, 1)
    lo = lane < HEAD_DIM
    sq = x * x
    s_lo = jnp.sum(jnp.where(lo, sq, 0.0), axis=-1, keepdims=True)
    s_hi = jnp.sum(jnp.where(lo, 0.0, sq), axis=-1, keepdims=True)
    ms = jnp.where(lo, s_lo, s_hi) * (1.0 / HEAD_DIM)
    return x * lax.rsqrt(ms + RMS_EPS) * w


LOG2E = 1.4426950408889634


def _ones_beside(v, value_lanes):
    return jnp.where(value_lanes, v, 1.0).astype(BF16)


def _softmax_av(s_loc, s_ctx, v_loc, v_ctx, sink):
    m = jnp.maximum(jnp.max(s_loc, axis=-1, keepdims=True), jnp.max(s_ctx, axis=-1, keepdims=True))
    if sink is not None:
        m = jnp.maximum(m, sink)
    p_loc = jnp.exp2(s_loc - m).astype(BF16)
    p_ctx = jnp.exp2(s_ctx - m).astype(BF16)
    acc = (jnp.dot(p_loc, v_loc, preferred_element_type=F32)
           + jnp.dot(p_ctx, v_ctx, preferred_element_type=F32))
    den = pltpu.roll(acc, LANES // 2, 1)
    if sink is not None:
        den = den + jnp.exp2(sink - m)
    return acc / den


NA_QROWS = TOK_BLK // GRID_W
NA_KROWS = 3 * NA_QROWS
N_RPB_COLS = 2 * NA_COLS - 1
N_RPB_ROWS = 2 * NA_ROWS - 1
NA_VARIANTS = 4


def _na_row_valid(variant, a, k):
    if variant == 0:
        return False
    if variant == 1:
        return NA_QROWS <= k < NA_QROWS + NA_ROWS
    if variant == 3:
        return k < NA_ROWS
    return a <= k < a + NA_ROWS


def _na_bias_kernel(rpb_ref, o_ref):
    h = pl.program_id(0)
    qc = _iota((GRID_W, GRID_W), 0)
    kc = _iota((GRID_W, GRID_W), 1)
    col_lo = jnp.clip(qc - NA_COLS // 2, 0, GRID_W - NA_COLS)
    col_ok = (kc >= col_lo) & (kc < col_lo + NA_COLS)
    d_col = kc - qc + (NA_COLS - 1)
    masked = jnp.full((GRID_W, GRID_W), NEG_INF, F32)
    blocks = []
    for dr in range(N_RPB_ROWS):
        acc = jnp.zeros((GRID_W, GRID_W), F32)
        for dc in range(N_RPB_COLS):
            val = rpb_ref[(h * N_RPB_ROWS + dr) * N_RPB_COLS + dc]
            acc = jnp.where(d_col == dc, val, acc)
        blocks.append(jnp.where(col_ok, acc * LOG2E, NEG_INF))
    for variant in range(NA_VARIANTS):
        for a in range(NA_QROWS):
            for k in range(NA_KROWS):
                dr = k - NA_QROWS - a + NA_ROWS - 1
                ok = _na_row_valid(variant, a, k) and 0 <= dr < N_RPB_ROWS
                o_ref[variant, 0, a * GRID_W:(a + 1) * GRID_W, k * GRID_W:(k + 1) * GRID_W] = (
                    blocks[dr] if ok else masked)


def _na_bias(rpb):
    heads = rpb.shape[0]
    return pl.pallas_call(
        _na_bias_kernel,
        grid=(heads,),
        in_specs=[pl.BlockSpec(memory_space=pltpu.SMEM)],
        out_specs=pl.BlockSpec((NA_VARIANTS, 1, TOK_BLK, 3 * TOK_BLK), lambda h: (0, h, 0, 0)),
        out_shape=jax.ShapeDtypeStruct((NA_VARIANTS, heads, TOK_BLK, 3 * TOK_BLK), F32),
        compiler_params=_cparams(1), name="na_bias",
    )(rpb.reshape(-1))


def _na_kernel(q_ref, kp_ref, kc_ref, kn_ref, vp_ref, vc_ref, vn_ref, kx_ref, vx_ref,
               bias_ref, qn_ref, kn_w_ref, o_ref):
    tq = q_ref.shape[1]
    lo = _iota((tq, LANES), 1) < HEAD_DIM
    lo_k = _iota((3 * tq, LANES), 1) < HEAD_DIM
    qn_w = qn_ref[...] * (HEAD_DIM ** -0.5 * LOG2E)
    for p in range(NA_HEADS // 2):
        pair = slice(p * LANES, (p + 1) * LANES)
        f32 = lambda ref: ref[0, :, pair].astype(F32)
        q = _pair_norm(f32(q_ref), qn_w)
        k_loc = _pair_norm(jnp.concatenate([f32(kp_ref), f32(kc_ref), f32(kn_ref)], axis=0),
                           kn_w_ref[...]).astype(BF16)
        k_ctx = _pair_norm(f32(kx_ref), kn_w_ref[...]).astype(BF16)
        v_loc = jnp.concatenate([f32(vp_ref), f32(vc_ref), f32(vn_ref)], axis=0)
        v_ctx = f32(vx_ref)
        outs = []
        for h in range(2):
            mine = lo if h == 0 else ~lo
            mine_k = lo_k if h == 0 else ~lo_k
            qh = jnp.where(mine, q, 0.0).astype(BF16)
            s_loc = _dot_nt(qh, k_loc) + bias_ref[0, 2 * p + h]
            s_ctx = _dot_nt(qh, k_ctx)
            outs.append(_softmax_av(s_loc, s_ctx, _ones_beside(v_loc, mine_k),
                                    _ones_beside(v_ctx, mine), None))
        o_ref[0, :, pair] = jnp.where(lo, outs[0], outs[1])


def _na_attention(proj, bias, qn_w, kn_w, col_q, col_k, col_v):
    bsz, n, _ = proj.shape
    nblk = n // TOK_BLK
    last = nblk - 1
    width = NA_HEADS * HEAD_DIM
    cq, ck, cv = col_q // width, col_k // width, col_v // width

    def blk(col, shift):
        return pl.BlockSpec((1, TOK_BLK, width),
                            lambda b, i: (b, jnp.clip(i + shift, 0, last), col))

    def ctx_blk(col):
        return pl.BlockSpec((1, TOK_BLK, width), lambda b, i: (b, 0, col))

    def variant(b, i):
        return (jnp.where(i == 0, 0, jnp.where(i == 1, 1, jnp.where(i == last, 3, 2))), 0, 0, 0)

    vec = pl.BlockSpec((1, LANES), lambda b, i: (0, 0))
    return pl.pallas_call(
        _na_kernel,
        grid=(bsz, nblk),
        in_specs=[blk(cq, 0), blk(ck, -1), blk(ck, 0), blk(ck, 1), blk(cv, -1), blk(cv, 0), blk(cv, 1),
                  ctx_blk(ck), ctx_blk(cv),
                  pl.BlockSpec((1, NA_HEADS, TOK_BLK, 3 * TOK_BLK), variant), vec, vec],
        out_specs=pl.BlockSpec((1, TOK_BLK, width), lambda b, i: (b, i, 0)),
        out_shape=jax.ShapeDtypeStruct((bsz, n, width), F32),
        compiler_params=_cparams(2), name="na_attn",
    )(proj, proj, proj, proj, proj, proj, proj, proj, proj, bias,
      jnp.tile(qn_w, 2).reshape(1, LANES), jnp.tile(kn_w, 2).reshape(1, LANES))


def _rope(x, cos, sin_signed):
    lane = _iota(x.shape, 1)
    first = (lane & (HEAD_DIM // 2 - 1)) < HEAD_DIM // 4
    quarter = HEAD_DIM // 4
    partner = jnp.where(first, pltpu.roll(x, LANES - quarter, 1), pltpu.roll(x, quarter, 1))
    return x * cos + partner * sin_signed


def _wa_kernel(sink_ref, q_ref, kp_ref, kc_ref, kn_ref, vp_ref, vc_ref, vn_ref, kx_ref, vx_ref,
               cq_ref, sq_ref, cp_ref, sp_ref, cn_ref, sn_ref, qn_ref, kn_w_ref, o_ref, *, last):
    i = pl.program_id(1)
    tq = q_ref.shape[1]
    f32 = lambda ref: ref[0].astype(F32)
    cos_loc = jnp.concatenate([cp_ref[...], cq_ref[...], cn_ref[...]], axis=0)
    sin_loc = jnp.concatenate([sp_ref[...], sq_ref[...], sn_ref[...]], axis=0)
    k_loc = _pair_norm(jnp.concatenate([f32(kp_ref), f32(kc_ref), f32(kn_ref)], axis=0),
                       kn_w_ref[...])
    k_loc = _rope(k_loc, cos_loc, sin_loc)
    k_ctx = _pair_norm(f32(kx_ref), kn_w_ref[...])
    v_loc = jnp.concatenate([f32(vp_ref), f32(vc_ref), f32(vn_ref)], axis=0)
    v_ctx = f32(vx_ref)
    half = LANES // 2
    lo = _iota((tq, LANES), 1) < HEAD_DIM
    lo_k = _iota((3 * tq, LANES), 1) < HEAD_DIM
    operands = {}
    for hh in range(2):
        mine, mine_k = (lo, lo_k) if hh == 0 else (~lo, ~lo_k)
        for kv in range(WA_KV_HEADS):
            mv = (lambda x: x) if kv == hh else (lambda x: pltpu.roll(x, half, 1))
            operands[hh, kv] = (mv(k_loc).astype(BF16), mv(k_ctx).astype(BF16),
                                _ones_beside(mv(v_loc), mine_k), _ones_beside(mv(v_ctx), mine))
    qn_w = qn_ref[...] * (HEAD_DIM ** -0.5 * LOG2E)

    qi = _iota((tq, 3 * tq), 0)
    kj = _iota((tq, 3 * tq), 1)
    rel = kj - tq - qi
    lo_col = jnp.where(i >= 2, 0, tq)
    hi_col = jnp.where(i >= 1, jnp.where(i < last, 3 * tq, 2 * tq), 0)
    ok = (rel >= -WA_WINDOW) & (rel <= WA_WINDOW) & (kj >= lo_col) & (kj < hi_col)
    mask = jnp.where(ok, 0.0, NEG_INF)

    group = WA_HEADS // WA_KV_HEADS
    for p in range(WA_HEADS // 2):
        q = _pair_norm(q_ref[0, :, p * LANES:(p + 1) * LANES].astype(F32), qn_w)
        q = _rope(q, cq_ref[...], sq_ref[...])
        outs = []
        for hh in range(2):
            h = 2 * p + hh
            kl, kx, vl, vx = operands[hh, h // group]
            qh = jnp.where(lo if hh == 0 else ~lo, q, 0.0).astype(BF16)
            s_loc = _dot_nt(qh, kl) + mask
            s_ctx = _dot_nt(qh, kx)
            outs.append(_softmax_av(s_loc, s_ctx, vl, vx, sink_ref[h] * LOG2E))
        o_ref[0, :, p * LANES:(p + 1) * LANES] = jnp.where(lo, outs[0], outs[1])


def _wa_attention(proj, sink, qn_w, kn_w, cos_t, sin_t, col_q, col_k, col_v):
    bsz, n, _ = proj.shape
    nblk = n // TOK_BLK
    last = nblk - 1
    qw = WA_HEADS * HEAD_DIM
    cq, ck, cv = col_q // qw, col_k // LANES, col_v // LANES

    def blk(col, shift):
        return pl.BlockSpec((1, TOK_BLK, LANES),
                            lambda b, i: (b, jnp.clip(i + shift, 0, last), col))

    def tab(shift):
        return pl.BlockSpec((TOK_BLK, LANES), lambda b, i: (jnp.clip(i + shift, 0, last), 0))

    def ctx_blk(col):
        return pl.BlockSpec((1, TOK_BLK, LANES), lambda b, i: (b, 0, col))

    vec = pl.BlockSpec((1, LANES), lambda b, i: (0, 0))
    return pl.pallas_call(
        functools.partial(_wa_kernel, last=last),
        grid=(bsz, nblk),
        in_specs=[pl.BlockSpec(memory_space=pltpu.SMEM),
                  pl.BlockSpec((1, TOK_BLK, qw), lambda b, i: (b, i, cq)),
                  blk(ck, -1), blk(ck, 0), blk(ck, 1), blk(cv, -1), blk(cv, 0), blk(cv, 1),
                  ctx_blk(ck), ctx_blk(cv),
                  tab(0), tab(0), tab(-1), tab(-1), tab(1), tab(1), vec, vec],
        out_specs=pl.BlockSpec((1, TOK_BLK, qw), lambda b, i: (b, i, 0)),
        out_shape=jax.ShapeDtypeStruct((bsz, n, qw), F32),
        compiler_params=_cparams(2), name="wa_attn",
    )(sink, proj, proj, proj, proj, proj, proj, proj, proj, proj,
      cos_t, sin_t, cos_t, sin_t, cos_t, sin_t,
      jnp.tile(qn_w, 2).reshape(1, LANES), jnp.tile(kn_w, 2).reshape(1, LANES))


def _rope_tables(n_ctx, t):
    quarter = HEAD_DIM // 4
    inv_freq = ROPE_THETA ** (-jnp.arange(quarter, dtype=F32) / quarter)
    pos = jnp.arange(t)
    lane = np.arange(LANES)
    in_head = lane % HEAD_DIM
    use_col = in_head >= HEAD_DIM // 2
    second = (in_head % (HEAD_DIM // 2)) >= quarter
    freq = inv_freq[in_head % quarter]
    p = jnp.where(use_col[None, :], (pos % GRID_W)[:, None], (pos // GRID_W)[:, None]).astype(F32)
    ang = p * freq[None, :]
    cos = jnp.cos(ang)
    sin = jnp.where(second[None, :], jnp.sin(ang), -jnp.sin(ang))
    cos = jnp.concatenate([jnp.ones((n_ctx, LANES), F32), cos], axis=0)
    sin = jnp.concatenate([jnp.zeros((n_ctx, LANES), F32), sin], axis=0)
    return cos, sin


HG_CHUNK = 128


def _hgrn_kernel(q_ref, f_ref, v_ref, hl_ref, *rest, layer, reverse, final):
    if final:
        g_ref, prev_ref, nw_ref, o_ref, st_ref = rest
    else:
        o_ref, st_ref = rest
    @pl.when(pl.program_id(1) == 0)
    def _():
        st_ref[...] = jnp.zeros_like(st_ref)

    a = hl_ref[0]
    e = jnp.exp(a - jnp.max(a, axis=0, keepdims=True))
    pr = e / jnp.sum(e, axis=0, keepdims=True)
    lb_all = jnp.zeros((1, a.shape[1]), F32)
    for j in range(1, layer + 1):
        lb_all = lb_all + pr[j:j + 1]

    for h in range(HG_HEADS):
        lanes = slice(h * HG_DK, (h + 1) * HG_DK)
        lb = lb_all[:, lanes]
        qs = _silu(q_ref[0, :, lanes])
        f = lb + (1.0 - lb) * _sigmoid(f_ref[0, :, lanes])
        o, st = _hgrn_block(qs, 1.0 - f, jnp.log(jnp.maximum(f, LOG_FLOOR)), v_ref[0, :, lanes],
                            st_ref[h], reverse)
        st_ref[h] = st
        if final:
            tot = prev_ref[0, :, lanes] + o
            o_ref[0, :, lanes] = _rms(tot, nw_ref[...]) * _silu(g_ref[0, :, lanes])
        else:
            o_ref[0, :, lanes] = o


def _hgrn_block(qs, kk, g, v, st, reverse):
    c = qs.shape[0]
    n_sub = c // HG_CHUNK
    row = _iota((c, HG_DK), 0)
    ri = _iota((c, c), 0)
    ci = _iota((c, c), 1)
    block_xor = ri ^ ci

    tri = jnp.where(((ci >= ri) if reverse else (ci <= ri)) & (block_xor < HG_CHUNK), 1.0, 0.0)
    tri = tri.astype(BF16)
    g_hi = g.astype(BF16)
    rest = g - g_hi.astype(F32)
    g_mid = rest.astype(BF16)
    g_lo = (rest - g_mid.astype(F32)).astype(BF16)
    tdot = functools.partial(jnp.dot, preferred_element_type=F32)
    b = tdot(tri, g_hi) + (tdot(tri, g_mid) + tdot(tri, g_lo))

    def ref_rows(hs):
        blk = 2 * hs
        inner = hs if reverse else hs - 1
        if blk >= SUBLANES:
            x = b.reshape(c // blk, blk, HG_DK)
            return jnp.broadcast_to(x[:, inner:inner + 1, :], x.shape).reshape(c, HG_DK)
        x = b.reshape(c // SUBLANES, SUBLANES, HG_DK)
        sub = _iota(x.shape, 1)
        out = None
        for p in range(SUBLANES // blk):
            r = p * blk + inner
            piece = jnp.broadcast_to(x[:, r:r + 1, :], x.shape)
            out = piece if out is None else jnp.where(sub >= p * blk, piece, out)
        return out.reshape(c, HG_DK)

    att = jnp.zeros((c, c), F32)
    hs = HG_CHUNK // 2
    while hs >= 1:
        blk = 2 * hs
        decay = jnp.exp(-jnp.abs(b - ref_rows(hs)))
        in_block = row & (blk - 1)
        later_half = (in_block < hs) if reverse else (in_block >= hs)
        q_l = jnp.where(later_half, qs * decay, 0.0)
        k_l = jnp.where(later_half, 0.0, kk * decay)
        a_l = _dot_nt(q_l, k_l)
        att = att + (a_l if blk == c else jnp.where(block_xor < blk, a_l, 0.0))
        hs //= 2
    o = _dot(att, v) + jnp.sum(qs * kk, axis=-1, keepdims=True) * v

    q_dec = qs * jnp.exp(b)
    b3 = b.reshape(n_sub, HG_CHUNK, HG_DK)
    end = 0 if reverse else HG_CHUNK - 1
    b_end = jnp.broadcast_to(b3[:, end:end + 1, :], b3.shape).reshape(c, HG_DK)
    k_dec = kk * jnp.exp(b_end - b)
    outs = [None] * n_sub
    for s in (reversed(range(n_sub)) if reverse else range(n_sub)):
        rows = slice(s * HG_CHUNK, (s + 1) * HG_CHUNK)
        outs[s] = o[rows] + _dot_nt(q_dec[rows], st)
        st = st * jnp.exp(b_end[s * HG_CHUNK:s * HG_CHUNK + 1, :]) + _dot_tn(v[rows], k_dec[rows])
    return jnp.concatenate(outs, axis=0), st


def _hgrn_pass(proj, hl, layer, reverse, col_q, col_f, col_v, final_args=None):
    bsz, n, _ = proj.shape
    nchunk = n // TOK_BLK
    width = HG_HEADS * HG_DK
    cq, cf, cv = col_q // width, col_f // width, col_v // width

    def chunk(t):
        return jnp.where(t == 0, 0, nchunk - t) if reverse else t

    def blk(col):
        return pl.BlockSpec((1, TOK_BLK, width), lambda b, t: (b, chunk(t), col))

    in_specs = [blk(cq), blk(cf), blk(cv),
                pl.BlockSpec((1,) + hl.shape[1:], lambda b, t: (1 if reverse else 0, 0, 0))]
    args = [proj, proj, proj, hl]
    final = final_args is not None
    if final:
        col_g, prev, norm_w = final_args
        in_specs += [blk(col_g // width), blk(0), pl.BlockSpec((1, HG_DK), lambda b, t: (0, 0))]
        args += [proj, prev, norm_w.reshape(1, HG_DK)]
    return pl.pallas_call(
        functools.partial(_hgrn_kernel, layer=layer, reverse=reverse, final=final),
        grid=(bsz, nchunk),
        in_specs=in_specs,
        out_specs=blk(0),
        out_shape=jax.ShapeDtypeStruct((bsz, n, width), F32),
        scratch_shapes=[pltpu.VMEM((HG_HEADS, HG_DK, HG_DK), F32)],
        compiler_params=_cparams(2), name="hgrn_bwd" if reverse else "hgrn_fwd",
    )(*args)


def _merge_kernel(x_ref, ya_ref, yb_ref, yc_ref, ga_ref, gb_ref, gc_ref, mod_ref,
                  wa_ref, wb_ref, wc_ref, wo_ref, o_ref, *, n_ctx, ctx_row):
    b, i = pl.program_id(0), pl.program_id(1)
    tm, d = x_ref.shape[1:]
    gate_of = lambda ref: _sigmoid(ref[0].astype(F32))
    merged = (gate_of(ga_ref) * _dot(ya_ref[0], wa_ref[...])
              + gate_of(gb_ref) * _dot(yb_ref[0], wb_ref[...])
              + gate_of(gc_ref) * _dot(yc_ref[0], wc_ref[...]))
    mix = _dot(merged, wo_ref[...])
    gate = _mod_rows(mod_ref, 2, b, i * tm, n_ctx, ctx_row, tm, d)
    o_ref[0] = x_ref[0] + gate * mix


def _merge(xa, ya, yb, yc, proj, mod_l, w_pa, w_pb, w_pc, w_out, col_gate, n_ctx, ctx_row):
    bsz, n, d = xa.shape
    tm = 384 if n % 384 == 0 else TOK_BLK
    g0 = col_gate // d
    tile = lambda w: pl.BlockSpec((1, tm, w), lambda b, i: (b, i, 0))
    gate = lambda k: pl.BlockSpec((1, tm, d), lambda b, i: (b, i, g0 + k))
    full = lambda w: pl.BlockSpec(w.shape, lambda b, i: (0, 0))
    return pl.pallas_call(
        functools.partial(_merge_kernel, n_ctx=n_ctx, ctx_row=ctx_row),
        grid=(bsz, n // tm),
        in_specs=[tile(d), tile(ya.shape[2]), tile(yb.shape[2]), tile(yc.shape[2]),
                  gate(0), gate(1), gate(2), full(mod_l),
                  full(w_pa), full(w_pb), full(w_pc), full(w_out)],
        out_specs=tile(d),
        out_shape=jax.ShapeDtypeStruct((bsz, n, d), F32),
        compiler_params=_cparams(2), name="merge",
    )(xa, ya, yb, yc, proj, proj, proj, mod_l, w_pa, w_pb, w_pc, w_out)


def _route_kernel(x_ref, mod_ref, nw_ref, wr_ref, br_ref, h_ref, idx_ref, wt_ref, cnt_ref,
                  meta_ref, carry_ref, *, n_ctx, ctx_row):
    b, i = pl.program_id(0), pl.program_id(1)
    tm, d = x_ref.shape[1:]

    @pl.when((b == 0) & (i == 0))
    def _():
        carry_ref[...] = jnp.zeros_like(carry_ref)

    h = _rms(x_ref[0], nw_ref[...])
    shift = _mod_rows(mod_ref, 3, b, i * tm, n_ctx, ctx_row, tm, d)
    scale = _mod_rows(mod_ref, 4, b, i * tm, n_ctx, ctx_row, tm, d)
    h = h * (1.0 + scale) + shift
    h_ref[0] = h

    logits = _dot3_nt(wr_ref[...], h)
    ex = jnp.exp(logits - jnp.max(logits, axis=0, keepdims=True))
    probs = ex / jnp.sum(ex, axis=0, keepdims=True)
    sel = probs + br_ref[...]

    def row(x, r):
        return x[r:r + 1, :]

    best = None
    g_idx = None
    for g in range(N_GROUPS):
        r0 = g * EXPERTS_PER_GROUP
        a0, a1, a2, a3 = (row(sel, r0 + j) for j in range(EXPERTS_PER_GROUP))
        hi1, lo1 = jnp.maximum(a0, a1), jnp.minimum(a0, a1)
        hi2, lo2 = jnp.maximum(a2, a3), jnp.minimum(a2, a3)
        score = jnp.maximum(hi1, hi2) + jnp.maximum(jnp.minimum(hi1, hi2), jnp.maximum(lo1, lo2))
        if g == 0:
            best, g_idx = score, jnp.zeros_like(score, dtype=I32)
        else:
            better = score > best
            best = jnp.where(better, score, best)
            g_idx = jnp.where(better, g, g_idx)

    def pick(x, j):
        out = row(x, j)
        for g in range(1, N_GROUPS):
            out = jnp.where(g_idx == g, row(x, g * EXPERTS_PER_GROUP + j), out)
        return out

    in_grp = [pick(sel, j) for j in range(EXPERTS_PER_GROUP)]
    in_prob = [pick(probs, j) for j in range(EXPERTS_PER_GROUP)]

    def first_argmax(vals, skip):
        bv, bi, bp = None, None, None
        for j in range(EXPERTS_PER_GROUP):
            v = vals[j] if skip is None else jnp.where(skip == j, -jnp.inf, vals[j])
            if bv is None:
                bv, bi, bp = v, jnp.zeros_like(g_idx), in_prob[0]
            else:
                better = v > bv
                bv = jnp.where(better, v, bv)
                bi = jnp.where(better, j, bi)
                bp = jnp.where(better, in_prob[j], bp)
        return bi, bp

    loc0, p0 = first_argmax(in_grp, None)
    loc1, p1 = first_argmax(in_grp, loc0)
    e0 = g_idx * EXPERTS_PER_GROUP + loc0
    e1 = g_idx * EXPERTS_PER_GROUP + loc1
    wsum = p0 + p1
    w0, w1 = p0 / wsum, p1 / wsum

    er = _iota((N_EXPERTS, tm), 0)
    hit0 = er == e0
    hit1 = er == e1
    hot = jnp.where(hit0 | hit1, 1.0, 0.0)
    upper = jnp.where(_iota((tm, tm), 0) < _iota((tm, tm), 1), 1.0, 0.0)
    before = _dot(hot, upper)
    cnt = jnp.sum(hot, axis=1, keepdims=True)
    e_col = _iota((N_EXPERTS, 1), 0)

    def slot_starts(align):
        padded = jnp.floor((cnt + (align - 1)) * (1.0 / align)) * align
        off = jnp.zeros((N_EXPERTS, 1), F32)
        for e in range(N_EXPERTS - 1):
            off = off + jnp.where(e_col > e, padded[e:e + 1, :], 0.0)
        return padded, off

    cnt_pad, off_d = slot_starts(SUBLANES)
    _, off_c = slot_starts(MOE_CHUNK)
    positions = []
    for off in (off_d, off_c):
        for hit in (hit0, hit1):
            positions.append(jnp.sum(jnp.where(hit, before + off, 0.0), axis=0, keepdims=True))

    lane = _iota((N_EXPERTS, LANES), 1)
    carry = carry_ref[...]
    meta = jnp.zeros((N_EXPERTS, LANES), F32)
    for field, val in enumerate((carry, cnt, off_d, off_c)):
        meta = jnp.where(lane == field, val, meta)
    meta_ref[0] = meta.astype(I32)
    carry_ref[...] = carry + cnt_pad
    cnt_ref[...] = carry + cnt_pad

    idx_ref[...] = jnp.zeros_like(idx_ref)
    wt_ref[...] = jnp.zeros_like(wt_ref)
    for r, val in enumerate(positions):
        idx_ref[r:r + 1, :] = val.astype(I32)
    for r, val in enumerate((w0, w1)):
        wt_ref[r:r + 1, :] = val


def _route(x1, mod_l, norm_w, w_router_t, b_router, n_ctx, ctx_row):
    bsz, n, d = x1.shape
    tm = TOK_BLK
    nt = n // tm
    tile = pl.BlockSpec((1, tm, d), lambda b, i: (b, i, 0))
    lane_tile = pl.BlockSpec((SUBLANES, tm), lambda b, i: (0, b * nt + i))
    full = lambda a: pl.BlockSpec(a.shape, lambda b, i: (0,) * a.ndim)
    br = b_router.reshape(N_EXPERTS, 1)
    nw = norm_w.reshape(1, d)
    return pl.pallas_call(
        functools.partial(_route_kernel, n_ctx=n_ctx, ctx_row=ctx_row),
        grid=(bsz, nt),
        in_specs=[tile, full(mod_l), full(nw), full(w_router_t), full(br)],
        out_specs=[tile, lane_tile, lane_tile,
                   pl.BlockSpec((N_EXPERTS, LANES), lambda b, i: (0, 0)),
                   pl.BlockSpec((1, N_EXPERTS, LANES), lambda b, i: (b * nt + i, 0, 0))],
        out_shape=[jax.ShapeDtypeStruct((bsz, n, d), F32),
                   jax.ShapeDtypeStruct((SUBLANES, bsz * n), I32),
                   jax.ShapeDtypeStruct((SUBLANES, bsz * n), F32),
                   jax.ShapeDtypeStruct((N_EXPERTS, LANES), F32),
                   jax.ShapeDtypeStruct((bsz * nt, N_EXPERTS, LANES), I32)],
        scratch_shapes=[pltpu.VMEM((N_EXPERTS, LANES), F32)],
        compiler_params=_cparams(2), name="route",
    )(x1, mod_l, nw, w_router_t, br)


def _round_up(x, m):
    return -(-x // m) * m


MOE_MAX_CHUNKS = TOK_BLK // MOE_CHUNK
MOE_SORT_ROWS = _round_up(2 * TOK_BLK + N_EXPERTS * (SUBLANES - 1) + MOE_CHUNK - 1, LANES)
MOE_GATHER_ROWS = _round_up(2 * TOK_BLK + N_EXPERTS * (MOE_CHUNK - 1), LANES)
META_FIELDS = 4


def _moe_blocks(n_assign, n_tiles):
    slack = N_EXPERTS * (n_tiles * (SUBLANES - 1) + MOE_CHUNK - 1)
    return -(-(n_assign + slack) // MOE_BLK) + N_EXPERTS


def _chunk_copies(meta_ref, start_ref, tile, slot_field, do, make_copy):
    def expert(e, carry):
        base = (tile * N_EXPERTS + e) * META_FIELDS
        before, rows, slot = meta_ref[base], meta_ref[base + 1], meta_ref[base + slot_field]
        for c in range(MOE_MAX_CHUNKS):
            @pl.when(rows > c * MOE_CHUNK)
            def _():
                do(make_copy(pl.multiple_of(slot + c * MOE_CHUNK, SUBLANES),
                             pl.multiple_of(start_ref[e] + before + c * MOE_CHUNK, SUBLANES)))
        return carry
    lax.fori_loop(0, N_EXPERTS, expert, 0)


def _dispatch_kernel(meta_ref, cnt_ref, pos_ref, h_ref, xs_ref, blk_e_ref, blk_rows_ref,
                     start_ref, sort_ref, zero_ref, sem, *, nblk):
    t = pl.program_id(0)
    nt = pl.num_programs(0)
    tm = h_ref.shape[0]

    @pl.when(t == 0)
    def _():
        def expert(e, end):
            start_ref[e] = end
            return end + ((cnt_ref[e] + MOE_CHUNK - 1 + MOE_BLK - 1) // MOE_BLK) * MOE_BLK
        start_ref[N_EXPERTS] = lax.fori_loop(0, N_EXPERTS, expert, 0)

        def block(j, carry):
            def count(e, acc):
                return acc + jnp.where(start_ref[e + 1] <= j * MOE_BLK, 1, 0)
            e = jnp.minimum(lax.fori_loop(0, N_EXPERTS, count, 0), N_EXPERTS - 1)
            blk_e_ref[j] = e
            blk_rows_ref[j] = jnp.clip(cnt_ref[e] - (j * MOE_BLK - start_ref[e]), 0, MOE_BLK)
            return carry
        lax.fori_loop(0, nblk, block, 0)

    r = _iota((MOE_SORT_ROWS, tm), 0)
    onehot = jnp.where((r == pos_ref[0:1, :]) | (r == pos_ref[1:2, :]), 1.0, 0.0)
    buf = t % 2
    sort_ref[buf] = _dot(onehot, h_ref[...])

    def copies(which):
        def make_copy(src_row, dst_row):
            return pltpu.make_async_copy(sort_ref.at[which, pl.ds(src_row, MOE_CHUNK), :],
                                         xs_ref.at[pl.ds(dst_row, MOE_CHUNK), :], sem.at[which])
        return make_copy

    @pl.when(t > 0)
    def _():
        _chunk_copies(meta_ref, start_ref, t - 1, 2, lambda cp: cp.wait(), copies(1 - buf))

    _chunk_copies(meta_ref, start_ref, t, 2, lambda cp: cp.start(), copies(buf))

    @pl.when(t == nt - 1)
    def _():
        _chunk_copies(meta_ref, start_ref, t, 2, lambda cp: cp.wait(), copies(buf))

        zero_ref[...] = jnp.zeros_like(zero_ref)

        def zero_copy(row, size):
            return pltpu.make_async_copy(zero_ref.at[pl.ds(0, size), :],
                                         xs_ref.at[pl.ds(pl.multiple_of(row, SUBLANES), size), :],
                                         sem.at[buf])

        def fill(do):
            def expert(e, carry):
                lo = start_ref[e] + cnt_ref[e]
                gap = start_ref[e + 1] - lo
                n_big = gap // MOE_CHUNK

                def big(k, c2):
                    do(zero_copy(lo + k * MOE_CHUNK, MOE_CHUNK))
                    return c2
                lax.fori_loop(0, n_big, big, 0)

                def small(k, c2):
                    do(zero_copy(lo + n_big * MOE_CHUNK + k * SUBLANES, SUBLANES))
                    return c2
                lax.fori_loop(0, (gap - n_big * MOE_CHUNK) // SUBLANES, small, 0)
                return carry
            lax.fori_loop(0, N_EXPERTS, expert, 0)

            def unused(k, carry):
                do(zero_copy(start_ref[N_EXPERTS] + k * MOE_ZERO_ROWS, MOE_ZERO_ROWS))
                return carry
            lax.fori_loop(0, (nblk * MOE_BLK - start_ref[N_EXPERTS]) // MOE_ZERO_ROWS, unused, 0)

        fill(lambda cp: cp.start())
        fill(lambda cp: cp.wait())


def _dispatch(meta, counts, pos_rows, h_flat, nblk):
    ntok, d = h_flat.shape
    tm = TOK_BLK
    whole_smem = pl.BlockSpec(memory_space=pltpu.SMEM)
    return pl.pallas_call(
        functools.partial(_dispatch_kernel, nblk=nblk),
        grid=(ntok // tm,),
        in_specs=[whole_smem, whole_smem,
                  pl.BlockSpec((SUBLANES, tm), lambda i: (0, i)),
                  pl.BlockSpec((tm, d), lambda i: (i, 0))],
        out_specs=[pl.BlockSpec(memory_space=pl.ANY), whole_smem, whole_smem, whole_smem],
        out_shape=[jax.ShapeDtypeStruct((nblk * MOE_BLK, d), F32),
                   jax.ShapeDtypeStruct((nblk,), I32),
                   jax.ShapeDtypeStruct((nblk,), I32),
                   jax.ShapeDtypeStruct((N_EXPERTS + 1,), I32)],
        scratch_shapes=[pltpu.VMEM((2, MOE_SORT_ROWS, d), F32), pltpu.VMEM((MOE_ZERO_ROWS, d), F32),
                        pltpu.SemaphoreType.DMA((2,))],
        compiler_params=_cparams(1), name="dispatch",
    )(meta, counts, pos_rows, h_flat)


def _expert_kernel(blk_e_ref, blk_rows_ref, x_ref, wg_ref, wu_ref, wd_ref, o_ref, g_bf, u_bf, d_bf):
    i = pl.program_id(0)
    rows = blk_rows_ref[i]

    @pl.when((i == 0) | (blk_e_ref[i] != blk_e_ref[jnp.maximum(i - 1, 0)]))
    def _():
        g_bf[...] = wg_ref[0, 0].astype(BF16)
        u_bf[...] = wu_ref[0, 0].astype(BF16)
        d_bf[...] = wd_ref[0, 0].astype(BF16)

    @pl.when(rows > 0)
    def _():
        x = x_ref[...].astype(BF16)
        hid = _silu(_dot(x, g_bf[...])) * _dot(x, u_bf[...])
        o_ref[...] = _dot(hid, d_bf[...])

    @pl.when(rows == 0)
    def _():
        o_ref[...] = jnp.zeros_like(o_ref)


def _experts(blk_e, blk_rows, xs, w_gate, w_up, w_down, layer):
    d = xs.shape[1]
    ff = w_gate.shape[3]
    nblk = blk_e.shape[0]
    weight = lambda shape: pl.BlockSpec((1, 1) + shape, lambda i, be, rows: (layer, be[i], 0, 0))
    return pl.pallas_call(
        _expert_kernel,
        grid_spec=pltpu.PrefetchScalarGridSpec(
            num_scalar_prefetch=2, grid=(nblk,),
            in_specs=[pl.BlockSpec((MOE_BLK, d), lambda i, be, rows: (i, 0)),
                      weight((d, ff)), weight((d, ff)), weight((ff, d))],
            out_specs=pl.BlockSpec((MOE_BLK, d), lambda i, be, rows: (i, 0)),
            scratch_shapes=[pltpu.VMEM((d, ff), BF16), pltpu.VMEM((d, ff), BF16),
                            pltpu.VMEM((ff, d), BF16)]),
        out_shape=jax.ShapeDtypeStruct(xs.shape, F32),
        compiler_params=_cparams(1), name="experts",
    )(blk_e, blk_rows, xs, w_gate, w_up, w_down)


def _combine_kernel(meta_ref, start_ref, x_ref, pos_ref, wt_ref, mod_ref, ys_ref, o_ref,
                    gath_ref, sem, *, n_ctx, ctx_row):
    b, i = pl.program_id(0), pl.program_id(1)
    nt = pl.num_programs(1)
    t = b * nt + i
    last = pl.num_programs(0) * nt - 1
    tm, d = x_ref.shape[1:]
    buf = t % 2

    def copies(which):
        def make_copy(slot_row, ys_row):
            return pltpu.make_async_copy(ys_ref.at[pl.ds(ys_row, MOE_CHUNK), :],
                                         gath_ref.at[which, pl.ds(slot_row, MOE_CHUNK), :],
                                         sem.at[which])
        return make_copy

    @pl.when(t == 0)
    def _():
        gath_ref[...] = jnp.zeros_like(gath_ref)
        _chunk_copies(meta_ref, start_ref, t, 3, lambda cp: cp.start(), copies(buf))

    @pl.when(t < last)
    def _():
        _chunk_copies(meta_ref, start_ref, t + 1, 3, lambda cp: cp.start(), copies(1 - buf))

    _chunk_copies(meta_ref, start_ref, t, 3, lambda cp: cp.wait(), copies(buf))

    col = _iota((tm, MOE_GATHER_ROWS), 1)
    pos = pos_ref[...]
    wt = wt_ref[...]
    sel = (jnp.where(col == pos[:, 0:1], wt[:, 0:1], 0.0)
           + jnp.where(col == pos[:, 1:2], wt[:, 1:2], 0.0))
    hi, lo = _hi_lo(sel)
    rows = gath_ref[buf].astype(BF16)
    y = (jnp.dot(hi, rows, preferred_element_type=F32)
         + jnp.dot(lo, rows, preferred_element_type=F32))
    gate = _mod_rows(mod_ref, 5, b, i * tm, n_ctx, ctx_row, tm, d)
    o_ref[0] = x_ref[0] + gate * y


def _combine(meta, start, x1, pos_cols, wt_cols, mod_l, ys, n_ctx, ctx_row, latent_only):
    bsz, n, d = x1.shape
    tm = TOK_BLK
    nt = n // tm
    whole_smem = pl.BlockSpec(memory_space=pltpu.SMEM)
    cols = pl.BlockSpec((tm, 2), lambda b, i: (b * nt + i, 0))
    if latent_only:
        assert n_ctx == tm
        out_spec = pl.BlockSpec((1, tm, d), lambda b, i: (b, jnp.maximum(i - 1, 0), 0))
        out_rows = n - n_ctx
    else:
        out_spec = pl.BlockSpec((1, tm, d), lambda b, i: (b, i, 0))
        out_rows = n
    return pl.pallas_call(
        functools.partial(_combine_kernel, n_ctx=n_ctx, ctx_row=ctx_row),
        grid=(bsz, nt),
        in_specs=[whole_smem, whole_smem,
                  pl.BlockSpec((1, tm, d), lambda b, i: (b, i, 0)), cols, cols,
                  pl.BlockSpec(mod_l.shape, lambda b, i: (0, 0)),
                  pl.BlockSpec(memory_space=pl.ANY)],
        out_specs=out_spec,
        out_shape=jax.ShapeDtypeStruct((bsz, out_rows, d), F32),
        scratch_shapes=[pltpu.VMEM((2, MOE_GATHER_ROWS, d), F32), pltpu.SemaphoreType.DMA((2,))],
        compiler_params=_cparams(2), name="combine",
    )(meta, start, x1, pos_cols, wt_cols, mod_l, ys)


def _layer(xa, mod_l, layer, n_ctx, ctx_row, p, rope):
    bsz, n, d = xa.shape
    na_w = NA_HEADS * HEAD_DIM
    hg_w = HG_HEADS * HG_DK
    wa_qw = WA_HEADS * HEAD_DIM
    wa_kvw = WA_KV_HEADS * HEAD_DIM
    hg0 = 3 * na_w
    wa0 = hg0 + 5 * hg_w
    gate0 = wa0 + wa_qw + 2 * wa_kvw
    n_cols = gate0 + N_BRANCHES * d
    assert p["w_in"].shape[2] == n_cols
    w_att = _take_cols(p["w_in"], layer, [(gate0, n_cols), (0, hg0), (wa0, gate0)])
    w_hg = _take_cols(p["w_in"], layer, [(hg0, wa0)])
    col_gate = 0
    col_na = [N_BRANCHES * d + k * na_w for k in range(3)]
    col_wq = N_BRANCHES * d + 3 * na_w
    col_wk = col_wq + wa_qw
    col_wv = col_wk + wa_kvw
    col_hg = [k * hg_w for k in range(5)]
    proj, h = _inproj(xa, mod_l, p["norm1"], w_att, n_ctx, ctx_row, 768, BF16)
    proj_hg = _project(h, w_hg, 1280, F32)

    ya = _na_attention(proj, _na_bias(p["na_rpb"]), p["na_q_norm"], p["na_k_norm"], *col_na)
    yc = _wa_attention(proj, p["wa_sink"], p["wa_q_norm"], p["wa_k_norm"], rope[0], rope[1],
                       col_wq, col_wk, col_wv)
    o_f = _hgrn_pass(proj_hg, p["hg_lower"], layer, False, col_hg[0], col_hg[1], col_hg[3])
    yb = _hgrn_pass(proj_hg, p["hg_lower"], layer, True, col_hg[0], col_hg[2], col_hg[3],
                    final_args=(col_hg[4], o_f, p["hg_norm"]))

    bf = lambda w: w.astype(BF16)
    x1 = _merge(xa, ya, yb, yc, proj, mod_l, bf(p["w_pa"]), bf(p["w_pb"]), bf(p["w_pc"]),
                bf(p["w_out"]), col_gate, n_ctx, ctx_row)

    h2, idx, wts, counts, meta = _route(x1, mod_l, p["norm2"], p["w_router"].T, p["b_router"],
                                        n_ctx, ctx_row)
    ntok = bsz * n
    meta_flat = meta[:, :, :META_FIELDS].reshape(-1)
    xs, blk_e, blk_rows, start = _dispatch(meta_flat, counts[:, 0].astype(I32), idx,
                                           h2.reshape(ntok, d),
                                           _moe_blocks(2 * ntok, ntok // TOK_BLK))
    ys = _experts(blk_e, blk_rows, xs, p["w_gate"], p["w_up"], p["w_down"], layer)
    return _combine(meta_flat, start, x1, idx[2:4].T, wts[:2].T, mod_l, ys, n_ctx, ctx_row,
                    latent_only=p["last"])


def kernel(x, c, ctx, c_ctx, w_ada, b_ada, norm1, norm2, w_in, na_q_norm, na_k_norm, na_rpb, hg_lower,
           hg_norm, wa_q_norm, wa_k_norm, wa_sink, w_pa, w_pb, w_pc, w_out, w_router, b_router,
           w_gate, w_up, w_down):
    bsz, t, d = x.shape
    n_ctx = ctx.shape[1]
    depth = w_ada.shape[0]
    assert n_ctx == TOK_BLK and t % TOK_BLK == 0 and t // TOK_BLK >= 3
    assert t % GRID_W == 0 and bsz + 1 <= SUBLANES

    xa = jnp.concatenate([ctx, x], axis=1)
    cond = jnp.concatenate([c, c_ctx[None], jnp.zeros((SUBLANES - bsz - 1, d), F32)], axis=0)
    mod = _ada(cond, w_ada, b_ada)
    rope = _rope_tables(n_ctx, t)
    for l in range(depth):
        p = dict(norm1=norm1[l], norm2=norm2[l], w_in=w_in, na_q_norm=na_q_norm[l],
                 na_k_norm=na_k_norm[l], na_rpb=na_rpb[l], hg_lower=hg_lower, hg_norm=hg_norm[l],
                 wa_q_norm=wa_q_norm[l], wa_k_norm=wa_k_norm[l], wa_sink=wa_sink[l],
                 w_pa=w_pa[l], w_pb=w_pb[l], w_pc=w_pc[l], w_out=w_out[l],
                 w_router=w_router, b_router=b_router, w_gate=w_gate, w_up=w_up,
                 w_down=w_down, last=l == depth - 1)
        xa = _layer(xa, mod[l], l, n_ctx, bsz, p, rope)
    return xa
```

```python
import functools

import numpy as np
import jax
import jax.numpy as jnp
from jax import lax
from jax.experimental import pallas as pl
from jax.experimental.pallas import tpu as pltpu

F32 = jnp.float32
BF16 = jnp.bfloat16
I32 = jnp.int32

GRID_W = 64
NA_HEADS = 8
NA_ROWS = 8
NA_COLS = 16
HEAD_DIM = 64
HG_HEADS = 4
HG_DK = 128
LOG_FLOOR = 1e-30
WA_HEADS = 8
WA_KV_HEADS = 2
WA_WINDOW = 128
ROPE_THETA = 10000.0
N_EXPERTS = 16
N_GROUPS = 4
EXPERTS_PER_GROUP = N_EXPERTS // N_GROUPS
N_BRANCHES = 3
NEG_INF = -1e30
RMS_EPS = 1e-6

LANES = 128
SUBLANES = 8
VMEM_LIMIT = 56 * 1024 * 1024

TOK_BLK = 256
MOE_BLK = 1024
MOE_CHUNK = 32
MOE_ZERO_ROWS = 256


def _cparams(n_axes):
    return pltpu.CompilerParams(
        dimension_semantics=("arbitrary",) * n_axes, vmem_limit_bytes=VMEM_LIMIT)


def _sigmoid(x):
    return 1.0 / (1.0 + jnp.exp(-x))


def _silu(x):
    return x * _sigmoid(x)


def _dot(a, b):
    return jnp.dot(a.astype(BF16), b.astype(BF16), preferred_element_type=F32)


def _dot_nt(a, b):
    return lax.dot_general(a.astype(BF16), b.astype(BF16), (((1,), (1,)), ((), ())),
                           preferred_element_type=F32)


def _dot_tn(a, b):
    return lax.dot_general(a.astype(BF16), b.astype(BF16), (((0,), (0,)), ((), ())),
                           preferred_element_type=F32)


def _hi_lo(a):
    hi = a.astype(BF16)
    lo = (a - hi.astype(F32)).astype(BF16)
    return hi, lo


def _dot3(a, b):
    ah, al = _hi_lo(a)
    bh, bl = _hi_lo(b)
    d = functools.partial(jnp.dot, preferred_element_type=F32)
    return d(ah, bh) + (d(ah, bl) + d(al, bh))


def _dot3_nt(a, b):
    ah, al = _hi_lo(a)
    bh, bl = _hi_lo(b)
    d = functools.partial(lax.dot_general, dimension_numbers=(((1,), (1,)), ((), ())),
                          preferred_element_type=F32)
    return d(ah, bh) + (d(ah, bl) + d(al, bh))


def _rms(x, w):
    return x * lax.rsqrt(jnp.mean(x * x, axis=-1, keepdims=True) + RMS_EPS) * w


def _iota(shape, dim):
    return lax.broadcasted_iota(I32, shape, dim)


def _ada_kernel(cond_ref, w_ref, b_ref, o_ref):
    o_ref[0] = _dot3(_silu(cond_ref[...]), w_ref[0]) + b_ref[0]


def _ada(cond, w_ada, b_ada):
    depth, d, d6 = w_ada.shape
    rows = cond.shape[0]
    tn = d6 // 4
    return pl.pallas_call(
        _ada_kernel,
        grid=(depth, d6 // tn),
        in_specs=[pl.BlockSpec((rows, d), lambda l, j: (0, 0)),
                  pl.BlockSpec((1, d, tn), lambda l, j: (l, 0, j)),
                  pl.BlockSpec((1, 1, tn), lambda l, j: (l, 0, j))],
        out_specs=pl.BlockSpec((1, rows, tn), lambda l, j: (l, 0, j)),
        out_shape=jax.ShapeDtypeStruct((depth, rows, d6), F32),
        compiler_params=_cparams(2), name="ada",
    )(cond, w_ada, b_ada.reshape(depth, 1, d6))


def _mod_rows(mod_ref, which, b, row0, n_ctx, ctx_row, tm, d):
    lat = mod_ref[pl.ds(b, 1), which * d:(which + 1) * d]
    cx = mod_ref[ctx_row:ctx_row + 1, which * d:(which + 1) * d]
    row = row0 + _iota((tm, d), 0)
    return jnp.where(row < n_ctx, cx, lat)


def _inproj_kernel(x_ref, mod_ref, nw_ref, w_ref, o_ref, h_ref, *, n_ctx, ctx_row):
    b, i, j = pl.program_id(0), pl.program_id(1), pl.program_id(2)
    tm, d = h_ref.shape[1:]

    @pl.when(j == 0)
    def _():
        h = _rms(x_ref[0], nw_ref[...])
        shift = _mod_rows(mod_ref, 0, b, i * tm, n_ctx, ctx_row, tm, d)
        scale = _mod_rows(mod_ref, 1, b, i * tm, n_ctx, ctx_row, tm, d)
        h_ref[0] = (h * (1.0 + scale) + shift).astype(BF16)

    o_ref[0] = jnp.dot(h_ref[0], w_ref[...], preferred_element_type=F32).astype(o_ref.dtype)


def _matmul_kernel(h_ref, w_ref, o_ref):
    o_ref[0] = jnp.dot(h_ref[0], w_ref[...], preferred_element_type=F32).astype(o_ref.dtype)


def _project(h, w, tn, out_dtype):
    bsz, n, d = h.shape
    cols = w.shape[1]
    tm = 1408 if n % 1408 == 0 else TOK_BLK
    return pl.pallas_call(
        _matmul_kernel,
        grid=(bsz, n // tm, cols // tn),
        in_specs=[pl.BlockSpec((1, tm, d), lambda b, i, j: (b, i, 0)),
                  pl.BlockSpec((d, tn), lambda b, i, j: (0, j))],
        out_specs=pl.BlockSpec((1, tm, tn), lambda b, i, j: (b, i, j)),
        out_shape=jax.ShapeDtypeStruct((bsz, n, cols), out_dtype),
        compiler_params=_cparams(3), name="project",
    )(h, w)


W_IN_COL_BLK = 256


def _take_cols_kernel(src_ref, w_ref, o_ref):
    del src_ref
    o_ref[...] = w_ref[0].astype(BF16)


def _take_cols(w_in, layer, col_ranges):
    d = w_in.shape[1]
    blk = W_IN_COL_BLK
    src = []
    for start, stop in col_ranges:
        assert start % blk == 0 and stop % blk == 0
        src += list(range(start // blk, stop // blk))
    return pl.pallas_call(
        _take_cols_kernel,
        grid_spec=pltpu.PrefetchScalarGridSpec(
            num_scalar_prefetch=1, grid=(len(src),),
            in_specs=[pl.BlockSpec((1, d, blk), lambda j, src: (layer, 0, src[j]))],
            out_specs=pl.BlockSpec((d, blk), lambda j, src: (0, j))),
        out_shape=jax.ShapeDtypeStruct((d, len(src) * blk), BF16),
        compiler_params=_cparams(1), name="take_cols",
    )(jnp.asarray(src, I32), w_in)


def _inproj(xa, mod_l, norm_w, w, n_ctx, ctx_row, tn, out_dtype):
    bsz, n, d = xa.shape
    cols = w.shape[1]
    tm = 1408 if n % 1408 == 0 else TOK_BLK
    return pl.pallas_call(
        functools.partial(_inproj_kernel, n_ctx=n_ctx, ctx_row=ctx_row),
        grid=(bsz, n // tm, cols // tn),
        in_specs=[pl.BlockSpec((1, tm, d), lambda b, i, j: (b, i, 0)),
                  pl.BlockSpec(mod_l.shape, lambda b, i, j: (0, 0)),
                  pl.BlockSpec((1, d), lambda b, i, j: (0, 0)),
                  pl.BlockSpec((d, tn), lambda b, i, j: (0, j))],
        out_specs=[pl.BlockSpec((1, tm, tn), lambda b, i, j: (b, i, j)),
                   pl.BlockSpec((1, tm, d), lambda b, i, j: (b, i, 0))],
        out_shape=[jax.ShapeDtypeStruct((bsz, n, cols), out_dtype),
                   jax.ShapeDtypeStruct((bsz, n, d), BF16)],
        compiler_params=_cparams(3), name="inproj",
    )(xa, mod_l, norm_w.reshape(1, d), w)


def _pair_norm(x, w):
    lane = _iota(x.shape, 1)
    lo = lane < HEAD_DIM
    sq = x * x
    s_lo = jnp.sum(jnp.where(lo, sq, 0.0), axis=-1, keepdims=True)
    s_hi = jnp.sum(jnp.where(lo, 0.0, sq), axis=-1, keepdims=True)
    ms = jnp.where(lo, s_lo, s_hi) * (1.0 / HEAD_DIM)
    return x * lax.rsqrt(ms + RMS_EPS) * w


LOG2E = 1.4426950408889634


def _ones_beside(v, value_lanes):
    return jnp.where(value_lanes, v, 1.0).astype(BF16)


def _softmax_av(s_loc, s_ctx, v_loc, v_ctx, sink):
    m = jnp.maximum(jnp.max(s_loc, axis=-1, keepdims=True), jnp.max(s_ctx, axis=-1, keepdims=True))
    if sink is not None:
        m = jnp.maximum(m, sink)
    p_loc = jnp.exp2(s_loc - m).astype(BF16)
    p_ctx = jnp.exp2(s_ctx - m).astype(BF16)
    acc = (jnp.dot(p_loc, v_loc, preferred_element_type=F32)
           + jnp.dot(p_ctx, v_ctx, preferred_element_type=F32))
    den = pltpu.roll(acc, LANES // 2, 1)
    if sink is not None:
        den = den + jnp.exp2(sink - m)
    return acc / den


NA_QROWS = TOK_BLK // GRID_W
NA_KROWS = 3 * NA_QROWS
N_RPB_COLS = 2 * NA_COLS - 1
N_RPB_ROWS = 2 * NA_ROWS - 1
NA_VARIANTS = 4


def _na_row_valid(variant, a, k):
    if variant == 0:
        return False
    if variant == 1:
        return NA_QROWS <= k < NA_QROWS + NA_ROWS
    if variant == 3:
        return k < NA_ROWS
    return a <= k < a + NA_ROWS


def _na_bias_kernel(rpb_ref, o_ref):
    h = pl.program_id(0)
    qc = _iota((GRID_W, GRID_W), 0)
    kc = _iota((GRID_W, GRID_W), 1)
    col_lo = jnp.clip(qc - NA_COLS // 2, 0, GRID_W - NA_COLS)
    col_ok = (kc >= col_lo) & (kc < col_lo + NA_COLS)
    d_col = kc - qc + (NA_COLS - 1)
    masked = jnp.full((GRID_W, GRID_W), NEG_INF, F32)
    blocks = []
    for dr in range(N_RPB_ROWS):
        acc = jnp.zeros((GRID_W, GRID_W), F32)
        for dc in range(N_RPB_COLS):
            val = rpb_ref[(h * N_RPB_ROWS + dr) * N_RPB_COLS + dc]
            acc = jnp.where(d_col == dc, val, acc)
        blocks.append(jnp.where(col_ok, acc * LOG2E, NEG_INF))
    for variant in range(NA_VARIANTS):
        for a in range(NA_QROWS):
            for k in range(NA_KROWS):
                dr = k - NA_QROWS - a + NA_ROWS - 1
                ok = _na_row_valid(variant, a, k) and 0 <= dr < N_RPB_ROWS
                o_ref[variant, 0, a * GRID_W:(a + 1) * GRID_W, k * GRID_W:(k + 1) * GRID_W] = (
                    blocks[dr] if ok else masked)


def _na_bias(rpb):
    heads = rpb.shape[0]
    return pl.pallas_call(
        _na_bias_kernel,
        grid=(heads,),
        in_specs=[pl.BlockSpec(memory_space=pltpu.SMEM)],
        out_specs=pl.BlockSpec((NA_VARIANTS, 1, TOK_BLK, 3 * TOK_BLK), lambda h: (0, h, 0, 0)),
        out_shape=jax.ShapeDtypeStruct((NA_VARIANTS, heads, TOK_BLK, 3 * TOK_BLK), F32),
        compiler_params=_cparams(1), name="na_bias",
    )(rpb.reshape(-1))


def _na_kernel(q_ref, kp_ref, kc_ref, kn_ref, vp_ref, vc_ref, vn_ref, kx_ref, vx_ref,
               bias_ref, qn_ref, kn_w_ref, o_ref):
    tq = q_ref.shape[1]
    lo = _iota((tq, LANES), 1) < HEAD_DIM
    lo_k = _iota((3 * tq, LANES), 1) < HEAD_DIM
    qn_w = qn_ref[...] * (HEAD_DIM ** -0.5 * LOG2E)
    for p in range(NA_HEADS // 2):
        pair = slice(p * LANES, (p + 1) * LANES)
        f32 = lambda ref: ref[0, :, pair].astype(F32)
        q = _pair_norm(f32(q_ref), qn_w)
        k_loc = _pair_norm(jnp.concatenate([f32(kp_ref), f32(kc_ref), f32(kn_ref)], axis=0),
                           kn_w_ref[...]).astype(BF16)
        k_ctx = _pair_norm(f32(kx_ref), kn_w_ref[...]).astype(BF16)
        v_loc = jnp.concatenate([f32(vp_ref), f32(vc_ref), f32(vn_ref)], axis=0)
        v_ctx = f32(vx_ref)
        q2 = jnp.concatenate([jnp.where(lo, q, 0.0), jnp.where(lo, 0.0, q)], axis=0).astype(BF16)
        s_loc2 = _dot_nt(q2, k_loc)
        s_ctx2 = _dot_nt(q2, k_ctx)
        outs = []
        for h in range(2):
            mine = lo if h == 0 else ~lo
            mine_k = lo_k if h == 0 else ~lo_k
            rows = slice(h * tq, (h + 1) * tq)
            outs.append(_softmax_av(s_loc2[rows] + bias_ref[0, 2 * p + h], s_ctx2[rows],
                                    _ones_beside(v_loc, mine_k), _ones_beside(v_ctx, mine), None))
        o_ref[0, :, pair] = jnp.where(lo, outs[0], outs[1])


def _na_attention(proj, bias, qn_w, kn_w, col_q, col_k, col_v):
    bsz, n, _ = proj.shape
    nblk = n // TOK_BLK
    last = nblk - 1
    width = NA_HEADS * HEAD_DIM
    cq, ck, cv = col_q // width, col_k // width, col_v // width

    def blk(col, shift):
        return pl.BlockSpec((1, TOK_BLK, width),
                            lambda b, i: (b, jnp.clip(i + shift, 0, last), col))

    def ctx_blk(col):
        return pl.BlockSpec((1, TOK_BLK, width), lambda b, i: (b, 0, col))

    def variant(b, i):
        return (jnp.where(i == 0, 0, jnp.where(i == 1, 1, jnp.where(i == last, 3, 2))), 0, 0, 0)

    vec = pl.BlockSpec((1, LANES), lambda b, i: (0, 0))
    return pl.pallas_call(
        _na_kernel,
        grid=(bsz, nblk),
        in_specs=[blk(cq, 0), blk(ck, -1), blk(ck, 0), blk(ck, 1), blk(cv, -1), blk(cv, 0), blk(cv, 1),
                  ctx_blk(ck), ctx_blk(cv),
                  pl.BlockSpec((1, NA_HEADS, TOK_BLK, 3 * TOK_BLK), variant), vec, vec],
        out_specs=pl.BlockSpec((1, TOK_BLK, width), lambda b, i: (b, i, 0)),
        out_shape=jax.ShapeDtypeStruct((bsz, n, width), F32),
        compiler_params=_cparams(2), name="na_attn",
    )(proj, proj, proj, proj, proj, proj, proj, proj, proj, bias,
      jnp.tile(qn_w, 2).reshape(1, LANES), jnp.tile(kn_w, 2).reshape(1, LANES))


def _rope(x, cos, sin_signed):
    lane = _iota(x.shape, 1)
    first = (lane & (HEAD_DIM // 2 - 1)) < HEAD_DIM // 4
    quarter = HEAD_DIM // 4
    partner = jnp.where(first, pltpu.roll(x, LANES - quarter, 1), pltpu.roll(x, quarter, 1))
    return x * cos + partner * sin_signed


def _wa_kernel(sink_ref, q_ref, kp_ref, kc_ref, kn_ref, vp_ref, vc_ref, vn_ref, kx_ref, vx_ref,
               cq_ref, sq_ref, cp_ref, sp_ref, cn_ref, sn_ref, qn_ref, kn_w_ref, o_ref, *, last):
    i = pl.program_id(1)
    tq = q_ref.shape[1]
    tn = kp_ref.shape[1]
    n_loc = tq + 2 * tn
    f32 = lambda ref: ref[0].astype(F32)
    cos_loc = jnp.concatenate([cp_ref[...], cq_ref[...], cn_ref[...]], axis=0)
    sin_loc = jnp.concatenate([sp_ref[...], sq_ref[...], sn_ref[...]], axis=0)
    k_loc = _pair_norm(jnp.concatenate([f32(kp_ref), f32(kc_ref), f32(kn_ref)], axis=0),
                       kn_w_ref[...])
    k_loc = _rope(k_loc, cos_loc, sin_loc).astype(BF16)
    k_ctx = _pair_norm(f32(kx_ref), kn_w_ref[...]).astype(BF16)
    v_loc = jnp.concatenate([f32(vp_ref), f32(vc_ref), f32(vn_ref)], axis=0)
    v_ctx = f32(vx_ref)
    half = LANES // 2
    lo = _iota((tq, LANES), 1) < HEAD_DIM
    lo_loc = _iota((n_loc, LANES), 1) < HEAD_DIM
    qn_w = qn_ref[...] * (HEAD_DIM ** -0.5 * LOG2E)

    qi = _iota((tq, n_loc), 0)
    kj = _iota((tq, n_loc), 1)
    rel = kj - tn - qi
    lo_col = jnp.where(i >= 2, 0, tn)
    hi_col = jnp.where(i >= 1, jnp.where(i < last, n_loc, tn + tq), 0)
    ok = (rel >= -WA_WINDOW) & (rel <= WA_WINDOW) & (kj >= lo_col) & (kj < hi_col)
    mask = jnp.where(ok, 0.0, NEG_INF)

    group = WA_HEADS // WA_KV_HEADS
    mask_g = jnp.concatenate([mask] * group, axis=0)
    q_pairs = [_rope(_pair_norm(q_ref[0, :, p * LANES:(p + 1) * LANES].astype(F32), qn_w),
                     cq_ref[...], sq_ref[...]) for p in range(WA_HEADS // 2)]
    outs = [None] * WA_HEADS
    for kv in range(WA_KV_HEADS):
        mine, mine_loc = (lo, lo_loc) if kv == 0 else (~lo, ~lo_loc)
        heads = range(kv * group, (kv + 1) * group)
        stack = []
        for h in heads:
            q = q_pairs[h // 2]
            q = q if h % 2 == kv else pltpu.roll(q, half, 1)
            stack.append(jnp.where(mine, q, 0.0).astype(BF16))
        q_g = jnp.concatenate(stack, axis=0)
        sink_g = jnp.concatenate([jnp.full((tq, 1), sink_ref[h] * LOG2E, F32) for h in heads], axis=0)
        o_g = _softmax_av(_dot_nt(q_g, k_loc) + mask_g, _dot_nt(q_g, k_ctx),
                          _ones_beside(v_loc, mine_loc), _ones_beside(v_ctx, mine), sink_g)
        for n, h in enumerate(heads):
            o = o_g[n * tq:(n + 1) * tq]
            outs[h] = o if h % 2 == kv else pltpu.roll(o, half, 1)
    for p in range(WA_HEADS // 2):
        o_ref[0, :, p * LANES:(p + 1) * LANES] = jnp.where(lo, outs[2 * p], outs[2 * p + 1])


def _wa_attention(proj, sink, qn_w, kn_w, cos_t, sin_t, col_q, col_k, col_v):
    bsz, n, _ = proj.shape
    nblk = n // TOK_BLK
    last = nblk - 1
    qw = WA_HEADS * HEAD_DIM
    cq, ck, cv = col_q // qw, col_k // LANES, col_v // LANES

    assert TOK_BLK % WA_WINDOW == 0
    per = TOK_BLK // WA_WINDOW

    def near(i, shift):
        if shift == 0:
            return i
        return jnp.clip(per * i - 1 if shift < 0 else per * (i + 1), 0, per * nblk - 1)

    def blk(col, shift):
        rows = TOK_BLK if shift == 0 else WA_WINDOW
        return pl.BlockSpec((1, rows, LANES), lambda b, i: (b, near(i, shift), col))

    def tab(shift):
        rows = TOK_BLK if shift == 0 else WA_WINDOW
        return pl.BlockSpec((rows, LANES), lambda b, i: (near(i, shift), 0))

    def ctx_blk(col):
        return pl.BlockSpec((1, TOK_BLK, LANES), lambda b, i: (b, 0, col))

    vec = pl.BlockSpec((1, LANES), lambda b, i: (0, 0))
    return pl.pallas_call(
        functools.partial(_wa_kernel, last=last),
        grid=(bsz, nblk),
        in_specs=[pl.BlockSpec(memory_space=pltpu.SMEM),
                  pl.BlockSpec((1, TOK_BLK, qw), lambda b, i: (b, i, cq)),
                  blk(ck, -1), blk(ck, 0), blk(ck, 1), blk(cv, -1), blk(cv, 0), blk(cv, 1),
                  ctx_blk(ck), ctx_blk(cv),
                  tab(0), tab(0), tab(-1), tab(-1), tab(1), tab(1), vec, vec],
        out_specs=pl.BlockSpec((1, TOK_BLK, qw), lambda b, i: (b, i, 0)),
        out_shape=jax.ShapeDtypeStruct((bsz, n, qw), F32),
        compiler_params=_cparams(2), name="wa_attn",
    )(sink, proj, proj, proj, proj, proj, proj, proj, proj, proj,
      cos_t, sin_t, cos_t, sin_t, cos_t, sin_t,
      jnp.tile(qn_w, 2).reshape(1, LANES), jnp.tile(kn_w, 2).reshape(1, LANES))


def _rope_tables(n_ctx, t):
    quarter = HEAD_DIM // 4
    inv_freq = ROPE_THETA ** (-jnp.arange(quarter, dtype=F32) / quarter)
    pos = jnp.arange(t)
    lane = np.arange(LANES)
    in_head = lane % HEAD_DIM
    use_col = in_head >= HEAD_DIM // 2
    second = (in_head % (HEAD_DIM // 2)) >= quarter
    freq = inv_freq[in_head % quarter]
    p = jnp.where(use_col[None, :], (pos % GRID_W)[:, None], (pos // GRID_W)[:, None]).astype(F32)
    ang = p * freq[None, :]
    cos = jnp.cos(ang)
    sin = jnp.where(second[None, :], jnp.sin(ang), -jnp.sin(ang))
    cos = jnp.concatenate([jnp.ones((n_ctx, LANES), F32), cos], axis=0)
    sin = jnp.concatenate([jnp.zeros((n_ctx, LANES), F32), sin], axis=0)
    return cos, sin


HG_CHUNK = 128


def _hgrn_kernel(q_ref, f_ref, v_ref, hl_ref, *rest, layer, reverse, final):
    if final:
        g_ref, prev_ref, nw_ref, o_ref, st_ref = rest
    else:
        o_ref, st_ref = rest
    @pl.when(pl.program_id(1) == 0)
    def _():
        st_ref[...] = jnp.zeros_like(st_ref)

    a = hl_ref[0]
    e = jnp.exp(a - jnp.max(a, axis=0, keepdims=True))
    pr = e / jnp.sum(e, axis=0, keepdims=True)
    lb_all = jnp.zeros((1, a.shape[1]), F32)
    for j in range(1, layer + 1):
        lb_all = lb_all + pr[j:j + 1]

    for h in range(HG_HEADS):
        lanes = slice(h * HG_DK, (h + 1) * HG_DK)
        lb = lb_all[:, lanes]
        qs = _silu(q_ref[0, :, lanes])
        f = lb + (1.0 - lb) * _sigmoid(f_ref[0, :, lanes])
        o, st = _hgrn_block(qs, 1.0 - f, jnp.log(jnp.maximum(f, LOG_FLOOR)), v_ref[0, :, lanes],
                            st_ref[h], reverse)
        st_ref[h] = st
        if final:
            tot = prev_ref[0, :, lanes] + o
            o_ref[0, :, lanes] = _rms(tot, nw_ref[...]) * _silu(g_ref[0, :, lanes])
        else:
            o_ref[0, :, lanes] = o


def _hgrn_block(qs, kk, g, v, st, reverse):
    c = qs.shape[0]
    n_sub = c // HG_CHUNK
    row = _iota((c, HG_DK), 0)
    ri = _iota((c, c), 0)
    ci = _iota((c, c), 1)
    block_xor = ri ^ ci

    tri = jnp.where(((ci >= ri) if reverse else (ci <= ri)) & (block_xor < HG_CHUNK), 1.0, 0.0)
    tri = tri.astype(BF16)
    g_hi = g.astype(BF16)
    rest = g - g_hi.astype(F32)
    g_mid = rest.astype(BF16)
    g_lo = (rest - g_mid.astype(F32)).astype(BF16)
    tdot = functools.partial(jnp.dot, preferred_element_type=F32)
    b = tdot(tri, g_hi) + (tdot(tri, g_mid) + tdot(tri, g_lo))

    def ref_rows(hs):
        blk = 2 * hs
        inner = hs if reverse else hs - 1
        if blk >= SUBLANES:
            x = b.reshape(c // blk, blk, HG_DK)
            return jnp.broadcast_to(x[:, inner:inner + 1, :], x.shape).reshape(c, HG_DK)
        x = b.reshape(c // SUBLANES, SUBLANES, HG_DK)
        sub = _iota(x.shape, 1)
        out = None
        for p in range(SUBLANES // blk):
            r = p * blk + inner
            piece = jnp.broadcast_to(x[:, r:r + 1, :], x.shape)
            out = piece if out is None else jnp.where(sub >= p * blk, piece, out)
        return out.reshape(c, HG_DK)

    att = jnp.zeros((c, c), F32)
    hs = HG_CHUNK // 2
    while hs >= 1:
        blk = 2 * hs
        decay = jnp.exp(-jnp.abs(b - ref_rows(hs)))
        in_block = row & (blk - 1)
        later_half = (in_block < hs) if reverse else (in_block >= hs)
        q_l = jnp.where(later_half, qs * decay, 0.0)
        k_l = jnp.where(later_half, 0.0, kk * decay)
        a_l = _dot_nt(q_l, k_l)
        att = att + (a_l if blk == c else jnp.where(block_xor < blk, a_l, 0.0))
        hs //= 2
    o = _dot(att, v) + jnp.sum(qs * kk, axis=-1, keepdims=True) * v

    q_dec = qs * jnp.exp(b)
    b3 = b.reshape(n_sub, HG_CHUNK, HG_DK)
    end = 0 if reverse else HG_CHUNK - 1
    b_end = jnp.broadcast_to(b3[:, end:end + 1, :], b3.shape).reshape(c, HG_DK)
    k_dec = kk * jnp.exp(b_end - b)
    outs = [None] * n_sub
    for s in (reversed(range(n_sub)) if reverse else range(n_sub)):
        rows = slice(s * HG_CHUNK, (s + 1) * HG_CHUNK)
        outs[s] = o[rows] + _dot_nt(q_dec[rows], st)
        st = st * jnp.exp(b_end[s * HG_CHUNK:s * HG_CHUNK + 1, :]) + _dot_tn(v[rows], k_dec[rows])
    return jnp.concatenate(outs, axis=0), st


def _hgrn_pass(proj, hl, layer, reverse, col_q, col_f, col_v, final_args=None):
    bsz, n, _ = proj.shape
    nchunk = n // TOK_BLK
    width = HG_HEADS * HG_DK
    cq, cf, cv = col_q // width, col_f // width, col_v // width

    def chunk(t):
        return jnp.where(t == 0, 0, nchunk - t) if reverse else t

    def blk(col):
        return pl.BlockSpec((1, TOK_BLK, width), lambda b, t: (b, chunk(t), col))

    in_specs = [blk(cq), blk(cf), blk(cv),
                pl.BlockSpec((1,) + hl.shape[1:], lambda b, t: (1 if reverse else 0, 0, 0))]
    args = [proj, proj, proj, hl]
    final = final_args is not None
    if final:
        col_g, prev, norm_w = final_args
        in_specs += [blk(col_g // width), blk(0), pl.BlockSpec((1, HG_DK), lambda b, t: (0, 0))]
        args += [proj, prev, norm_w.reshape(1, HG_DK)]
    return pl.pallas_call(
        functools.partial(_hgrn_kernel, layer=layer, reverse=reverse, final=final),
        grid=(bsz, nchunk),
        in_specs=in_specs,
        out_specs=blk(0),
        out_shape=jax.ShapeDtypeStruct((bsz, n, width), F32),
        scratch_shapes=[pltpu.VMEM((HG_HEADS, HG_DK, HG_DK), F32)],
        compiler_params=_cparams(2), name="hgrn_bwd" if reverse else "hgrn_fwd",
    )(*args)


def _merge_kernel(x_ref, ya_ref, yb_ref, yc_ref, ga_ref, gb_ref, gc_ref, mod_ref,
                  wa_ref, wb_ref, wc_ref, wo_ref, o_ref, *, n_ctx, ctx_row):
    b, i = pl.program_id(0), pl.program_id(1)
    tm, d = x_ref.shape[1:]
    gate_of = lambda ref: _sigmoid(ref[0].astype(F32))
    merged = (gate_of(ga_ref) * _dot(ya_ref[0], wa_ref[...])
              + gate_of(gb_ref) * _dot(yb_ref[0], wb_ref[...])
              + gate_of(gc_ref) * _dot(yc_ref[0], wc_ref[...]))
    mix = _dot(merged, wo_ref[...])
    gate = _mod_rows(mod_ref, 2, b, i * tm, n_ctx, ctx_row, tm, d)
    o_ref[0] = x_ref[0] + gate * mix


def _merge(xa, ya, yb, yc, proj, mod_l, w_pa, w_pb, w_pc, w_out, col_gate, n_ctx, ctx_row):
    bsz, n, d = xa.shape
    tm = 704 if n % 704 == 0 else TOK_BLK
    g0 = col_gate // d
    tile = lambda w: pl.BlockSpec((1, tm, w), lambda b, i: (b, i, 0))
    gate = lambda k: pl.BlockSpec((1, tm, d), lambda b, i: (b, i, g0 + k))
    full = lambda w: pl.BlockSpec(w.shape, lambda b, i: (0, 0))
    return pl.pallas_call(
        functools.partial(_merge_kernel, n_ctx=n_ctx, ctx_row=ctx_row),
        grid=(bsz, n // tm),
        in_specs=[tile(d), tile(ya.shape[2]), tile(yb.shape[2]), tile(yc.shape[2]),
                  gate(0), gate(1), gate(2), full(mod_l),
                  full(w_pa), full(w_pb), full(w_pc), full(w_out)],
        out_specs=tile(d),
        out_shape=jax.ShapeDtypeStruct((bsz, n, d), F32),
        compiler_params=_cparams(2), name="merge",
    )(xa, ya, yb, yc, proj, proj, proj, mod_l, w_pa, w_pb, w_pc, w_out)


def _route_kernel(x_ref, mod_ref, nw_ref, wr_ref, br_ref, h_ref, idx_ref, wt_ref, cnt_ref,
                  meta_ref, carry_ref, *, n_ctx, ctx_row):
    b, i = pl.program_id(0), pl.program_id(1)
    tm, d = x_ref.shape[1:]

    @pl.when((b == 0) & (i == 0))
    def _():
        carry_ref[...] = jnp.zeros_like(carry_ref)

    h = _rms(x_ref[0], nw_ref[...])
    shift = _mod_rows(mod_ref, 3, b, i * tm, n_ctx, ctx_row, tm, d)
    scale = _mod_rows(mod_ref, 4, b, i * tm, n_ctx, ctx_row, tm, d)
    h = h * (1.0 + scale) + shift
    h_ref[0] = h

    logits = _dot3_nt(wr_ref[...], h)
    ex = jnp.exp(logits - jnp.max(logits, axis=0, keepdims=True))
    probs = ex / jnp.sum(ex, axis=0, keepdims=True)
    sel = probs + br_ref[...]

    def row(x, r):
        return x[r:r + 1, :]

    best = None
    g_idx = None
    for g in range(N_GROUPS):
        r0 = g * EXPERTS_PER_GROUP
        a0, a1, a2, a3 = (row(sel, r0 + j) for j in range(EXPERTS_PER_GROUP))
        hi1, lo1 = jnp.maximum(a0, a1), jnp.minimum(a0, a1)
        hi2, lo2 = jnp.maximum(a2, a3), jnp.minimum(a2, a3)
        score = jnp.maximum(hi1, hi2) + jnp.maximum(jnp.minimum(hi1, hi2), jnp.maximum(lo1, lo2))
        if g == 0:
            best, g_idx = score, jnp.zeros_like(score, dtype=I32)
        else:
            better = score > best
            best = jnp.where(better, score, best)
            g_idx = jnp.where(better, g, g_idx)

    def pick(x, j):
        out = row(x, j)
        for g in range(1, N_GROUPS):
            out = jnp.where(g_idx == g, row(x, g * EXPERTS_PER_GROUP + j), out)
        return out

    in_grp = [pick(sel, j) for j in range(EXPERTS_PER_GROUP)]
    in_prob = [pick(probs, j) for j in range(EXPERTS_PER_GROUP)]

    def first_argmax(vals, skip):
        bv, bi, bp = None, None, None
        for j in range(EXPERTS_PER_GROUP):
            v = vals[j] if skip is None else jnp.where(skip == j, -jnp.inf, vals[j])
            if bv is None:
                bv, bi, bp = v, jnp.zeros_like(g_idx), in_prob[0]
            else:
                better = v > bv
                bv = jnp.where(better, v, bv)
                bi = jnp.where(better, j, bi)
                bp = jnp.where(better, in_prob[j], bp)
        return bi, bp

    loc0, p0 = first_argmax(in_grp, None)
    loc1, p1 = first_argmax(in_grp, loc0)
    e0 = g_idx * EXPERTS_PER_GROUP + loc0
    e1 = g_idx * EXPERTS_PER_GROUP + loc1
    wsum = p0 + p1
    w0, w1 = p0 / wsum, p1 / wsum

    er = _iota((N_EXPERTS, tm), 0)
    hit0 = er == e0
    hit1 = er == e1
    hot = jnp.where(hit0 | hit1, 1.0, 0.0)
    upper = jnp.where(_iota((tm, tm), 0) < _iota((tm, tm), 1), 1.0, 0.0)
    before = _dot(hot, upper)
    cnt = jnp.sum(hot, axis=1, keepdims=True)
    e_col = _iota((N_EXPERTS, 1), 0)

    def slot_starts(align):
        padded = jnp.floor((cnt + (align - 1)) * (1.0 / align)) * align
        off = jnp.zeros((N_EXPERTS, 1), F32)
        for e in range(N_EXPERTS - 1):
            off = off + jnp.where(e_col > e, padded[e:e + 1, :], 0.0)
        return padded, off

    cnt_pad, off_d = slot_starts(SUBLANES)
    _, off_c = slot_starts(MOE_CHUNK)
    positions = []
    for off in (off_d, off_c):
        for hit in (hit0, hit1):
            positions.append(jnp.sum(jnp.where(hit, before + off, 0.0), axis=0, keepdims=True))

    lane = _iota((N_EXPERTS, LANES), 1)
    carry = carry_ref[...]
    meta = jnp.zeros((N_EXPERTS, LANES), F32)
    for field, val in enumerate((carry, cnt, off_d, off_c)):
        meta = jnp.where(lane == field, val, meta)
    meta_ref[0] = meta.astype(I32)
    carry_ref[...] = carry + cnt_pad
    cnt_ref[...] = carry + cnt_pad

    idx_ref[...] = jnp.zeros_like(idx_ref)
    wt_ref[...] = jnp.zeros_like(wt_ref)
    for r, val in enumerate(positions):
        idx_ref[r:r + 1, :] = val.astype(I32)
    for r, val in enumerate((w0, w1)):
        wt_ref[r:r + 1, :] = val


def _route(x1, mod_l, norm_w, w_router_t, b_router, n_ctx, ctx_row):
    bsz, n, d = x1.shape
    tm = TOK_BLK
    nt = n // tm
    tile = pl.BlockSpec((1, tm, d), lambda b, i: (b, i, 0))
    lane_tile = pl.BlockSpec((SUBLANES, tm), lambda b, i: (0, b * nt + i))
    full = lambda a: pl.BlockSpec(a.shape, lambda b, i: (0,) * a.ndim)
    br = b_router.reshape(N_EXPERTS, 1)
    nw = norm_w.reshape(1, d)
    return pl.pallas_call(
        functools.partial(_route_kernel, n_ctx=n_ctx, ctx_row=ctx_row),
        grid=(bsz, nt),
        in_specs=[tile, full(mod_l), full(nw), full(w_router_t), full(br)],
        out_specs=[tile, lane_tile, lane_tile,
                   pl.BlockSpec((N_EXPERTS, LANES), lambda b, i: (0, 0)),
                   pl.BlockSpec((1, N_EXPERTS, LANES), lambda b, i: (b * nt + i, 0, 0))],
        out_shape=[jax.ShapeDtypeStruct((bsz, n, d), F32),
                   jax.ShapeDtypeStruct((SUBLANES, bsz * n), I32),
                   jax.ShapeDtypeStruct((SUBLANES, bsz * n), F32),
                   jax.ShapeDtypeStruct((N_EXPERTS, LANES), F32),
                   jax.ShapeDtypeStruct((bsz * nt, N_EXPERTS, LANES), I32)],
        scratch_shapes=[pltpu.VMEM((N_EXPERTS, LANES), F32)],
        compiler_params=_cparams(2), name="route",
    )(x1, mod_l, nw, w_router_t, br)


def _round_up(x, m):
    return -(-x // m) * m


MOE_MAX_CHUNKS = TOK_BLK // MOE_CHUNK
MOE_SORT_ROWS = _round_up(2 * TOK_BLK + N_EXPERTS * (SUBLANES - 1) + MOE_CHUNK - 1, LANES)
MOE_GATHER_ROWS = _round_up(2 * TOK_BLK + N_EXPERTS * (MOE_CHUNK - 1), LANES)
META_FIELDS = 4


def _moe_blocks(n_assign, n_tiles):
    slack = N_EXPERTS * (n_tiles * (SUBLANES - 1) + MOE_CHUNK - 1)
    return -(-(n_assign + slack) // MOE_BLK) + N_EXPERTS


def _chunk_copies(meta_ref, start_ref, tile, slot_field, do, make_copy):
    def expert(e, carry):
        base = (tile * N_EXPERTS + e) * META_FIELDS
        before, rows, slot = meta_ref[base], meta_ref[base + 1], meta_ref[base + slot_field]
        for c in range(MOE_MAX_CHUNKS):
            @pl.when(rows > c * MOE_CHUNK)
            def _():
                do(make_copy(pl.multiple_of(slot + c * MOE_CHUNK, SUBLANES),
                             pl.multiple_of(start_ref[e] + before + c * MOE_CHUNK, SUBLANES)))
        return carry
    lax.fori_loop(0, N_EXPERTS, expert, 0)


def _dispatch_kernel(meta_ref, cnt_ref, pos_ref, h_ref, xs_ref, blk_e_ref, blk_rows_ref,
                     start_ref, sort_ref, zero_ref, sem, *, nblk):
    t = pl.program_id(0)
    nt = pl.num_programs(0)
    tm = h_ref.shape[0]

    @pl.when(t == 0)
    def _():
        def expert(e, end):
            start_ref[e] = end
            return end + ((cnt_ref[e] + MOE_CHUNK - 1 + MOE_BLK - 1) // MOE_BLK) * MOE_BLK
        start_ref[N_EXPERTS] = lax.fori_loop(0, N_EXPERTS, expert, 0)

        def block(j, carry):
            def count(e, acc):
                return acc + jnp.where(start_ref[e + 1] <= j * MOE_BLK, 1, 0)
            e = jnp.minimum(lax.fori_loop(0, N_EXPERTS, count, 0), N_EXPERTS - 1)
            blk_e_ref[j] = e
            blk_rows_ref[j] = jnp.clip(cnt_ref[e] - (j * MOE_BLK - start_ref[e]), 0, MOE_BLK)
            return carry
        lax.fori_loop(0, nblk, block, 0)

    r = _iota((MOE_SORT_ROWS, tm), 0)
    onehot = jnp.where((r == pos_ref[0:1, :]) | (r == pos_ref[1:2, :]), 1.0, 0.0)
    buf = t % 2
    sort_ref[buf] = _dot(onehot, h_ref[...])

    def copies(which):
        def make_copy(src_row, dst_row):
            return pltpu.make_async_copy(sort_ref.at[which, pl.ds(src_row, MOE_CHUNK), :],
                                         xs_ref.at[pl.ds(dst_row, MOE_CHUNK), :], sem.at[which])
        return make_copy

    @pl.when(t > 0)
    def _():
        _chunk_copies(meta_ref, start_ref, t - 1, 2, lambda cp: cp.wait(), copies(1 - buf))

    _chunk_copies(meta_ref, start_ref, t, 2, lambda cp: cp.start(), copies(buf))

    @pl.when(t == nt - 1)
    def _():
        _chunk_copies(meta_ref, start_ref, t, 2, lambda cp: cp.wait(), copies(buf))

        zero_ref[...] = jnp.zeros_like(zero_ref)

        def zero_copy(row, size):
            return pltpu.make_async_copy(zero_ref.at[pl.ds(0, size), :],
                                         xs_ref.at[pl.ds(pl.multiple_of(row, SUBLANES), size), :],
                                         sem.at[buf])

        def fill(do):
            def expert(e, carry):
                lo = start_ref[e] + cnt_ref[e]
                gap = start_ref[e + 1] - lo
                n_big = gap // MOE_CHUNK

                def big(k, c2):
                    do(zero_copy(lo + k * MOE_CHUNK, MOE_CHUNK))
                    return c2
                lax.fori_loop(0, n_big, big, 0)

                def small(k, c2):
                    do(zero_copy(lo + n_big * MOE_CHUNK + k * SUBLANES, SUBLANES))
                    return c2
                lax.fori_loop(0, (gap - n_big * MOE_CHUNK) // SUBLANES, small, 0)
                return carry
            lax.fori_loop(0, N_EXPERTS, expert, 0)

            def unused(k, carry):
                do(zero_copy(start_ref[N_EXPERTS] + k * MOE_ZERO_ROWS, MOE_ZERO_ROWS))
                return carry
            lax.fori_loop(0, (nblk * MOE_BLK - start_ref[N_EXPERTS]) // MOE_ZERO_ROWS, unused, 0)

        fill(lambda cp: cp.start())
        fill(lambda cp: cp.wait())


def _dispatch(meta, counts, pos_rows, h_flat, nblk):
    ntok, d = h_flat.shape
    tm = TOK_BLK
    whole_smem = pl.BlockSpec(memory_space=pltpu.SMEM)
    return pl.pallas_call(
        functools.partial(_dispatch_kernel, nblk=nblk),
        grid=(ntok // tm,),
        in_specs=[whole_smem, whole_smem,
                  pl.BlockSpec((SUBLANES, tm), lambda i: (0, i)),
                  pl.BlockSpec((tm, d), lambda i: (i, 0))],
        out_specs=[pl.BlockSpec(memory_space=pl.ANY), whole_smem, whole_smem, whole_smem],
        out_shape=[jax.ShapeDtypeStruct((nblk * MOE_BLK, d), F32),
                   jax.ShapeDtypeStruct((nblk,), I32),
                   jax.ShapeDtypeStruct((nblk,), I32),
                   jax.ShapeDtypeStruct((N_EXPERTS + 1,), I32)],
        scratch_shapes=[pltpu.VMEM((2, MOE_SORT_ROWS, d), F32), pltpu.VMEM((MOE_ZERO_ROWS, d), F32),
                        pltpu.SemaphoreType.DMA((2,))],
        compiler_params=_cparams(1), name="dispatch",
    )(meta, counts, pos_rows, h_flat)


def _expert_kernel(blk_e_ref, blk_rows_ref, x_ref, wg_ref, wu_ref, wd_ref, o_ref, g_bf, u_bf, d_bf):
    i = pl.program_id(0)
    rows = blk_rows_ref[i]

    @pl.when((i == 0) | (blk_e_ref[i] != blk_e_ref[jnp.maximum(i - 1, 0)]))
    def _():
        g_bf[...] = wg_ref[0, 0].astype(BF16)
        u_bf[...] = wu_ref[0, 0].astype(BF16)
        d_bf[...] = wd_ref[0, 0].astype(BF16)

    @pl.when(rows > 0)
    def _():
        x = x_ref[...].astype(BF16)
        hid = _silu(_dot(x, g_bf[...])) * _dot(x, u_bf[...])
        o_ref[...] = _dot(hid, d_bf[...])

    @pl.when(rows == 0)
    def _():
        o_ref[...] = jnp.zeros_like(o_ref)


def _experts(blk_e, blk_rows, xs, w_gate, w_up, w_down, layer):
    d = xs.shape[1]
    ff = w_gate.shape[3]
    nblk = blk_e.shape[0]
    weight = lambda shape: pl.BlockSpec((1, 1) + shape, lambda i, be, rows: (layer, be[i], 0, 0))
    return pl.pallas_call(
        _expert_kernel,
        grid_spec=pltpu.PrefetchScalarGridSpec(
            num_scalar_prefetch=2, grid=(nblk,),
            in_specs=[pl.BlockSpec((MOE_BLK, d), lambda i, be, rows: (i, 0)),
                      weight((d, ff)), weight((d, ff)), weight((ff, d))],
            out_specs=pl.BlockSpec((MOE_BLK, d), lambda i, be, rows: (i, 0)),
            scratch_shapes=[pltpu.VMEM((d, ff), BF16), pltpu.VMEM((d, ff), BF16),
                            pltpu.VMEM((ff, d), BF16)]),
        out_shape=jax.ShapeDtypeStruct(xs.shape, F32),
        compiler_params=_cparams(1), name="experts",
    )(blk_e, blk_rows, xs, w_gate, w_up, w_down)


def _combine_kernel(meta_ref, start_ref, x_ref, pos_ref, wt_ref, mod_ref, ys_ref, o_ref,
                    gath_ref, sem, *, n_ctx, ctx_row):
    b, i = pl.program_id(0), pl.program_id(1)
    nt = pl.num_programs(1)
    t = b * nt + i
    last = pl.num_programs(0) * nt - 1
    tm, d = x_ref.shape[1:]
    buf = t % 2

    def copies(which):
        def make_copy(slot_row, ys_row):
            return pltpu.make_async_copy(ys_ref.at[pl.ds(ys_row, MOE_CHUNK), :],
                                         gath_ref.at[which, pl.ds(slot_row, MOE_CHUNK), :],
                                         sem.at[which])
        return make_copy

    @pl.when(t == 0)
    def _():
        gath_ref[...] = jnp.zeros_like(gath_ref)
        _chunk_copies(meta_ref, start_ref, t, 3, lambda cp: cp.start(), copies(buf))

    @pl.when(t < last)
    def _():
        _chunk_copies(meta_ref, start_ref, t + 1, 3, lambda cp: cp.start(), copies(1 - buf))

    _chunk_copies(meta_ref, start_ref, t, 3, lambda cp: cp.wait(), copies(buf))

    col = _iota((tm, MOE_GATHER_ROWS), 1)
    pos = pos_ref[...]
    wt = wt_ref[...]
    sel = (jnp.where(col == pos[:, 0:1], wt[:, 0:1], 0.0)
           + jnp.where(col == pos[:, 1:2], wt[:, 1:2], 0.0))
    hi, lo = _hi_lo(sel)
    rows = gath_ref[buf].astype(BF16)
    y = (jnp.dot(hi, rows, preferred_element_type=F32)
         + jnp.dot(lo, rows, preferred_element_type=F32))
    gate = _mod_rows(mod_ref, 5, b, i * tm, n_ctx, ctx_row, tm, d)
    o_ref[0] = x_ref[0] + gate * y


def _combine(meta, start, x1, pos_cols, wt_cols, mod_l, ys, n_ctx, ctx_row, latent_only):
    bsz, n, d = x1.shape
    tm = TOK_BLK
    nt = n // tm
    whole_smem = pl.BlockSpec(memory_space=pltpu.SMEM)
    cols = pl.BlockSpec((tm, 2), lambda b, i: (b * nt + i, 0))
    if latent_only:
        assert n_ctx == tm
        out_spec = pl.BlockSpec((1, tm, d), lambda b, i: (b, jnp.maximum(i - 1, 0), 0))
        out_rows = n - n_ctx
    else:
        out_spec = pl.BlockSpec((1, tm, d), lambda b, i: (b, i, 0))
        out_rows = n
    return pl.pallas_call(
        functools.partial(_combine_kernel, n_ctx=n_ctx, ctx_row=ctx_row),
        grid=(bsz, nt),
        in_specs=[whole_smem, whole_smem,
                  pl.BlockSpec((1, tm, d), lambda b, i: (b, i, 0)), cols, cols,
                  pl.BlockSpec(mod_l.shape, lambda b, i: (0, 0)),
                  pl.BlockSpec(memory_space=pl.ANY)],
        out_specs=out_spec,
        out_shape=jax.ShapeDtypeStruct((bsz, out_rows, d), F32),
        scratch_shapes=[pltpu.VMEM((2, MOE_GATHER_ROWS, d), F32), pltpu.SemaphoreType.DMA((2,))],
        compiler_params=_cparams(2), name="combine",
    )(meta, start, x1, pos_cols, wt_cols, mod_l, ys)


def _layer(xa, mod_l, layer, n_ctx, ctx_row, p, rope):
    bsz, n, d = xa.shape
    na_w = NA_HEADS * HEAD_DIM
    hg_w = HG_HEADS * HG_DK
    wa_qw = WA_HEADS * HEAD_DIM
    wa_kvw = WA_KV_HEADS * HEAD_DIM
    hg0 = 3 * na_w
    wa0 = hg0 + 5 * hg_w
    gate0 = wa0 + wa_qw + 2 * wa_kvw
    n_cols = gate0 + N_BRANCHES * d
    assert p["w_in"].shape[2] == n_cols
    w_att = _take_cols(p["w_in"], layer, [(gate0, n_cols), (0, hg0), (wa0, gate0)])
    w_hg = _take_cols(p["w_in"], layer, [(hg0, wa0)])
    col_gate = 0
    col_na = [N_BRANCHES * d + k * na_w for k in range(3)]
    col_wq = N_BRANCHES * d + 3 * na_w
    col_wk = col_wq + wa_qw
    col_wv = col_wk + wa_kvw
    col_hg = [k * hg_w for k in range(5)]
    proj, h = _inproj(xa, mod_l, p["norm1"], w_att, n_ctx, ctx_row, 768, BF16)
    proj_hg = _project(h, w_hg, 1280, F32)

    ya = _na_attention(proj, _na_bias(p["na_rpb"]), p["na_q_norm"], p["na_k_norm"], *col_na)
    yc = _wa_attention(proj, p["wa_sink"], p["wa_q_norm"], p["wa_k_norm"], rope[0], rope[1],
                       col_wq, col_wk, col_wv)
    o_f = _hgrn_pass(proj_hg, p["hg_lower"], layer, False, col_hg[0], col_hg[1], col_hg[3])
    yb = _hgrn_pass(proj_hg, p["hg_lower"], layer, True, col_hg[0], col_hg[2], col_hg[3],
                    final_args=(col_hg[4], o_f, p["hg_norm"]))

    bf = lambda w: w.astype(BF16)
    x1 = _merge(xa, ya, yb, yc, proj, mod_l, bf(p["w_pa"]), bf(p["w_pb"]), bf(p["w_pc"]),
                bf(p["w_out"]), col_gate, n_ctx, ctx_row)

    h2, idx, wts, counts, meta = _route(x1, mod_l, p["norm2"], p["w_router"].T, p["b_router"],
                                        n_ctx, ctx_row)
    ntok = bsz * n
    meta_flat = meta[:, :, :META_FIELDS].reshape(-1)
    xs, blk_e, blk_rows, start = _dispatch(meta_flat, counts[:, 0].astype(I32), idx,
                                           h2.reshape(ntok, d),
                                           _moe_blocks(2 * ntok, ntok // TOK_BLK))
    ys = _experts(blk_e, blk_rows, xs, p["w_gate"], p["w_up"], p["w_down"], layer)
    return _combine(meta_flat, start, x1, idx[2:4].T, wts[:2].T, mod_l, ys, n_ctx, ctx_row,
                    latent_only=p["last"])


def kernel(x, c, ctx, c_ctx, w_ada, b_ada, norm1, norm2, w_in, na_q_norm, na_k_norm, na_rpb, hg_lower,
           hg_norm, wa_q_norm, wa_k_norm, wa_sink, w_pa, w_pb, w_pc, w_out, w_router, b_router,
           w_gate, w_up, w_down):
    bsz, t, d = x.shape
    n_ctx = ctx.shape[1]
    depth = w_ada.shape[0]
    assert n_ctx == TOK_BLK and t % TOK_BLK == 0 and t // TOK_BLK >= 3
    assert t % GRID_W == 0 and bsz + 1 <= SUBLANES

    xa = jnp.concatenate([ctx, x], axis=1)
    cond = jnp.concatenate([c, c_ctx[None], jnp.zeros((SUBLANES - bsz - 1, d), F32)], axis=0)
    mod = _ada(cond, w_ada, b_ada)
    rope = _rope_tables(n_ctx, t)
    for l in range(depth):
        p = dict(norm1=norm1[l], norm2=norm2[l], w_in=w_in, na_q_norm=na_q_norm[l],
                 na_k_norm=na_k_norm[l], na_rpb=na_rpb[l], hg_lower=hg_lower, hg_norm=hg_norm[l],
                 wa_q_norm=wa_q_norm[l], wa_k_norm=wa_k_norm[l], wa_sink=wa_sink[l],
                 w_pa=w_pa[l], w_pb=w_pb[l], w_pc=w_pc[l], w_out=w_out[l],
                 w_router=w_router, b_router=b_router, w_gate=w_gate, w_up=w_up,
                 w_down=w_down, last=l == depth - 1)
        xa = _layer(xa, mod[l], l, n_ctx, bsz, p, rope)
    return xa
```

```python
import functools

import numpy as np
import jax
import jax.numpy as jnp
from jax import lax
from jax.experimental import pallas as pl
from jax.experimental.pallas import tpu as pltpu

F32 = jnp.float32
BF16 = jnp.bfloat16
I32 = jnp.int32

GRID_W = 64
NA_HEADS = 8
NA_ROWS = 8
NA_COLS = 16
HEAD_DIM = 64
HG_HEADS = 4
HG_DK = 128
LOG_FLOOR = 1e-30
WA_HEADS = 8
WA_KV_HEADS = 2
WA_WINDOW = 128
ROPE_THETA = 10000.0
N_EXPERTS = 16
N_GROUPS = 4
EXPERTS_PER_GROUP = N_EXPERTS // N_GROUPS
N_BRANCHES = 3
NEG_INF = -1e30
RMS_EPS = 1e-6

LANES = 128
SUBLANES = 8
VMEM_LIMIT = 56 * 1024 * 1024

TOK_BLK = 256
MOE_BLK = 1024
MOE_CHUNK = 32
MOE_ZERO_ROWS = 256


def _cparams(n_axes):
    return pltpu.CompilerParams(
        dimension_semantics=("arbitrary",) * n_axes, vmem_limit_bytes=VMEM_LIMIT)


def _sigmoid(x):
    return 1.0 / (1.0 + jnp.exp(-x))


def _silu(x):
    return x * _sigmoid(x)


def _dot(a, b):
    return jnp.dot(a.astype(BF16), b.astype(BF16), preferred_element_type=F32)


def _dot_nt(a, b):
    return lax.dot_general(a.astype(BF16), b.astype(BF16), (((1,), (1,)), ((), ())),
                           preferred_element_type=F32)


def _dot_tn(a, b):
    return lax.dot_general(a.astype(BF16), b.astype(BF16), (((0,), (0,)), ((), ())),
                           preferred_element_type=F32)


def _hi_lo(a):
    hi = a.astype(BF16)
    lo = (a - hi.astype(F32)).astype(BF16)
    return hi, lo


def _dot3(a, b):
    ah, al = _hi_lo(a)
    bh, bl = _hi_lo(b)
    d = functools.partial(jnp.dot, preferred_element_type=F32)
    return d(ah, bh) + (d(ah, bl) + d(al, bh))


def _dot3_nt(a, b):
    ah, al = _hi_lo(a)
    bh, bl = _hi_lo(b)
    d = functools.partial(lax.dot_general, dimension_numbers=(((1,), (1,)), ((), ())),
                          preferred_element_type=F32)
    return d(ah, bh) + (d(ah, bl) + d(al, bh))


def _rms(x, w):
    return x * lax.rsqrt(jnp.mean(x * x, axis=-1, keepdims=True) + RMS_EPS) * w


def _iota(shape, dim):
    return lax.broadcasted_iota(I32, shape, dim)


def _ada_kernel(cond_ref, w_ref, b_ref, o_ref):
    o_ref[0] = _dot3(_silu(cond_ref[...]), w_ref[0]) + b_ref[0]


def _ada(cond, w_ada, b_ada):
    depth, d, d6 = w_ada.shape
    rows = cond.shape[0]
    tn = d6 // 4
    return pl.pallas_call(
        _ada_kernel,
        grid=(depth, d6 // tn),
        in_specs=[pl.BlockSpec((rows, d), lambda l, j: (0, 0)),
                  pl.BlockSpec((1, d, tn), lambda l, j: (l, 0, j)),
                  pl.BlockSpec((1, 1, tn), lambda l, j: (l, 0, j))],
        out_specs=pl.BlockSpec((1, rows, tn), lambda l, j: (l, 0, j)),
        out_shape=jax.ShapeDtypeStruct((depth, rows, d6), F32),
        compiler_params=_cparams(2), name="ada",
    )(cond, w_ada, b_ada.reshape(depth, 1, d6))


def _mod_rows(mod_ref, which, b, row0, n_ctx, ctx_row, tm, d):
    lat = mod_ref[pl.ds(b, 1), which * d:(which + 1) * d]
    cx = mod_ref[ctx_row:ctx_row + 1, which * d:(which + 1) * d]
    row = row0 + _iota((tm, d), 0)
    return jnp.where(row < n_ctx, cx, lat)


def _inproj_kernel(x_ref, mod_ref, nw_ref, w_ref, o_ref, h_ref, *, n_ctx, ctx_row):
    b, i, j = pl.program_id(0), pl.program_id(1), pl.program_id(2)
    tm, d = h_ref.shape[1:]

    @pl.when(j == 0)
    def _():
        h = _rms(x_ref[0], nw_ref[...])
        shift = _mod_rows(mod_ref, 0, b, i * tm, n_ctx, ctx_row, tm, d)
        scale = _mod_rows(mod_ref, 1, b, i * tm, n_ctx, ctx_row, tm, d)
        h_ref[0] = (h * (1.0 + scale) + shift).astype(BF16)

    o_ref[0] = jnp.dot(h_ref[0], w_ref[...], preferred_element_type=F32).astype(o_ref.dtype)


def _matmul_kernel(h_ref, w_ref, o_ref):
    o_ref[0] = jnp.dot(h_ref[0], w_ref[...], preferred_element_type=F32).astype(o_ref.dtype)


def _project(h, w, tn, out_dtype):
    bsz, n, d = h.shape
    cols = w.shape[1]
    tm = 1408 if n % 1408 == 0 else TOK_BLK
    return pl.pallas_call(
        _matmul_kernel,
        grid=(bsz, n // tm, cols // tn),
        in_specs=[pl.BlockSpec((1, tm, d), lambda b, i, j: (b, i, 0)),
                  pl.BlockSpec((d, tn), lambda b, i, j: (0, j))],
        out_specs=pl.BlockSpec((1, tm, tn), lambda b, i, j: (b, i, j)),
        out_shape=jax.ShapeDtypeStruct((bsz, n, cols), out_dtype),
        compiler_params=_cparams(3), name="project",
    )(h, w)


W_IN_COL_BLK = 256


def _take_cols_kernel(src_ref, w_ref, o_ref):
    del src_ref
    o_ref[...] = w_ref[0].astype(BF16)


def _take_cols(w_in, layer, col_ranges):
    d = w_in.shape[1]
    blk = W_IN_COL_BLK
    src = []
    for start, stop in col_ranges:
        assert start % blk == 0 and stop % blk == 0
        src += list(range(start // blk, stop // blk))
    return pl.pallas_call(
        _take_cols_kernel,
        grid_spec=pltpu.PrefetchScalarGridSpec(
            num_scalar_prefetch=1, grid=(len(src),),
            in_specs=[pl.BlockSpec((1, d, blk), lambda j, src: (layer, 0, src[j]))],
            out_specs=pl.BlockSpec((d, blk), lambda j, src: (0, j))),
        out_shape=jax.ShapeDtypeStruct((d, len(src) * blk), BF16),
        compiler_params=_cparams(1), name="take_cols",
    )(jnp.asarray(src, I32), w_in)


def _inproj(xa, mod_l, norm_w, w, n_ctx, ctx_row, tn, out_dtype):
    bsz, n, d = xa.shape
    cols = w.shape[1]
    tm = 1408 if n % 1408 == 0 else TOK_BLK
    return pl.pallas_call(
        functools.partial(_inproj_kernel, n_ctx=n_ctx, ctx_row=ctx_row),
        grid=(bsz, n // tm, cols // tn),
        in_specs=[pl.BlockSpec((1, tm, d), lambda b, i, j: (b, i, 0)),
                  pl.BlockSpec(mod_l.shape, lambda b, i, j: (0, 0)),
                  pl.BlockSpec((1, d), lambda b, i, j: (0, 0)),
                  pl.BlockSpec((d, tn), lambda b, i, j: (0, j))],
        out_specs=[pl.BlockSpec((1, tm, tn), lambda b, i, j: (b, i, j)),
                   pl.BlockSpec((1, tm, d), lambda b, i, j: (b, i, 0))],
        out_shape=[jax.ShapeDtypeStruct((bsz, n, cols), out_dtype),
                   jax.ShapeDtypeStruct((bsz, n, d), BF16)],
        compiler_params=_cparams(3), name="inproj",
    )(xa, mod_l, norm_w.reshape(1, d), w)


def _pair_norm(x, w):
    lane = _iota(x.shape, 1)
    lo = lane < HEAD_DIM
    sq = x * x
    s_lo = jnp.sum(jnp.where(lo, sq, 0.0), axis=-1, keepdims=True)
    s_hi = jnp.sum(jnp.where(lo, 0.0, sq), axis=-1, keepdims=True)
    ms = jnp.where(lo, s_lo, s_hi) * (1.0 / HEAD_DIM)
    return x * lax.rsqrt(ms + RMS_EPS) * w


LOG2E = 1.4426950408889634


def _ones_beside(v, value_lanes):
    return jnp.where(value_lanes, v, 1.0).astype(BF16)


def _softmax_av(s_loc, s_ctx, v_loc, v_ctx, sink):
    m = jnp.maximum(jnp.max(s_loc, axis=-1, keepdims=True), jnp.max(s_ctx, axis=-1, keepdims=True))
    if sink is not None:
        m = jnp.maximum(m, sink)
    p_loc = jnp.exp2(s_loc - m).astype(BF16)
    p_ctx = jnp.exp2(s_ctx - m).astype(BF16)
    acc = (jnp.dot(p_loc, v_loc, preferred_element_type=F32)
           + jnp.dot(p_ctx, v_ctx, preferred_element_type=F32))
    den = pltpu.roll(acc, LANES // 2, 1)
    if sink is not None:
        den = den + jnp.exp2(sink - m)
    return acc / den


NA_QROWS = TOK_BLK // GRID_W
NA_KROWS = 3 * NA_QROWS
N_RPB_COLS = 2 * NA_COLS - 1
N_RPB_ROWS = 2 * NA_ROWS - 1
NA_VARIANTS = 4


def _na_row_valid(variant, a, k):
    if variant == 0:
        return False
    if variant == 1:
        return NA_QROWS <= k < NA_QROWS + NA_ROWS
    if variant == 3:
        return k < NA_ROWS
    return a <= k < a + NA_ROWS


def _na_bias_kernel(rpb_ref, o_ref):
    h = pl.program_id(0)
    qc = _iota((GRID_W, GRID_W), 0)
    kc = _iota((GRID_W, GRID_W), 1)
    col_lo = jnp.clip(qc - NA_COLS // 2, 0, GRID_W - NA_COLS)
    col_ok = (kc >= col_lo) & (kc < col_lo + NA_COLS)
    d_col = kc - qc + (NA_COLS - 1)
    masked = jnp.full((GRID_W, GRID_W), NEG_INF, F32)
    blocks = []
    for dr in range(N_RPB_ROWS):
        acc = jnp.zeros((GRID_W, GRID_W), F32)
        for dc in range(N_RPB_COLS):
            val = rpb_ref[(h * N_RPB_ROWS + dr) * N_RPB_COLS + dc]
            acc = jnp.where(d_col == dc, val, acc)
        blocks.append(jnp.where(col_ok, acc * LOG2E, NEG_INF))
    for variant in range(NA_VARIANTS):
        for a in range(NA_QROWS):
            for k in range(NA_KROWS):
                dr = k - NA_QROWS - a + NA_ROWS - 1
                ok = _na_row_valid(variant, a, k) and 0 <= dr < N_RPB_ROWS
                o_ref[variant, 0, a * GRID_W:(a + 1) * GRID_W, k * GRID_W:(k + 1) * GRID_W] = (
                    blocks[dr] if ok else masked)


def _na_bias(rpb):
    heads = rpb.shape[0]
    return pl.pallas_call(
        _na_bias_kernel,
        grid=(heads,),
        in_specs=[pl.BlockSpec(memory_space=pltpu.SMEM)],
        out_specs=pl.BlockSpec((NA_VARIANTS, 1, TOK_BLK, 3 * TOK_BLK), lambda h: (0, h, 0, 0)),
        out_shape=jax.ShapeDtypeStruct((NA_VARIANTS, heads, TOK_BLK, 3 * TOK_BLK), F32),
        compiler_params=_cparams(1), name="na_bias",
    )(rpb.reshape(-1))


def _na_kernel(q_ref, kp_ref, kc_ref, kn_ref, vp_ref, vc_ref, vn_ref, kx_ref, vx_ref,
               bias_ref, qn_ref, kn_w_ref, o_ref):
    tq = q_ref.shape[1]
    lo = _iota((tq, LANES), 1) < HEAD_DIM
    lo_k = _iota((3 * tq, LANES), 1) < HEAD_DIM
    qn_w = qn_ref[...] * (HEAD_DIM ** -0.5 * LOG2E)
    for p in range(NA_HEADS // 2):
        pair = slice(p * LANES, (p + 1) * LANES)
        f32 = lambda ref: ref[0, :, pair].astype(F32)
        q = _pair_norm(f32(q_ref), qn_w)
        k_loc = _pair_norm(jnp.concatenate([f32(kp_ref), f32(kc_ref), f32(kn_ref)], axis=0),
                           kn_w_ref[...]).astype(BF16)
        k_ctx = _pair_norm(f32(kx_ref), kn_w_ref[...]).astype(BF16)
        v_loc = jnp.concatenate([f32(vp_ref), f32(vc_ref), f32(vn_ref)], axis=0)
        v_ctx = f32(vx_ref)
        q2 = jnp.concatenate([jnp.where(lo, q, 0.0), jnp.where(lo, 0.0, q)], axis=0).astype(BF16)
        s_loc2 = _dot_nt(q2, k_loc)
        s_ctx2 = _dot_nt(q2, k_ctx)
        outs = []
        for h in range(2):
            mine = lo if h == 0 else ~lo
            mine_k = lo_k if h == 0 else ~lo_k
            rows = slice(h * tq, (h + 1) * tq)
            outs.append(_softmax_av(s_loc2[rows] + bias_ref[0, 2 * p + h], s_ctx2[rows],
                                    _ones_beside(v_loc, mine_k), _ones_beside(v_ctx, mine), None))
        o_ref[0, :, pair] = jnp.where(lo, outs[0], outs[1]).astype(o_ref.dtype)


def _na_attention(proj, bias, qn_w, kn_w, col_q, col_k, col_v):
    bsz, n, _ = proj.shape
    nblk = n // TOK_BLK
    last = nblk - 1
    width = NA_HEADS * HEAD_DIM
    cq, ck, cv = col_q // width, col_k // width, col_v // width

    def blk(col, shift):
        return pl.BlockSpec((1, TOK_BLK, width),
                            lambda b, i: (b, jnp.clip(i + shift, 0, last), col))

    def ctx_blk(col):
        return pl.BlockSpec((1, TOK_BLK, width), lambda b, i: (b, 0, col))

    def variant(b, i):
        return (jnp.where(i == 0, 0, jnp.where(i == 1, 1, jnp.where(i == last, 3, 2))), 0, 0, 0)

    vec = pl.BlockSpec((1, LANES), lambda b, i: (0, 0))
    return pl.pallas_call(
        _na_kernel,
        grid=(bsz, nblk),
        in_specs=[blk(cq, 0), blk(ck, -1), blk(ck, 0), blk(ck, 1), blk(cv, -1), blk(cv, 0), blk(cv, 1),
                  ctx_blk(ck), ctx_blk(cv),
                  pl.BlockSpec((1, NA_HEADS, TOK_BLK, 3 * TOK_BLK), variant), vec, vec],
        out_specs=pl.BlockSpec((1, TOK_BLK, width), lambda b, i: (b, i, 0)),
        out_shape=jax.ShapeDtypeStruct((bsz, n, width), BF16),
        compiler_params=_cparams(2), name="na_attn",
    )(proj, proj, proj, proj, proj, proj, proj, proj, proj, bias,
      jnp.tile(qn_w, 2).reshape(1, LANES), jnp.tile(kn_w, 2).reshape(1, LANES))


def _rope(x, cos, sin_signed):
    lane = _iota(x.shape, 1)
    first = (lane & (HEAD_DIM // 2 - 1)) < HEAD_DIM // 4
    quarter = HEAD_DIM // 4
    partner = jnp.where(first, pltpu.roll(x, LANES - quarter, 1), pltpu.roll(x, quarter, 1))
    return x * cos + partner * sin_signed


def _wa_kernel(sink_ref, q_ref, kp_ref, kc_ref, kn_ref, vp_ref, vc_ref, vn_ref, kx_ref, vx_ref,
               cq_ref, sq_ref, cp_ref, sp_ref, cn_ref, sn_ref, qn_ref, kn_w_ref, o_ref, *, last):
    i = pl.program_id(1)
    tq = q_ref.shape[1]
    tn = kp_ref.shape[1]
    n_loc = tq + 2 * tn
    f32 = lambda ref: ref[0].astype(F32)
    cos_loc = jnp.concatenate([cp_ref[...], cq_ref[...], cn_ref[...]], axis=0)
    sin_loc = jnp.concatenate([sp_ref[...], sq_ref[...], sn_ref[...]], axis=0)
    k_loc = _pair_norm(jnp.concatenate([f32(kp_ref), f32(kc_ref), f32(kn_ref)], axis=0),
                       kn_w_ref[...])
    k_loc = _rope(k_loc, cos_loc, sin_loc).astype(BF16)
    k_ctx = _pair_norm(f32(kx_ref), kn_w_ref[...]).astype(BF16)
    v_loc = jnp.concatenate([f32(vp_ref), f32(vc_ref), f32(vn_ref)], axis=0)
    v_ctx = f32(vx_ref)
    half = LANES // 2
    lo = _iota((tq, LANES), 1) < HEAD_DIM
    lo_loc = _iota((n_loc, LANES), 1) < HEAD_DIM
    qn_w = qn_ref[...] * (HEAD_DIM ** -0.5 * LOG2E)

    qi = _iota((tq, n_loc), 0)
    kj = _iota((tq, n_loc), 1)
    rel = kj - tn - qi
    lo_col = jnp.where(i >= 2, 0, tn)
    hi_col = jnp.where(i >= 1, jnp.where(i < last, n_loc, tn + tq), 0)
    ok = (rel >= -WA_WINDOW) & (rel <= WA_WINDOW) & (kj >= lo_col) & (kj < hi_col)
    mask = jnp.where(ok, 0.0, NEG_INF)

    group = WA_HEADS // WA_KV_HEADS
    mask_g = jnp.concatenate([mask] * group, axis=0)
    q_pairs = [_rope(_pair_norm(q_ref[0, :, p * LANES:(p + 1) * LANES].astype(F32), qn_w),
                     cq_ref[...], sq_ref[...]) for p in range(WA_HEADS // 2)]
    outs = [None] * WA_HEADS
    for kv in range(WA_KV_HEADS):
        mine, mine_loc = (lo, lo_loc) if kv == 0 else (~lo, ~lo_loc)
        heads = range(kv * group, (kv + 1) * group)
        stack = []
        for h in heads:
            q = q_pairs[h // 2]
            q = q if h % 2 == kv else pltpu.roll(q, half, 1)
            stack.append(jnp.where(mine, q, 0.0).astype(BF16))
        q_g = jnp.concatenate(stack, axis=0)
        sink_g = jnp.concatenate([jnp.full((tq, 1), sink_ref[h] * LOG2E, F32) for h in heads], axis=0)
        o_g = _softmax_av(_dot_nt(q_g, k_loc) + mask_g, _dot_nt(q_g, k_ctx),
                          _ones_beside(v_loc, mine_loc), _ones_beside(v_ctx, mine), sink_g)
        for n, h in enumerate(heads):
            o = o_g[n * tq:(n + 1) * tq]
            outs[h] = o if h % 2 == kv else pltpu.roll(o, half, 1)
    for p in range(WA_HEADS // 2):
        o_ref[0, :, p * LANES:(p + 1) * LANES] = jnp.where(
            lo, outs[2 * p], outs[2 * p + 1]).astype(o_ref.dtype)


def _wa_attention(proj, sink, qn_w, kn_w, cos_t, sin_t, col_q, col_k, col_v):
    bsz, n, _ = proj.shape
    nblk = n // TOK_BLK
    last = nblk - 1
    qw = WA_HEADS * HEAD_DIM
    cq, ck, cv = col_q // qw, col_k // LANES, col_v // LANES

    assert TOK_BLK % WA_WINDOW == 0
    per = TOK_BLK // WA_WINDOW

    def near(i, shift):
        if shift == 0:
            return i
        return jnp.clip(per * i - 1 if shift < 0 else per * (i + 1), 0, per * nblk - 1)

    def blk(col, shift):
        rows = TOK_BLK if shift == 0 else WA_WINDOW
        return pl.BlockSpec((1, rows, LANES), lambda b, i: (b, near(i, shift), col))

    def tab(shift):
        rows = TOK_BLK if shift == 0 else WA_WINDOW
        return pl.BlockSpec((rows, LANES), lambda b, i: (near(i, shift), 0))

    def ctx_blk(col):
        return pl.BlockSpec((1, TOK_BLK, LANES), lambda b, i: (b, 0, col))

    vec = pl.BlockSpec((1, LANES), lambda b, i: (0, 0))
    return pl.pallas_call(
        functools.partial(_wa_kernel, last=last),
        grid=(bsz, nblk),
        in_specs=[pl.BlockSpec(memory_space=pltpu.SMEM),
                  pl.BlockSpec((1, TOK_BLK, qw), lambda b, i: (b, i, cq)),
                  blk(ck, -1), blk(ck, 0), blk(ck, 1), blk(cv, -1), blk(cv, 0), blk(cv, 1),
                  ctx_blk(ck), ctx_blk(cv),
                  tab(0), tab(0), tab(-1), tab(-1), tab(1), tab(1), vec, vec],
        out_specs=pl.BlockSpec((1, TOK_BLK, qw), lambda b, i: (b, i, 0)),
        out_shape=jax.ShapeDtypeStruct((bsz, n, qw), BF16),
        compiler_params=_cparams(2), name="wa_attn",
    )(sink, proj, proj, proj, proj, proj, proj, proj, proj, proj,
      cos_t, sin_t, cos_t, sin_t, cos_t, sin_t,
      jnp.tile(qn_w, 2).reshape(1, LANES), jnp.tile(kn_w, 2).reshape(1, LANES))


def _rope_tables(n_ctx, t):
    quarter = HEAD_DIM // 4
    inv_freq = ROPE_THETA ** (-jnp.arange(quarter, dtype=F32) / quarter)
    pos = jnp.arange(t)
    lane = np.arange(LANES)
    in_head = lane % HEAD_DIM
    use_col = in_head >= HEAD_DIM // 2
    second = (in_head % (HEAD_DIM // 2)) >= quarter
    freq = inv_freq[in_head % quarter]
    p = jnp.where(use_col[None, :], (pos % GRID_W)[:, None], (pos // GRID_W)[:, None]).astype(F32)
    ang = p * freq[None, :]
    cos = jnp.cos(ang)
    sin = jnp.where(second[None, :], jnp.sin(ang), -jnp.sin(ang))
    cos = jnp.concatenate([jnp.ones((n_ctx, LANES), F32), cos], axis=0)
    sin = jnp.concatenate([jnp.zeros((n_ctx, LANES), F32), sin], axis=0)
    return cos, sin


HG_CHUNK = 128


def _hgrn_kernel(q_ref, f_ref, v_ref, hl_ref, *rest, layer, reverse, final):
    if final:
        g_ref, prev_ref, nw_ref, o_ref, st_ref = rest
    else:
        o_ref, st_ref = rest
    @pl.when(pl.program_id(1) == 0)
    def _():
        st_ref[...] = jnp.zeros_like(st_ref)

    a = hl_ref[0]
    e = jnp.exp(a - jnp.max(a, axis=0, keepdims=True))
    pr = e / jnp.sum(e, axis=0, keepdims=True)
    lb_all = jnp.zeros((1, a.shape[1]), F32)
    for j in range(1, layer + 1):
        lb_all = lb_all + pr[j:j + 1]

    for h in range(HG_HEADS):
        lanes = slice(h * HG_DK, (h + 1) * HG_DK)
        lb = lb_all[:, lanes]
        qs = _silu(q_ref[0, :, lanes].astype(F32))
        f = lb + (1.0 - lb) * _sigmoid(f_ref[0, :, lanes])
        o, st = _hgrn_block(qs, 1.0 - f, jnp.log(jnp.maximum(f, LOG_FLOOR)),
                            v_ref[0, :, lanes].astype(F32), st_ref[h], reverse)
        st_ref[h] = st
        if final:
            tot = prev_ref[0, :, lanes] + o
            gated = _rms(tot, nw_ref[...]) * _silu(g_ref[0, :, lanes].astype(F32))
            o_ref[0, :, lanes] = gated.astype(o_ref.dtype)
        else:
            o_ref[0, :, lanes] = o


def _hgrn_block(qs, kk, g, v, st, reverse):
    c = qs.shape[0]
    n_sub = c // HG_CHUNK
    row = _iota((c, HG_DK), 0)
    ri = _iota((c, c), 0)
    ci = _iota((c, c), 1)
    block_xor = ri ^ ci

    tri = jnp.where(((ci >= ri) if reverse else (ci <= ri)) & (block_xor < HG_CHUNK), 1.0, 0.0)
    tri = tri.astype(BF16)
    g_hi = g.astype(BF16)
    rest = g - g_hi.astype(F32)
    g_mid = rest.astype(BF16)
    g_lo = (rest - g_mid.astype(F32)).astype(BF16)
    tdot = functools.partial(jnp.dot, preferred_element_type=F32)
    b = tdot(tri, g_hi) + (tdot(tri, g_mid) + tdot(tri, g_lo))

    def ref_rows(hs):
        blk = 2 * hs
        inner = hs if reverse else hs - 1
        if blk >= SUBLANES:
            x = b.reshape(c // blk, blk, HG_DK)
            return jnp.broadcast_to(x[:, inner:inner + 1, :], x.shape).reshape(c, HG_DK)
        x = b.reshape(c // SUBLANES, SUBLANES, HG_DK)
        sub = _iota(x.shape, 1)
        out = None
        for p in range(SUBLANES // blk):
            r = p * blk + inner
            piece = jnp.broadcast_to(x[:, r:r + 1, :], x.shape)
            out = piece if out is None else jnp.where(sub >= p * blk, piece, out)
        return out.reshape(c, HG_DK)

    att = jnp.zeros((c, c), F32)
    hs = HG_CHUNK // 2
    while hs >= 1:
        blk = 2 * hs
        decay = jnp.exp(-jnp.abs(b - ref_rows(hs)))
        in_block = row & (blk - 1)
        later_half = (in_block < hs) if reverse else (in_block >= hs)
        q_l = jnp.where(later_half, qs * decay, 0.0)
        k_l = jnp.where(later_half, 0.0, kk * decay)
        a_l = _dot_nt(q_l, k_l)
        att = att + (a_l if blk == c else jnp.where(block_xor < blk, a_l, 0.0))
        hs //= 2
    o = _dot(att, v) + jnp.sum(qs * kk, axis=-1, keepdims=True) * v

    q_dec = qs * jnp.exp(b)
    b3 = b.reshape(n_sub, HG_CHUNK, HG_DK)
    end = 0 if reverse else HG_CHUNK - 1
    b_end = jnp.broadcast_to(b3[:, end:end + 1, :], b3.shape).reshape(c, HG_DK)
    k_dec = kk * jnp.exp(b_end - b)
    outs = [None] * n_sub
    for s in (reversed(range(n_sub)) if reverse else range(n_sub)):
        rows = slice(s * HG_CHUNK, (s + 1) * HG_CHUNK)
        outs[s] = o[rows] + _dot_nt(q_dec[rows], st)
        st = st * jnp.exp(b_end[s * HG_CHUNK:s * HG_CHUNK + 1, :]) + _dot_tn(v[rows], k_dec[rows])
    return jnp.concatenate(outs, axis=0), st


def _hgrn_pass(hl, layer, reverse, q_src, f_src, v_src, final_args=None):
    bsz, n, _ = q_src[0].shape
    nchunk = n // TOK_BLK
    width = HG_HEADS * HG_DK

    def chunk(t):
        return jnp.where(t == 0, 0, nchunk - t) if reverse else t

    def blk(col):
        return pl.BlockSpec((1, TOK_BLK, width), lambda b, t: (b, chunk(t), col // width))

    in_specs = [blk(q_src[1]), blk(f_src[1]), blk(v_src[1]),
                pl.BlockSpec((1,) + hl.shape[1:], lambda b, t: (1 if reverse else 0, 0, 0))]
    args = [q_src[0], f_src[0], v_src[0], hl]
    final = final_args is not None
    if final:
        g_src, prev, norm_w = final_args
        in_specs += [blk(g_src[1]), blk(0), pl.BlockSpec((1, HG_DK), lambda b, t: (0, 0))]
        args += [g_src[0], prev, norm_w.reshape(1, HG_DK)]
    return pl.pallas_call(
        functools.partial(_hgrn_kernel, layer=layer, reverse=reverse, final=final),
        grid=(bsz, nchunk),
        in_specs=in_specs,
        out_specs=blk(0),
        out_shape=jax.ShapeDtypeStruct((bsz, n, width), BF16 if final else F32),
        scratch_shapes=[pltpu.VMEM((HG_HEADS, HG_DK, HG_DK), F32)],
        compiler_params=_cparams(2), name="hgrn_bwd" if reverse else "hgrn_fwd",
    )(*args)


def _merge_kernel(x_ref, ya_ref, yb_ref, yc_ref, ga_ref, gb_ref, gc_ref, mod_ref,
                  wa_ref, wb_ref, wc_ref, wo_ref, o_ref, *, n_ctx, ctx_row):
    b, i = pl.program_id(0), pl.program_id(1)
    tm, d = x_ref.shape[1:]
    gate_of = lambda ref: _sigmoid(ref[0].astype(F32))
    merged = (gate_of(ga_ref) * _dot(ya_ref[0], wa_ref[...])
              + gate_of(gb_ref) * _dot(yb_ref[0], wb_ref[...])
              + gate_of(gc_ref) * _dot(yc_ref[0], wc_ref[...]))
    mix = _dot(merged, wo_ref[...])
    gate = _mod_rows(mod_ref, 2, b, i * tm, n_ctx, ctx_row, tm, d)
    o_ref[0] = x_ref[0] + gate * mix


def _merge(xa, ya, yb, yc, proj, mod_l, w_pa, w_pb, w_pc, w_out, col_gate, n_ctx, ctx_row):
    bsz, n, d = xa.shape
    tm = 704 if n % 704 == 0 else TOK_BLK
    g0 = col_gate // d
    tile = lambda w: pl.BlockSpec((1, tm, w), lambda b, i: (b, i, 0))
    gate = lambda k: pl.BlockSpec((1, tm, d), lambda b, i: (b, i, g0 + k))
    full = lambda w: pl.BlockSpec(w.shape, lambda b, i: (0, 0))
    return pl.pallas_call(
        functools.partial(_merge_kernel, n_ctx=n_ctx, ctx_row=ctx_row),
        grid=(bsz, n // tm),
        in_specs=[tile(d), tile(ya.shape[2]), tile(yb.shape[2]), tile(yc.shape[2]),
                  gate(0), gate(1), gate(2), full(mod_l),
                  full(w_pa), full(w_pb), full(w_pc), full(w_out)],
        out_specs=tile(d),
        out_shape=jax.ShapeDtypeStruct((bsz, n, d), F32),
        compiler_params=_cparams(2), name="merge",
    )(xa, ya, yb, yc, proj, proj, proj, mod_l, w_pa, w_pb, w_pc, w_out)


def _route_kernel(x_ref, mod_ref, nw_ref, wr_ref, br_ref, h_ref, idx_ref, wt_ref, cnt_ref,
                  meta_ref, carry_ref, *, n_ctx, ctx_row):
    b, i = pl.program_id(0), pl.program_id(1)
    tm, d = x_ref.shape[1:]

    @pl.when((b == 0) & (i == 0))
    def _():
        carry_ref[...] = jnp.zeros_like(carry_ref)

    h = _rms(x_ref[0], nw_ref[...])
    shift = _mod_rows(mod_ref, 3, b, i * tm, n_ctx, ctx_row, tm, d)
    scale = _mod_rows(mod_ref, 4, b, i * tm, n_ctx, ctx_row, tm, d)
    h = h * (1.0 + scale) + shift
    h_ref[0] = h.astype(h_ref.dtype)

    logits = _dot3_nt(wr_ref[...], h)
    ex = jnp.exp(logits - jnp.max(logits, axis=0, keepdims=True))
    probs = ex / jnp.sum(ex, axis=0, keepdims=True)
    sel = probs + br_ref[...]

    def row(x, r):
        return x[r:r + 1, :]

    best = None
    g_idx = None
    for g in range(N_GROUPS):
        r0 = g * EXPERTS_PER_GROUP
        a0, a1, a2, a3 = (row(sel, r0 + j) for j in range(EXPERTS_PER_GROUP))
        hi1, lo1 = jnp.maximum(a0, a1), jnp.minimum(a0, a1)
        hi2, lo2 = jnp.maximum(a2, a3), jnp.minimum(a2, a3)
        score = jnp.maximum(hi1, hi2) + jnp.maximum(jnp.minimum(hi1, hi2), jnp.maximum(lo1, lo2))
        if g == 0:
            best, g_idx = score, jnp.zeros_like(score, dtype=I32)
        else:
            better = score > best
            best = jnp.where(better, score, best)
            g_idx = jnp.where(better, g, g_idx)

    def pick(x, j):
        out = row(x, j)
        for g in range(1, N_GROUPS):
            out = jnp.where(g_idx == g, row(x, g * EXPERTS_PER_GROUP + j), out)
        return out

    in_grp = [pick(sel, j) for j in range(EXPERTS_PER_GROUP)]
    in_prob = [pick(probs, j) for j in range(EXPERTS_PER_GROUP)]

    def first_argmax(vals, skip):
        bv, bi, bp = None, None, None
        for j in range(EXPERTS_PER_GROUP):
            v = vals[j] if skip is None else jnp.where(skip == j, -jnp.inf, vals[j])
            if bv is None:
                bv, bi, bp = v, jnp.zeros_like(g_idx), in_prob[0]
            else:
                better = v > bv
                bv = jnp.where(better, v, bv)
                bi = jnp.where(better, j, bi)
                bp = jnp.where(better, in_prob[j], bp)
        return bi, bp

    loc0, p0 = first_argmax(in_grp, None)
    loc1, p1 = first_argmax(in_grp, loc0)
    e0 = g_idx * EXPERTS_PER_GROUP + loc0
    e1 = g_idx * EXPERTS_PER_GROUP + loc1
    wsum = p0 + p1
    w0, w1 = p0 / wsum, p1 / wsum

    er = _iota((N_EXPERTS, tm), 0)
    hit0 = er == e0
    hit1 = er == e1
    hot = jnp.where(hit0 | hit1, 1.0, 0.0)
    upper = jnp.where(_iota((tm, tm), 0) < _iota((tm, tm), 1), 1.0, 0.0)
    before = _dot(hot, upper)
    cnt = jnp.sum(hot, axis=1, keepdims=True)
    e_col = _iota((N_EXPERTS, 1), 0)

    def slot_starts(align):
        padded = jnp.floor((cnt + (align - 1)) * (1.0 / align)) * align
        off = jnp.zeros((N_EXPERTS, 1), F32)
        for e in range(N_EXPERTS - 1):
            off = off + jnp.where(e_col > e, padded[e:e + 1, :], 0.0)
        return padded, off

    cnt_pad, off_d = slot_starts(SUBLANES)
    _, off_c = slot_starts(MOE_CHUNK)
    positions = []
    for off in (off_d, off_c):
        for hit in (hit0, hit1):
            positions.append(jnp.sum(jnp.where(hit, before + off, 0.0), axis=0, keepdims=True))

    lane = _iota((N_EXPERTS, LANES), 1)
    carry = carry_ref[...]
    meta = jnp.zeros((N_EXPERTS, LANES), F32)
    for field, val in enumerate((carry, cnt, off_d, off_c)):
        meta = jnp.where(lane == field, val, meta)
    meta_ref[0] = meta.astype(I32)
    carry_ref[...] = carry + cnt_pad
    cnt_ref[...] = carry + cnt_pad

    idx_ref[...] = jnp.zeros_like(idx_ref)
    wt_ref[...] = jnp.zeros_like(wt_ref)
    for r, val in enumerate(positions):
        idx_ref[r:r + 1, :] = val.astype(I32)
    for r, val in enumerate((w0, w1)):
        wt_ref[r:r + 1, :] = val


def _route(x1, mod_l, norm_w, w_router_t, b_router, n_ctx, ctx_row):
    bsz, n, d = x1.shape
    tm = TOK_BLK
    nt = n // tm
    tile = pl.BlockSpec((1, tm, d), lambda b, i: (b, i, 0))
    lane_tile = pl.BlockSpec((SUBLANES, tm), lambda b, i: (0, b * nt + i))
    full = lambda a: pl.BlockSpec(a.shape, lambda b, i: (0,) * a.ndim)
    br = b_router.reshape(N_EXPERTS, 1)
    nw = norm_w.reshape(1, d)
    return pl.pallas_call(
        functools.partial(_route_kernel, n_ctx=n_ctx, ctx_row=ctx_row),
        grid=(bsz, nt),
        in_specs=[tile, full(mod_l), full(nw), full(w_router_t), full(br)],
        out_specs=[tile, lane_tile, lane_tile,
                   pl.BlockSpec((N_EXPERTS, LANES), lambda b, i: (0, 0)),
                   pl.BlockSpec((1, N_EXPERTS, LANES), lambda b, i: (b * nt + i, 0, 0))],
        out_shape=[jax.ShapeDtypeStruct((bsz, n, d), BF16),
                   jax.ShapeDtypeStruct((SUBLANES, bsz * n), I32),
                   jax.ShapeDtypeStruct((SUBLANES, bsz * n), F32),
                   jax.ShapeDtypeStruct((N_EXPERTS, LANES), F32),
                   jax.ShapeDtypeStruct((bsz * nt, N_EXPERTS, LANES), I32)],
        scratch_shapes=[pltpu.VMEM((N_EXPERTS, LANES), F32)],
        compiler_params=_cparams(2), name="route",
    )(x1, mod_l, nw, w_router_t, br)


def _round_up(x, m):
    return -(-x // m) * m


MOE_MAX_CHUNKS = TOK_BLK // MOE_CHUNK
MOE_SORT_ROWS = _round_up(2 * TOK_BLK + N_EXPERTS * (SUBLANES - 1) + MOE_CHUNK - 1, LANES)
MOE_GATHER_ROWS = _round_up(2 * TOK_BLK + N_EXPERTS * (MOE_CHUNK - 1), LANES)
META_FIELDS = 4


def _moe_blocks(n_assign, n_tiles):
    slack = N_EXPERTS * (n_tiles * (SUBLANES - 1) + MOE_CHUNK - 1)
    return -(-(n_assign + slack) // MOE_BLK) + N_EXPERTS


def _chunk_copies(meta_ref, start_ref, tile, slot_field, do, make_copy):
    def expert(e, carry):
        base = (tile * N_EXPERTS + e) * META_FIELDS
        before, rows, slot = meta_ref[base], meta_ref[base + 1], meta_ref[base + slot_field]
        for c in range(MOE_MAX_CHUNKS):
            @pl.when(rows > c * MOE_CHUNK)
            def _():
                do(make_copy(pl.multiple_of(slot + c * MOE_CHUNK, SUBLANES),
                             pl.multiple_of(start_ref[e] + before + c * MOE_CHUNK, SUBLANES)))
        return carry
    lax.fori_loop(0, N_EXPERTS, expert, 0)


def _dispatch_kernel(meta_ref, cnt_ref, pos_ref, h_ref, xs_ref, blk_e_ref, blk_rows_ref,
                     start_ref, sort_ref, zero_ref, sem, *, nblk):
    t = pl.program_id(0)
    nt = pl.num_programs(0)
    tm = h_ref.shape[0]

    @pl.when(t == 0)
    def _():
        def expert(e, end):
            start_ref[e] = end
            return end + ((cnt_ref[e] + MOE_CHUNK - 1 + MOE_BLK - 1) // MOE_BLK) * MOE_BLK
        start_ref[N_EXPERTS] = lax.fori_loop(0, N_EXPERTS, expert, 0)

        def block(j, carry):
            def count(e, acc):
                return acc + jnp.where(start_ref[e + 1] <= j * MOE_BLK, 1, 0)
            e = jnp.minimum(lax.fori_loop(0, N_EXPERTS, count, 0), N_EXPERTS - 1)
            blk_e_ref[j] = e
            blk_rows_ref[j] = jnp.clip(cnt_ref[e] - (j * MOE_BLK - start_ref[e]), 0, MOE_BLK)
            return carry
        lax.fori_loop(0, nblk, block, 0)

    r = _iota((MOE_SORT_ROWS, tm), 0)
    onehot = jnp.where((r == pos_ref[0:1, :]) | (r == pos_ref[1:2, :]), 1.0, 0.0)
    buf = t % 2
    sort_ref[buf] = _dot(onehot, h_ref[...])

    def copies(which):
        def make_copy(src_row, dst_row):
            return pltpu.make_async_copy(sort_ref.at[which, pl.ds(src_row, MOE_CHUNK), :],
                                         xs_ref.at[pl.ds(dst_row, MOE_CHUNK), :], sem.at[which])
        return make_copy

    @pl.when(t > 0)
    def _():
        _chunk_copies(meta_ref, start_ref, t - 1, 2, lambda cp: cp.wait(), copies(1 - buf))

    _chunk_copies(meta_ref, start_ref, t, 2, lambda cp: cp.start(), copies(buf))

    @pl.when(t == nt - 1)
    def _():
        _chunk_copies(meta_ref, start_ref, t, 2, lambda cp: cp.wait(), copies(buf))

        zero_ref[...] = jnp.zeros_like(zero_ref)

        def zero_copy(row, size):
            return pltpu.make_async_copy(zero_ref.at[pl.ds(0, size), :],
                                         xs_ref.at[pl.ds(pl.multiple_of(row, SUBLANES), size), :],
                                         sem.at[buf])

        def fill(do):
            def expert(e, carry):
                lo = start_ref[e] + cnt_ref[e]
                gap = start_ref[e + 1] - lo
                n_big = gap // MOE_CHUNK

                def big(k, c2):
                    do(zero_copy(lo + k * MOE_CHUNK, MOE_CHUNK))
                    return c2
                lax.fori_loop(0, n_big, big, 0)

                def small(k, c2):
                    do(zero_copy(lo + n_big * MOE_CHUNK + k * SUBLANES, SUBLANES))
                    return c2
                lax.fori_loop(0, (gap - n_big * MOE_CHUNK) // SUBLANES, small, 0)
                return carry
            lax.fori_loop(0, N_EXPERTS, expert, 0)

            def unused(k, carry):
                do(zero_copy(start_ref[N_EXPERTS] + k * MOE_ZERO_ROWS, MOE_ZERO_ROWS))
                return carry
            lax.fori_loop(0, (nblk * MOE_BLK - start_ref[N_EXPERTS]) // MOE_ZERO_ROWS, unused, 0)

        fill(lambda cp: cp.start())
        fill(lambda cp: cp.wait())


def _dispatch(meta, counts, pos_rows, h_flat, nblk):
    ntok, d = h_flat.shape
    tm = TOK_BLK
    whole_smem = pl.BlockSpec(memory_space=pltpu.SMEM)
    return pl.pallas_call(
        functools.partial(_dispatch_kernel, nblk=nblk),
        grid=(ntok // tm,),
        in_specs=[whole_smem, whole_smem,
                  pl.BlockSpec((SUBLANES, tm), lambda i: (0, i)),
                  pl.BlockSpec((tm, d), lambda i: (i, 0))],
        out_specs=[pl.BlockSpec(memory_space=pl.ANY), whole_smem, whole_smem, whole_smem],
        out_shape=[jax.ShapeDtypeStruct((nblk * MOE_BLK, d), F32),
                   jax.ShapeDtypeStruct((nblk,), I32),
                   jax.ShapeDtypeStruct((nblk,), I32),
                   jax.ShapeDtypeStruct((N_EXPERTS + 1,), I32)],
        scratch_shapes=[pltpu.VMEM((2, MOE_SORT_ROWS, d), F32), pltpu.VMEM((MOE_ZERO_ROWS, d), F32),
                        pltpu.SemaphoreType.DMA((2,))],
        compiler_params=_cparams(1), name="dispatch",
    )(meta, counts, pos_rows, h_flat)


def _expert_kernel(blk_e_ref, blk_rows_ref, x_ref, wg_ref, wu_ref, wd_ref, o_ref, g_bf, u_bf, d_bf):
    i = pl.program_id(0)
    rows = blk_rows_ref[i]

    @pl.when((i == 0) | (blk_e_ref[i] != blk_e_ref[jnp.maximum(i - 1, 0)]))
    def _():
        g_bf[...] = wg_ref[0, 0].astype(BF16)
        u_bf[...] = wu_ref[0, 0].astype(BF16)
        d_bf[...] = wd_ref[0, 0].astype(BF16)

    blk = x_ref.shape[0]
    sizes = [blk // 4, blk // 2, blk]
    for lower, size in zip([0] + sizes[:-1], sizes):
        @pl.when((rows > lower) & (rows <= size))
        def _():
            x = x_ref[0:size, :].astype(BF16)
            hid = _silu(_dot(x, g_bf[...])) * _dot(x, u_bf[...])
            o_ref[0:size, :] = _dot(hid, d_bf[...])
            if size < blk:
                o_ref[size:blk, :] = jnp.zeros((blk - size, o_ref.shape[1]), o_ref.dtype)

    @pl.when(rows == 0)
    def _():
        o_ref[...] = jnp.zeros_like(o_ref)


def _experts(blk_e, blk_rows, xs, w_gate, w_up, w_down, layer):
    d = xs.shape[1]
    ff = w_gate.shape[3]
    nblk = blk_e.shape[0]
    weight = lambda shape: pl.BlockSpec((1, 1) + shape, lambda i, be, rows: (layer, be[i], 0, 0))
    return pl.pallas_call(
        _expert_kernel,
        grid_spec=pltpu.PrefetchScalarGridSpec(
            num_scalar_prefetch=2, grid=(nblk,),
            in_specs=[pl.BlockSpec((MOE_BLK, d), lambda i, be, rows: (i, 0)),
                      weight((d, ff)), weight((d, ff)), weight((ff, d))],
            out_specs=pl.BlockSpec((MOE_BLK, d), lambda i, be, rows: (i, 0)),
            scratch_shapes=[pltpu.VMEM((d, ff), BF16), pltpu.VMEM((d, ff), BF16),
                            pltpu.VMEM((ff, d), BF16)]),
        out_shape=jax.ShapeDtypeStruct(xs.shape, F32),
        compiler_params=_cparams(1), name="experts",
    )(blk_e, blk_rows, xs, w_gate, w_up, w_down)


def _combine_kernel(meta_ref, start_ref, x_ref, pos_ref, wt_ref, mod_ref, ys_ref, o_ref,
                    gath_ref, sem, *, n_ctx, ctx_row):
    b, i = pl.program_id(0), pl.program_id(1)
    nt = pl.num_programs(1)
    t = b * nt + i
    last = pl.num_programs(0) * nt - 1
    tm, d = x_ref.shape[1:]
    buf = t % 2

    def copies(which):
        def make_copy(slot_row, ys_row):
            return pltpu.make_async_copy(ys_ref.at[pl.ds(ys_row, MOE_CHUNK), :],
                                         gath_ref.at[which, pl.ds(slot_row, MOE_CHUNK), :],
                                         sem.at[which])
        return make_copy

    @pl.when(t == 0)
    def _():
        gath_ref[...] = jnp.zeros_like(gath_ref)
        _chunk_copies(meta_ref, start_ref, t, 3, lambda cp: cp.start(), copies(buf))

    @pl.when(t < last)
    def _():
        _chunk_copies(meta_ref, start_ref, t + 1, 3, lambda cp: cp.start(), copies(1 - buf))

    _chunk_copies(meta_ref, start_ref, t, 3, lambda cp: cp.wait(), copies(buf))

    col = _iota((tm, MOE_GATHER_ROWS), 1)
    pos = pos_ref[...]
    wt = wt_ref[...]
    sel = (jnp.where(col == pos[:, 0:1], wt[:, 0:1], 0.0)
           + jnp.where(col == pos[:, 1:2], wt[:, 1:2], 0.0))
    hi, lo = _hi_lo(sel)
    rows = gath_ref[buf].astype(BF16)
    y = (jnp.dot(hi, rows, preferred_element_type=F32)
         + jnp.dot(lo, rows, preferred_element_type=F32))
    gate = _mod_rows(mod_ref, 5, b, i * tm, n_ctx, ctx_row, tm, d)
    o_ref[0] = x_ref[0] + gate * y


def _combine(meta, start, x1, pos_cols, wt_cols, mod_l, ys, n_ctx, ctx_row, latent_only):
    bsz, n, d = x1.shape
    tm = TOK_BLK
    nt = n // tm
    whole_smem = pl.BlockSpec(memory_space=pltpu.SMEM)
    cols = pl.BlockSpec((tm, 2), lambda b, i: (b * nt + i, 0))
    if latent_only:
        assert n_ctx == tm
        out_spec = pl.BlockSpec((1, tm, d), lambda b, i: (b, jnp.maximum(i - 1, 0), 0))
        out_rows = n - n_ctx
    else:
        out_spec = pl.BlockSpec((1, tm, d), lambda b, i: (b, i, 0))
        out_rows = n
    return pl.pallas_call(
        functools.partial(_combine_kernel, n_ctx=n_ctx, ctx_row=ctx_row),
        grid=(bsz, nt),
        in_specs=[whole_smem, whole_smem,
                  pl.BlockSpec((1, tm, d), lambda b, i: (b, i, 0)), cols, cols,
                  pl.BlockSpec(mod_l.shape, lambda b, i: (0, 0)),
                  pl.BlockSpec(memory_space=pl.ANY)],
        out_specs=out_spec,
        out_shape=jax.ShapeDtypeStruct((bsz, out_rows, d), F32),
        scratch_shapes=[pltpu.VMEM((2, MOE_GATHER_ROWS, d), F32), pltpu.SemaphoreType.DMA((2,))],
        compiler_params=_cparams(2), name="combine",
    )(meta, start, x1, pos_cols, wt_cols, mod_l, ys)


def _layer(xa, mod_l, layer, n_ctx, ctx_row, p, rope):
    bsz, n, d = xa.shape
    na_w = NA_HEADS * HEAD_DIM
    hg_w = HG_HEADS * HG_DK
    wa_qw = WA_HEADS * HEAD_DIM
    wa_kvw = WA_KV_HEADS * HEAD_DIM
    hg0 = 3 * na_w
    wa0 = hg0 + 5 * hg_w
    gate0 = wa0 + wa_qw + 2 * wa_kvw
    n_cols = gate0 + N_BRANCHES * d
    assert p["w_in"].shape[2] == n_cols
    w_att = _take_cols(p["w_in"], layer, [(gate0, n_cols), (0, hg0), (wa0, gate0)])
    w_hg_qig = _take_cols(p["w_in"], layer, [(hg0, hg0 + hg_w), (hg0 + 3 * hg_w, wa0)])
    w_hg_f = _take_cols(p["w_in"], layer, [(hg0 + hg_w, hg0 + 3 * hg_w)])
    col_gate = 0
    col_na = [N_BRANCHES * d + k * na_w for k in range(3)]
    col_wq = N_BRANCHES * d + 3 * na_w
    col_wk = col_wq + wa_qw
    col_wv = col_wk + wa_kvw
    proj, h = _inproj(xa, mod_l, p["norm1"], w_att, n_ctx, ctx_row, 768, BF16)
    hg_qig = _project(h, w_hg_qig, 768, BF16)
    hg_f = _project(h, w_hg_f, 1024, F32)

    ya = _na_attention(proj, _na_bias(p["na_rpb"]), p["na_q_norm"], p["na_k_norm"], *col_na)
    yc = _wa_attention(proj, p["wa_sink"], p["wa_q_norm"], p["wa_k_norm"], rope[0], rope[1],
                       col_wq, col_wk, col_wv)
    hg_q, hg_i, hg_g = ((hg_qig, k * hg_w) for k in range(3))
    o_f = _hgrn_pass(p["hg_lower"], layer, False, hg_q, (hg_f, 0), hg_i)
    yb = _hgrn_pass(p["hg_lower"], layer, True, hg_q, (hg_f, hg_w), hg_i,
                    final_args=(hg_g, o_f, p["hg_norm"]))

    bf = lambda w: w.astype(BF16)
    x1 = _merge(xa, ya, yb, yc, proj, mod_l, bf(p["w_pa"]), bf(p["w_pb"]), bf(p["w_pc"]),
                bf(p["w_out"]), col_gate, n_ctx, ctx_row)

    h2, idx, wts, counts, meta = _route(x1, mod_l, p["norm2"], p["w_router"].T, p["b_router"],
                                        n_ctx, ctx_row)
    ntok = bsz * n
    meta_flat = meta[:, :, :META_FIELDS].reshape(-1)
    xs, blk_e, blk_rows, start = _dispatch(meta_flat, counts[:, 0].astype(I32), idx,
                                           h2.reshape(ntok, d),
                                           _moe_blocks(2 * ntok, ntok // TOK_BLK))
    ys = _experts(blk_e, blk_rows, xs, p["w_gate"], p["w_up"], p["w_down"], layer)
    return _combine(meta_flat, start, x1, idx[2:4].T, wts[:2].T, mod_l, ys, n_ctx, ctx_row,
                    latent_only=p["last"])


def kernel(x, c, ctx, c_ctx, w_ada, b_ada, norm1, norm2, w_in, na_q_norm, na_k_norm, na_rpb, hg_lower,
           hg_norm, wa_q_norm, wa_k_norm, wa_sink, w_pa, w_pb, w_pc, w_out, w_router, b_router,
           w_gate, w_up, w_down):
    bsz, t, d = x.shape
    n_ctx = ctx.shape[1]
    depth = w_ada.shape[0]
    assert n_ctx == TOK_BLK and t % TOK_BLK == 0 and t // TOK_BLK >= 3
    assert t % GRID_W == 0 and bsz + 1 <= SUBLANES

    xa = jnp.concatenate([ctx, x], axis=1)
    cond = jnp.concatenate([c, c_ctx[None], jnp.zeros((SUBLANES - bsz - 1, d), F32)], axis=0)
    mod = _ada(cond, w_ada, b_ada)
    rope = _rope_tables(n_ctx, t)
    for l in range(depth):
        p = dict(norm1=norm1[l], norm2=norm2[l], w_in=w_in, na_q_norm=na_q_norm[l],
                 na_k_norm=na_k_norm[l], na_rpb=na_rpb[l], hg_lower=hg_lower, hg_norm=hg_norm[l],
                 wa_q_norm=wa_q_norm[l], wa_k_norm=wa_k_norm[l], wa_sink=wa_sink[l],
                 w_pa=w_pa[l], w_pb=w_pb[l], w_pc=w_pc[l], w_out=w_out[l],
                 w_router=w_router, b_router=b_router, w_gate=w_gate, w_up=w_up,
                 w_down=w_down, last=l == depth - 1)
        xa = _layer(xa, mod[l], l, n_ctx, bsz, p, rope)
    return xa
```

```python
import functools

import numpy as np
import jax
import jax.numpy as jnp
from jax import lax
from jax.experimental import pallas as pl
from jax.experimental.pallas import tpu as pltpu

F32 = jnp.float32
BF16 = jnp.bfloat16
I32 = jnp.int32

GRID_W = 64
NA_HEADS = 8
NA_ROWS = 8
NA_COLS = 16
HEAD_DIM = 64
HG_HEADS = 4
HG_DK = 128
LOG_FLOOR = 1e-30
WA_HEADS = 8
WA_KV_HEADS = 2
WA_WINDOW = 128
ROPE_THETA = 10000.0
N_EXPERTS = 16
N_GROUPS = 4
EXPERTS_PER_GROUP = N_EXPERTS // N_GROUPS
N_BRANCHES = 3
NEG_INF = -1e30
RMS_EPS = 1e-6

LANES = 128
SUBLANES = 8
VMEM_LIMIT = 56 * 1024 * 1024

TOK_BLK = 256
MOE_BLK = 1024
MOE_CHUNK = 32
MOE_ZERO_ROWS = 256


def _cparams(n_axes):
    return pltpu.CompilerParams(
        dimension_semantics=("arbitrary",) * n_axes, vmem_limit_bytes=VMEM_LIMIT)


def _sigmoid(x):
    return 1.0 / (1.0 + jnp.exp(-x))


def _silu(x):
    return x * _sigmoid(x)


def _dot(a, b):
    return jnp.dot(a.astype(BF16), b.astype(BF16), preferred_element_type=F32)


def _dot_nt(a, b):
    return lax.dot_general(a.astype(BF16), b.astype(BF16), (((1,), (1,)), ((), ())),
                           preferred_element_type=F32)


def _dot_tn(a, b):
    return lax.dot_general(a.astype(BF16), b.astype(BF16), (((0,), (0,)), ((), ())),
                           preferred_element_type=F32)


def _hi_lo(a):
    hi = a.astype(BF16)
    lo = (a - hi.astype(F32)).astype(BF16)
    return hi, lo


def _dot3(a, b):
    ah, al = _hi_lo(a)
    bh, bl = _hi_lo(b)
    d = functools.partial(jnp.dot, preferred_element_type=F32)
    return d(ah, bh) + (d(ah, bl) + d(al, bh))


def _dot3_nt(a, b):
    ah, al = _hi_lo(a)
    bh, bl = _hi_lo(b)
    d = functools.partial(lax.dot_general, dimension_numbers=(((1,), (1,)), ((), ())),
                          preferred_element_type=F32)
    return d(ah, bh) + (d(ah, bl) + d(al, bh))


def _rms(x, w):
    return x * lax.rsqrt(jnp.mean(x * x, axis=-1, keepdims=True) + RMS_EPS) * w


def _iota(shape, dim):
    return lax.broadcasted_iota(I32, shape, dim)


def _ada_kernel(cond_ref, w_ref, b_ref, o_ref):
    o_ref[0] = _dot3(_silu(cond_ref[...]), w_ref[0]) + b_ref[0]


def _ada(cond, w_ada, b_ada):
    depth, d, d6 = w_ada.shape
    rows = cond.shape[0]
    tn = d6 // 4
    return pl.pallas_call(
        _ada_kernel,
        grid=(depth, d6 // tn),
        in_specs=[pl.BlockSpec((rows, d), lambda l, j: (0, 0)),
                  pl.BlockSpec((1, d, tn), lambda l, j: (l, 0, j)),
                  pl.BlockSpec((1, 1, tn), lambda l, j: (l, 0, j))],
        out_specs=pl.BlockSpec((1, rows, tn), lambda l, j: (l, 0, j)),
        out_shape=jax.ShapeDtypeStruct((depth, rows, d6), F32),
        compiler_params=_cparams(2), name="ada",
    )(cond, w_ada, b_ada.reshape(depth, 1, d6))


def _mod_rows(mod_ref, which, b, row0, n_ctx, ctx_row, tm, d):
    lat = mod_ref[pl.ds(b, 1), which * d:(which + 1) * d]
    cx = mod_ref[ctx_row:ctx_row + 1, which * d:(which + 1) * d]
    row = row0 + _iota((tm, d), 0)
    return jnp.where(row < n_ctx, cx, lat)


def _inproj_kernel(x_ref, mod_ref, nw_ref, w_ref, o_ref, h_ref, *, n_ctx, ctx_row):
    b, i, j = pl.program_id(0), pl.program_id(1), pl.program_id(2)
    tm, d = h_ref.shape[1:]

    @pl.when(j == 0)
    def _():
        h = _rms(x_ref[0], nw_ref[...])
        shift = _mod_rows(mod_ref, 0, b, i * tm, n_ctx, ctx_row, tm, d)
        scale = _mod_rows(mod_ref, 1, b, i * tm, n_ctx, ctx_row, tm, d)
        h_ref[0] = (h * (1.0 + scale) + shift).astype(BF16)

    o_ref[0] = jnp.dot(h_ref[0], w_ref[...], preferred_element_type=F32).astype(o_ref.dtype)


def _matmul_kernel(h_ref, w_ref, o_ref):
    o_ref[0] = jnp.dot(h_ref[0], w_ref[...], preferred_element_type=F32).astype(o_ref.dtype)


def _project(h, w, tn, out_dtype):
    bsz, n, d = h.shape
    cols = w.shape[1]
    tm = 1408 if n % 1408 == 0 else TOK_BLK
    return pl.pallas_call(
        _matmul_kernel,
        grid=(bsz, n // tm, cols // tn),
        in_specs=[pl.BlockSpec((1, tm, d), lambda b, i, j: (b, i, 0)),
                  pl.BlockSpec((d, tn), lambda b, i, j: (0, j))],
        out_specs=pl.BlockSpec((1, tm, tn), lambda b, i, j: (b, i, j)),
        out_shape=jax.ShapeDtypeStruct((bsz, n, cols), out_dtype),
        compiler_params=_cparams(3), name="project",
    )(h, w)


W_IN_COL_BLK = 256


def _take_cols_kernel(src_ref, w_ref, o_ref):
    del src_ref
    o_ref[...] = w_ref[0].astype(BF16)


def _take_cols(w_in, layer, col_ranges):
    d = w_in.shape[1]
    blk = W_IN_COL_BLK
    src = []
    for start, stop in col_ranges:
        assert start % blk == 0 and stop % blk == 0
        src += list(range(start // blk, stop // blk))
    return pl.pallas_call(
        _take_cols_kernel,
        grid_spec=pltpu.PrefetchScalarGridSpec(
            num_scalar_prefetch=1, grid=(len(src),),
            in_specs=[pl.BlockSpec((1, d, blk), lambda j, src: (layer, 0, src[j]))],
            out_specs=pl.BlockSpec((d, blk), lambda j, src: (0, j))),
        out_shape=jax.ShapeDtypeStruct((d, len(src) * blk), BF16),
        compiler_params=_cparams(1), name="take_cols",
    )(jnp.asarray(src, I32), w_in)


def _inproj(xa, mod_l, norm_w, w, n_ctx, ctx_row, tn, out_dtype):
    bsz, n, d = xa.shape
    cols = w.shape[1]
    tm = 1408 if n % 1408 == 0 else TOK_BLK
    return pl.pallas_call(
        functools.partial(_inproj_kernel, n_ctx=n_ctx, ctx_row=ctx_row),
        grid=(bsz, n // tm, cols // tn),
        in_specs=[pl.BlockSpec((1, tm, d), lambda b, i, j: (b, i, 0)),
                  pl.BlockSpec(mod_l.shape, lambda b, i, j: (0, 0)),
                  pl.BlockSpec((1, d), lambda b, i, j: (0, 0)),
                  pl.BlockSpec((d, tn), lambda b, i, j: (0, j))],
        out_specs=[pl.BlockSpec((1, tm, tn), lambda b, i, j: (b, i, j)),
                   pl.BlockSpec((1, tm, d), lambda b, i, j: (b, i, 0))],
        out_shape=[jax.ShapeDtypeStruct((bsz, n, cols), out_dtype),
                   jax.ShapeDtypeStruct((bsz, n, d), BF16)],
        compiler_params=_cparams(3), name="inproj",
    )(xa, mod_l, norm_w.reshape(1, d), w)


def _pair_norm(x, w):
    lane = _iota(x.shape, 1)
    lo = lane < HEAD_DIM
    sq = x * x
    s_lo = jnp.sum(jnp.where(lo, sq, 0.0), axis=-1, keepdims=True)
    s_hi = jnp.sum(jnp.where(lo, 0.0, sq), axis=-1, keepdims=True)
    ms = jnp.where(lo, s_lo, s_hi) * (1.0 / HEAD_DIM)
    return x * lax.rsqrt(ms + RMS_EPS) * w


LOG2E = 1.4426950408889634


def _ones_beside(v, value_lanes):
    return jnp.where(value_lanes, v, 1.0).astype(BF16)


def _softmax_av(s_loc, s_ctx, v_loc, v_ctx, sink):
    m = jnp.maximum(jnp.max(s_loc, axis=-1, keepdims=True), jnp.max(s_ctx, axis=-1, keepdims=True))
    if sink is not None:
        m = jnp.maximum(m, sink)
    p_loc = jnp.exp2(s_loc - m).astype(BF16)
    p_ctx = jnp.exp2(s_ctx - m).astype(BF16)
    acc = (jnp.dot(p_loc, v_loc, preferred_element_type=F32)
           + jnp.dot(p_ctx, v_ctx, preferred_element_type=F32))
    den = pltpu.roll(acc, LANES // 2, 1)
    if sink is not None:
        den = den + jnp.exp2(sink - m)
    return acc / den


NA_QROWS = TOK_BLK // GRID_W
NA_KROWS = 3 * NA_QROWS
N_RPB_COLS = 2 * NA_COLS - 1
N_RPB_ROWS = 2 * NA_ROWS - 1
NA_VARIANTS = 4


def _na_row_valid(variant, a, k):
    if variant == 0:
        return False
    if variant == 1:
        return NA_QROWS <= k < NA_QROWS + NA_ROWS
    if variant == 3:
        return k < NA_ROWS
    return a <= k < a + NA_ROWS


def _na_bias_kernel(rpb_ref, o_ref):
    h = pl.program_id(0)
    qc = _iota((GRID_W, GRID_W), 0)
    kc = _iota((GRID_W, GRID_W), 1)
    col_lo = jnp.clip(qc - NA_COLS // 2, 0, GRID_W - NA_COLS)
    col_ok = (kc >= col_lo) & (kc < col_lo + NA_COLS)
    d_col = kc - qc + (NA_COLS - 1)
    masked = jnp.full((GRID_W, GRID_W), NEG_INF, F32)
    blocks = []
    for dr in range(N_RPB_ROWS):
        acc = jnp.zeros((GRID_W, GRID_W), F32)
        for dc in range(N_RPB_COLS):
            val = rpb_ref[(h * N_RPB_ROWS + dr) * N_RPB_COLS + dc]
            acc = jnp.where(d_col == dc, val, acc)
        blocks.append(jnp.where(col_ok, acc * LOG2E, NEG_INF))
    for variant in range(NA_VARIANTS):
        for a in range(NA_QROWS):
            for k in range(NA_KROWS):
                dr = k - NA_QROWS - a + NA_ROWS - 1
                ok = _na_row_valid(variant, a, k) and 0 <= dr < N_RPB_ROWS
                o_ref[variant, 0, a * GRID_W:(a + 1) * GRID_W, k * GRID_W:(k + 1) * GRID_W] = (
                    blocks[dr] if ok else masked)


def _na_bias(rpb):
    heads = rpb.shape[0]
    return pl.pallas_call(
        _na_bias_kernel,
        grid=(heads,),
        in_specs=[pl.BlockSpec(memory_space=pltpu.SMEM)],
        out_specs=pl.BlockSpec((NA_VARIANTS, 1, TOK_BLK, 3 * TOK_BLK), lambda h: (0, h, 0, 0)),
        out_shape=jax.ShapeDtypeStruct((NA_VARIANTS, heads, TOK_BLK, 3 * TOK_BLK), F32),
        compiler_params=_cparams(1), name="na_bias",
    )(rpb.reshape(-1))


def _na_kernel(q_ref, kp_ref, kc_ref, kn_ref, vp_ref, vc_ref, vn_ref, kx_ref, vx_ref,
               bias_ref, qn_ref, kn_w_ref, o_ref):
    tq = q_ref.shape[1]
    lo = _iota((tq, LANES), 1) < HEAD_DIM
    lo_k = _iota((3 * tq, LANES), 1) < HEAD_DIM
    qn_w = qn_ref[...] * (HEAD_DIM ** -0.5 * LOG2E)
    for b, p in [(b, p) for b in range(q_ref.shape[0]) for p in range(NA_HEADS // 2)]:
        pair = slice(p * LANES, (p + 1) * LANES)
        f32 = lambda ref: ref[b, :, pair].astype(F32)
        q = _pair_norm(f32(q_ref), qn_w)
        k_loc = _pair_norm(jnp.concatenate([f32(kp_ref), f32(kc_ref), f32(kn_ref)], axis=0),
                           kn_w_ref[...]).astype(BF16)
        k_ctx = _pair_norm(f32(kx_ref), kn_w_ref[...]).astype(BF16)
        v_loc = jnp.concatenate([f32(vp_ref), f32(vc_ref), f32(vn_ref)], axis=0)
        v_ctx = f32(vx_ref)
        q2 = jnp.concatenate([jnp.where(lo, q, 0.0), jnp.where(lo, 0.0, q)], axis=0).astype(BF16)
        s_loc2 = _dot_nt(q2, k_loc)
        s_ctx2 = _dot_nt(q2, k_ctx)
        outs = []
        for h in range(2):
            mine = lo if h == 0 else ~lo
            mine_k = lo_k if h == 0 else ~lo_k
            rows = slice(h * tq, (h + 1) * tq)
            outs.append(_softmax_av(s_loc2[rows] + bias_ref[0, 2 * p + h], s_ctx2[rows],
                                    _ones_beside(v_loc, mine_k), _ones_beside(v_ctx, mine), None))
        o_ref[b, :, pair] = jnp.where(lo, outs[0], outs[1]).astype(o_ref.dtype)


def _na_attention(proj, bias, qn_w, kn_w, col_q, col_k, col_v):
    bsz, n, _ = proj.shape
    nblk = n // TOK_BLK
    last = nblk - 1
    width = NA_HEADS * HEAD_DIM
    cq, ck, cv = col_q // width, col_k // width, col_v // width

    def blk(col, shift):
        return pl.BlockSpec((bsz, TOK_BLK, width),
                            lambda i: (0, jnp.clip(i + shift, 0, last), col))

    def ctx_blk(col):
        return pl.BlockSpec((bsz, TOK_BLK, width), lambda i: (0, 0, col))

    def variant(i):
        return (jnp.where(i == 0, 0, jnp.where(i == 1, 1, jnp.where(i == last, 3, 2))), 0, 0, 0)

    vec = pl.BlockSpec((1, LANES), lambda i: (0, 0))
    return pl.pallas_call(
        _na_kernel,
        grid=(nblk,),
        in_specs=[blk(cq, 0), blk(ck, -1), blk(ck, 0), blk(ck, 1), blk(cv, -1), blk(cv, 0), blk(cv, 1),
                  ctx_blk(ck), ctx_blk(cv),
                  pl.BlockSpec((1, NA_HEADS, TOK_BLK, 3 * TOK_BLK), variant), vec, vec],
        out_specs=pl.BlockSpec((bsz, TOK_BLK, width), lambda i: (0, i, 0)),
        out_shape=jax.ShapeDtypeStruct((bsz, n, width), BF16),
        compiler_params=_cparams(1), name="na_attn",
    )(proj, proj, proj, proj, proj, proj, proj, proj, proj, bias,
      jnp.tile(qn_w, 2).reshape(1, LANES), jnp.tile(kn_w, 2).reshape(1, LANES))


def _rope(x, cos, sin_signed):
    lane = _iota(x.shape, 1)
    first = (lane & (HEAD_DIM // 2 - 1)) < HEAD_DIM // 4
    quarter = HEAD_DIM // 4
    partner = jnp.where(first, pltpu.roll(x, LANES - quarter, 1), pltpu.roll(x, quarter, 1))
    return x * cos + partner * sin_signed


def _wa_kernel(sink_ref, q_ref, kp_ref, kc_ref, kn_ref, vp_ref, vc_ref, vn_ref, kx_ref, vx_ref,
               cq_ref, sq_ref, cp_ref, sp_ref, cn_ref, sn_ref, qn_ref, kn_w_ref, o_ref, *, last):
    i = pl.program_id(1)
    tq = q_ref.shape[1]
    tn = kp_ref.shape[1]
    n_loc = tq + 2 * tn
    cos_loc = jnp.concatenate([cp_ref[...], cq_ref[...], cn_ref[...]], axis=0)
    sin_loc = jnp.concatenate([sp_ref[...], sq_ref[...], sn_ref[...]], axis=0)
    qn_w = qn_ref[...] * (HEAD_DIM ** -0.5 * LOG2E)

    qi = _iota((tq, n_loc), 0)
    kj = _iota((tq, n_loc), 1)
    rel = kj - tn - qi
    lo_col = jnp.where(i >= 2, 0, tn)
    hi_col = jnp.where(i >= 1, jnp.where(i < last, n_loc, tn + tq), 0)
    ok = (rel >= -WA_WINDOW) & (rel <= WA_WINDOW) & (kj >= lo_col) & (kj < hi_col)
    mask = jnp.where(ok, 0.0, NEG_INF)

    group = WA_HEADS // WA_KV_HEADS
    mask_g = jnp.concatenate([mask] * group, axis=0)
    for b in range(q_ref.shape[0]):
        _wa_one_batch(b, sink_ref, q_ref, (kp_ref, kc_ref, kn_ref), (vp_ref, vc_ref, vn_ref),
                      kx_ref, vx_ref, (cos_loc, sin_loc), (cq_ref[...], sq_ref[...]), qn_w,
                      kn_w_ref[...], mask_g, o_ref)


def _wa_one_batch(b, sink_ref, q_ref, k_refs, v_refs, kx_ref, vx_ref, rope_loc, rope_q, qn_w, kn_w,
                  mask_g, o_ref):
    tq = q_ref.shape[1]
    half = LANES // 2
    group = WA_HEADS // WA_KV_HEADS
    f32 = lambda ref: ref[b].astype(F32)
    k_loc = _pair_norm(jnp.concatenate([f32(r) for r in k_refs], axis=0), kn_w)
    k_loc = _rope(k_loc, *rope_loc).astype(BF16)
    k_ctx = _pair_norm(f32(kx_ref), kn_w).astype(BF16)
    v_loc = jnp.concatenate([f32(r) for r in v_refs], axis=0)
    v_ctx = f32(vx_ref)
    lo = _iota((tq, LANES), 1) < HEAD_DIM
    lo_loc = _iota(v_loc.shape, 1) < HEAD_DIM
    q_pairs = [_rope(_pair_norm(q_ref[b, :, p * LANES:(p + 1) * LANES].astype(F32), qn_w), *rope_q)
               for p in range(WA_HEADS // 2)]
    outs = [None] * WA_HEADS
    for kv in range(WA_KV_HEADS):
        mine, mine_loc = (lo, lo_loc) if kv == 0 else (~lo, ~lo_loc)
        heads = range(kv * group, (kv + 1) * group)
        stack = []
        for h in heads:
            q = q_pairs[h // 2]
            q = q if h % 2 == kv else pltpu.roll(q, half, 1)
            stack.append(jnp.where(mine, q, 0.0).astype(BF16))
        q_g = jnp.concatenate(stack, axis=0)
        sink_g = jnp.concatenate([jnp.full((tq, 1), sink_ref[h] * LOG2E, F32) for h in heads], axis=0)
        o_g = _softmax_av(_dot_nt(q_g, k_loc) + mask_g, _dot_nt(q_g, k_ctx),
                          _ones_beside(v_loc, mine_loc), _ones_beside(v_ctx, mine), sink_g)
        for n, h in enumerate(heads):
            o = o_g[n * tq:(n + 1) * tq]
            outs[h] = o if h % 2 == kv else pltpu.roll(o, half, 1)
    for p in range(WA_HEADS // 2):
        o_ref[b, :, p * LANES:(p + 1) * LANES] = jnp.where(
            lo, outs[2 * p], outs[2 * p + 1]).astype(o_ref.dtype)


def _wa_attention(proj, sink, qn_w, kn_w, cos_t, sin_t, col_q, col_k, col_v):
    bsz, n, _ = proj.shape
    nblk = n // TOK_BLK
    last = nblk - 1
    qw = WA_HEADS * HEAD_DIM
    cq, ck, cv = col_q // qw, col_k // LANES, col_v // LANES

    assert TOK_BLK % WA_WINDOW == 0
    per = TOK_BLK // WA_WINDOW

    def near(i, shift):
        if shift == 0:
            return i
        return jnp.clip(per * i - 1 if shift < 0 else per * (i + 1), 0, per * nblk - 1)

    def blk(col, shift):
        rows = TOK_BLK if shift == 0 else WA_WINDOW
        return pl.BlockSpec((1, rows, LANES), lambda b, i: (b, near(i, shift), col))

    def tab(shift):
        rows = TOK_BLK if shift == 0 else WA_WINDOW
        return pl.BlockSpec((rows, LANES), lambda b, i: (near(i, shift), 0))

    def ctx_blk(col):
        return pl.BlockSpec((1, TOK_BLK, LANES), lambda b, i: (b, 0, col))

    vec = pl.BlockSpec((1, LANES), lambda b, i: (0, 0))
    return pl.pallas_call(
        functools.partial(_wa_kernel, last=last),
        grid=(bsz, nblk),
        in_specs=[pl.BlockSpec(memory_space=pltpu.SMEM),
                  pl.BlockSpec((1, TOK_BLK, qw), lambda b, i: (b, i, cq)),
                  blk(ck, -1), blk(ck, 0), blk(ck, 1), blk(cv, -1), blk(cv, 0), blk(cv, 1),
                  ctx_blk(ck), ctx_blk(cv),
                  tab(0), tab(0), tab(-1), tab(-1), tab(1), tab(1), vec, vec],
        out_specs=pl.BlockSpec((1, TOK_BLK, qw), lambda b, i: (b, i, 0)),
        out_shape=jax.ShapeDtypeStruct((bsz, n, qw), BF16),
        compiler_params=_cparams(2), name="wa_attn",
    )(sink, proj, proj, proj, proj, proj, proj, proj, proj, proj,
      cos_t, sin_t, cos_t, sin_t, cos_t, sin_t,
      jnp.tile(qn_w, 2).reshape(1, LANES), jnp.tile(kn_w, 2).reshape(1, LANES))


def _rope_tables(n_ctx, t):
    quarter = HEAD_DIM // 4
    inv_freq = ROPE_THETA ** (-jnp.arange(quarter, dtype=F32) / quarter)
    pos = jnp.arange(t)
    lane = np.arange(LANES)
    in_head = lane % HEAD_DIM
    use_col = in_head >= HEAD_DIM // 2
    second = (in_head % (HEAD_DIM // 2)) >= quarter
    freq = inv_freq[in_head % quarter]
    p = jnp.where(use_col[None, :], (pos % GRID_W)[:, None], (pos // GRID_W)[:, None]).astype(F32)
    ang = p * freq[None, :]
    cos = jnp.cos(ang)
    sin = jnp.where(second[None, :], jnp.sin(ang), -jnp.sin(ang))
    cos = jnp.concatenate([jnp.ones((n_ctx, LANES), F32), cos], axis=0)
    sin = jnp.concatenate([jnp.zeros((n_ctx, LANES), F32), sin], axis=0)
    return cos, sin


HG_CHUNK = 128


def _hgrn_kernel(q_ref, f_ref, v_ref, hl_ref, *rest, layer, reverse, final):
    if final:
        g_ref, prev_ref, nw_ref, o_ref, st_ref = rest
    else:
        o_ref, st_ref = rest
    @pl.when(pl.program_id(0) == 0)
    def _():
        st_ref[...] = jnp.zeros_like(st_ref)

    a = hl_ref[0]
    e = jnp.exp(a - jnp.max(a, axis=0, keepdims=True))
    pr = e / jnp.sum(e, axis=0, keepdims=True)
    lb_all = jnp.zeros((1, a.shape[1]), F32)
    for j in range(1, layer + 1):
        lb_all = lb_all + pr[j:j + 1]

    for b, h in [(b, h) for b in range(q_ref.shape[0]) for h in range(HG_HEADS)]:
        lanes = slice(h * HG_DK, (h + 1) * HG_DK)
        lb = lb_all[:, lanes]
        qs = _silu(q_ref[b, :, lanes].astype(F32))
        f = lb + (1.0 - lb) * _sigmoid(f_ref[b, :, lanes])
        o, st = _hgrn_block(qs, 1.0 - f, jnp.log(jnp.maximum(f, LOG_FLOOR)),
                            v_ref[b, :, lanes].astype(F32), st_ref[b, h], reverse)
        st_ref[b, h] = st
        if final:
            tot = prev_ref[b, :, lanes] + o
            gated = _rms(tot, nw_ref[...]) * _silu(g_ref[b, :, lanes].astype(F32))
            o_ref[b, :, lanes] = gated.astype(o_ref.dtype)
        else:
            o_ref[b, :, lanes] = o


def _hgrn_block(qs, kk, g, v, st, reverse):
    c = qs.shape[0]
    n_sub = c // HG_CHUNK
    row = _iota((c, HG_DK), 0)
    ri = _iota((c, c), 0)
    ci = _iota((c, c), 1)
    block_xor = ri ^ ci

    tri = jnp.where(((ci >= ri) if reverse else (ci <= ri)) & (block_xor < HG_CHUNK), 1.0, 0.0)
    tri = tri.astype(BF16)
    g_hi = g.astype(BF16)
    rest = g - g_hi.astype(F32)
    g_mid = rest.astype(BF16)
    g_lo = (rest - g_mid.astype(F32)).astype(BF16)
    tdot = functools.partial(jnp.dot, preferred_element_type=F32)
    b = tdot(tri, g_hi) + (tdot(tri, g_mid) + tdot(tri, g_lo))

    def ref_rows(hs):
        blk = 2 * hs
        inner = hs if reverse else hs - 1
        if blk >= SUBLANES:
            x = b.reshape(c // blk, blk, HG_DK)
            return jnp.broadcast_to(x[:, inner:inner + 1, :], x.shape).reshape(c, HG_DK)
        x = b.reshape(c // SUBLANES, SUBLANES, HG_DK)
        sub = _iota(x.shape, 1)
        out = None
        for p in range(SUBLANES // blk):
            r = p * blk + inner
            piece = jnp.broadcast_to(x[:, r:r + 1, :], x.shape)
            out = piece if out is None else jnp.where(sub >= p * blk, piece, out)
        return out.reshape(c, HG_DK)

    att = jnp.zeros((c, c), F32)
    hs = HG_CHUNK // 2
    while hs >= 1:
        blk = 2 * hs
        decay = jnp.exp(-jnp.abs(b - ref_rows(hs)))
        in_block = row & (blk - 1)
        later_half = (in_block < hs) if reverse else (in_block >= hs)
        q_l = jnp.where(later_half, qs * decay, 0.0)
        k_l = jnp.where(later_half, 0.0, kk * decay)
        a_l = _dot_nt(q_l, k_l)
        att = att + (a_l if blk == c else jnp.where(block_xor < blk, a_l, 0.0))
        hs //= 2
    o = _dot(att, v) + jnp.sum(qs * kk, axis=-1, keepdims=True) * v

    q_dec = qs * jnp.exp(b)
    b3 = b.reshape(n_sub, HG_CHUNK, HG_DK)
    end = 0 if reverse else HG_CHUNK - 1
    b_end = jnp.broadcast_to(b3[:, end:end + 1, :], b3.shape).reshape(c, HG_DK)
    k_dec = kk * jnp.exp(b_end - b)
    outs = [None] * n_sub
    for s in (reversed(range(n_sub)) if reverse else range(n_sub)):
        rows = slice(s * HG_CHUNK, (s + 1) * HG_CHUNK)
        outs[s] = o[rows] + _dot_nt(q_dec[rows], st)
        st = st * jnp.exp(b_end[s * HG_CHUNK:s * HG_CHUNK + 1, :]) + _dot_tn(v[rows], k_dec[rows])
    return jnp.concatenate(outs, axis=0), st


def _hgrn_pass(hl, layer, reverse, q_src, f_src, v_src, final_args=None):
    bsz, n, _ = q_src[0].shape
    nchunk = n // TOK_BLK
    width = HG_HEADS * HG_DK

    def chunk(t):
        return jnp.where(t == 0, 0, nchunk - t) if reverse else t

    def blk(col):
        return pl.BlockSpec((bsz, TOK_BLK, width), lambda t: (0, chunk(t), col // width))

    in_specs = [blk(q_src[1]), blk(f_src[1]), blk(v_src[1]),
                pl.BlockSpec((1,) + hl.shape[1:], lambda t: (1 if reverse else 0, 0, 0))]
    args = [q_src[0], f_src[0], v_src[0], hl]
    final = final_args is not None
    if final:
        g_src, prev, norm_w = final_args
        in_specs += [blk(g_src[1]), blk(0), pl.BlockSpec((1, HG_DK), lambda t: (0, 0))]
        args += [g_src[0], prev, norm_w.reshape(1, HG_DK)]
    return pl.pallas_call(
        functools.partial(_hgrn_kernel, layer=layer, reverse=reverse, final=final),
        grid=(nchunk,),
        in_specs=in_specs,
        out_specs=blk(0),
        out_shape=jax.ShapeDtypeStruct((bsz, n, width), BF16 if final else F32),
        scratch_shapes=[pltpu.VMEM((bsz, HG_HEADS, HG_DK, HG_DK), F32)],
        compiler_params=_cparams(1), name="hgrn_bwd" if reverse else "hgrn_fwd",
    )(*args)


def _merge_kernel(x_ref, ya_ref, yb_ref, yc_ref, ga_ref, gb_ref, gc_ref, mod_ref,
                  wa_ref, wb_ref, wc_ref, wo_ref, o_ref, *, n_ctx, ctx_row):
    b, i = pl.program_id(0), pl.program_id(1)
    tm, d = x_ref.shape[1:]
    gate_of = lambda ref: _sigmoid(ref[0].astype(F32))
    merged = (gate_of(ga_ref) * _dot(ya_ref[0], wa_ref[...])
              + gate_of(gb_ref) * _dot(yb_ref[0], wb_ref[...])
              + gate_of(gc_ref) * _dot(yc_ref[0], wc_ref[...]))
    mix = _dot(merged, wo_ref[...])
    gate = _mod_rows(mod_ref, 2, b, i * tm, n_ctx, ctx_row, tm, d)
    o_ref[0] = x_ref[0] + gate * mix


def _merge(xa, ya, yb, yc, proj, mod_l, w_pa, w_pb, w_pc, w_out, col_gate, n_ctx, ctx_row):
    bsz, n, d = xa.shape
    tm = 704 if n % 704 == 0 else TOK_BLK
    g0 = col_gate // d
    tile = lambda w: pl.BlockSpec((1, tm, w), lambda b, i: (b, i, 0))
    gate = lambda k: pl.BlockSpec((1, tm, d), lambda b, i: (b, i, g0 + k))
    full = lambda w: pl.BlockSpec(w.shape, lambda b, i: (0, 0))
    return pl.pallas_call(
        functools.partial(_merge_kernel, n_ctx=n_ctx, ctx_row=ctx_row),
        grid=(bsz, n // tm),
        in_specs=[tile(d), tile(ya.shape[2]), tile(yb.shape[2]), tile(yc.shape[2]),
                  gate(0), gate(1), gate(2), full(mod_l),
                  full(w_pa), full(w_pb), full(w_pc), full(w_out)],
        out_specs=tile(d),
        out_shape=jax.ShapeDtypeStruct((bsz, n, d), F32),
        compiler_params=_cparams(2), name="merge",
    )(xa, ya, yb, yc, proj, proj, proj, mod_l, w_pa, w_pb, w_pc, w_out)


def _route_kernel(x_ref, mod_ref, nw_ref, wr_ref, br_ref, h_ref, idx_ref, wt_ref, cnt_ref,
                  meta_ref, carry_ref, *, n_ctx, ctx_row):
    b, i = pl.program_id(0), pl.program_id(1)
    tm, d = x_ref.shape[1:]

    @pl.when((b == 0) & (i == 0))
    def _():
        carry_ref[...] = jnp.zeros_like(carry_ref)

    h = _rms(x_ref[0], nw_ref[...])
    shift = _mod_rows(mod_ref, 3, b, i * tm, n_ctx, ctx_row, tm, d)
    scale = _mod_rows(mod_ref, 4, b, i * tm, n_ctx, ctx_row, tm, d)
    h = h * (1.0 + scale) + shift
    h_ref[0] = h.astype(h_ref.dtype)

    logits = _dot3_nt(wr_ref[...], h)
    ex = jnp.exp(logits - jnp.max(logits, axis=0, keepdims=True))
    probs = ex / jnp.sum(ex, axis=0, keepdims=True)
    sel = probs + br_ref[...]

    def row(x, r):
        return x[r:r + 1, :]

    best = None
    g_idx = None
    for g in range(N_GROUPS):
        r0 = g * EXPERTS_PER_GROUP
        a0, a1, a2, a3 = (row(sel, r0 + j) for j in range(EXPERTS_PER_GROUP))
        hi1, lo1 = jnp.maximum(a0, a1), jnp.minimum(a0, a1)
        hi2, lo2 = jnp.maximum(a2, a3), jnp.minimum(a2, a3)
        score = jnp.maximum(hi1, hi2) + jnp.maximum(jnp.minimum(hi1, hi2), jnp.maximum(lo1, lo2))
        if g == 0:
            best, g_idx = score, jnp.zeros_like(score, dtype=I32)
        else:
            better = score > best
            best = jnp.where(better, score, best)
            g_idx = jnp.where(better, g, g_idx)

    def pick(x, j):
        out = row(x, j)
        for g in range(1, N_GROUPS):
            out = jnp.where(g_idx == g, row(x, g * EXPERTS_PER_GROUP + j), out)
        return out

    in_grp = [pick(sel, j) for j in range(EXPERTS_PER_GROUP)]
    in_prob = [pick(probs, j) for j in range(EXPERTS_PER_GROUP)]

    def first_argmax(vals, skip):
        bv, bi, bp = None, None, None
        for j in range(EXPERTS_PER_GROUP):
            v = vals[j] if skip is None else jnp.where(skip == j, -jnp.inf, vals[j])
            if bv is None:
                bv, bi, bp = v, jnp.zeros_like(g_idx), in_prob[0]
            else:
                better = v > bv
                bv = jnp.where(better, v, bv)
                bi = jnp.where(better, j, bi)
                bp = jnp.where(better, in_prob[j], bp)
        return bi, bp

    loc0, p0 = first_argmax(in_grp, None)
    loc1, p1 = first_argmax(in_grp, loc0)
    e0 = g_idx * EXPERTS_PER_GROUP + loc0
    e1 = g_idx * EXPERTS_PER_GROUP + loc1
    wsum = p0 + p1
    w0, w1 = p0 / wsum, p1 / wsum

    er = _iota((N_EXPERTS, tm), 0)
    hit0 = er == e0
    hit1 = er == e1
    hot = jnp.where(hit0 | hit1, 1.0, 0.0)
    upper = jnp.where(_iota((tm, tm), 0) < _iota((tm, tm), 1), 1.0, 0.0)
    before = _dot(hot, upper)
    cnt = jnp.sum(hot, axis=1, keepdims=True)
    e_col = _iota((N_EXPERTS, 1), 0)

    def slot_starts(align):
        padded = jnp.floor((cnt + (align - 1)) * (1.0 / align)) * align
        off = jnp.zeros((N_EXPERTS, 1), F32)
        for e in range(N_EXPERTS - 1):
            off = off + jnp.where(e_col > e, padded[e:e + 1, :], 0.0)
        return padded, off

    cnt_pad, off_d = slot_starts(SUBLANES)
    _, off_c = slot_starts(MOE_CHUNK)
    positions = []
    for off in (off_d, off_c):
        for hit in (hit0, hit1):
            positions.append(jnp.sum(jnp.where(hit, before + off, 0.0), axis=0, keepdims=True))

    lane = _iota((N_EXPERTS, LANES), 1)
    carry = carry_ref[...]
    meta = jnp.zeros((N_EXPERTS, LANES), F32)
    for field, val in enumerate((carry, cnt, off_d, off_c)):
        meta = jnp.where(lane == field, val, meta)
    meta_ref[0] = meta.astype(I32)
    carry_ref[...] = carry + cnt_pad
    cnt_ref[...] = carry + cnt_pad

    idx_ref[...] = jnp.zeros_like(idx_ref)
    wt_ref[...] = jnp.zeros_like(wt_ref)
    for r, val in enumerate(positions):
        idx_ref[r:r + 1, :] = val.astype(I32)
    for r, val in enumerate((w0, w1)):
        wt_ref[r:r + 1, :] = val


def _route(x1, mod_l, norm_w, w_router_t, b_router, n_ctx, ctx_row):
    bsz, n, d = x1.shape
    tm = TOK_BLK
    nt = n // tm
    tile = pl.BlockSpec((1, tm, d), lambda b, i: (b, i, 0))
    lane_tile = pl.BlockSpec((SUBLANES, tm), lambda b, i: (0, b * nt + i))
    full = lambda a: pl.BlockSpec(a.shape, lambda b, i: (0,) * a.ndim)
    br = b_router.reshape(N_EXPERTS, 1)
    nw = norm_w.reshape(1, d)
    return pl.pallas_call(
        functools.partial(_route_kernel, n_ctx=n_ctx, ctx_row=ctx_row),
        grid=(bsz, nt),
        in_specs=[tile, full(mod_l), full(nw), full(w_router_t), full(br)],
        out_specs=[tile, lane_tile, lane_tile,
                   pl.BlockSpec((N_EXPERTS, LANES), lambda b, i: (0, 0)),
                   pl.BlockSpec((1, N_EXPERTS, LANES), lambda b, i: (b * nt + i, 0, 0))],
        out_shape=[jax.ShapeDtypeStruct((bsz, n, d), BF16),
                   jax.ShapeDtypeStruct((SUBLANES, bsz * n), I32),
                   jax.ShapeDtypeStruct((SUBLANES, bsz * n), F32),
                   jax.ShapeDtypeStruct((N_EXPERTS, LANES), F32),
                   jax.ShapeDtypeStruct((bsz * nt, N_EXPERTS, LANES), I32)],
        scratch_shapes=[pltpu.VMEM((N_EXPERTS, LANES), F32)],
        compiler_params=_cparams(2), name="route",
    )(x1, mod_l, nw, w_router_t, br)


def _round_up(x, m):
    return -(-x // m) * m


MOE_MAX_CHUNKS = TOK_BLK // MOE_CHUNK
MOE_SORT_ROWS = _round_up(2 * TOK_BLK + N_EXPERTS * (SUBLANES - 1) + MOE_CHUNK - 1, LANES)
MOE_GATHER_ROWS = _round_up(2 * TOK_BLK + N_EXPERTS * (MOE_CHUNK - 1), LANES)
META_FIELDS = 4


def _moe_blocks(n_assign, n_tiles):
    slack = N_EXPERTS * (n_tiles * (SUBLANES - 1) + MOE_CHUNK - 1)
    return -(-(n_assign + slack) // MOE_BLK) + N_EXPERTS


def _chunk_copies(meta_ref, start_ref, tile, slot_field, do, make_copy):
    def expert(e, carry):
        base = (tile * N_EXPERTS + e) * META_FIELDS
        before, rows, slot = meta_ref[base], meta_ref[base + 1], meta_ref[base + slot_field]
        for c in range(MOE_MAX_CHUNKS):
            @pl.when(rows > c * MOE_CHUNK)
            def _():
                do(make_copy(pl.multiple_of(slot + c * MOE_CHUNK, SUBLANES),
                             pl.multiple_of(start_ref[e] + before + c * MOE_CHUNK, SUBLANES)))
        return carry
    lax.fori_loop(0, N_EXPERTS, expert, 0)


def _dispatch_kernel(meta_ref, cnt_ref, pos_ref, h_ref, xs_ref, blk_e_ref, blk_rows_ref,
                     start_ref, sort_ref, zero_ref, sem, *, nblk):
    t = pl.program_id(0)
    nt = pl.num_programs(0)
    tm = h_ref.shape[0]

    @pl.when(t == 0)
    def _():
        def expert(e, end):
            start_ref[e] = end
            return end + ((cnt_ref[e] + MOE_CHUNK - 1 + MOE_BLK - 1) // MOE_BLK) * MOE_BLK
        start_ref[N_EXPERTS] = lax.fori_loop(0, N_EXPERTS, expert, 0)

        def block(j, carry):
            def count(e, acc):
                return acc + jnp.where(start_ref[e + 1] <= j * MOE_BLK, 1, 0)
            e = jnp.minimum(lax.fori_loop(0, N_EXPERTS, count, 0), N_EXPERTS - 1)
            blk_e_ref[j] = e
            blk_rows_ref[j] = jnp.clip(cnt_ref[e] - (j * MOE_BLK - start_ref[e]), 0, MOE_BLK)
            return carry
        lax.fori_loop(0, nblk, block, 0)

    r = _iota((MOE_SORT_ROWS, tm), 0)
    onehot = jnp.where((r == pos_ref[0:1, :]) | (r == pos_ref[1:2, :]), 1.0, 0.0)
    buf = t % 2
    sort_ref[buf] = _dot(onehot, h_ref[...])

    def copies(which):
        def make_copy(src_row, dst_row):
            return pltpu.make_async_copy(sort_ref.at[which, pl.ds(src_row, MOE_CHUNK), :],
                                         xs_ref.at[pl.ds(dst_row, MOE_CHUNK), :], sem.at[which])
        return make_copy

    @pl.when(t > 0)
    def _():
        _chunk_copies(meta_ref, start_ref, t - 1, 2, lambda cp: cp.wait(), copies(1 - buf))

    _chunk_copies(meta_ref, start_ref, t, 2, lambda cp: cp.start(), copies(buf))

    @pl.when(t == nt - 1)
    def _():
        _chunk_copies(meta_ref, start_ref, t, 2, lambda cp: cp.wait(), copies(buf))

        zero_ref[...] = jnp.zeros_like(zero_ref)

        def zero_copy(row, size):
            return pltpu.make_async_copy(zero_ref.at[pl.ds(0, size), :],
                                         xs_ref.at[pl.ds(pl.multiple_of(row, SUBLANES), size), :],
                                         sem.at[buf])

        def fill(do):
            def expert(e, carry):
                lo = start_ref[e] + cnt_ref[e]
                gap = start_ref[e + 1] - lo
                n_big = gap // MOE_CHUNK

                def big(k, c2):
                    do(zero_copy(lo + k * MOE_CHUNK, MOE_CHUNK))
                    return c2
                lax.fori_loop(0, n_big, big, 0)

                def small(k, c2):
                    do(zero_copy(lo + n_big * MOE_CHUNK + k * SUBLANES, SUBLANES))
                    return c2
                lax.fori_loop(0, (gap - n_big * MOE_CHUNK) // SUBLANES, small, 0)
                return carry
            lax.fori_loop(0, N_EXPERTS, expert, 0)

            def unused(k, carry):
                do(zero_copy(start_ref[N_EXPERTS] + k * MOE_ZERO_ROWS, MOE_ZERO_ROWS))
                return carry
            lax.fori_loop(0, (nblk * MOE_BLK - start_ref[N_EXPERTS]) // MOE_ZERO_ROWS, unused, 0)

        fill(lambda cp: cp.start())
        fill(lambda cp: cp.wait())


def _dispatch(meta, counts, pos_rows, h_flat, nblk):
    ntok, d = h_flat.shape
    tm = TOK_BLK
    whole_smem = pl.BlockSpec(memory_space=pltpu.SMEM)
    return pl.pallas_call(
        functools.partial(_dispatch_kernel, nblk=nblk),
        grid=(ntok // tm,),
        in_specs=[whole_smem, whole_smem,
                  pl.BlockSpec((SUBLANES, tm), lambda i: (0, i)),
                  pl.BlockSpec((tm, d), lambda i: (i, 0))],
        out_specs=[pl.BlockSpec(memory_space=pl.ANY), whole_smem, whole_smem, whole_smem],
        out_shape=[jax.ShapeDtypeStruct((nblk * MOE_BLK, d), F32),
                   jax.ShapeDtypeStruct((nblk,), I32),
                   jax.ShapeDtypeStruct((nblk,), I32),
                   jax.ShapeDtypeStruct((N_EXPERTS + 1,), I32)],
        scratch_shapes=[pltpu.VMEM((2, MOE_SORT_ROWS, d), F32), pltpu.VMEM((MOE_ZERO_ROWS, d), F32),
                        pltpu.SemaphoreType.DMA((2,))],
        compiler_params=_cparams(1), name="dispatch",
    )(meta, counts, pos_rows, h_flat)


def _expert_kernel(blk_e_ref, blk_rows_ref, x_ref, wg_ref, wu_ref, wd_ref, o_ref, g_bf, u_bf, d_bf):
    i = pl.program_id(0)
    rows = blk_rows_ref[i]

    @pl.when((i == 0) | (blk_e_ref[i] != blk_e_ref[jnp.maximum(i - 1, 0)]))
    def _():
        g_bf[...] = wg_ref[0, 0].astype(BF16)
        u_bf[...] = wu_ref[0, 0].astype(BF16)
        d_bf[...] = wd_ref[0, 0].astype(BF16)

    @pl.when(rows > 0)
    def _():
        x = x_ref[...].astype(BF16)
        hid = _silu(_dot(x, g_bf[...])) * _dot(x, u_bf[...])
        o_ref[...] = _dot(hid, d_bf[...])

    @pl.when(rows == 0)
    def _():
        o_ref[...] = jnp.zeros_like(o_ref)


def _experts(blk_e, blk_rows, xs, w_gate, w_up, w_down, layer):
    d = xs.shape[1]
    ff = w_gate.shape[3]
    nblk = blk_e.shape[0]
    weight = lambda shape: pl.BlockSpec((1, 1) + shape, lambda i, be, rows: (layer, be[i], 0, 0))
    return pl.pallas_call(
        _expert_kernel,
        grid_spec=pltpu.PrefetchScalarGridSpec(
            num_scalar_prefetch=2, grid=(nblk,),
            in_specs=[pl.BlockSpec((MOE_BLK, d), lambda i, be, rows: (i, 0)),
                      weight((d, ff)), weight((d, ff)), weight((ff, d))],
            out_specs=pl.BlockSpec((MOE_BLK, d), lambda i, be, rows: (i, 0)),
            scratch_shapes=[pltpu.VMEM((d, ff), BF16), pltpu.VMEM((d, ff), BF16),
                            pltpu.VMEM((ff, d), BF16)]),
        out_shape=jax.ShapeDtypeStruct(xs.shape, F32),
        compiler_params=_cparams(1), name="experts",
    )(blk_e, blk_rows, xs, w_gate, w_up, w_down)


def _combine_kernel(meta_ref, start_ref, x_ref, pos_ref, wt_ref, mod_ref, ys_ref, o_ref,
                    gath_ref, sem, *, n_ctx, ctx_row):
    b, i = pl.program_id(0), pl.program_id(1)
    nt = pl.num_programs(1)
    t = b * nt + i
    last = pl.num_programs(0) * nt - 1
    tm, d = x_ref.shape[1:]
    buf = t % 2

    def copies(which):
        def make_copy(slot_row, ys_row):
            return pltpu.make_async_copy(ys_ref.at[pl.ds(ys_row, MOE_CHUNK), :],
                                         gath_ref.at[which, pl.ds(slot_row, MOE_CHUNK), :],
                                         sem.at[which])
        return make_copy

    @pl.when(t == 0)
    def _():
        gath_ref[...] = jnp.zeros_like(gath_ref)
        _chunk_copies(meta_ref, start_ref, t, 3, lambda cp: cp.start(), copies(buf))

    @pl.when(t < last)
    def _():
        _chunk_copies(meta_ref, start_ref, t + 1, 3, lambda cp: cp.start(), copies(1 - buf))

    _chunk_copies(meta_ref, start_ref, t, 3, lambda cp: cp.wait(), copies(buf))

    col = _iota((tm, MOE_GATHER_ROWS), 1)
    pos = pos_ref[...]
    wt = wt_ref[...]
    sel = (jnp.where(col == pos[:, 0:1], wt[:, 0:1], 0.0)
           + jnp.where(col == pos[:, 1:2], wt[:, 1:2], 0.0))
    hi, lo = _hi_lo(sel)
    rows = gath_ref[buf].astype(BF16)
    y = (jnp.dot(hi, rows, preferred_element_type=F32)
         + jnp.dot(lo, rows, preferred_element_type=F32))
    gate = _mod_rows(mod_ref, 5, b, i * tm, n_ctx, ctx_row, tm, d)
    o_ref[0] = x_ref[0] + gate * y


def _combine(meta, start, x1, pos_cols, wt_cols, mod_l, ys, n_ctx, ctx_row, latent_only):
    bsz, n, d = x1.shape
    tm = TOK_BLK
    nt = n // tm
    whole_smem = pl.BlockSpec(memory_space=pltpu.SMEM)
    cols = pl.BlockSpec((tm, 2), lambda b, i: (b * nt + i, 0))
    if latent_only:
        assert n_ctx == tm
        out_spec = pl.BlockSpec((1, tm, d), lambda b, i: (b, jnp.maximum(i - 1, 0), 0))
        out_rows = n - n_ctx
    else:
        out_spec = pl.BlockSpec((1, tm, d), lambda b, i: (b, i, 0))
        out_rows = n
    return pl.pallas_call(
        functools.partial(_combine_kernel, n_ctx=n_ctx, ctx_row=ctx_row),
        grid=(bsz, nt),
        in_specs=[whole_smem, whole_smem,
                  pl.BlockSpec((1, tm, d), lambda b, i: (b, i, 0)), cols, cols,
                  pl.BlockSpec(mod_l.shape, lambda b, i: (0, 0)),
                  pl.BlockSpec(memory_space=pl.ANY)],
        out_specs=out_spec,
        out_shape=jax.ShapeDtypeStruct((bsz, out_rows, d), F32),
        scratch_shapes=[pltpu.VMEM((2, MOE_GATHER_ROWS, d), F32), pltpu.SemaphoreType.DMA((2,))],
        compiler_params=_cparams(2), name="combine",
    )(meta, start, x1, pos_cols, wt_cols, mod_l, ys)


def _layer(xa, mod_l, layer, n_ctx, ctx_row, p, rope):
    bsz, n, d = xa.shape
    na_w = NA_HEADS * HEAD_DIM
    hg_w = HG_HEADS * HG_DK
    wa_qw = WA_HEADS * HEAD_DIM
    wa_kvw = WA_KV_HEADS * HEAD_DIM
    hg0 = 3 * na_w
    wa0 = hg0 + 5 * hg_w
    gate0 = wa0 + wa_qw + 2 * wa_kvw
    n_cols = gate0 + N_BRANCHES * d
    assert p["w_in"].shape[2] == n_cols
    w_att = _take_cols(p["w_in"], layer, [(gate0, n_cols), (0, hg0), (wa0, gate0)])
    w_hg_qig = _take_cols(p["w_in"], layer, [(hg0, hg0 + hg_w), (hg0 + 3 * hg_w, wa0)])
    w_hg_f = _take_cols(p["w_in"], layer, [(hg0 + hg_w, hg0 + 3 * hg_w)])
    col_gate = 0
    col_na = [N_BRANCHES * d + k * na_w for k in range(3)]
    col_wq = N_BRANCHES * d + 3 * na_w
    col_wk = col_wq + wa_qw
    col_wv = col_wk + wa_kvw
    proj, h = _inproj(xa, mod_l, p["norm1"], w_att, n_ctx, ctx_row, 1792, BF16)
    hg_qig = _project(h, w_hg_qig, 768, BF16)
    hg_f = _project(h, w_hg_f, 1024, F32)

    ya = _na_attention(proj, _na_bias(p["na_rpb"]), p["na_q_norm"], p["na_k_norm"], *col_na)
    yc = _wa_attention(proj, p["wa_sink"], p["wa_q_norm"], p["wa_k_norm"], rope[0], rope[1],
                       col_wq, col_wk, col_wv)
    hg_q, hg_i, hg_g = ((hg_qig, k * hg_w) for k in range(3))
    o_f = _hgrn_pass(p["hg_lower"], layer, False, hg_q, (hg_f, 0), hg_i)
    yb = _hgrn_pass(p["hg_lower"], layer, True, hg_q, (hg_f, hg_w), hg_i,
                    final_args=(hg_g, o_f, p["hg_norm"]))

    bf = lambda w: w.astype(BF16)
    x1 = _merge(xa, ya, yb, yc, proj, mod_l, bf(p["w_pa"]), bf(p["w_pb"]), bf(p["w_pc"]),
                bf(p["w_out"]), col_gate, n_ctx, ctx_row)

    h2, idx, wts, counts, meta = _route(x1, mod_l, p["norm2"], p["w_router"].T, p["b_router"],
                                        n_ctx, ctx_row)
    ntok = bsz * n
    meta_flat = meta[:, :, :META_FIELDS].reshape(-1)
    xs, blk_e, blk_rows, start = _dispatch(meta_flat, counts[:, 0].astype(I32), idx,
                                           h2.reshape(ntok, d),
                                           _moe_blocks(2 * ntok, ntok // TOK_BLK))
    ys = _experts(blk_e, blk_rows, xs, p["w_gate"], p["w_up"], p["w_down"], layer)
    return _combine(meta_flat, start, x1, idx[2:4].T, wts[:2].T, mod_l, ys, n_ctx, ctx_row,
                    latent_only=p["last"])


def kernel(x, c, ctx, c_ctx, w_ada, b_ada, norm1, norm2, w_in, na_q_norm, na_k_norm, na_rpb, hg_lower,
           hg_norm, wa_q_norm, wa_k_norm, wa_sink, w_pa, w_pb, w_pc, w_out, w_router, b_router,
           w_gate, w_up, w_down):
    bsz, t, d = x.shape
    n_ctx = ctx.shape[1]
    depth = w_ada.shape[0]
    assert n_ctx == TOK_BLK and t % TOK_BLK == 0 and t // TOK_BLK >= 3
    assert t % GRID_W == 0 and bsz + 1 <= SUBLANES

    xa = jnp.concatenate([ctx, x], axis=1)
    cond = jnp.concatenate([c, c_ctx[None], jnp.zeros((SUBLANES - bsz - 1, d), F32)], axis=0)
    mod = _ada(cond, w_ada, b_ada)
    rope = _rope_tables(n_ctx, t)
    for l in range(depth):
        p = dict(norm1=norm1[l], norm2=norm2[l], w_in=w_in, na_q_norm=na_q_norm[l],
                 na_k_norm=na_k_norm[l], na_rpb=na_rpb[l], hg_lower=hg_lower, hg_norm=hg_norm[l],
                 wa_q_norm=wa_q_norm[l], wa_k_norm=wa_k_norm[l], wa_sink=wa_sink[l],
                 w_pa=w_pa[l], w_pb=w_pb[l], w_pc=w_pc[l], w_out=w_out[l],
                 w_router=w_router, b_router=b_router, w_gate=w_gate, w_up=w_up,
                 w_down=w_down, last=l == depth - 1)
        xa = _layer(xa, mod[l], l, n_ctx, bsz, p, rope)
    return xa
```

```python
import functools

import numpy as np
import jax
import jax.numpy as jnp
from jax import lax
from jax.experimental import pallas as pl
from jax.experimental.pallas import tpu as pltpu

F32 = jnp.float32
BF16 = jnp.bfloat16
I32 = jnp.int32

GRID_W = 64
NA_HEADS = 8
NA_ROWS = 8
NA_COLS = 16
HEAD_DIM = 64
HG_HEADS = 4
HG_DK = 128
LOG_FLOOR = 1e-30
WA_HEADS = 8
WA_KV_HEADS = 2
WA_WINDOW = 128
ROPE_THETA = 10000.0
N_EXPERTS = 16
N_GROUPS = 4
EXPERTS_PER_GROUP = N_EXPERTS // N_GROUPS
N_BRANCHES = 3
NEG_INF = -1e30
RMS_EPS = 1e-6

LANES = 128
SUBLANES = 8
VMEM_LIMIT = 56 * 1024 * 1024

TOK_BLK = 256
MOE_BLK = 1024
MOE_CHUNK = 32
MOE_ZERO_ROWS = 256


def _cparams(n_axes):
    return pltpu.CompilerParams(
        dimension_semantics=("arbitrary",) * n_axes, vmem_limit_bytes=VMEM_LIMIT)


def _sigmoid(x):
    return 1.0 / (1.0 + jnp.exp(-x))


def _silu(x):
    return x * _sigmoid(x)


def _dot(a, b):
    return jnp.dot(a.astype(BF16), b.astype(BF16), preferred_element_type=F32)


def _dot_nt(a, b):
    return lax.dot_general(a.astype(BF16), b.astype(BF16), (((1,), (1,)), ((), ())),
                           preferred_element_type=F32)


def _dot_tn(a, b):
    return lax.dot_general(a.astype(BF16), b.astype(BF16), (((0,), (0,)), ((), ())),
                           preferred_element_type=F32)


def _hi_lo(a):
    hi = a.astype(BF16)
    lo = (a - hi.astype(F32)).astype(BF16)
    return hi, lo


def _dot3(a, b):
    ah, al = _hi_lo(a)
    bh, bl = _hi_lo(b)
    d = functools.partial(jnp.dot, preferred_element_type=F32)
    return d(ah, bh) + (d(ah, bl) + d(al, bh))


def _dot3_nt(a, b):
    ah, al = _hi_lo(a)
    bh, bl = _hi_lo(b)
    d = functools.partial(lax.dot_general, dimension_numbers=(((1,), (1,)), ((), ())),
                          preferred_element_type=F32)
    return d(ah, bh) + (d(ah, bl) + d(al, bh))


def _rms(x, w):
    return x * lax.rsqrt(jnp.mean(x * x, axis=-1, keepdims=True) + RMS_EPS) * w


def _iota(shape, dim):
    return lax.broadcasted_iota(I32, shape, dim)


def _ada_kernel(cond_ref, w_ref, b_ref, o_ref):
    o_ref[0] = _dot3(_silu(cond_ref[...]), w_ref[0]) + b_ref[0]


def _ada(cond, w_ada, b_ada):
    depth, d, d6 = w_ada.shape
    rows = cond.shape[0]
    tn = d6 // 4
    return pl.pallas_call(
        _ada_kernel,
        grid=(depth, d6 // tn),
        in_specs=[pl.BlockSpec((rows, d), lambda l, j: (0, 0)),
                  pl.BlockSpec((1, d, tn), lambda l, j: (l, 0, j)),
                  pl.BlockSpec((1, 1, tn), lambda l, j: (l, 0, j))],
        out_specs=pl.BlockSpec((1, rows, tn), lambda l, j: (l, 0, j)),
        out_shape=jax.ShapeDtypeStruct((depth, rows, d6), F32),
        compiler_params=_cparams(2), name="ada",
    )(cond, w_ada, b_ada.reshape(depth, 1, d6))


def _mod_rows(mod_ref, which, b, row0, n_ctx, ctx_row, tm, d):
    lat = mod_ref[pl.ds(b, 1), which * d:(which + 1) * d]
    cx = mod_ref[ctx_row:ctx_row + 1, which * d:(which + 1) * d]
    row = row0 + _iota((tm, d), 0)
    return jnp.where(row < n_ctx, cx, lat)


def _inproj_kernel(x_ref, mod_ref, nw_ref, w_ref, o_ref, h_ref, *, n_ctx, ctx_row):
    b, i, j = pl.program_id(0), pl.program_id(1), pl.program_id(2)
    tm, d = h_ref.shape[1:]

    @pl.when(j == 0)
    def _():
        h = _rms(x_ref[0], nw_ref[...])
        shift = _mod_rows(mod_ref, 0, b, i * tm, n_ctx, ctx_row, tm, d)
        scale = _mod_rows(mod_ref, 1, b, i * tm, n_ctx, ctx_row, tm, d)
        h_ref[0] = (h * (1.0 + scale) + shift).astype(BF16)

    o_ref[0] = jnp.dot(h_ref[0], w_ref[...], preferred_element_type=F32).astype(o_ref.dtype)


def _matmul_kernel(h_ref, w_ref, o_ref):
    o_ref[0] = jnp.dot(h_ref[0], w_ref[...], preferred_element_type=F32).astype(o_ref.dtype)


def _project(h, w, tn, out_dtype):
    bsz, n, d = h.shape
    cols = w.shape[1]
    tm = 1408 if n % 1408 == 0 else TOK_BLK
    return pl.pallas_call(
        _matmul_kernel,
        grid=(bsz, n // tm, cols // tn),
        in_specs=[pl.BlockSpec((1, tm, d), lambda b, i, j: (b, i, 0)),
                  pl.BlockSpec((d, tn), lambda b, i, j: (0, j))],
        out_specs=pl.BlockSpec((1, tm, tn), lambda b, i, j: (b, i, j)),
        out_shape=jax.ShapeDtypeStruct((bsz, n, cols), out_dtype),
        compiler_params=_cparams(3), name="project",
    )(h, w)


W_IN_COL_BLK = 256


def _take_cols_kernel(src_ref, w_ref, o_ref):
    del src_ref
    o_ref[...] = w_ref[0].astype(BF16)


def _take_cols(w_in, layer, col_ranges):
    d = w_in.shape[1]
    blk = W_IN_COL_BLK
    src = []
    for start, stop in col_ranges:
        assert start % blk == 0 and stop % blk == 0
        src += list(range(start // blk, stop // blk))
    return pl.pallas_call(
        _take_cols_kernel,
        grid_spec=pltpu.PrefetchScalarGridSpec(
            num_scalar_prefetch=1, grid=(len(src),),
            in_specs=[pl.BlockSpec((1, d, blk), lambda j, src: (layer, 0, src[j]))],
            out_specs=pl.BlockSpec((d, blk), lambda j, src: (0, j))),
        out_shape=jax.ShapeDtypeStruct((d, len(src) * blk), BF16),
        compiler_params=_cparams(1), name="take_cols",
    )(jnp.asarray(src, I32), w_in)


def _inproj(xa, mod_l, norm_w, w, n_ctx, ctx_row, tn, out_dtype):
    bsz, n, d = xa.shape
    cols = w.shape[1]
    tm = 1408 if n % 1408 == 0 else TOK_BLK
    return pl.pallas_call(
        functools.partial(_inproj_kernel, n_ctx=n_ctx, ctx_row=ctx_row),
        grid=(bsz, n // tm, cols // tn),
        in_specs=[pl.BlockSpec((1, tm, d), lambda b, i, j: (b, i, 0)),
                  pl.BlockSpec(mod_l.shape, lambda b, i, j: (0, 0)),
                  pl.BlockSpec((1, d), lambda b, i, j: (0, 0)),
                  pl.BlockSpec((d, tn), lambda b, i, j: (0, j))],
        out_specs=[pl.BlockSpec((1, tm, tn), lambda b, i, j: (b, i, j)),
                   pl.BlockSpec((1, tm, d), lambda b, i, j: (b, i, 0))],
        out_shape=[jax.ShapeDtypeStruct((bsz, n, cols), out_dtype),
                   jax.ShapeDtypeStruct((bsz, n, d), BF16)],
        compiler_params=_cparams(3), name="inproj",
    )(xa, mod_l, norm_w.reshape(1, d), w)


def _pair_norm(x, w):
    lane = _iota(x.shape, 1)
    lo = lane < HEAD_DIM
    sq = x * x
    s_lo = jnp.sum(jnp.where(lo, sq, 0.0), axis=-1, keepdims=True)
    s_hi = jnp.sum(jnp.where(lo, 0.0, sq), axis=-1, keepdims=True)
    ms = jnp.where(lo, s_lo, s_hi) * (1.0 / HEAD_DIM)
    return x * lax.rsqrt(ms + RMS_EPS) * w


LOG2E = 1.4426950408889634


def _ones_beside(v, value_lanes):
    return jnp.where(value_lanes, v, 1.0).astype(BF16)


def _softmax_av(s_loc, s_ctx, v_loc, v_ctx, sink):
    m = jnp.maximum(jnp.max(s_loc, axis=-1, keepdims=True), jnp.max(s_ctx, axis=-1, keepdims=True))
    if sink is not None:
        m = jnp.maximum(m, sink)
    p_loc = jnp.exp2(s_loc - m).astype(BF16)
    p_ctx = jnp.exp2(s_ctx - m).astype(BF16)
    acc = (jnp.dot(p_loc, v_loc, preferred_element_type=F32)
           + jnp.dot(p_ctx, v_ctx, preferred_element_type=F32))
    den = pltpu.roll(acc, LANES // 2, 1)
    if sink is not None:
        den = den + jnp.exp2(sink - m)
    return acc / den


NA_QROWS = TOK_BLK // GRID_W
NA_KROWS = 3 * NA_QROWS
N_RPB_COLS = 2 * NA_COLS - 1
N_RPB_ROWS = 2 * NA_ROWS - 1
NA_VARIANTS = 4


def _na_row_valid(variant, a, k):
    if variant == 0:
        return False
    if variant == 1:
        return NA_QROWS <= k < NA_QROWS + NA_ROWS
    if variant == 3:
        return k < NA_ROWS
    return a <= k < a + NA_ROWS


def _na_bias_kernel(rpb_ref, o_ref):
    h = pl.program_id(0)
    qc = _iota((GRID_W, GRID_W), 0)
    kc = _iota((GRID_W, GRID_W), 1)
    col_lo = jnp.clip(qc - NA_COLS // 2, 0, GRID_W - NA_COLS)
    col_ok = (kc >= col_lo) & (kc < col_lo + NA_COLS)
    d_col = kc - qc + (NA_COLS - 1)
    masked = jnp.full((GRID_W, GRID_W), NEG_INF, F32)
    blocks = []
    for dr in range(N_RPB_ROWS):
        acc = jnp.zeros((GRID_W, GRID_W), F32)
        for dc in range(N_RPB_COLS):
            val = rpb_ref[(h * N_RPB_ROWS + dr) * N_RPB_COLS + dc]
            acc = jnp.where(d_col == dc, val, acc)
        blocks.append(jnp.where(col_ok, acc * LOG2E, NEG_INF))
    for variant in range(NA_VARIANTS):
        for a in range(NA_QROWS):
            for k in range(NA_KROWS):
                dr = k - NA_QROWS - a + NA_ROWS - 1
                ok = _na_row_valid(variant, a, k) and 0 <= dr < N_RPB_ROWS
                o_ref[variant, 0, a * GRID_W:(a + 1) * GRID_W, k * GRID_W:(k + 1) * GRID_W] = (
                    blocks[dr] if ok else masked)


def _na_bias(rpb):
    heads = rpb.shape[0]
    return pl.pallas_call(
        _na_bias_kernel,
        grid=(heads,),
        in_specs=[pl.BlockSpec(memory_space=pltpu.SMEM)],
        out_specs=pl.BlockSpec((NA_VARIANTS, 1, TOK_BLK, 3 * TOK_BLK), lambda h: (0, h, 0, 0)),
        out_shape=jax.ShapeDtypeStruct((NA_VARIANTS, heads, TOK_BLK, 3 * TOK_BLK), F32),
        compiler_params=_cparams(1), name="na_bias",
    )(rpb.reshape(-1))


def _na_kernel(q_ref, kp_ref, kc_ref, kn_ref, vp_ref, vc_ref, vn_ref, kx_ref, vx_ref,
               bias_ref, qn_ref, kn_w_ref, o_ref):
    tq = q_ref.shape[1]
    lo = _iota((tq, LANES), 1) < HEAD_DIM
    lo_k = _iota((3 * tq, LANES), 1) < HEAD_DIM
    qn_w = qn_ref[...] * (HEAD_DIM ** -0.5 * LOG2E)
    for b, p in [(b, p) for b in range(q_ref.shape[0]) for p in range(NA_HEADS // 2)]:
        pair = slice(p * LANES, (p + 1) * LANES)
        f32 = lambda ref: ref[b, :, pair].astype(F32)
        q = _pair_norm(f32(q_ref), qn_w)
        k_loc = _pair_norm(jnp.concatenate([f32(kp_ref), f32(kc_ref), f32(kn_ref)], axis=0),
                           kn_w_ref[...]).astype(BF16)
        k_ctx = _pair_norm(f32(kx_ref), kn_w_ref[...]).astype(BF16)
        v_loc = jnp.concatenate([f32(vp_ref), f32(vc_ref), f32(vn_ref)], axis=0)
        v_ctx = f32(vx_ref)
        q2 = jnp.concatenate([jnp.where(lo, q, 0.0), jnp.where(lo, 0.0, q)], axis=0).astype(BF16)
        s_loc2 = _dot_nt(q2, k_loc)
        s_ctx2 = _dot_nt(q2, k_ctx)
        outs = []
        for h in range(2):
            mine = lo if h == 0 else ~lo
            mine_k = lo_k if h == 0 else ~lo_k
            rows = slice(h * tq, (h + 1) * tq)
            outs.append(_softmax_av(s_loc2[rows] + bias_ref[0, 2 * p + h], s_ctx2[rows],
                                    _ones_beside(v_loc, mine_k), _ones_beside(v_ctx, mine), None))
        o_ref[b, :, pair] = jnp.where(lo, outs[0], outs[1]).astype(o_ref.dtype)


def _na_attention(proj, bias, qn_w, kn_w, col_q, col_k, col_v):
    bsz, n, _ = proj.shape
    nblk = n // TOK_BLK
    last = nblk - 1
    width = NA_HEADS * HEAD_DIM
    cq, ck, cv = col_q // width, col_k // width, col_v // width

    def blk(col, shift):
        return pl.BlockSpec((bsz, TOK_BLK, width),
                            lambda i: (0, jnp.clip(i + shift, 0, last), col))

    def ctx_blk(col):
        return pl.BlockSpec((bsz, TOK_BLK, width), lambda i: (0, 0, col))

    def variant(i):
        return (jnp.where(i == 0, 0, jnp.where(i == 1, 1, jnp.where(i == last, 3, 2))), 0, 0, 0)

    vec = pl.BlockSpec((1, LANES), lambda i: (0, 0))
    return pl.pallas_call(
        _na_kernel,
        grid=(nblk,),
        in_specs=[blk(cq, 0), blk(ck, -1), blk(ck, 0), blk(ck, 1), blk(cv, -1), blk(cv, 0), blk(cv, 1),
                  ctx_blk(ck), ctx_blk(cv),
                  pl.BlockSpec((1, NA_HEADS, TOK_BLK, 3 * TOK_BLK), variant), vec, vec],
        out_specs=pl.BlockSpec((bsz, TOK_BLK, width), lambda i: (0, i, 0)),
        out_shape=jax.ShapeDtypeStruct((bsz, n, width), BF16),
        compiler_params=_cparams(1), name="na_attn",
    )(proj, proj, proj, proj, proj, proj, proj, proj, proj, bias,
      jnp.tile(qn_w, 2).reshape(1, LANES), jnp.tile(kn_w, 2).reshape(1, LANES))


WA_STACK = 4


def _rope(x, cos, sin_signed):
    lane = _iota(x.shape, 1)
    first = (lane & (HEAD_DIM // 2 - 1)) < HEAD_DIM // 4
    quarter = HEAD_DIM // 4
    partner = jnp.where(first, pltpu.roll(x, LANES - quarter, 1), pltpu.roll(x, quarter, 1))
    return x * cos + partner * sin_signed


def _wa_kernel(sink_ref, q_ref, kp_ref, kc_ref, kn_ref, vp_ref, vc_ref, vn_ref, kx_ref, vx_ref,
               cq_ref, sq_ref, cp_ref, sp_ref, cn_ref, sn_ref, qn_ref, kn_w_ref, o_ref, *, last):
    i = pl.program_id(1)
    tq = q_ref.shape[1]
    tn = kp_ref.shape[1]
    n_loc = tq + 2 * tn
    cos_loc = jnp.concatenate([cp_ref[...], cq_ref[...], cn_ref[...]], axis=0)
    sin_loc = jnp.concatenate([sp_ref[...], sq_ref[...], sn_ref[...]], axis=0)
    qn_w = qn_ref[...] * (HEAD_DIM ** -0.5 * LOG2E)

    qi = _iota((tq, n_loc), 0)
    kj = _iota((tq, n_loc), 1)
    rel = kj - tn - qi
    lo_col = jnp.where(i >= 2, 0, tn)
    hi_col = jnp.where(i >= 1, jnp.where(i < last, n_loc, tn + tq), 0)
    ok = (rel >= -WA_WINDOW) & (rel <= WA_WINDOW) & (kj >= lo_col) & (kj < hi_col)
    mask = jnp.where(ok, 0.0, NEG_INF)

    mask_g = jnp.concatenate([mask] * WA_STACK, axis=0)
    for b in range(q_ref.shape[0]):
        _wa_one_batch(b, sink_ref, q_ref, (kp_ref, kc_ref, kn_ref), (vp_ref, vc_ref, vn_ref),
                      kx_ref, vx_ref, (cos_loc, sin_loc), (cq_ref[...], sq_ref[...]), qn_w,
                      kn_w_ref[...], mask_g, o_ref)


def _wa_one_batch(b, sink_ref, q_ref, k_refs, v_refs, kx_ref, vx_ref, rope_loc, rope_q, qn_w, kn_w,
                  mask_g, o_ref):
    tq = q_ref.shape[1]
    half = LANES // 2
    group = WA_HEADS // WA_KV_HEADS
    f32 = lambda ref: ref[b].astype(F32)
    k_loc = _pair_norm(jnp.concatenate([f32(r) for r in k_refs], axis=0), kn_w)
    k_loc = _rope(k_loc, *rope_loc).astype(BF16)
    k_ctx = _pair_norm(f32(kx_ref), kn_w).astype(BF16)
    v_loc = jnp.concatenate([f32(r) for r in v_refs], axis=0)
    v_ctx = f32(vx_ref)
    lo = _iota((tq, LANES), 1) < HEAD_DIM
    lo_loc = _iota(v_loc.shape, 1) < HEAD_DIM
    q_pairs = [_rope(_pair_norm(q_ref[b, :, p * LANES:(p + 1) * LANES].astype(F32), qn_w), *rope_q)
               for p in range(WA_HEADS // 2)]
    outs = [None] * WA_HEADS
    for first in range(0, WA_HEADS, WA_STACK):
        heads = range(first, first + WA_STACK)
        kv = first // group
        mine, mine_loc = (lo, lo_loc) if kv == 0 else (~lo, ~lo_loc)
        stack = []
        for h in heads:
            q = q_pairs[h // 2]
            q = q if h % 2 == kv else pltpu.roll(q, half, 1)
            stack.append(jnp.where(mine, q, 0.0).astype(BF16))
        q_g = jnp.concatenate(stack, axis=0)
        sink_g = jnp.concatenate([jnp.full((tq, 1), sink_ref[h] * LOG2E, F32) for h in heads], axis=0)
        o_g = _softmax_av(_dot_nt(q_g, k_loc) + mask_g, _dot_nt(q_g, k_ctx),
                          _ones_beside(v_loc, mine_loc), _ones_beside(v_ctx, mine), sink_g)
        for n, h in enumerate(heads):
            o = o_g[n * tq:(n + 1) * tq]
            outs[h] = o if h % 2 == kv else pltpu.roll(o, half, 1)
    for p in range(WA_HEADS // 2):
        o_ref[b, :, p * LANES:(p + 1) * LANES] = jnp.where(
            lo, outs[2 * p], outs[2 * p + 1]).astype(o_ref.dtype)


def _wa_attention(proj, sink, qn_w, kn_w, cos_t, sin_t, col_q, col_k, col_v):
    bsz, n, _ = proj.shape
    nblk = n // TOK_BLK
    last = nblk - 1
    qw = WA_HEADS * HEAD_DIM
    cq, ck, cv = col_q // qw, col_k // LANES, col_v // LANES

    assert TOK_BLK % WA_WINDOW == 0
    per = TOK_BLK // WA_WINDOW

    def near(i, shift):
        if shift == 0:
            return i
        return jnp.clip(per * i - 1 if shift < 0 else per * (i + 1), 0, per * nblk - 1)

    def blk(col, shift):
        rows = TOK_BLK if shift == 0 else WA_WINDOW
        return pl.BlockSpec((1, rows, LANES), lambda b, i: (b, near(i, shift), col))

    def tab(shift):
        rows = TOK_BLK if shift == 0 else WA_WINDOW
        return pl.BlockSpec((rows, LANES), lambda b, i: (near(i, shift), 0))

    def ctx_blk(col):
        return pl.BlockSpec((1, TOK_BLK, LANES), lambda b, i: (b, 0, col))

    vec = pl.BlockSpec((1, LANES), lambda b, i: (0, 0))
    return pl.pallas_call(
        functools.partial(_wa_kernel, last=last),
        grid=(bsz, nblk),
        in_specs=[pl.BlockSpec(memory_space=pltpu.SMEM),
                  pl.BlockSpec((1, TOK_BLK, qw), lambda b, i: (b, i, cq)),
                  blk(ck, -1), blk(ck, 0), blk(ck, 1), blk(cv, -1), blk(cv, 0), blk(cv, 1),
                  ctx_blk(ck), ctx_blk(cv),
                  tab(0), tab(0), tab(-1), tab(-1), tab(1), tab(1), vec, vec],
        out_specs=pl.BlockSpec((1, TOK_BLK, qw), lambda b, i: (b, i, 0)),
        out_shape=jax.ShapeDtypeStruct((bsz, n, qw), BF16),
        compiler_params=_cparams(2), name="wa_attn",
    )(sink, proj, proj, proj, proj, proj, proj, proj, proj, proj,
      cos_t, sin_t, cos_t, sin_t, cos_t, sin_t,
      jnp.tile(qn_w, 2).reshape(1, LANES), jnp.tile(kn_w, 2).reshape(1, LANES))


def _rope_tables(n_ctx, t):
    quarter = HEAD_DIM // 4
    inv_freq = ROPE_THETA ** (-jnp.arange(quarter, dtype=F32) / quarter)
    pos = jnp.arange(t)
    lane = np.arange(LANES)
    in_head = lane % HEAD_DIM
    use_col = in_head >= HEAD_DIM // 2
    second = (in_head % (HEAD_DIM // 2)) >= quarter
    freq = inv_freq[in_head % quarter]
    p = jnp.where(use_col[None, :], (pos % GRID_W)[:, None], (pos // GRID_W)[:, None]).astype(F32)
    ang = p * freq[None, :]
    cos = jnp.cos(ang)
    sin = jnp.where(second[None, :], jnp.sin(ang), -jnp.sin(ang))
    cos = jnp.concatenate([jnp.ones((n_ctx, LANES), F32), cos], axis=0)
    sin = jnp.concatenate([jnp.zeros((n_ctx, LANES), F32), sin], axis=0)
    return cos, sin


HG_CHUNK = 128


def _hgrn_kernel(q_ref, f_ref, v_ref, hl_ref, *rest, layer, reverse, final):
    if final:
        g_ref, prev_ref, nw_ref, o_ref, st_ref = rest
    else:
        o_ref, st_ref = rest
    @pl.when(pl.program_id(0) == 0)
    def _():
        st_ref[...] = jnp.zeros_like(st_ref)

    a = hl_ref[0]
    e = jnp.exp(a - jnp.max(a, axis=0, keepdims=True))
    pr = e / jnp.sum(e, axis=0, keepdims=True)
    lb_all = jnp.zeros((1, a.shape[1]), F32)
    for j in range(1, layer + 1):
        lb_all = lb_all + pr[j:j + 1]

    for b, h in [(b, h) for b in range(q_ref.shape[0]) for h in range(HG_HEADS)]:
        lanes = slice(h * HG_DK, (h + 1) * HG_DK)
        lb = lb_all[:, lanes]
        qs = _silu(q_ref[b, :, lanes].astype(F32))
        f = lb + (1.0 - lb) * _sigmoid(f_ref[b, :, lanes])
        o, st = _hgrn_block(qs, 1.0 - f, jnp.log(jnp.maximum(f, LOG_FLOOR)),
                            v_ref[b, :, lanes].astype(F32), st_ref[b, h], reverse)
        st_ref[b, h] = st
        if final:
            tot = prev_ref[b, :, lanes] + o
            gated = _rms(tot, nw_ref[...]) * _silu(g_ref[b, :, lanes].astype(F32))
            o_ref[b, :, lanes] = gated.astype(o_ref.dtype)
        else:
            o_ref[b, :, lanes] = o


def _hgrn_block(qs, kk, g, v, st, reverse):
    c = qs.shape[0]
    n_sub = c // HG_CHUNK
    row = _iota((c, HG_DK), 0)
    ri = _iota((c, c), 0)
    ci = _iota((c, c), 1)
    block_xor = ri ^ ci

    tri = jnp.where(((ci >= ri) if reverse else (ci <= ri)) & (block_xor < HG_CHUNK), 1.0, 0.0)
    tri = tri.astype(BF16)
    g_hi = g.astype(BF16)
    rest = g - g_hi.astype(F32)
    g_mid = rest.astype(BF16)
    g_lo = (rest - g_mid.astype(F32)).astype(BF16)
    tdot = functools.partial(jnp.dot, preferred_element_type=F32)
    b = tdot(tri, g_hi) + (tdot(tri, g_mid) + tdot(tri, g_lo))

    def ref_rows(hs):
        blk = 2 * hs
        inner = hs if reverse else hs - 1
        if blk >= SUBLANES:
            x = b.reshape(c // blk, blk, HG_DK)
            return jnp.broadcast_to(x[:, inner:inner + 1, :], x.shape).reshape(c, HG_DK)
        x = b.reshape(c // SUBLANES, SUBLANES, HG_DK)
        sub = _iota(x.shape, 1)
        out = None
        for p in range(SUBLANES // blk):
            r = p * blk + inner
            piece = jnp.broadcast_to(x[:, r:r + 1, :], x.shape)
            out = piece if out is None else jnp.where(sub >= p * blk, piece, out)
        return out.reshape(c, HG_DK)

    att = jnp.zeros((c, c), F32)
    hs = HG_CHUNK // 2
    while hs >= 1:
        blk = 2 * hs
        decay = jnp.exp(-jnp.abs(b - ref_rows(hs)))
        in_block = row & (blk - 1)
        later_half = (in_block < hs) if reverse else (in_block >= hs)
        q_l = jnp.where(later_half, qs * decay, 0.0)
        k_l = jnp.where(later_half, 0.0, kk * decay)
        a_l = _dot_nt(q_l, k_l)
        att = att + (a_l if blk == c else jnp.where(block_xor < blk, a_l, 0.0))
        hs //= 2
    o = _dot(att, v) + jnp.sum(qs * kk, axis=-1, keepdims=True) * v

    q_dec = qs * jnp.exp(b)
    b3 = b.reshape(n_sub, HG_CHUNK, HG_DK)
    end = 0 if reverse else HG_CHUNK - 1
    b_end = jnp.broadcast_to(b3[:, end:end + 1, :], b3.shape).reshape(c, HG_DK)
    k_dec = kk * jnp.exp(b_end - b)
    outs = [None] * n_sub
    for s in (reversed(range(n_sub)) if reverse else range(n_sub)):
        rows = slice(s * HG_CHUNK, (s + 1) * HG_CHUNK)
        outs[s] = o[rows] + _dot_nt(q_dec[rows], st)
        st = st * jnp.exp(b_end[s * HG_CHUNK:s * HG_CHUNK + 1, :]) + _dot_tn(v[rows], k_dec[rows])
    return jnp.concatenate(outs, axis=0), st


def _hgrn_pass(hl, layer, reverse, q_src, f_src, v_src, final_args=None):
    bsz, n, _ = q_src[0].shape
    nchunk = n // TOK_BLK
    width = HG_HEADS * HG_DK

    def chunk(t):
        return jnp.where(t == 0, 0, nchunk - t) if reverse else t

    def blk(col):
        return pl.BlockSpec((bsz, TOK_BLK, width), lambda t: (0, chunk(t), col // width))

    in_specs = [blk(q_src[1]), blk(f_src[1]), blk(v_src[1]),
                pl.BlockSpec((1,) + hl.shape[1:], lambda t: (1 if reverse else 0, 0, 0))]
    args = [q_src[0], f_src[0], v_src[0], hl]
    final = final_args is not None
    if final:
        g_src, prev, norm_w = final_args
        in_specs += [blk(g_src[1]), blk(0), pl.BlockSpec((1, HG_DK), lambda t: (0, 0))]
        args += [g_src[0], prev, norm_w.reshape(1, HG_DK)]
    return pl.pallas_call(
        functools.partial(_hgrn_kernel, layer=layer, reverse=reverse, final=final),
        grid=(nchunk,),
        in_specs=in_specs,
        out_specs=blk(0),
        out_shape=jax.ShapeDtypeStruct((bsz, n, width), BF16 if final else F32),
        scratch_shapes=[pltpu.VMEM((bsz, HG_HEADS, HG_DK, HG_DK), F32)],
        compiler_params=_cparams(1), name="hgrn_bwd" if reverse else "hgrn_fwd",
    )(*args)


def _merge_kernel(x_ref, ya_ref, yb_ref, yc_ref, ga_ref, gb_ref, gc_ref, mod_ref,
                  wa_ref, wb_ref, wc_ref, wo_ref, o_ref, *, n_ctx, ctx_row):
    b, i = pl.program_id(0), pl.program_id(1)
    tm, d = x_ref.shape[1:]
    gate_of = lambda ref: _sigmoid(ref[0].astype(F32))
    merged = (gate_of(ga_ref) * _dot(ya_ref[0], wa_ref[...])
              + gate_of(gb_ref) * _dot(yb_ref[0], wb_ref[...])
              + gate_of(gc_ref) * _dot(yc_ref[0], wc_ref[...]))
    mix = _dot(merged, wo_ref[...])
    gate = _mod_rows(mod_ref, 2, b, i * tm, n_ctx, ctx_row, tm, d)
    o_ref[0] = x_ref[0] + gate * mix


def _merge(xa, ya, yb, yc, proj, mod_l, w_pa, w_pb, w_pc, w_out, col_gate, n_ctx, ctx_row):
    bsz, n, d = xa.shape
    tm = 704 if n % 704 == 0 else TOK_BLK
    g0 = col_gate // d
    tile = lambda w: pl.BlockSpec((1, tm, w), lambda b, i: (b, i, 0))
    gate = lambda k: pl.BlockSpec((1, tm, d), lambda b, i: (b, i, g0 + k))
    full = lambda w: pl.BlockSpec(w.shape, lambda b, i: (0, 0))
    return pl.pallas_call(
        functools.partial(_merge_kernel, n_ctx=n_ctx, ctx_row=ctx_row),
        grid=(bsz, n // tm),
        in_specs=[tile(d), tile(ya.shape[2]), tile(yb.shape[2]), tile(yc.shape[2]),
                  gate(0), gate(1), gate(2), full(mod_l),
                  full(w_pa), full(w_pb), full(w_pc), full(w_out)],
        out_specs=tile(d),
        out_shape=jax.ShapeDtypeStruct((bsz, n, d), F32),
        compiler_params=_cparams(2), name="merge",
    )(xa, ya, yb, yc, proj, proj, proj, mod_l, w_pa, w_pb, w_pc, w_out)


def _route_kernel(x_ref, mod_ref, nw_ref, wr_ref, br_ref, h_ref, idx_ref, wt_ref, cnt_ref,
                  meta_ref, carry_ref, *, n_ctx, ctx_row):
    b, i = pl.program_id(0), pl.program_id(1)
    tm, d = x_ref.shape[1:]

    @pl.when((b == 0) & (i == 0))
    def _():
        carry_ref[...] = jnp.zeros_like(carry_ref)

    h = _rms(x_ref[0], nw_ref[...])
    shift = _mod_rows(mod_ref, 3, b, i * tm, n_ctx, ctx_row, tm, d)
    scale = _mod_rows(mod_ref, 4, b, i * tm, n_ctx, ctx_row, tm, d)
    h = h * (1.0 + scale) + shift
    h_ref[0] = h.astype(h_ref.dtype)

    logits = _dot3_nt(wr_ref[...], h)
    ex = jnp.exp(logits - jnp.max(logits, axis=0, keepdims=True))
    probs = ex / jnp.sum(ex, axis=0, keepdims=True)
    sel = probs + br_ref[...]

    def row(x, r):
        return x[r:r + 1, :]

    best = None
    g_idx = None
    for g in range(N_GROUPS):
        r0 = g * EXPERTS_PER_GROUP
        a0, a1, a2, a3 = (row(sel, r0 + j) for j in range(EXPERTS_PER_GROUP))
        hi1, lo1 = jnp.maximum(a0, a1), jnp.minimum(a0, a1)
        hi2, lo2 = jnp.maximum(a2, a3), jnp.minimum(a2, a3)
        score = jnp.maximum(hi1, hi2) + jnp.maximum(jnp.minimum(hi1, hi2), jnp.maximum(lo1, lo2))
        if g == 0:
            best, g_idx = score, jnp.zeros_like(score, dtype=I32)
        else:
            better = score > best
            best = jnp.where(better, score, best)
            g_idx = jnp.where(better, g, g_idx)

    def pick(x, j):
        out = row(x, j)
        for g in range(1, N_GROUPS):
            out = jnp.where(g_idx == g, row(x, g * EXPERTS_PER_GROUP + j), out)
        return out

    in_grp = [pick(sel, j) for j in range(EXPERTS_PER_GROUP)]
    in_prob = [pick(probs, j) for j in range(EXPERTS_PER_GROUP)]

    def first_argmax(vals, skip):
        bv, bi, bp = None, None, None
        for j in range(EXPERTS_PER_GROUP):
            v = vals[j] if skip is None else jnp.where(skip == j, -jnp.inf, vals[j])
            if bv is None:
                bv, bi, bp = v, jnp.zeros_like(g_idx), in_prob[0]
            else:
                better = v > bv
                bv = jnp.where(better, v, bv)
                bi = jnp.where(better, j, bi)
                bp = jnp.where(better, in_prob[j], bp)
        return bi, bp

    loc0, p0 = first_argmax(in_grp, None)
    loc1, p1 = first_argmax(in_grp, loc0)
    e0 = g_idx * EXPERTS_PER_GROUP + loc0
    e1 = g_idx * EXPERTS_PER_GROUP + loc1
    wsum = p0 + p1
    w0, w1 = p0 / wsum, p1 / wsum

    er = _iota((N_EXPERTS, tm), 0)
    hit0 = er == e0
    hit1 = er == e1
    hot = jnp.where(hit0 | hit1, 1.0, 0.0)
    upper = jnp.where(_iota((tm, tm), 0) < _iota((tm, tm), 1), 1.0, 0.0)
    before = _dot(hot, upper)
    cnt = jnp.sum(hot, axis=1, keepdims=True)
    e_col = _iota((N_EXPERTS, 1), 0)

    cnt_pad = jnp.floor((cnt + (SUBLANES - 1)) * (1.0 / SUBLANES)) * SUBLANES
    off = jnp.zeros((N_EXPERTS, 1), F32)
    for e in range(N_EXPERTS - 1):
        off = off + jnp.where(e_col > e, cnt_pad[e:e + 1, :], 0.0)
    positions = [jnp.sum(jnp.where(hit, before + off, 0.0), axis=0, keepdims=True)
                 for hit in (hit0, hit1)]

    lane = _iota((N_EXPERTS, LANES), 1)
    carry = carry_ref[...]
    meta = jnp.zeros((N_EXPERTS, LANES), F32)
    for field, val in enumerate((carry, cnt, off)):
        meta = jnp.where(lane == field, val, meta)
    meta_ref[0] = meta.astype(I32)
    carry_ref[...] = carry + cnt_pad
    cnt_ref[...] = carry + cnt_pad

    idx_ref[...] = jnp.zeros_like(idx_ref)
    wt_ref[...] = jnp.zeros_like(wt_ref)
    for r, val in enumerate(positions):
        idx_ref[r:r + 1, :] = val.astype(I32)
    for r, val in enumerate((w0, w1)):
        wt_ref[r:r + 1, :] = val


def _route(x1, mod_l, norm_w, w_router_t, b_router, n_ctx, ctx_row):
    bsz, n, d = x1.shape
    tm = TOK_BLK
    nt = n // tm
    tile = pl.BlockSpec((1, tm, d), lambda b, i: (b, i, 0))
    lane_tile = pl.BlockSpec((SUBLANES, tm), lambda b, i: (0, b * nt + i))
    full = lambda a: pl.BlockSpec(a.shape, lambda b, i: (0,) * a.ndim)
    br = b_router.reshape(N_EXPERTS, 1)
    nw = norm_w.reshape(1, d)
    return pl.pallas_call(
        functools.partial(_route_kernel, n_ctx=n_ctx, ctx_row=ctx_row),
        grid=(bsz, nt),
        in_specs=[tile, full(mod_l), full(nw), full(w_router_t), full(br)],
        out_specs=[tile, lane_tile, lane_tile,
                   pl.BlockSpec((N_EXPERTS, LANES), lambda b, i: (0, 0)),
                   pl.BlockSpec((1, N_EXPERTS, LANES), lambda b, i: (b * nt + i, 0, 0))],
        out_shape=[jax.ShapeDtypeStruct((bsz, n, d), BF16),
                   jax.ShapeDtypeStruct((SUBLANES, bsz * n), I32),
                   jax.ShapeDtypeStruct((SUBLANES, bsz * n), F32),
                   jax.ShapeDtypeStruct((N_EXPERTS, LANES), F32),
                   jax.ShapeDtypeStruct((bsz * nt, N_EXPERTS, LANES), I32)],
        scratch_shapes=[pltpu.VMEM((N_EXPERTS, LANES), F32)],
        compiler_params=_cparams(2), name="route",
    )(x1, mod_l, nw, w_router_t, br)


def _round_up(x, m):
    return -(-x // m) * m


MOE_MAX_CHUNKS = TOK_BLK // MOE_CHUNK
MOE_SORT_ROWS = _round_up(2 * TOK_BLK + N_EXPERTS * (SUBLANES - 1), LANES)
META_FIELDS = 3


def _moe_blocks(n_assign, n_tiles):
    slack = N_EXPERTS * n_tiles * (SUBLANES - 1)
    return -(-(n_assign + slack) // MOE_BLK) + N_EXPERTS


def _run_copies(meta_ref, start_ref, tile, do, make_copy):
    def expert(e, carry):
        base = (tile * N_EXPERTS + e) * META_FIELDS
        before, rows, slot = meta_ref[base], meta_ref[base + 1], meta_ref[base + 2]
        padded = ((rows + SUBLANES - 1) // SUBLANES) * SUBLANES
        n_big = padded // MOE_CHUNK
        dst = start_ref[e] + before
        for c in range(MOE_MAX_CHUNKS):
            @pl.when(n_big > c)
            def _():
                do(make_copy(pl.multiple_of(slot + c * MOE_CHUNK, SUBLANES),
                             pl.multiple_of(dst + c * MOE_CHUNK, SUBLANES), MOE_CHUNK))
        tail = n_big * MOE_CHUNK
        for k in range(MOE_CHUNK // SUBLANES - 1):
            @pl.when(padded - tail > k * SUBLANES)
            def _():
                do(make_copy(pl.multiple_of(slot + tail + k * SUBLANES, SUBLANES),
                             pl.multiple_of(dst + tail + k * SUBLANES, SUBLANES), SUBLANES))
        return carry
    lax.fori_loop(0, N_EXPERTS, expert, 0)


def _dispatch_kernel(meta_ref, cnt_ref, pos_ref, h_ref, xs_ref, blk_e_ref, blk_rows_ref,
                     start_ref, sort_ref, zero_ref, sem, *, nblk):
    t = pl.program_id(0)
    nt = pl.num_programs(0)
    tm = h_ref.shape[0]

    @pl.when(t == 0)
    def _():
        def expert(e, end):
            start_ref[e] = end
            return end + ((cnt_ref[e] + MOE_BLK - 1) // MOE_BLK) * MOE_BLK
        start_ref[N_EXPERTS] = lax.fori_loop(0, N_EXPERTS, expert, 0)

        def block(j, carry):
            def count(e, acc):
                return acc + jnp.where(start_ref[e + 1] <= j * MOE_BLK, 1, 0)
            e = jnp.minimum(lax.fori_loop(0, N_EXPERTS, count, 0), N_EXPERTS - 1)
            blk_e_ref[j] = e
            blk_rows_ref[j] = jnp.clip(cnt_ref[e] - (j * MOE_BLK - start_ref[e]), 0, MOE_BLK)
            return carry
        lax.fori_loop(0, nblk, block, 0)

    r = _iota((MOE_SORT_ROWS, tm), 0)
    onehot = jnp.where((r == pos_ref[0:1, :]) | (r == pos_ref[1:2, :]), 1.0, 0.0)
    buf = t % 2
    sort_ref[buf] = _dot(onehot, h_ref[...])

    def copies(which):
        def make_copy(src_row, dst_row, rows):
            return pltpu.make_async_copy(sort_ref.at[which, pl.ds(src_row, rows), :],
                                         xs_ref.at[pl.ds(dst_row, rows), :], sem.at[which])
        return make_copy

    _run_copies(meta_ref, start_ref, t, lambda cp: cp.start(), copies(buf))

    @pl.when(t > 0)
    def _():
        _run_copies(meta_ref, start_ref, t - 1, lambda cp: cp.wait(), copies(1 - buf))

    @pl.when(t == nt - 1)
    def _():
        _run_copies(meta_ref, start_ref, t, lambda cp: cp.wait(), copies(buf))

        zero_ref[...] = jnp.zeros_like(zero_ref)

        def zero_copy(row, size):
            return pltpu.make_async_copy(zero_ref.at[pl.ds(0, size), :],
                                         xs_ref.at[pl.ds(pl.multiple_of(row, SUBLANES), size), :],
                                         sem.at[buf])

        def fill(do):
            def expert(e, carry):
                lo = start_ref[e] + cnt_ref[e]
                gap = start_ref[e + 1] - lo
                n_big = gap // MOE_CHUNK

                def big(k, c2):
                    do(zero_copy(lo + k * MOE_CHUNK, MOE_CHUNK))
                    return c2
                lax.fori_loop(0, n_big, big, 0)

                def small(k, c2):
                    do(zero_copy(lo + n_big * MOE_CHUNK + k * SUBLANES, SUBLANES))
                    return c2
                lax.fori_loop(0, (gap - n_big * MOE_CHUNK) // SUBLANES, small, 0)
                return carry
            lax.fori_loop(0, N_EXPERTS, expert, 0)

            def unused(k, carry):
                do(zero_copy(start_ref[N_EXPERTS] + k * MOE_ZERO_ROWS, MOE_ZERO_ROWS))
                return carry
            lax.fori_loop(0, (nblk * MOE_BLK - start_ref[N_EXPERTS]) // MOE_ZERO_ROWS, unused, 0)

        fill(lambda cp: cp.start())
        fill(lambda cp: cp.wait())


def _dispatch(meta, counts, pos_rows, h_flat, nblk):
    ntok, d = h_flat.shape
    tm = TOK_BLK
    whole_smem = pl.BlockSpec(memory_space=pltpu.SMEM)
    return pl.pallas_call(
        functools.partial(_dispatch_kernel, nblk=nblk),
        grid=(ntok // tm,),
        in_specs=[whole_smem, whole_smem,
                  pl.BlockSpec((SUBLANES, tm), lambda i: (0, i)),
                  pl.BlockSpec((tm, d), lambda i: (i, 0))],
        out_specs=[pl.BlockSpec(memory_space=pl.ANY), whole_smem, whole_smem, whole_smem],
        out_shape=[jax.ShapeDtypeStruct((nblk * MOE_BLK, d), F32),
                   jax.ShapeDtypeStruct((nblk,), I32),
                   jax.ShapeDtypeStruct((nblk,), I32),
                   jax.ShapeDtypeStruct((N_EXPERTS + 1,), I32)],
        scratch_shapes=[pltpu.VMEM((2, MOE_SORT_ROWS, d), F32), pltpu.VMEM((MOE_ZERO_ROWS, d), F32),
                        pltpu.SemaphoreType.DMA((2,))],
        compiler_params=_cparams(1), name="dispatch",
    )(meta, counts, pos_rows, h_flat)


def _expert_kernel(blk_e_ref, blk_rows_ref, x_ref, wg_ref, wu_ref, wd_ref, o_ref, g_bf, u_bf, d_bf):
    i = pl.program_id(0)
    rows = blk_rows_ref[i]

    @pl.when((i == 0) | (blk_e_ref[i] != blk_e_ref[jnp.maximum(i - 1, 0)]))
    def _():
        g_bf[...] = wg_ref[0, 0].astype(BF16)
        u_bf[...] = wu_ref[0, 0].astype(BF16)
        d_bf[...] = wd_ref[0, 0].astype(BF16)

    @pl.when(rows > 0)
    def _():
        x = x_ref[...].astype(BF16)
        hid = _silu(_dot(x, g_bf[...])) * _dot(x, u_bf[...])
        o_ref[...] = _dot(hid, d_bf[...])

    @pl.when(rows == 0)
    def _():
        o_ref[...] = jnp.zeros_like(o_ref)


def _experts(blk_e, blk_rows, xs, w_gate, w_up, w_down, layer):
    d = xs.shape[1]
    ff = w_gate.shape[3]
    nblk = blk_e.shape[0]
    weight = lambda shape: pl.BlockSpec((1, 1) + shape, lambda i, be, rows: (layer, be[i], 0, 0))
    return pl.pallas_call(
        _expert_kernel,
        grid_spec=pltpu.PrefetchScalarGridSpec(
            num_scalar_prefetch=2, grid=(nblk,),
            in_specs=[pl.BlockSpec((MOE_BLK, d), lambda i, be, rows: (i, 0)),
                      weight((d, ff)), weight((d, ff)), weight((ff, d))],
            out_specs=pl.BlockSpec((MOE_BLK, d), lambda i, be, rows: (i, 0)),
            scratch_shapes=[pltpu.VMEM((d, ff), BF16), pltpu.VMEM((d, ff), BF16),
                            pltpu.VMEM((ff, d), BF16)]),
        out_shape=jax.ShapeDtypeStruct(xs.shape, F32),
        compiler_params=_cparams(1), name="experts",
    )(blk_e, blk_rows, xs, w_gate, w_up, w_down)


def _combine_kernel(meta_ref, start_ref, x_ref, pos_ref, wt_ref, mod_ref, ys_ref, o_ref,
                    gath_ref, sem, *, n_ctx, ctx_row):
    b, i = pl.program_id(0), pl.program_id(1)
    nt = pl.num_programs(1)
    t = b * nt + i
    last = pl.num_programs(0) * nt - 1
    tm, d = x_ref.shape[1:]
    buf = t % 2

    def copies(which):
        def make_copy(slot_row, ys_row, rows):
            return pltpu.make_async_copy(ys_ref.at[pl.ds(ys_row, rows), :],
                                         gath_ref.at[which, pl.ds(slot_row, rows), :],
                                         sem.at[which])
        return make_copy

    @pl.when(t == 0)
    def _():
        gath_ref[...] = jnp.zeros_like(gath_ref)
        _run_copies(meta_ref, start_ref, t, lambda cp: cp.start(), copies(buf))

    @pl.when(t < last)
    def _():
        _run_copies(meta_ref, start_ref, t + 1, lambda cp: cp.start(), copies(1 - buf))

    _run_copies(meta_ref, start_ref, t, lambda cp: cp.wait(), copies(buf))

    col = _iota((tm, MOE_SORT_ROWS), 1)
    pos = pos_ref[...]
    wt = wt_ref[...]
    sel = (jnp.where(col == pos[:, 0:1], wt[:, 0:1], 0.0)
           + jnp.where(col == pos[:, 1:2], wt[:, 1:2], 0.0))
    hi, lo = _hi_lo(sel)
    rows = gath_ref[buf].astype(BF16)
    y = (jnp.dot(hi, rows, preferred_element_type=F32)
         + jnp.dot(lo, rows, preferred_element_type=F32))
    gate = _mod_rows(mod_ref, 5, b, i * tm, n_ctx, ctx_row, tm, d)
    o_ref[0] = x_ref[0] + gate * y


def _combine(meta, start, x1, pos_cols, wt_cols, mod_l, ys, n_ctx, ctx_row, latent_only):
    bsz, n, d = x1.shape
    tm = TOK_BLK
    nt = n // tm
    whole_smem = pl.BlockSpec(memory_space=pltpu.SMEM)
    cols = pl.BlockSpec((tm, 2), lambda b, i: (b * nt + i, 0))
    if latent_only:
        assert n_ctx == tm
        out_spec = pl.BlockSpec((1, tm, d), lambda b, i: (b, jnp.maximum(i - 1, 0), 0))
        out_rows = n - n_ctx
    else:
        out_spec = pl.BlockSpec((1, tm, d), lambda b, i: (b, i, 0))
        out_rows = n
    return pl.pallas_call(
        functools.partial(_combine_kernel, n_ctx=n_ctx, ctx_row=ctx_row),
        grid=(bsz, nt),
        in_specs=[whole_smem, whole_smem,
                  pl.BlockSpec((1, tm, d), lambda b, i: (b, i, 0)), cols, cols,
                  pl.BlockSpec(mod_l.shape, lambda b, i: (0, 0)),
                  pl.BlockSpec(memory_space=pl.ANY)],
        out_specs=out_spec,
        out_shape=jax.ShapeDtypeStruct((bsz, out_rows, d), F32),
        scratch_shapes=[pltpu.VMEM((2, MOE_SORT_ROWS, d), F32), pltpu.SemaphoreType.DMA((2,))],
        compiler_params=_cparams(2), name="combine",
    )(meta, start, x1, pos_cols, wt_cols, mod_l, ys)


def _layer(xa, mod_l, layer, n_ctx, ctx_row, p, rope):
    bsz, n, d = xa.shape
    na_w = NA_HEADS * HEAD_DIM
    hg_w = HG_HEADS * HG_DK
    wa_qw = WA_HEADS * HEAD_DIM
    wa_kvw = WA_KV_HEADS * HEAD_DIM
    hg0 = 3 * na_w
    wa0 = hg0 + 5 * hg_w
    gate0 = wa0 + wa_qw + 2 * wa_kvw
    n_cols = gate0 + N_BRANCHES * d
    assert p["w_in"].shape[2] == n_cols
    w_att = _take_cols(p["w_in"], layer, [(gate0, n_cols), (0, hg0), (wa0, gate0)])
    w_hg_qig = _take_cols(p["w_in"], layer, [(hg0, hg0 + hg_w), (hg0 + 3 * hg_w, wa0)])
    w_hg_f = _take_cols(p["w_in"], layer, [(hg0 + hg_w, hg0 + 3 * hg_w)])
    col_gate = 0
    col_na = [N_BRANCHES * d + k * na_w for k in range(3)]
    col_wq = N_BRANCHES * d + 3 * na_w
    col_wk = col_wq + wa_qw
    col_wv = col_wk + wa_kvw
    proj, h = _inproj(xa, mod_l, p["norm1"], w_att, n_ctx, ctx_row, 1792, BF16)
    hg_qig = _project(h, w_hg_qig, 768, BF16)
    hg_f = _project(h, w_hg_f, 1024, F32)

    ya = _na_attention(proj, _na_bias(p["na_rpb"]), p["na_q_norm"], p["na_k_norm"], *col_na)
    yc = _wa_attention(proj, p["wa_sink"], p["wa_q_norm"], p["wa_k_norm"], rope[0], rope[1],
                       col_wq, col_wk, col_wv)
    hg_q, hg_i, hg_g = ((hg_qig, k * hg_w) for k in range(3))
    o_f = _hgrn_pass(p["hg_lower"], layer, False, hg_q, (hg_f, 0), hg_i)
    yb = _hgrn_pass(p["hg_lower"], layer, True, hg_q, (hg_f, hg_w), hg_i,
                    final_args=(hg_g, o_f, p["hg_norm"]))

    bf = lambda w: w.astype(BF16)
    x1 = _merge(xa, ya, yb, yc, proj, mod_l, bf(p["w_pa"]), bf(p["w_pb"]), bf(p["w_pc"]),
                bf(p["w_out"]), col_gate, n_ctx, ctx_row)

    h2, idx, wts, counts, meta = _route(x1, mod_l, p["norm2"], p["w_router"].T, p["b_router"],
                                        n_ctx, ctx_row)
    ntok = bsz * n
    meta_flat = meta[:, :, :META_FIELDS].reshape(-1)
    xs, blk_e, blk_rows, start = _dispatch(meta_flat, counts[:, 0].astype(I32), idx,
                                           h2.reshape(ntok, d),
                                           _moe_blocks(2 * ntok, ntok // TOK_BLK))
    ys = _experts(blk_e, blk_rows, xs, p["w_gate"], p["w_up"], p["w_down"], layer)
    return _combine(meta_flat, start, x1, idx[:2].T, wts[:2].T, mod_l, ys, n_ctx, ctx_row,
                    latent_only=p["last"])


def kernel(x, c, ctx, c_ctx, w_ada, b_ada, norm1, norm2, w_in, na_q_norm, na_k_norm, na_rpb, hg_lower,
           hg_norm, wa_q_norm, wa_k_norm, wa_sink, w_pa, w_pb, w_pc, w_out, w_router, b_router,
           w_gate, w_up, w_down):
    bsz, t, d = x.shape
    n_ctx = ctx.shape[1]
    depth = w_ada.shape[0]
    assert n_ctx == TOK_BLK and t % TOK_BLK == 0 and t // TOK_BLK >= 3
    assert t % GRID_W == 0 and bsz + 1 <= SUBLANES

    xa = jnp.concatenate([ctx, x], axis=1)
    cond = jnp.concatenate([c, c_ctx[None], jnp.zeros((SUBLANES - bsz - 1, d), F32)], axis=0)
    mod = _ada(cond, w_ada, b_ada)
    rope = _rope_tables(n_ctx, t)
    for l in range(depth):
        p = dict(norm1=norm1[l], norm2=norm2[l], w_in=w_in, na_q_norm=na_q_norm[l],
                 na_k_norm=na_k_norm[l], na_rpb=na_rpb[l], hg_lower=hg_lower, hg_norm=hg_norm[l],
                 wa_q_norm=wa_q_norm[l], wa_k_norm=wa_k_norm[l], wa_sink=wa_sink[l],
                 w_pa=w_pa[l], w_pb=w_pb[l], w_pc=w_pc[l], w_out=w_out[l],
                 w_router=w_router, b_router=b_router, w_gate=w_gate, w_up=w_up,
                 w_down=w_down, last=l == depth - 1)
        xa = _layer(xa, mod[l], l, n_ctx, bsz, p, rope)
    return xa
```

```python
import functools

import numpy as np
import jax
import jax.numpy as jnp
from jax import lax
from jax.experimental import pallas as pl
from jax.experimental.pallas import tpu as pltpu

F32 = jnp.float32
BF16 = jnp.bfloat16
I32 = jnp.int32

GRID_W = 64
NA_HEADS = 8
NA_ROWS = 8
NA_COLS = 16
HEAD_DIM = 64
HG_HEADS = 4
HG_DK = 128
LOG_FLOOR = 1e-30
WA_HEADS = 8
WA_KV_HEADS = 2
WA_WINDOW = 128
ROPE_THETA = 10000.0
N_EXPERTS = 16
N_GROUPS = 4
EXPERTS_PER_GROUP = N_EXPERTS // N_GROUPS
N_BRANCHES = 3
NEG_INF = -1e30
RMS_EPS = 1e-6

LANES = 128
SUBLANES = 8
VMEM_LIMIT = 56 * 1024 * 1024

TOK_BLK = 256
MOE_BLK = 1024
MOE_CHUNK = 32
MOE_ZERO_ROWS = 256


def _cparams(n_axes):
    return pltpu.CompilerParams(
        dimension_semantics=("arbitrary",) * n_axes, vmem_limit_bytes=VMEM_LIMIT)


def _sigmoid(x):
    return 1.0 / (1.0 + jnp.exp(-x))


def _silu(x):
    return x * _sigmoid(x)


def _dot(a, b):
    return jnp.dot(a.astype(BF16), b.astype(BF16), preferred_element_type=F32)


def _dot_nt(a, b):
    return lax.dot_general(a.astype(BF16), b.astype(BF16), (((1,), (1,)), ((), ())),
                           preferred_element_type=F32)


def _dot_tn(a, b):
    return lax.dot_general(a.astype(BF16), b.astype(BF16), (((0,), (0,)), ((), ())),
                           preferred_element_type=F32)


def _hi_lo(a):
    hi = a.astype(BF16)
    lo = (a - hi.astype(F32)).astype(BF16)
    return hi, lo


def _dot3(a, b):
    ah, al = _hi_lo(a)
    bh, bl = _hi_lo(b)
    d = functools.partial(jnp.dot, preferred_element_type=F32)
    return d(ah, bh) + (d(ah, bl) + d(al, bh))


def _dot3_nt(a, b):
    ah, al = _hi_lo(a)
    bh, bl = _hi_lo(b)
    d = functools.partial(lax.dot_general, dimension_numbers=(((1,), (1,)), ((), ())),
                          preferred_element_type=F32)
    return d(ah, bh) + (d(ah, bl) + d(al, bh))


def _rms(x, w):
    return x * lax.rsqrt(jnp.mean(x * x, axis=-1, keepdims=True) + RMS_EPS) * w


def _iota(shape, dim):
    return lax.broadcasted_iota(I32, shape, dim)


def _ada_kernel(cond_ref, w_ref, b_ref, o_ref):
    o_ref[0] = _dot3(_silu(cond_ref[...]), w_ref[0]) + b_ref[0]


def _ada(cond, w_ada, b_ada):
    depth, d, d6 = w_ada.shape
    rows = cond.shape[0]
    tn = d6 // 4
    return pl.pallas_call(
        _ada_kernel,
        grid=(depth, d6 // tn),
        in_specs=[pl.BlockSpec((rows, d), lambda l, j: (0, 0)),
                  pl.BlockSpec((1, d, tn), lambda l, j: (l, 0, j)),
                  pl.BlockSpec((1, 1, tn), lambda l, j: (l, 0, j))],
        out_specs=pl.BlockSpec((1, rows, tn), lambda l, j: (l, 0, j)),
        out_shape=jax.ShapeDtypeStruct((depth, rows, d6), F32),
        compiler_params=_cparams(2), name="ada",
    )(cond, w_ada, b_ada.reshape(depth, 1, d6))


def _mod_rows(mod_ref, which, b, row0, n_ctx, ctx_row, tm, d):
    lat = mod_ref[pl.ds(b, 1), which * d:(which + 1) * d]
    cx = mod_ref[ctx_row:ctx_row + 1, which * d:(which + 1) * d]
    row = row0 + _iota((tm, d), 0)
    return jnp.where(row < n_ctx, cx, lat)


def _inproj_kernel(x_ref, mod_ref, nw_ref, w_ref, o_ref, h_ref, *, n_ctx, ctx_row):
    b, i, j = pl.program_id(0), pl.program_id(1), pl.program_id(2)
    tm, d = h_ref.shape[1:]

    @pl.when(j == 0)
    def _():
        h = _rms(x_ref[0], nw_ref[...])
        shift = _mod_rows(mod_ref, 0, b, i * tm, n_ctx, ctx_row, tm, d)
        scale = _mod_rows(mod_ref, 1, b, i * tm, n_ctx, ctx_row, tm, d)
        h_ref[0] = (h * (1.0 + scale) + shift).astype(BF16)

    o_ref[0] = jnp.dot(h_ref[0], w_ref[...], preferred_element_type=F32).astype(o_ref.dtype)


def _matmul_kernel(h_ref, w_ref, o_ref):
    o_ref[0] = jnp.dot(h_ref[0], w_ref[...], preferred_element_type=F32).astype(o_ref.dtype)


def _project(h, w, tn, out_dtype):
    bsz, n, d = h.shape
    cols = w.shape[1]
    tm = 1408 if n % 1408 == 0 else TOK_BLK
    return pl.pallas_call(
        _matmul_kernel,
        grid=(bsz, n // tm, cols // tn),
        in_specs=[pl.BlockSpec((1, tm, d), lambda b, i, j: (b, i, 0)),
                  pl.BlockSpec((d, tn), lambda b, i, j: (0, j))],
        out_specs=pl.BlockSpec((1, tm, tn), lambda b, i, j: (b, i, j)),
        out_shape=jax.ShapeDtypeStruct((bsz, n, cols), out_dtype),
        compiler_params=_cparams(3), name="project",
    )(h, w)


W_IN_COL_BLK = 256


def _take_cols_kernel(src_ref, w_ref, o_ref):
    del src_ref
    o_ref[...] = w_ref[0].astype(BF16)


def _take_cols(w_in, layer, col_ranges):
    d = w_in.shape[1]
    blk = W_IN_COL_BLK
    src = []
    for start, stop in col_ranges:
        assert start % blk == 0 and stop % blk == 0
        src += list(range(start // blk, stop // blk))
    return pl.pallas_call(
        _take_cols_kernel,
        grid_spec=pltpu.PrefetchScalarGridSpec(
            num_scalar_prefetch=1, grid=(len(src),),
            in_specs=[pl.BlockSpec((1, d, blk), lambda j, src: (layer, 0, src[j]))],
            out_specs=pl.BlockSpec((d, blk), lambda j, src: (0, j))),
        out_shape=jax.ShapeDtypeStruct((d, len(src) * blk), BF16),
        compiler_params=_cparams(1), name="take_cols",
    )(jnp.asarray(src, I32), w_in)


def _inproj(xa, mod_l, norm_w, w, n_ctx, ctx_row, tn, out_dtype):
    bsz, n, d = xa.shape
    cols = w.shape[1]
    tm = 1408 if n % 1408 == 0 else TOK_BLK
    return pl.pallas_call(
        functools.partial(_inproj_kernel, n_ctx=n_ctx, ctx_row=ctx_row),
        grid=(bsz, n // tm, cols // tn),
        in_specs=[pl.BlockSpec((1, tm, d), lambda b, i, j: (b, i, 0)),
                  pl.BlockSpec(mod_l.shape, lambda b, i, j: (0, 0)),
                  pl.BlockSpec((1, d), lambda b, i, j: (0, 0)),
                  pl.BlockSpec((d, tn), lambda b, i, j: (0, j))],
        out_specs=[pl.BlockSpec((1, tm, tn), lambda b, i, j: (b, i, j)),
                   pl.BlockSpec((1, tm, d), lambda b, i, j: (b, i, 0))],
        out_shape=[jax.ShapeDtypeStruct((bsz, n, cols), out_dtype),
                   jax.ShapeDtypeStruct((bsz, n, d), BF16)],
        compiler_params=_cparams(3), name="inproj",
    )(xa, mod_l, norm_w.reshape(1, d), w)


def _pair_norm(x, w):
    lane = _iota(x.shape, 1)
    lo = lane < HEAD_DIM
    sq = x * x
    s_lo = jnp.sum(jnp.where(lo, sq, 0.0), axis=-1, keepdims=True)
    s_hi = jnp.sum(jnp.where(lo, 0.0, sq), axis=-1, keepdims=True)
    ms = jnp.where(lo, s_lo, s_hi) * (1.0 / HEAD_DIM)
    return x * lax.rsqrt(ms + RMS_EPS) * w


LOG2E = 1.4426950408889634


def _ones_beside(v, value_lanes):
    return jnp.where(value_lanes, v, 1.0).astype(BF16)


def _softmax_av(s_loc, s_ctx, v_loc, v_ctx, sink):
    m = jnp.maximum(jnp.max(s_loc, axis=-1, keepdims=True), jnp.max(s_ctx, axis=-1, keepdims=True))
    if sink is not None:
        m = jnp.maximum(m, sink)
    p_loc = jnp.exp2(s_loc - m).astype(BF16)
    p_ctx = jnp.exp2(s_ctx - m).astype(BF16)
    acc = (jnp.dot(p_loc, v_loc, preferred_element_type=F32)
           + jnp.dot(p_ctx, v_ctx, preferred_element_type=F32))
    den = pltpu.roll(acc, LANES // 2, 1)
    if sink is not None:
        den = den + jnp.exp2(sink - m)
    return acc / den


NA_QROWS = TOK_BLK // GRID_W
NA_KROWS = 3 * NA_QROWS
N_RPB_COLS = 2 * NA_COLS - 1
N_RPB_ROWS = 2 * NA_ROWS - 1
NA_VARIANTS = 4


def _na_row_valid(variant, a, k):
    if variant == 0:
        return False
    if variant == 1:
        return NA_QROWS <= k < NA_QROWS + NA_ROWS
    if variant == 3:
        return k < NA_ROWS
    return a <= k < a + NA_ROWS


def _na_bias_kernel(rpb_ref, o_ref):
    h = pl.program_id(0)
    qc = _iota((GRID_W, GRID_W), 0)
    kc = _iota((GRID_W, GRID_W), 1)
    col_lo = jnp.clip(qc - NA_COLS // 2, 0, GRID_W - NA_COLS)
    col_ok = (kc >= col_lo) & (kc < col_lo + NA_COLS)
    d_col = kc - qc + (NA_COLS - 1)
    masked = jnp.full((GRID_W, GRID_W), NEG_INF, F32)
    blocks = []
    for dr in range(N_RPB_ROWS):
        acc = jnp.zeros((GRID_W, GRID_W), F32)
        for dc in range(N_RPB_COLS):
            val = rpb_ref[(h * N_RPB_ROWS + dr) * N_RPB_COLS + dc]
            acc = jnp.where(d_col == dc, val, acc)
        blocks.append(jnp.where(col_ok, acc * LOG2E, NEG_INF))
    for variant in range(NA_VARIANTS):
        for a in range(NA_QROWS):
            for k in range(NA_KROWS):
                dr = k - NA_QROWS - a + NA_ROWS - 1
                ok = _na_row_valid(variant, a, k) and 0 <= dr < N_RPB_ROWS
                o_ref[variant, 0, a * GRID_W:(a + 1) * GRID_W, k * GRID_W:(k + 1) * GRID_W] = (
                    blocks[dr] if ok else masked)


def _na_bias(rpb):
    heads = rpb.shape[0]
    return pl.pallas_call(
        _na_bias_kernel,
        grid=(heads,),
        in_specs=[pl.BlockSpec(memory_space=pltpu.SMEM)],
        out_specs=pl.BlockSpec((NA_VARIANTS, 1, TOK_BLK, 3 * TOK_BLK), lambda h: (0, h, 0, 0)),
        out_shape=jax.ShapeDtypeStruct((NA_VARIANTS, heads, TOK_BLK, 3 * TOK_BLK), F32),
        compiler_params=_cparams(1), name="na_bias",
    )(rpb.reshape(-1))


def _na_kernel(q_ref, kp_ref, kc_ref, kn_ref, vp_ref, vc_ref, vn_ref, kx_ref, vx_ref,
               bias_ref, qn_ref, kn_w_ref, o_ref):
    tq = q_ref.shape[1]
    lo = _iota((tq, LANES), 1) < HEAD_DIM
    lo_k = _iota((3 * tq, LANES), 1) < HEAD_DIM
    qn_w = qn_ref[...] * (HEAD_DIM ** -0.5 * LOG2E)
    for b, p in [(b, p) for b in range(q_ref.shape[0]) for p in range(NA_HEADS // 2)]:
        pair = slice(p * LANES, (p + 1) * LANES)
        f32 = lambda ref: ref[b, :, pair].astype(F32)
        q = _pair_norm(f32(q_ref), qn_w)
        k_loc = _pair_norm(jnp.concatenate([f32(kp_ref), f32(kc_ref), f32(kn_ref)], axis=0),
                           kn_w_ref[...]).astype(BF16)
        k_ctx = _pair_norm(f32(kx_ref), kn_w_ref[...]).astype(BF16)
        v_loc = jnp.concatenate([f32(vp_ref), f32(vc_ref), f32(vn_ref)], axis=0)
        v_ctx = f32(vx_ref)
        q2 = jnp.concatenate([jnp.where(lo, q, 0.0), jnp.where(lo, 0.0, q)], axis=0).astype(BF16)
        s_loc2 = _dot_nt(q2, k_loc)
        s_ctx2 = _dot_nt(q2, k_ctx)
        outs = []
        for h in range(2):
            mine = lo if h == 0 else ~lo
            mine_k = lo_k if h == 0 else ~lo_k
            rows = slice(h * tq, (h + 1) * tq)
            outs.append(_softmax_av(s_loc2[rows] + bias_ref[0, 2 * p + h], s_ctx2[rows],
                                    _ones_beside(v_loc, mine_k), _ones_beside(v_ctx, mine), None))
        o_ref[b, :, pair] = jnp.where(lo, outs[0], outs[1]).astype(o_ref.dtype)


def _na_attention(proj, bias, qn_w, kn_w, col_q, col_k, col_v):
    bsz, n, _ = proj.shape
    nblk = n // TOK_BLK
    last = nblk - 1
    width = NA_HEADS * HEAD_DIM
    cq, ck, cv = col_q // width, col_k // width, col_v // width

    def blk(col, shift):
        return pl.BlockSpec((bsz, TOK_BLK, width),
                            lambda i: (0, jnp.clip(i + shift, 0, last), col))

    def ctx_blk(col):
        return pl.BlockSpec((bsz, TOK_BLK, width), lambda i: (0, 0, col))

    def variant(i):
        return (jnp.where(i == 0, 0, jnp.where(i == 1, 1, jnp.where(i == last, 3, 2))), 0, 0, 0)

    vec = pl.BlockSpec((1, LANES), lambda i: (0, 0))
    return pl.pallas_call(
        _na_kernel,
        grid=(nblk,),
        in_specs=[blk(cq, 0), blk(ck, -1), blk(ck, 0), blk(ck, 1), blk(cv, -1), blk(cv, 0), blk(cv, 1),
                  ctx_blk(ck), ctx_blk(cv),
                  pl.BlockSpec((1, NA_HEADS, TOK_BLK, 3 * TOK_BLK), variant), vec, vec],
        out_specs=pl.BlockSpec((bsz, TOK_BLK, width), lambda i: (0, i, 0)),
        out_shape=jax.ShapeDtypeStruct((bsz, n, width), BF16),
        compiler_params=_cparams(1), name="na_attn",
    )(proj, proj, proj, proj, proj, proj, proj, proj, proj, bias,
      jnp.tile(qn_w, 2).reshape(1, LANES), jnp.tile(kn_w, 2).reshape(1, LANES))


WA_STACK = 4


def _rope(x, cos, sin_signed):
    lane = _iota(x.shape, 1)
    first = (lane & (HEAD_DIM // 2 - 1)) < HEAD_DIM // 4
    quarter = HEAD_DIM // 4
    partner = jnp.where(first, pltpu.roll(x, LANES - quarter, 1), pltpu.roll(x, quarter, 1))
    return x * cos + partner * sin_signed


def _wa_kernel(sink_ref, q_ref, kp_ref, kc_ref, kn_ref, vp_ref, vc_ref, vn_ref, kx_ref, vx_ref,
               cq_ref, sq_ref, cp_ref, sp_ref, cn_ref, sn_ref, qn_ref, kn_w_ref, o_ref, *, last):
    i = pl.program_id(1)
    tq = q_ref.shape[1]
    tn = kp_ref.shape[1]
    n_loc = tq + 2 * tn
    cos_loc = jnp.concatenate([cp_ref[...], cq_ref[...], cn_ref[...]], axis=0)
    sin_loc = jnp.concatenate([sp_ref[...], sq_ref[...], sn_ref[...]], axis=0)
    qn_w = qn_ref[...] * (HEAD_DIM ** -0.5 * LOG2E)

    qi = _iota((tq, n_loc), 0)
    kj = _iota((tq, n_loc), 1)
    rel = kj - tn - qi
    lo_col = jnp.where(i >= 2, 0, tn)
    hi_col = jnp.where(i >= 1, jnp.where(i < last, n_loc, tn + tq), 0)
    ok = (rel >= -WA_WINDOW) & (rel <= WA_WINDOW) & (kj >= lo_col) & (kj < hi_col)
    mask = jnp.where(ok, 0.0, NEG_INF)

    mask_g = jnp.concatenate([mask] * WA_STACK, axis=0)
    for b in range(q_ref.shape[0]):
        _wa_one_batch(b, sink_ref, q_ref, (kp_ref, kc_ref, kn_ref), (vp_ref, vc_ref, vn_ref),
                      kx_ref, vx_ref, (cos_loc, sin_loc), (cq_ref[...], sq_ref[...]), qn_w,
                      kn_w_ref[...], mask_g, o_ref)


def _wa_one_batch(b, sink_ref, q_ref, k_refs, v_refs, kx_ref, vx_ref, rope_loc, rope_q, qn_w, kn_w,
                  mask_g, o_ref):
    tq = q_ref.shape[1]
    half = LANES // 2
    group = WA_HEADS // WA_KV_HEADS
    f32 = lambda ref: ref[b].astype(F32)
    k_loc = _pair_norm(jnp.concatenate([f32(r) for r in k_refs], axis=0), kn_w)
    k_loc = _rope(k_loc, *rope_loc).astype(BF16)
    k_ctx = _pair_norm(f32(kx_ref), kn_w).astype(BF16)
    v_loc = jnp.concatenate([f32(r) for r in v_refs], axis=0)
    v_ctx = f32(vx_ref)
    lo = _iota((tq, LANES), 1) < HEAD_DIM
    lo_loc = _iota(v_loc.shape, 1) < HEAD_DIM
    q_pairs = [_rope(_pair_norm(q_ref[b, :, p * LANES:(p + 1) * LANES].astype(F32), qn_w), *rope_q)
               for p in range(WA_HEADS // 2)]
    outs = [None] * WA_HEADS
    for first in range(0, WA_HEADS, WA_STACK):
        heads = range(first, first + WA_STACK)
        kv = first // group
        mine, mine_loc = (lo, lo_loc) if kv == 0 else (~lo, ~lo_loc)
        stack = []
        for h in heads:
            q = q_pairs[h // 2]
            q = q if h % 2 == kv else pltpu.roll(q, half, 1)
            stack.append(jnp.where(mine, q, 0.0).astype(BF16))
        q_g = jnp.concatenate(stack, axis=0)
        sink_g = jnp.concatenate([jnp.full((tq, 1), sink_ref[h] * LOG2E, F32) for h in heads], axis=0)
        o_g = _softmax_av(_dot_nt(q_g, k_loc) + mask_g, _dot_nt(q_g, k_ctx),
                          _ones_beside(v_loc, mine_loc), _ones_beside(v_ctx, mine), sink_g)
        for n, h in enumerate(heads):
            o = o_g[n * tq:(n + 1) * tq]
            outs[h] = o if h % 2 == kv else pltpu.roll(o, half, 1)
    for p in range(WA_HEADS // 2):
        o_ref[b, :, p * LANES:(p + 1) * LANES] = jnp.where(
            lo, outs[2 * p], outs[2 * p + 1]).astype(o_ref.dtype)


def _wa_attention(proj, sink, qn_w, kn_w, cos_t, sin_t, col_q, col_k, col_v):
    bsz, n, _ = proj.shape
    nblk = n // TOK_BLK
    last = nblk - 1
    qw = WA_HEADS * HEAD_DIM
    cq, ck, cv = col_q // qw, col_k // LANES, col_v // LANES

    assert TOK_BLK % WA_WINDOW == 0
    per = TOK_BLK // WA_WINDOW

    def near(i, shift):
        if shift == 0:
            return i
        return jnp.clip(per * i - 1 if shift < 0 else per * (i + 1), 0, per * nblk - 1)

    def blk(col, shift):
        rows = TOK_BLK if shift == 0 else WA_WINDOW
        return pl.BlockSpec((1, rows, LANES), lambda b, i: (b, near(i, shift), col))

    def tab(shift):
        rows = TOK_BLK if shift == 0 else WA_WINDOW
        return pl.BlockSpec((rows, LANES), lambda b, i: (near(i, shift), 0))

    def ctx_blk(col):
        return pl.BlockSpec((1, TOK_BLK, LANES), lambda b, i: (b, 0, col))

    vec = pl.BlockSpec((1, LANES), lambda b, i: (0, 0))
    return pl.pallas_call(
        functools.partial(_wa_kernel, last=last),
        grid=(bsz, nblk),
        in_specs=[pl.BlockSpec(memory_space=pltpu.SMEM),
                  pl.BlockSpec((1, TOK_BLK, qw), lambda b, i: (b, i, cq)),
                  blk(ck, -1), blk(ck, 0), blk(ck, 1), blk(cv, -1), blk(cv, 0), blk(cv, 1),
                  ctx_blk(ck), ctx_blk(cv),
                  tab(0), tab(0), tab(-1), tab(-1), tab(1), tab(1), vec, vec],
        out_specs=pl.BlockSpec((1, TOK_BLK, qw), lambda b, i: (b, i, 0)),
        out_shape=jax.ShapeDtypeStruct((bsz, n, qw), BF16),
        compiler_params=_cparams(2), name="wa_attn",
    )(sink, proj, proj, proj, proj, proj, proj, proj, proj, proj,
      cos_t, sin_t, cos_t, sin_t, cos_t, sin_t,
      jnp.tile(qn_w, 2).reshape(1, LANES), jnp.tile(kn_w, 2).reshape(1, LANES))


def _rope_tables(n_ctx, t):
    quarter = HEAD_DIM // 4
    inv_freq = ROPE_THETA ** (-jnp.arange(quarter, dtype=F32) / quarter)
    pos = jnp.arange(t)
    lane = np.arange(LANES)
    in_head = lane % HEAD_DIM
    use_col = in_head >= HEAD_DIM // 2
    second = (in_head % (HEAD_DIM // 2)) >= quarter
    freq = inv_freq[in_head % quarter]
    p = jnp.where(use_col[None, :], (pos % GRID_W)[:, None], (pos // GRID_W)[:, None]).astype(F32)
    ang = p * freq[None, :]
    cos = jnp.cos(ang)
    sin = jnp.where(second[None, :], jnp.sin(ang), -jnp.sin(ang))
    cos = jnp.concatenate([jnp.ones((n_ctx, LANES), F32), cos], axis=0)
    sin = jnp.concatenate([jnp.zeros((n_ctx, LANES), F32), sin], axis=0)
    return cos, sin


HG_CHUNK = 256


def _hgrn_kernel(q_ref, f_ref, v_ref, hl_ref, *rest, layer, reverse, final):
    if final:
        g_ref, prev_ref, nw_ref, o_ref, st_ref = rest
    else:
        o_ref, st_ref = rest
    @pl.when(pl.program_id(0) == 0)
    def _():
        st_ref[...] = jnp.zeros_like(st_ref)

    a = hl_ref[0]
    e = jnp.exp(a - jnp.max(a, axis=0, keepdims=True))
    pr = e / jnp.sum(e, axis=0, keepdims=True)
    lb_all = jnp.zeros((1, a.shape[1]), F32)
    for j in range(1, layer + 1):
        lb_all = lb_all + pr[j:j + 1]

    for b, h in [(b, h) for b in range(q_ref.shape[0]) for h in range(HG_HEADS)]:
        lanes = slice(h * HG_DK, (h + 1) * HG_DK)
        lb = lb_all[:, lanes]
        qs = _silu(q_ref[b, :, lanes].astype(F32))
        f = lb + (1.0 - lb) * _sigmoid(f_ref[b, :, lanes])
        o, st = _hgrn_block(qs, 1.0 - f, jnp.log(jnp.maximum(f, LOG_FLOOR)),
                            v_ref[b, :, lanes].astype(F32), st_ref[b, h], reverse)
        st_ref[b, h] = st
        if final:
            tot = prev_ref[b, :, lanes] + o
            gated = _rms(tot, nw_ref[...]) * _silu(g_ref[b, :, lanes].astype(F32))
            o_ref[b, :, lanes] = gated.astype(o_ref.dtype)
        else:
            o_ref[b, :, lanes] = o


def _hgrn_block(qs, kk, g, v, st, reverse):
    c = qs.shape[0]
    n_sub = c // HG_CHUNK
    row = _iota((c, HG_DK), 0)
    ri = _iota((c, c), 0)
    ci = _iota((c, c), 1)
    block_xor = ri ^ ci

    tri = jnp.where(((ci >= ri) if reverse else (ci <= ri)) & (block_xor < HG_CHUNK), 1.0, 0.0)
    tri = tri.astype(BF16)
    g_hi = g.astype(BF16)
    rest = g - g_hi.astype(F32)
    g_mid = rest.astype(BF16)
    g_lo = (rest - g_mid.astype(F32)).astype(BF16)
    tdot = functools.partial(jnp.dot, preferred_element_type=F32)
    b = tdot(tri, g_hi) + (tdot(tri, g_mid) + tdot(tri, g_lo))

    def ref_rows(hs):
        blk = 2 * hs
        inner = hs if reverse else hs - 1
        if blk >= SUBLANES:
            x = b.reshape(c // blk, blk, HG_DK)
            return jnp.broadcast_to(x[:, inner:inner + 1, :], x.shape).reshape(c, HG_DK)
        x = b.reshape(c // SUBLANES, SUBLANES, HG_DK)
        sub = _iota(x.shape, 1)
        out = None
        for p in range(SUBLANES // blk):
            r = p * blk + inner
            piece = jnp.broadcast_to(x[:, r:r + 1, :], x.shape)
            out = piece if out is None else jnp.where(sub >= p * blk, piece, out)
        return out.reshape(c, HG_DK)

    att = jnp.zeros((c, c), F32)
    hs = HG_CHUNK // 2
    while hs >= 1:
        blk = 2 * hs
        decay = jnp.exp(-jnp.abs(b - ref_rows(hs)))
        in_block = row & (blk - 1)
        later_half = (in_block < hs) if reverse else (in_block >= hs)
        q_l = jnp.where(later_half, qs * decay, 0.0)
        k_l = jnp.where(later_half, 0.0, kk * decay)
        a_l = _dot_nt(q_l, k_l)
        att = att + (a_l if blk == c else jnp.where(block_xor < blk, a_l, 0.0))
        hs //= 2
    o = _dot(att, v) + jnp.sum(qs * kk, axis=-1, keepdims=True) * v

    q_dec = qs * jnp.exp(b)
    b3 = b.reshape(n_sub, HG_CHUNK, HG_DK)
    end = 0 if reverse else HG_CHUNK - 1
    b_end = jnp.broadcast_to(b3[:, end:end + 1, :], b3.shape).reshape(c, HG_DK)
    k_dec = kk * jnp.exp(b_end - b)
    outs = [None] * n_sub
    for s in (reversed(range(n_sub)) if reverse else range(n_sub)):
        rows = slice(s * HG_CHUNK, (s + 1) * HG_CHUNK)
        outs[s] = o[rows] + _dot_nt(q_dec[rows], st)
        st = st * jnp.exp(b_end[s * HG_CHUNK:s * HG_CHUNK + 1, :]) + _dot_tn(v[rows], k_dec[rows])
    return jnp.concatenate(outs, axis=0), st


def _hgrn_pass(hl, layer, reverse, q_src, f_src, v_src, final_args=None):
    bsz, n, _ = q_src[0].shape
    nchunk = n // TOK_BLK
    width = HG_HEADS * HG_DK

    def chunk(t):
        return jnp.where(t == 0, 0, nchunk - t) if reverse else t

    def blk(col):
        return pl.BlockSpec((bsz, TOK_BLK, width), lambda t: (0, chunk(t), col // width))

    in_specs = [blk(q_src[1]), blk(f_src[1]), blk(v_src[1]),
                pl.BlockSpec((1,) + hl.shape[1:], lambda t: (1 if reverse else 0, 0, 0))]
    args = [q_src[0], f_src[0], v_src[0], hl]
    final = final_args is not None
    if final:
        g_src, prev, norm_w = final_args
        in_specs += [blk(g_src[1]), blk(0), pl.BlockSpec((1, HG_DK), lambda t: (0, 0))]
        args += [g_src[0], prev, norm_w.reshape(1, HG_DK)]
    return pl.pallas_call(
        functools.partial(_hgrn_kernel, layer=layer, reverse=reverse, final=final),
        grid=(nchunk,),
        in_specs=in_specs,
        out_specs=blk(0),
        out_shape=jax.ShapeDtypeStruct((bsz, n, width), BF16 if final else F32),
        scratch_shapes=[pltpu.VMEM((bsz, HG_HEADS, HG_DK, HG_DK), F32)],
        compiler_params=_cparams(1), name="hgrn_bwd" if reverse else "hgrn_fwd",
    )(*args)


def _merge_kernel(x_ref, ya_ref, yb_ref, yc_ref, ga_ref, gb_ref, gc_ref, mod_ref,
                  wa_ref, wb_ref, wc_ref, wo_ref, nw_ref, wr_ref, br_ref,
                  o_ref, h_ref, idx_ref, wt_ref, cnt_ref, meta_ref, carry_ref, *, n_ctx, ctx_row):
    b, i = pl.program_id(0), pl.program_id(1)
    tm, d = x_ref.shape[1:]

    @pl.when((b == 0) & (i == 0))
    def _():
        carry_ref[...] = jnp.zeros_like(carry_ref)

    gate_of = lambda ref: _sigmoid(ref[0].astype(F32))
    merged = (gate_of(ga_ref) * _dot(ya_ref[0], wa_ref[...])
              + gate_of(gb_ref) * _dot(yb_ref[0], wb_ref[...])
              + gate_of(gc_ref) * _dot(yc_ref[0], wc_ref[...]))
    mix = _dot(merged, wo_ref[...])
    gate = _mod_rows(mod_ref, 2, b, i * tm, n_ctx, ctx_row, tm, d)
    x1 = x_ref[0] + gate * mix
    o_ref[0] = x1

    for s in range(tm // TOK_BLK):
        rows = pl.ds(s * TOK_BLK, TOK_BLK)
        _route_tile(x1[s * TOK_BLK:(s + 1) * TOK_BLK], b, i * tm + s * TOK_BLK, mod_ref, nw_ref,
                    wr_ref, br_ref, h_ref.at[0, rows, :], idx_ref.at[:, rows], wt_ref.at[:, rows],
                    cnt_ref, meta_ref.at[s], carry_ref, n_ctx, ctx_row)


def _merge(xa, ya, yb, yc, proj, mod_l, w_pa, w_pb, w_pc, w_out, col_gate, norm_w, w_router_t,
           b_router, n_ctx, ctx_row):
    bsz, n, d = xa.shape
    tm = 3 * TOK_BLK if n % (3 * TOK_BLK) == 0 else TOK_BLK
    nt = n // tm
    per = tm // TOK_BLK
    g0 = col_gate // d
    tile = lambda w: pl.BlockSpec((1, tm, w), lambda b, i: (b, i, 0))
    gate = lambda k: pl.BlockSpec((1, tm, d), lambda b, i: (b, i, g0 + k))
    full = lambda a: pl.BlockSpec(a.shape, lambda b, i: (0,) * a.ndim)
    lane_tile = pl.BlockSpec((SUBLANES, tm), lambda b, i: (0, b * nt + i))
    br = b_router.reshape(N_EXPERTS, 1)
    nw = norm_w.reshape(1, d)
    return pl.pallas_call(
        functools.partial(_merge_kernel, n_ctx=n_ctx, ctx_row=ctx_row),
        grid=(bsz, nt),
        in_specs=[tile(d), tile(ya.shape[2]), tile(yb.shape[2]), tile(yc.shape[2]),
                  gate(0), gate(1), gate(2), full(mod_l),
                  full(w_pa), full(w_pb), full(w_pc), full(w_out),
                  full(nw), full(w_router_t), full(br)],
        out_specs=[tile(d), tile(d), lane_tile, lane_tile,
                   pl.BlockSpec((N_EXPERTS, LANES), lambda b, i: (0, 0)),
                   pl.BlockSpec((per, N_EXPERTS, LANES), lambda b, i: (b * nt + i, 0, 0))],
        out_shape=[jax.ShapeDtypeStruct((bsz, n, d), F32),
                   jax.ShapeDtypeStruct((bsz, n, d), BF16),
                   jax.ShapeDtypeStruct((SUBLANES, bsz * n), I32),
                   jax.ShapeDtypeStruct((SUBLANES, bsz * n), F32),
                   jax.ShapeDtypeStruct((N_EXPERTS, LANES), F32),
                   jax.ShapeDtypeStruct((bsz * nt * per, N_EXPERTS, LANES), I32)],
        scratch_shapes=[pltpu.VMEM((N_EXPERTS, LANES), F32)],
        compiler_params=_cparams(2), name="merge",
    )(xa, ya, yb, yc, proj, proj, proj, mod_l, w_pa, w_pb, w_pc, w_out, nw, w_router_t, br)


def _route_tile(x, b, row0, mod_ref, nw_ref, wr_ref, br_ref, h_ref, idx_ref, wt_ref, cnt_ref,
                meta_ref, carry_ref, n_ctx, ctx_row):
    tm, d = x.shape
    h = _rms(x, nw_ref[...])
    shift = _mod_rows(mod_ref, 3, b, row0, n_ctx, ctx_row, tm, d)
    scale = _mod_rows(mod_ref, 4, b, row0, n_ctx, ctx_row, tm, d)
    h = h * (1.0 + scale) + shift
    h_ref[...] = h.astype(h_ref.dtype)

    logits = _dot3_nt(wr_ref[...], h)
    ex = jnp.exp(logits - jnp.max(logits, axis=0, keepdims=True))
    probs = ex / jnp.sum(ex, axis=0, keepdims=True)
    sel = probs + br_ref[...]

    def row(x, r):
        return x[r:r + 1, :]

    best = None
    g_idx = None
    for g in range(N_GROUPS):
        r0 = g * EXPERTS_PER_GROUP
        a0, a1, a2, a3 = (row(sel, r0 + j) for j in range(EXPERTS_PER_GROUP))
        hi1, lo1 = jnp.maximum(a0, a1), jnp.minimum(a0, a1)
        hi2, lo2 = jnp.maximum(a2, a3), jnp.minimum(a2, a3)
        score = jnp.maximum(hi1, hi2) + jnp.maximum(jnp.minimum(hi1, hi2), jnp.maximum(lo1, lo2))
        if g == 0:
            best, g_idx = score, jnp.zeros_like(score, dtype=I32)
        else:
            better = score > best
            best = jnp.where(better, score, best)
            g_idx = jnp.where(better, g, g_idx)

    def pick(x, j):
        out = row(x, j)
        for g in range(1, N_GROUPS):
            out = jnp.where(g_idx == g, row(x, g * EXPERTS_PER_GROUP + j), out)
        return out

    in_grp = [pick(sel, j) for j in range(EXPERTS_PER_GROUP)]
    in_prob = [pick(probs, j) for j in range(EXPERTS_PER_GROUP)]

    def first_argmax(vals, skip):
        bv, bi, bp = None, None, None
        for j in range(EXPERTS_PER_GROUP):
            v = vals[j] if skip is None else jnp.where(skip == j, -jnp.inf, vals[j])
            if bv is None:
                bv, bi, bp = v, jnp.zeros_like(g_idx), in_prob[0]
            else:
                better = v > bv
                bv = jnp.where(better, v, bv)
                bi = jnp.where(better, j, bi)
                bp = jnp.where(better, in_prob[j], bp)
        return bi, bp

    loc0, p0 = first_argmax(in_grp, None)
    loc1, p1 = first_argmax(in_grp, loc0)
    e0 = g_idx * EXPERTS_PER_GROUP + loc0
    e1 = g_idx * EXPERTS_PER_GROUP + loc1
    wsum = p0 + p1
    w0, w1 = p0 / wsum, p1 / wsum

    er = _iota((N_EXPERTS, tm), 0)
    hit0 = er == e0
    hit1 = er == e1
    hot = jnp.where(hit0 | hit1, 1.0, 0.0)
    upper = jnp.where(_iota((tm, tm), 0) < _iota((tm, tm), 1), 1.0, 0.0)
    before = _dot(hot, upper)
    cnt = jnp.sum(hot, axis=1, keepdims=True)
    e_col = _iota((N_EXPERTS, 1), 0)

    def slot_starts(align):
        padded = jnp.floor((cnt + (align - 1)) * (1.0 / align)) * align
        off = jnp.zeros((N_EXPERTS, 1), F32)
        for e in range(N_EXPERTS - 1):
            off = off + jnp.where(e_col > e, padded[e:e + 1, :], 0.0)
        return padded, off

    cnt_pad, off_d = slot_starts(SUBLANES)
    _, off_c = slot_starts(MOE_CHUNK)
    positions = []
    for off in (off_d, off_c):
        for hit in (hit0, hit1):
            positions.append(jnp.sum(jnp.where(hit, before + off, 0.0), axis=0, keepdims=True))

    lane = _iota((N_EXPERTS, LANES), 1)
    carry = carry_ref[...]
    meta = jnp.zeros((N_EXPERTS, LANES), F32)
    for field, val in enumerate((carry, cnt, off_d, off_c)):
        meta = jnp.where(lane == field, val, meta)
    meta_ref[...] = meta.astype(I32)
    carry_ref[...] = carry + cnt_pad
    cnt_ref[...] = carry + cnt_pad

    idx_ref[...] = jnp.zeros_like(idx_ref)
    wt_ref[...] = jnp.zeros_like(wt_ref)
    for r, val in enumerate(positions):
        idx_ref[r:r + 1, :] = val.astype(I32)
    for r, val in enumerate((w0, w1)):
        wt_ref[r:r + 1, :] = val


def _round_up(x, m):
    return -(-x // m) * m


MOE_MAX_CHUNKS = TOK_BLK // MOE_CHUNK
MOE_SORT_ROWS = _round_up(2 * TOK_BLK + N_EXPERTS * (SUBLANES - 1) + MOE_CHUNK - 1, LANES)
MOE_GATHER_ROWS = _round_up(2 * TOK_BLK + N_EXPERTS * (MOE_CHUNK - 1), LANES)
META_FIELDS = 4


def _moe_blocks(n_assign, n_tiles):
    slack = N_EXPERTS * (n_tiles * (SUBLANES - 1) + MOE_CHUNK - 1)
    return -(-(n_assign + slack) // MOE_BLK) + N_EXPERTS


def _chunk_copies(meta_ref, start_ref, tile, slot_field, do, make_copy):
    def expert(e, carry):
        base = (tile * N_EXPERTS + e) * META_FIELDS
        before, rows, slot = meta_ref[base], meta_ref[base + 1], meta_ref[base + slot_field]
        for c in range(MOE_MAX_CHUNKS):
            @pl.when(rows > c * MOE_CHUNK)
            def _():
                do(make_copy(pl.multiple_of(slot + c * MOE_CHUNK, SUBLANES),
                             pl.multiple_of(start_ref[e] + before + c * MOE_CHUNK, SUBLANES)))
        return carry
    lax.fori_loop(0, N_EXPERTS, expert, 0)


def _dispatch_kernel(meta_ref, cnt_ref, pos_ref, h_ref, xs_ref, blk_e_ref, blk_rows_ref,
                     start_ref, sort_ref, zero_ref, sem, *, nblk):
    t = pl.program_id(0)
    nt = pl.num_programs(0)
    tm = h_ref.shape[0]

    @pl.when(t == 0)
    def _():
        def expert(e, end):
            start_ref[e] = end
            return end + ((cnt_ref[e] + MOE_CHUNK - 1 + MOE_BLK - 1) // MOE_BLK) * MOE_BLK
        start_ref[N_EXPERTS] = lax.fori_loop(0, N_EXPERTS, expert, 0)

        def block(j, carry):
            def count(e, acc):
                return acc + jnp.where(start_ref[e + 1] <= j * MOE_BLK, 1, 0)
            e = jnp.minimum(lax.fori_loop(0, N_EXPERTS, count, 0), N_EXPERTS - 1)
            blk_e_ref[j] = e
            blk_rows_ref[j] = jnp.clip(cnt_ref[e] - (j * MOE_BLK - start_ref[e]), 0, MOE_BLK)
            return carry
        lax.fori_loop(0, nblk, block, 0)

    r = _iota((MOE_SORT_ROWS, tm), 0)
    onehot = jnp.where((r == pos_ref[0:1, :]) | (r == pos_ref[1:2, :]), 1.0, 0.0)
    buf = t % 2
    sort_ref[buf] = _dot(onehot, h_ref[...])

    def copies(which):
        def make_copy(src_row, dst_row):
            return pltpu.make_async_copy(sort_ref.at[which, pl.ds(src_row, MOE_CHUNK), :],
                                         xs_ref.at[pl.ds(dst_row, MOE_CHUNK), :], sem.at[which])
        return make_copy

    @pl.when(t > 0)
    def _():
        _chunk_copies(meta_ref, start_ref, t - 1, 2, lambda cp: cp.wait(), copies(1 - buf))

    _chunk_copies(meta_ref, start_ref, t, 2, lambda cp: cp.start(), copies(buf))

    @pl.when(t == nt - 1)
    def _():
        _chunk_copies(meta_ref, start_ref, t, 2, lambda cp: cp.wait(), copies(buf))

        zero_ref[...] = jnp.zeros_like(zero_ref)

        def zero_copy(row, size):
            return pltpu.make_async_copy(zero_ref.at[pl.ds(0, size), :],
                                         xs_ref.at[pl.ds(pl.multiple_of(row, SUBLANES), size), :],
                                         sem.at[buf])

        def fill(do):
            def expert(e, carry):
                lo = start_ref[e] + cnt_ref[e]
                gap = start_ref[e + 1] - lo
                n_big = gap // MOE_CHUNK

                def big(k, c2):
                    do(zero_copy(lo + k * MOE_CHUNK, MOE_CHUNK))
                    return c2
                lax.fori_loop(0, n_big, big, 0)

                def small(k, c2):
                    do(zero_copy(lo + n_big * MOE_CHUNK + k * SUBLANES, SUBLANES))
                    return c2
                lax.fori_loop(0, (gap - n_big * MOE_CHUNK) // SUBLANES, small, 0)
                return carry
            lax.fori_loop(0, N_EXPERTS, expert, 0)

            def unused(k, carry):
                do(zero_copy(start_ref[N_EXPERTS] + k * MOE_ZERO_ROWS, MOE_ZERO_ROWS))
                return carry
            lax.fori_loop(0, (nblk * MOE_BLK - start_ref[N_EXPERTS]) // MOE_ZERO_ROWS, unused, 0)

        fill(lambda cp: cp.start())
        fill(lambda cp: cp.wait())


def _dispatch(meta, counts, pos_rows, h_flat, nblk):
    ntok, d = h_flat.shape
    tm = TOK_BLK
    whole_smem = pl.BlockSpec(memory_space=pltpu.SMEM)
    return pl.pallas_call(
        functools.partial(_dispatch_kernel, nblk=nblk),
        grid=(ntok // tm,),
        in_specs=[whole_smem, whole_smem,
                  pl.BlockSpec((SUBLANES, tm), lambda i: (0, i)),
                  pl.BlockSpec((tm, d), lambda i: (i, 0))],
        out_specs=[pl.BlockSpec(memory_space=pl.ANY), whole_smem, whole_smem, whole_smem],
        out_shape=[jax.ShapeDtypeStruct((nblk * MOE_BLK, d), F32),
                   jax.ShapeDtypeStruct((nblk,), I32),
                   jax.ShapeDtypeStruct((nblk,), I32),
                   jax.ShapeDtypeStruct((N_EXPERTS + 1,), I32)],
        scratch_shapes=[pltpu.VMEM((2, MOE_SORT_ROWS, d), F32), pltpu.VMEM((MOE_ZERO_ROWS, d), F32),
                        pltpu.SemaphoreType.DMA((2,))],
        compiler_params=_cparams(1), name="dispatch",
    )(meta, counts, pos_rows, h_flat)


def _expert_kernel(blk_e_ref, blk_rows_ref, x_ref, wg_ref, wu_ref, wd_ref, o_ref, g_bf, u_bf, d_bf):
    i = pl.program_id(0)
    rows = blk_rows_ref[i]

    @pl.when((i == 0) | (blk_e_ref[i] != blk_e_ref[jnp.maximum(i - 1, 0)]))
    def _():
        g_bf[...] = wg_ref[0, 0].astype(BF16)
        u_bf[...] = wu_ref[0, 0].astype(BF16)
        d_bf[...] = wd_ref[0, 0].astype(BF16)

    @pl.when(rows > 0)
    def _():
        x = x_ref[...].astype(BF16)
        hid = _silu(_dot(x, g_bf[...])) * _dot(x, u_bf[...])
        o_ref[...] = _dot(hid, d_bf[...])

    @pl.when(rows == 0)
    def _():
        o_ref[...] = jnp.zeros_like(o_ref)


def _experts(blk_e, blk_rows, xs, w_gate, w_up, w_down, layer):
    d = xs.shape[1]
    ff = w_gate.shape[3]
    nblk = blk_e.shape[0]
    weight = lambda shape: pl.BlockSpec((1, 1) + shape, lambda i, be, rows: (layer, be[i], 0, 0))
    return pl.pallas_call(
        _expert_kernel,
        grid_spec=pltpu.PrefetchScalarGridSpec(
            num_scalar_prefetch=2, grid=(nblk,),
            in_specs=[pl.BlockSpec((MOE_BLK, d), lambda i, be, rows: (i, 0)),
                      weight((d, ff)), weight((d, ff)), weight((ff, d))],
            out_specs=pl.BlockSpec((MOE_BLK, d), lambda i, be, rows: (i, 0)),
            scratch_shapes=[pltpu.VMEM((d, ff), BF16), pltpu.VMEM((d, ff), BF16),
                            pltpu.VMEM((ff, d), BF16)]),
        out_shape=jax.ShapeDtypeStruct(xs.shape, F32),
        compiler_params=_cparams(1), name="experts",
    )(blk_e, blk_rows, xs, w_gate, w_up, w_down)


def _combine_kernel(meta_ref, start_ref, x_ref, pos_ref, wt_ref, mod_ref, ys_ref, o_ref,
                    gath_ref, sem, *, n_ctx, ctx_row):
    b, i = pl.program_id(0), pl.program_id(1)
    nt = pl.num_programs(1)
    t = b * nt + i
    last = pl.num_programs(0) * nt - 1
    tm, d = x_ref.shape[1:]
    buf = t % 2

    def copies(which):
        def make_copy(slot_row, ys_row):
            return pltpu.make_async_copy(ys_ref.at[pl.ds(ys_row, MOE_CHUNK), :],
                                         gath_ref.at[which, pl.ds(slot_row, MOE_CHUNK), :],
                                         sem.at[which])
        return make_copy

    @pl.when(t == 0)
    def _():
        gath_ref[...] = jnp.zeros_like(gath_ref)
        _chunk_copies(meta_ref, start_ref, t, 3, lambda cp: cp.start(), copies(buf))

    @pl.when(t < last)
    def _():
        _chunk_copies(meta_ref, start_ref, t + 1, 3, lambda cp: cp.start(), copies(1 - buf))

    _chunk_copies(meta_ref, start_ref, t, 3, lambda cp: cp.wait(), copies(buf))

    col = _iota((tm, MOE_GATHER_ROWS), 1)
    pos = pos_ref[...]
    wt = wt_ref[...]
    sel = (jnp.where(col == pos[:, 0:1], wt[:, 0:1], 0.0)
           + jnp.where(col == pos[:, 1:2], wt[:, 1:2], 0.0))
    hi, lo = _hi_lo(sel)
    rows = gath_ref[buf].astype(BF16)
    y = (jnp.dot(hi, rows, preferred_element_type=F32)
         + jnp.dot(lo, rows, preferred_element_type=F32))
    gate = _mod_rows(mod_ref, 5, b, i * tm, n_ctx, ctx_row, tm, d)
    o_ref[0] = x_ref[0] + gate * y


def _combine(meta, start, x1, pos_cols, wt_cols, mod_l, ys, n_ctx, ctx_row, latent_only):
    bsz, n, d = x1.shape
    tm = TOK_BLK
    nt = n // tm
    whole_smem = pl.BlockSpec(memory_space=pltpu.SMEM)
    cols = pl.BlockSpec((tm, 2), lambda b, i: (b * nt + i, 0))
    if latent_only:
        assert n_ctx == tm
        out_spec = pl.BlockSpec((1, tm, d), lambda b, i: (b, jnp.maximum(i - 1, 0), 0))
        out_rows = n - n_ctx
    else:
        out_spec = pl.BlockSpec((1, tm, d), lambda b, i: (b, i, 0))
        out_rows = n
    return pl.pallas_call(
        functools.partial(_combine_kernel, n_ctx=n_ctx, ctx_row=ctx_row),
        grid=(bsz, nt),
        in_specs=[whole_smem, whole_smem,
                  pl.BlockSpec((1, tm, d), lambda b, i: (b, i, 0)), cols, cols,
                  pl.BlockSpec(mod_l.shape, lambda b, i: (0, 0)),
                  pl.BlockSpec(memory_space=pl.ANY)],
        out_specs=out_spec,
        out_shape=jax.ShapeDtypeStruct((bsz, out_rows, d), F32),
        scratch_shapes=[pltpu.VMEM((2, MOE_GATHER_ROWS, d), F32), pltpu.SemaphoreType.DMA((2,))],
        compiler_params=_cparams(2), name="combine",
    )(meta, start, x1, pos_cols, wt_cols, mod_l, ys)


def _layer(xa, mod_l, layer, n_ctx, ctx_row, p, rope):
    bsz, n, d = xa.shape
    na_w = NA_HEADS * HEAD_DIM
    hg_w = HG_HEADS * HG_DK
    wa_qw = WA_HEADS * HEAD_DIM
    wa_kvw = WA_KV_HEADS * HEAD_DIM
    hg0 = 3 * na_w
    wa0 = hg0 + 5 * hg_w
    gate0 = wa0 + wa_qw + 2 * wa_kvw
    n_cols = gate0 + N_BRANCHES * d
    assert p["w_in"].shape[2] == n_cols
    w_att = _take_cols(p["w_in"], layer, [(gate0, n_cols), (0, hg0), (wa0, gate0)])
    w_hg_qig = _take_cols(p["w_in"], layer, [(hg0, hg0 + hg_w), (hg0 + 3 * hg_w, wa0)])
    w_hg_f = _take_cols(p["w_in"], layer, [(hg0 + hg_w, hg0 + 3 * hg_w)])
    col_gate = 0
    col_na = [N_BRANCHES * d + k * na_w for k in range(3)]
    col_wq = N_BRANCHES * d + 3 * na_w
    col_wk = col_wq + wa_qw
    col_wv = col_wk + wa_kvw
    proj, h = _inproj(xa, mod_l, p["norm1"], w_att, n_ctx, ctx_row, 1792, BF16)
    hg_qig = _project(h, w_hg_qig, 768, BF16)
    hg_f = _project(h, w_hg_f, 1024, F32)

    ya = _na_attention(proj, _na_bias(p["na_rpb"]), p["na_q_norm"], p["na_k_norm"], *col_na)
    yc = _wa_attention(proj, p["wa_sink"], p["wa_q_norm"], p["wa_k_norm"], rope[0], rope[1],
                       col_wq, col_wk, col_wv)
    hg_q, hg_i, hg_g = ((hg_qig, k * hg_w) for k in range(3))
    o_f = _hgrn_pass(p["hg_lower"], layer, False, hg_q, (hg_f, 0), hg_i)
    yb = _hgrn_pass(p["hg_lower"], layer, True, hg_q, (hg_f, hg_w), hg_i,
                    final_args=(hg_g, o_f, p["hg_norm"]))

    bf = lambda w: w.astype(BF16)
    x1, h2, idx, wts, counts, meta = _merge(
        xa, ya, yb, yc, proj, mod_l, bf(p["w_pa"]), bf(p["w_pb"]), bf(p["w_pc"]), bf(p["w_out"]),
        col_gate, p["norm2"], p["w_router"].T, p["b_router"], n_ctx, ctx_row)
    ntok = bsz * n
    meta_flat = meta[:, :, :META_FIELDS].reshape(-1)
    xs, blk_e, blk_rows, start = _dispatch(meta_flat, counts[:, 0].astype(I32), idx,
                                           h2.reshape(ntok, d),
                                           _moe_blocks(2 * ntok, ntok // TOK_BLK))
    ys = _experts(blk_e, blk_rows, xs, p["w_gate"], p["w_up"], p["w_down"], layer)
    return _combine(meta_flat, start, x1, idx[2:4].T, wts[:2].T, mod_l, ys, n_ctx, ctx_row,
                    latent_only=p["last"])


def kernel(x, c, ctx, c_ctx, w_ada, b_ada, norm1, norm2, w_in, na_q_norm, na_k_norm, na_rpb, hg_lower,
           hg_norm, wa_q_norm, wa_k_norm, wa_sink, w_pa, w_pb, w_pc, w_out, w_router, b_router,
           w_gate, w_up, w_down):
    bsz, t, d = x.shape
    n_ctx = ctx.shape[1]
    depth = w_ada.shape[0]
    assert n_ctx == TOK_BLK and t % TOK_BLK == 0 and t // TOK_BLK >= 3
    assert t % GRID_W == 0 and bsz + 1 <= SUBLANES

    xa = jnp.concatenate([ctx, x], axis=1)
    cond = jnp.concatenate([c, c_ctx[None], jnp.zeros((SUBLANES - bsz - 1, d), F32)], axis=0)
    mod = _ada(cond, w_ada, b_ada)
    rope = _rope_tables(n_ctx, t)
    for l in range(depth):
        p = dict(norm1=norm1[l], norm2=norm2[l], w_in=w_in, na_q_norm=na_q_norm[l],
                 na_k_norm=na_k_norm[l], na_rpb=na_rpb[l], hg_lower=hg_lower, hg_norm=hg_norm[l],
                 wa_q_norm=wa_q_norm[l], wa_k_norm=wa_k_norm[l], wa_sink=wa_sink[l],
                 w_pa=w_pa[l], w_pb=w_pb[l], w_pc=w_pc[l], w_out=w_out[l],
                 w_router=w_router, b_router=b_router, w_gate=w_gate, w_up=w_up,
                 w_down=w_down, last=l == depth - 1)
        xa = _layer(xa, mod[l], l, n_ctx, bsz, p, rope)
    return xa
```

```python
import functools

import numpy as np
import jax
import jax.numpy as jnp
from jax import lax
from jax.experimental import pallas as pl
from jax.experimental.pallas import tpu as pltpu

F32 = jnp.float32
BF16 = jnp.bfloat16
I32 = jnp.int32

GRID_W = 64
NA_HEADS = 8
NA_ROWS = 8
NA_COLS = 16
HEAD_DIM = 64
HG_HEADS = 4
HG_DK = 128
LOG_FLOOR = 1e-30
WA_HEADS = 8
WA_KV_HEADS = 2
WA_WINDOW = 128
ROPE_THETA = 10000.0
N_EXPERTS = 16
N_GROUPS = 4
EXPERTS_PER_GROUP = N_EXPERTS // N_GROUPS
N_BRANCHES = 3
NEG_INF = -1e30
RMS_EPS = 1e-6

LANES = 128
SUBLANES = 8
VMEM_LIMIT = 56 * 1024 * 1024

TOK_BLK = 256
MOE_BLK = 1024
MOE_CHUNK = 32
MOE_PUT_CHUNK = 64
MOE_GATHER_DEPTH = 3
MOE_ZERO_ROWS = 256


def _cparams(n_axes):
    return pltpu.CompilerParams(
        dimension_semantics=("arbitrary",) * n_axes, vmem_limit_bytes=VMEM_LIMIT)


def _sigmoid(x):
    return 1.0 / (1.0 + jnp.exp(-x))


def _silu(x):
    return x * _sigmoid(x)


def _dot(a, b):
    return jnp.dot(a.astype(BF16), b.astype(BF16), preferred_element_type=F32)


def _dot_nt(a, b):
    return lax.dot_general(a.astype(BF16), b.astype(BF16), (((1,), (1,)), ((), ())),
                           preferred_element_type=F32)


def _dot_tn(a, b):
    return lax.dot_general(a.astype(BF16), b.astype(BF16), (((0,), (0,)), ((), ())),
                           preferred_element_type=F32)


def _hi_lo(a):
    hi = a.astype(BF16)
    lo = (a - hi.astype(F32)).astype(BF16)
    return hi, lo


def _dot3(a, b):
    ah, al = _hi_lo(a)
    bh, bl = _hi_lo(b)
    d = functools.partial(jnp.dot, preferred_element_type=F32)
    return d(ah, bh) + (d(ah, bl) + d(al, bh))


def _dot3_nt(a, b):
    ah, al = _hi_lo(a)
    bh, bl = _hi_lo(b)
    d = functools.partial(lax.dot_general, dimension_numbers=(((1,), (1,)), ((), ())),
                          preferred_element_type=F32)
    return d(ah, bh) + (d(ah, bl) + d(al, bh))


def _rms(x, w):
    return x * lax.rsqrt(jnp.mean(x * x, axis=-1, keepdims=True) + RMS_EPS) * w


def _iota(shape, dim):
    return lax.broadcasted_iota(I32, shape, dim)


def _ada_kernel(cond_ref, w_ref, b_ref, o_ref):
    o_ref[0] = _dot3(_silu(cond_ref[...]), w_ref[0]) + b_ref[0]


def _ada(cond, w_ada, b_ada):
    depth, d, d6 = w_ada.shape
    rows = cond.shape[0]
    tn = d6 // 4
    return pl.pallas_call(
        _ada_kernel,
        grid=(depth, d6 // tn),
        in_specs=[pl.BlockSpec((rows, d), lambda l, j: (0, 0)),
                  pl.BlockSpec((1, d, tn), lambda l, j: (l, 0, j)),
                  pl.BlockSpec((1, 1, tn), lambda l, j: (l, 0, j))],
        out_specs=pl.BlockSpec((1, rows, tn), lambda l, j: (l, 0, j)),
        out_shape=jax.ShapeDtypeStruct((depth, rows, d6), F32),
        compiler_params=_cparams(2), name="ada",
    )(cond, w_ada, b_ada.reshape(depth, 1, d6))


def _mod_rows(mod_ref, which, b, row0, n_ctx, ctx_row, tm, d):
    lat = mod_ref[pl.ds(b, 1), which * d:(which + 1) * d]
    cx = mod_ref[ctx_row:ctx_row + 1, which * d:(which + 1) * d]
    row = row0 + _iota((tm, d), 0)
    return jnp.where(row < n_ctx, cx, lat)


def _inproj_kernel(x_ref, mod_ref, nw_ref, w_ref, o_ref, h_ref, *, n_ctx, ctx_row):
    b, i, j = pl.program_id(0), pl.program_id(1), pl.program_id(2)
    tm, d = h_ref.shape[1:]

    @pl.when(j == 0)
    def _():
        h = _rms(x_ref[0], nw_ref[...])
        shift = _mod_rows(mod_ref, 0, b, i * tm, n_ctx, ctx_row, tm, d)
        scale = _mod_rows(mod_ref, 1, b, i * tm, n_ctx, ctx_row, tm, d)
        h_ref[0] = (h * (1.0 + scale) + shift).astype(BF16)

    o_ref[0] = jnp.dot(h_ref[0], w_ref[...], preferred_element_type=F32).astype(o_ref.dtype)


def _matmul_kernel(h_ref, w_ref, o_ref):
    o_ref[0] = jnp.dot(h_ref[0], w_ref[...], preferred_element_type=F32).astype(o_ref.dtype)


def _project(h, w, tn, out_dtype):
    bsz, n, d = h.shape
    cols = w.shape[1]
    tm = 1408 if n % 1408 == 0 else TOK_BLK
    return pl.pallas_call(
        _matmul_kernel,
        grid=(bsz, n // tm, cols // tn),
        in_specs=[pl.BlockSpec((1, tm, d), lambda b, i, j: (b, i, 0)),
                  pl.BlockSpec((d, tn), lambda b, i, j: (0, j))],
        out_specs=pl.BlockSpec((1, tm, tn), lambda b, i, j: (b, i, j)),
        out_shape=jax.ShapeDtypeStruct((bsz, n, cols), out_dtype),
        compiler_params=_cparams(3), name="project",
    )(h, w)


W_IN_COL_BLK = 256


def _take_cols_kernel(src_ref, w_ref, o_ref):
    del src_ref
    o_ref[...] = w_ref[0].astype(BF16)


def _take_cols(w_in, layer, col_ranges):
    d = w_in.shape[1]
    blk = W_IN_COL_BLK
    src = []
    for start, stop in col_ranges:
        assert start % blk == 0 and stop % blk == 0
        src += list(range(start // blk, stop // blk))
    return pl.pallas_call(
        _take_cols_kernel,
        grid_spec=pltpu.PrefetchScalarGridSpec(
            num_scalar_prefetch=1, grid=(len(src),),
            in_specs=[pl.BlockSpec((1, d, blk), lambda j, src: (layer, 0, src[j]))],
            out_specs=pl.BlockSpec((d, blk), lambda j, src: (0, j))),
        out_shape=jax.ShapeDtypeStruct((d, len(src) * blk), BF16),
        compiler_params=_cparams(1), name="take_cols",
    )(jnp.asarray(src, I32), w_in)


def _inproj(xa, mod_l, norm_w, w, n_ctx, ctx_row, tn, out_dtype):
    bsz, n, d = xa.shape
    cols = w.shape[1]
    tm = 1408 if n % 1408 == 0 else TOK_BLK
    return pl.pallas_call(
        functools.partial(_inproj_kernel, n_ctx=n_ctx, ctx_row=ctx_row),
        grid=(bsz, n // tm, cols // tn),
        in_specs=[pl.BlockSpec((1, tm, d), lambda b, i, j: (b, i, 0)),
                  pl.BlockSpec(mod_l.shape, lambda b, i, j: (0, 0)),
                  pl.BlockSpec((1, d), lambda b, i, j: (0, 0)),
                  pl.BlockSpec((d, tn), lambda b, i, j: (0, j))],
        out_specs=[pl.BlockSpec((1, tm, tn), lambda b, i, j: (b, i, j)),
                   pl.BlockSpec((1, tm, d), lambda b, i, j: (b, i, 0))],
        out_shape=[jax.ShapeDtypeStruct((bsz, n, cols), out_dtype),
                   jax.ShapeDtypeStruct((bsz, n, d), BF16)],
        compiler_params=_cparams(3), name="inproj",
    )(xa, mod_l, norm_w.reshape(1, d), w)


def _pair_norm(x, w):
    lane = _iota(x.shape, 1)
    lo = lane < HEAD_DIM
    sq = x * x
    s_lo = jnp.sum(jnp.where(lo, sq, 0.0), axis=-1, keepdims=True)
    s_hi = jnp.sum(jnp.where(lo, 0.0, sq), axis=-1, keepdims=True)
    ms = jnp.where(lo, s_lo, s_hi) * (1.0 / HEAD_DIM)
    return x * lax.rsqrt(ms + RMS_EPS) * w


LOG2E = 1.4426950408889634


def _ones_beside(v, value_lanes):
    return jnp.where(value_lanes, v, 1.0).astype(BF16)


def _softmax_av(s_loc, s_ctx, v_loc, v_ctx, sink):
    m = jnp.maximum(jnp.max(s_loc, axis=-1, keepdims=True), jnp.max(s_ctx, axis=-1, keepdims=True))
    if sink is not None:
        m = jnp.maximum(m, sink)
    p_loc = jnp.exp2(s_loc - m).astype(BF16)
    p_ctx = jnp.exp2(s_ctx - m).astype(BF16)
    acc = (jnp.dot(p_loc, v_loc, preferred_element_type=F32)
           + jnp.dot(p_ctx, v_ctx, preferred_element_type=F32))
    den = pltpu.roll(acc, LANES // 2, 1)
    if sink is not None:
        den = den + jnp.exp2(sink - m)
    return acc / den


NA_QROWS = TOK_BLK // GRID_W
NA_KROWS = 3 * NA_QROWS
N_RPB_COLS = 2 * NA_COLS - 1
N_RPB_ROWS = 2 * NA_ROWS - 1
NA_VARIANTS = 4


def _na_row_valid(variant, a, k):
    if variant == 0:
        return False
    if variant == 1:
        return NA_QROWS <= k < NA_QROWS + NA_ROWS
    if variant == 3:
        return k < NA_ROWS
    return a <= k < a + NA_ROWS


def _na_bias_kernel(rpb_ref, o_ref):
    h = pl.program_id(0)
    qc = _iota((GRID_W, GRID_W), 0)
    kc = _iota((GRID_W, GRID_W), 1)
    col_lo = jnp.clip(qc - NA_COLS // 2, 0, GRID_W - NA_COLS)
    col_ok = (kc >= col_lo) & (kc < col_lo + NA_COLS)
    d_col = kc - qc + (NA_COLS - 1)
    masked = jnp.full((GRID_W, GRID_W), NEG_INF, F32)
    blocks = []
    for dr in range(N_RPB_ROWS):
        acc = jnp.zeros((GRID_W, GRID_W), F32)
        for dc in range(N_RPB_COLS):
            val = rpb_ref[(h * N_RPB_ROWS + dr) * N_RPB_COLS + dc]
            acc = jnp.where(d_col == dc, val, acc)
        blocks.append(jnp.where(col_ok, acc * LOG2E, NEG_INF))
    for variant in range(NA_VARIANTS):
        for a in range(NA_QROWS):
            for k in range(NA_KROWS):
                dr = k - NA_QROWS - a + NA_ROWS - 1
                ok = _na_row_valid(variant, a, k) and 0 <= dr < N_RPB_ROWS
                o_ref[variant, 0, a * GRID_W:(a + 1) * GRID_W, k * GRID_W:(k + 1) * GRID_W] = (
                    blocks[dr] if ok else masked)


def _na_bias(rpb):
    heads = rpb.shape[0]
    return pl.pallas_call(
        _na_bias_kernel,
        grid=(heads,),
        in_specs=[pl.BlockSpec(memory_space=pltpu.SMEM)],
        out_specs=pl.BlockSpec((NA_VARIANTS, 1, TOK_BLK, 3 * TOK_BLK), lambda h: (0, h, 0, 0)),
        out_shape=jax.ShapeDtypeStruct((NA_VARIANTS, heads, TOK_BLK, 3 * TOK_BLK), F32),
        compiler_params=_cparams(1), name="na_bias",
    )(rpb.reshape(-1))


def _na_kernel(q_ref, k0_ref, kn_ref, vp_ref, vc_ref, vn_ref, vx_ref,
               bias_ref, qn_ref, kn_w_ref, o_ref, ring_ref, kctx_ref):
    i = pl.program_id(0)
    tq = q_ref.shape[1]
    lo = _iota((tq, LANES), 1) < HEAD_DIM
    lo_k = _iota((3 * tq, LANES), 1) < HEAD_DIM
    qn_w = qn_ref[...] * (HEAD_DIM ** -0.5 * LOG2E)
    units = [(b, p) for b in range(q_ref.shape[0]) for p in range(NA_HEADS // 2)]
    pair_of = lambda p: slice(p * LANES, (p + 1) * LANES)

    def normed(ref, b, p):
        return _pair_norm(ref[b, :, pair_of(p)].astype(F32), kn_w_ref[...]).astype(BF16)

    @pl.when(i == 0)
    def _():
        for b, p in units:
            pair = pair_of(p)
            k0 = normed(k0_ref, b, p)
            kctx_ref[b, :, pair] = k0
            ring_ref[0, b, :, pair] = k0
            ring_ref[2, b, :, pair] = k0

    nxt = (i + 1) % 3
    for b, p in units:
        ring_ref[nxt, b, :, pair_of(p)] = normed(kn_ref, b, p)
    prv, cur = (i + 2) % 3, i % 3

    for b, p in units:
        pair = pair_of(p)
        f32 = lambda ref: ref[b, :, pair].astype(F32)
        q = _pair_norm(f32(q_ref), qn_w)
        k_loc = jnp.concatenate([ring_ref[prv, b, :, pair], ring_ref[cur, b, :, pair],
                                 ring_ref[nxt, b, :, pair]], axis=0)
        k_ctx = kctx_ref[b, :, pair]
        v_loc = jnp.concatenate([f32(vp_ref), f32(vc_ref), f32(vn_ref)], axis=0)
        v_ctx = f32(vx_ref)
        q2 = jnp.concatenate([jnp.where(lo, q, 0.0), jnp.where(lo, 0.0, q)], axis=0).astype(BF16)
        s_loc2 = _dot_nt(q2, k_loc)
        s_ctx2 = _dot_nt(q2, k_ctx)
        outs = []
        for h in range(2):
            mine = lo if h == 0 else ~lo
            mine_k = lo_k if h == 0 else ~lo_k
            rows = slice(h * tq, (h + 1) * tq)
            outs.append(_softmax_av(s_loc2[rows] + bias_ref[0, 2 * p + h], s_ctx2[rows],
                                    _ones_beside(v_loc, mine_k), _ones_beside(v_ctx, mine), None))
        o_ref[b, :, pair] = jnp.where(lo, outs[0], outs[1]).astype(o_ref.dtype)


def _na_attention(proj, bias, qn_w, kn_w, col_q, col_k, col_v):
    bsz, n, _ = proj.shape
    nblk = n // TOK_BLK
    last = nblk - 1
    width = NA_HEADS * HEAD_DIM
    cq, ck, cv = col_q // width, col_k // width, col_v // width

    def blk(col, shift):
        return pl.BlockSpec((bsz, TOK_BLK, width),
                            lambda i: (0, jnp.clip(i + shift, 0, last), col))

    def ctx_blk(col):
        return pl.BlockSpec((bsz, TOK_BLK, width), lambda i: (0, 0, col))

    def variant(i):
        return (jnp.where(i == 0, 0, jnp.where(i == 1, 1, jnp.where(i == last, 3, 2))), 0, 0, 0)

    vec = pl.BlockSpec((1, LANES), lambda i: (0, 0))
    return pl.pallas_call(
        _na_kernel,
        grid=(nblk,),
        in_specs=[blk(cq, 0), ctx_blk(ck), blk(ck, 1), blk(cv, -1), blk(cv, 0), blk(cv, 1),
                  ctx_blk(cv),
                  pl.BlockSpec((1, NA_HEADS, TOK_BLK, 3 * TOK_BLK), variant), vec, vec],
        out_specs=pl.BlockSpec((bsz, TOK_BLK, width), lambda i: (0, i, 0)),
        out_shape=jax.ShapeDtypeStruct((bsz, n, width), BF16),
        scratch_shapes=[pltpu.VMEM((3, bsz, TOK_BLK, width), BF16),
                        pltpu.VMEM((bsz, TOK_BLK, width), BF16)],
        compiler_params=_cparams(1), name="na_attn",
    )(proj, proj, proj, proj, proj, proj, proj, bias,
      jnp.tile(qn_w, 2).reshape(1, LANES), jnp.tile(kn_w, 2).reshape(1, LANES))


WA_STACK = 4


def _rope(x, cos, sin_signed):
    lane = _iota(x.shape, 1)
    first = (lane & (HEAD_DIM // 2 - 1)) < HEAD_DIM // 4
    quarter = HEAD_DIM // 4
    partner = jnp.where(first, pltpu.roll(x, LANES - quarter, 1), pltpu.roll(x, quarter, 1))
    return x * cos + partner * sin_signed


def _wa_kernel(sink_ref, q_ref, kp_ref, kc_ref, kn_ref, vp_ref, vc_ref, vn_ref, kx_ref, vx_ref,
               cq_ref, sq_ref, cp_ref, sp_ref, cn_ref, sn_ref, qn_ref, kn_w_ref, o_ref, *, last):
    i = pl.program_id(1)
    tq = q_ref.shape[1]
    tn = kp_ref.shape[1]
    n_loc = tq + 2 * tn
    cos_loc = jnp.concatenate([cp_ref[...], cq_ref[...], cn_ref[...]], axis=0)
    sin_loc = jnp.concatenate([sp_ref[...], sq_ref[...], sn_ref[...]], axis=0)
    qn_w = qn_ref[...] * (HEAD_DIM ** -0.5 * LOG2E)

    qi = _iota((tq, n_loc), 0)
    kj = _iota((tq, n_loc), 1)
    rel = kj - tn - qi
    lo_col = jnp.where(i >= 2, 0, tn)
    hi_col = jnp.where(i >= 1, jnp.where(i < last, n_loc, tn + tq), 0)
    ok = (rel >= -WA_WINDOW) & (rel <= WA_WINDOW) & (kj >= lo_col) & (kj < hi_col)
    mask = jnp.where(ok, 0.0, NEG_INF)

    mask_g = jnp.concatenate([mask] * WA_STACK, axis=0)
    for b in range(q_ref.shape[0]):
        _wa_one_batch(b, sink_ref, q_ref, (kp_ref, kc_ref, kn_ref), (vp_ref, vc_ref, vn_ref),
                      kx_ref, vx_ref, (cos_loc, sin_loc), (cq_ref[...], sq_ref[...]), qn_w,
                      kn_w_ref[...], mask_g, o_ref)


def _wa_one_batch(b, sink_ref, q_ref, k_refs, v_refs, kx_ref, vx_ref, rope_loc, rope_q, qn_w, kn_w,
                  mask_g, o_ref):
    tq = q_ref.shape[1]
    half = LANES // 2
    group = WA_HEADS // WA_KV_HEADS
    f32 = lambda ref: ref[b].astype(F32)
    k_loc = _pair_norm(jnp.concatenate([f32(r) for r in k_refs], axis=0), kn_w)
    k_loc = _rope(k_loc, *rope_loc).astype(BF16)
    k_ctx = _pair_norm(f32(kx_ref), kn_w).astype(BF16)
    v_loc = jnp.concatenate([f32(r) for r in v_refs], axis=0)
    v_ctx = f32(vx_ref)
    lo = _iota((tq, LANES), 1) < HEAD_DIM
    lo_loc = _iota(v_loc.shape, 1) < HEAD_DIM
    q_pairs = [_rope(_pair_norm(q_ref[b, :, p * LANES:(p + 1) * LANES].astype(F32), qn_w), *rope_q)
               for p in range(WA_HEADS // 2)]
    outs = [None] * WA_HEADS
    for first in range(0, WA_HEADS, WA_STACK):
        heads = range(first, first + WA_STACK)
        kv = first // group
        mine, mine_loc = (lo, lo_loc) if kv == 0 else (~lo, ~lo_loc)
        stack = []
        for h in heads:
            q = q_pairs[h // 2]
            q = q if h % 2 == kv else pltpu.roll(q, half, 1)
            stack.append(jnp.where(mine, q, 0.0).astype(BF16))
        q_g = jnp.concatenate(stack, axis=0)
        sink_g = jnp.concatenate([jnp.full((tq, 1), sink_ref[h] * LOG2E, F32) for h in heads], axis=0)
        o_g = _softmax_av(_dot_nt(q_g, k_loc) + mask_g, _dot_nt(q_g, k_ctx),
                          _ones_beside(v_loc, mine_loc), _ones_beside(v_ctx, mine), sink_g)
        for n, h in enumerate(heads):
            o = o_g[n * tq:(n + 1) * tq]
            outs[h] = o if h % 2 == kv else pltpu.roll(o, half, 1)
    for p in range(WA_HEADS // 2):
        o_ref[b, :, p * LANES:(p + 1) * LANES] = jnp.where(
            lo, outs[2 * p], outs[2 * p + 1]).astype(o_ref.dtype)


def _wa_attention(proj, sink, qn_w, kn_w, cos_t, sin_t, col_q, col_k, col_v):
    bsz, n, _ = proj.shape
    nblk = n // TOK_BLK
    last = nblk - 1
    qw = WA_HEADS * HEAD_DIM
    cq, ck, cv = col_q // qw, col_k // LANES, col_v // LANES

    assert TOK_BLK % WA_WINDOW == 0
    per = TOK_BLK // WA_WINDOW

    def near(i, shift):
        if shift == 0:
            return i
        return jnp.clip(per * i - 1 if shift < 0 else per * (i + 1), 0, per * nblk - 1)

    def blk(col, shift):
        rows = TOK_BLK if shift == 0 else WA_WINDOW
        return pl.BlockSpec((1, rows, LANES), lambda b, i: (b, near(i, shift), col))

    def tab(shift):
        rows = TOK_BLK if shift == 0 else WA_WINDOW
        return pl.BlockSpec((rows, LANES), lambda b, i: (near(i, shift), 0))

    def ctx_blk(col):
        return pl.BlockSpec((1, TOK_BLK, LANES), lambda b, i: (b, 0, col))

    vec = pl.BlockSpec((1, LANES), lambda b, i: (0, 0))
    return pl.pallas_call(
        functools.partial(_wa_kernel, last=last),
        grid=(bsz, nblk),
        in_specs=[pl.BlockSpec(memory_space=pltpu.SMEM),
                  pl.BlockSpec((1, TOK_BLK, qw), lambda b, i: (b, i, cq)),
                  blk(ck, -1), blk(ck, 0), blk(ck, 1), blk(cv, -1), blk(cv, 0), blk(cv, 1),
                  ctx_blk(ck), ctx_blk(cv),
                  tab(0), tab(0), tab(-1), tab(-1), tab(1), tab(1), vec, vec],
        out_specs=pl.BlockSpec((1, TOK_BLK, qw), lambda b, i: (b, i, 0)),
        out_shape=jax.ShapeDtypeStruct((bsz, n, qw), BF16),
        compiler_params=_cparams(2), name="wa_attn",
    )(sink, proj, proj, proj, proj, proj, proj, proj, proj, proj,
      cos_t, sin_t, cos_t, sin_t, cos_t, sin_t,
      jnp.tile(qn_w, 2).reshape(1, LANES), jnp.tile(kn_w, 2).reshape(1, LANES))


def _rope_tables(n_ctx, t):
    quarter = HEAD_DIM // 4
    inv_freq = ROPE_THETA ** (-jnp.arange(quarter, dtype=F32) / quarter)
    pos = jnp.arange(t)
    lane = np.arange(LANES)
    in_head = lane % HEAD_DIM
    use_col = in_head >= HEAD_DIM // 2
    second = (in_head % (HEAD_DIM // 2)) >= quarter
    freq = inv_freq[in_head % quarter]
    p = jnp.where(use_col[None, :], (pos % GRID_W)[:, None], (pos // GRID_W)[:, None]).astype(F32)
    ang = p * freq[None, :]
    cos = jnp.cos(ang)
    sin = jnp.where(second[None, :], jnp.sin(ang), -jnp.sin(ang))
    cos = jnp.concatenate([jnp.ones((n_ctx, LANES), F32), cos], axis=0)
    sin = jnp.concatenate([jnp.zeros((n_ctx, LANES), F32), sin], axis=0)
    return cos, sin


HG_CHUNK = 256


def _hgrn_kernel(q_ref, f_ref, v_ref, hl_ref, *rest, layer, reverse, final):
    if final:
        g_ref, prev_ref, nw_ref, o_ref, st_ref = rest
    else:
        o_ref, st_ref = rest
    @pl.when(pl.program_id(0) == 0)
    def _():
        st_ref[...] = jnp.zeros_like(st_ref)

    a = hl_ref[0]
    e = jnp.exp(a - jnp.max(a, axis=0, keepdims=True))
    pr = e / jnp.sum(e, axis=0, keepdims=True)
    lb_all = jnp.zeros((1, a.shape[1]), F32)
    for j in range(1, layer + 1):
        lb_all = lb_all + pr[j:j + 1]

    for b, h in [(b, h) for b in range(q_ref.shape[0]) for h in range(HG_HEADS)]:
        lanes = slice(h * HG_DK, (h + 1) * HG_DK)
        lb = lb_all[:, lanes]
        qs = _silu(q_ref[b, :, lanes].astype(F32))
        f = lb + (1.0 - lb) * _sigmoid(f_ref[b, :, lanes])
        o, st = _hgrn_block(qs, 1.0 - f, jnp.log(jnp.maximum(f, LOG_FLOOR)),
                            v_ref[b, :, lanes].astype(F32), st_ref[b, h], reverse)
        st_ref[b, h] = st
        if final:
            tot = prev_ref[b, :, lanes] + o
            gated = _rms(tot, nw_ref[...]) * _silu(g_ref[b, :, lanes].astype(F32))
            o_ref[b, :, lanes] = gated.astype(o_ref.dtype)
        else:
            o_ref[b, :, lanes] = o


def _hgrn_block(qs, kk, g, v, st, reverse):
    c = qs.shape[0]
    n_sub = c // HG_CHUNK
    row = _iota((c, HG_DK), 0)
    ri = _iota((c, c), 0)
    ci = _iota((c, c), 1)
    block_xor = ri ^ ci

    tri = jnp.where(((ci >= ri) if reverse else (ci <= ri)) & (block_xor < HG_CHUNK), 1.0, 0.0)
    tri = tri.astype(BF16)
    g_hi = g.astype(BF16)
    rest = g - g_hi.astype(F32)
    g_mid = rest.astype(BF16)
    g_lo = (rest - g_mid.astype(F32)).astype(BF16)
    tdot = functools.partial(jnp.dot, preferred_element_type=F32)
    b = tdot(tri, g_hi) + (tdot(tri, g_mid) + tdot(tri, g_lo))

    def ref_rows(hs):
        blk = 2 * hs
        inner = hs if reverse else hs - 1
        if blk >= SUBLANES:
            x = b.reshape(c // blk, blk, HG_DK)
            return jnp.broadcast_to(x[:, inner:inner + 1, :], x.shape).reshape(c, HG_DK)
        x = b.reshape(c // SUBLANES, SUBLANES, HG_DK)
        sub = _iota(x.shape, 1)
        out = None
        for p in range(SUBLANES // blk):
            r = p * blk + inner
            piece = jnp.broadcast_to(x[:, r:r + 1, :], x.shape)
            out = piece if out is None else jnp.where(sub >= p * blk, piece, out)
        return out.reshape(c, HG_DK)

    att = jnp.zeros((c, c), F32)
    hs = HG_CHUNK // 2
    while hs >= 1:
        blk = 2 * hs
        decay = jnp.exp(-jnp.abs(b - ref_rows(hs)))
        in_block = row & (blk - 1)
        later_half = (in_block < hs) if reverse else (in_block >= hs)
        q_l = jnp.where(later_half, qs * decay, 0.0)
        k_l = jnp.where(later_half, 0.0, kk * decay)
        a_l = _dot_nt(q_l, k_l)
        att = att + (a_l if blk == c else jnp.where(block_xor < blk, a_l, 0.0))
        hs //= 2
    o = _dot(att, v) + jnp.sum(qs * kk, axis=-1, keepdims=True) * v

    q_dec = qs * jnp.exp(b)
    b3 = b.reshape(n_sub, HG_CHUNK, HG_DK)
    end = 0 if reverse else HG_CHUNK - 1
    b_end = jnp.broadcast_to(b3[:, end:end + 1, :], b3.shape).reshape(c, HG_DK)
    k_dec = kk * jnp.exp(b_end - b)
    outs = [None] * n_sub
    for s in (reversed(range(n_sub)) if reverse else range(n_sub)):
        rows = slice(s * HG_CHUNK, (s + 1) * HG_CHUNK)
        outs[s] = o[rows] + _dot_nt(q_dec[rows], st)
        st = st * jnp.exp(b_end[s * HG_CHUNK:s * HG_CHUNK + 1, :]) + _dot_tn(v[rows], k_dec[rows])
    return jnp.concatenate(outs, axis=0), st


def _hgrn_pass(hl, layer, reverse, q_src, f_src, v_src, final_args=None):
    bsz, n, _ = q_src[0].shape
    nchunk = n // TOK_BLK
    width = HG_HEADS * HG_DK

    def chunk(t):
        return jnp.where(t == 0, 0, nchunk - t) if reverse else t

    def blk(col):
        return pl.BlockSpec((bsz, TOK_BLK, width), lambda t: (0, chunk(t), col // width))

    in_specs = [blk(q_src[1]), blk(f_src[1]), blk(v_src[1]),
                pl.BlockSpec((1,) + hl.shape[1:], lambda t: (1 if reverse else 0, 0, 0))]
    args = [q_src[0], f_src[0], v_src[0], hl]
    final = final_args is not None
    if final:
        g_src, prev, norm_w = final_args
        in_specs += [blk(g_src[1]), blk(0), pl.BlockSpec((1, HG_DK), lambda t: (0, 0))]
        args += [g_src[0], prev, norm_w.reshape(1, HG_DK)]
    return pl.pallas_call(
        functools.partial(_hgrn_kernel, layer=layer, reverse=reverse, final=final),
        grid=(nchunk,),
        in_specs=in_specs,
        out_specs=blk(0),
        out_shape=jax.ShapeDtypeStruct((bsz, n, width), BF16 if final else F32),
        scratch_shapes=[pltpu.VMEM((bsz, HG_HEADS, HG_DK, HG_DK), F32)],
        compiler_params=_cparams(1), name="hgrn_bwd" if reverse else "hgrn_fwd",
    )(*args)


def _merge_kernel(x_ref, ya_ref, yb_ref, yc_ref, ga_ref, gb_ref, gc_ref, mod_ref,
                  wa_ref, wb_ref, wc_ref, wo_ref, nw_ref, wr_ref, br_ref,
                  o_ref, h_ref, idx_ref, wt_ref, cnt_ref, meta_ref, carry_ref, *, n_ctx, ctx_row):
    b, i = pl.program_id(0), pl.program_id(1)
    tm, d = x_ref.shape[1:]

    @pl.when((b == 0) & (i == 0))
    def _():
        carry_ref[...] = jnp.zeros_like(carry_ref)

    gate_of = lambda ref: _sigmoid(ref[0].astype(F32))
    merged = (gate_of(ga_ref) * _dot(ya_ref[0], wa_ref[...])
              + gate_of(gb_ref) * _dot(yb_ref[0], wb_ref[...])
              + gate_of(gc_ref) * _dot(yc_ref[0], wc_ref[...]))
    mix = _dot(merged, wo_ref[...])
    gate = _mod_rows(mod_ref, 2, b, i * tm, n_ctx, ctx_row, tm, d)
    x1 = x_ref[0] + gate * mix
    o_ref[0] = x1

    for s in range(tm // TOK_BLK):
        rows = pl.ds(s * TOK_BLK, TOK_BLK)
        _route_tile(x1[s * TOK_BLK:(s + 1) * TOK_BLK], b, i * tm + s * TOK_BLK, mod_ref, nw_ref,
                    wr_ref, br_ref, h_ref.at[0, rows, :], idx_ref.at[:, rows], wt_ref.at[:, rows],
                    cnt_ref, meta_ref.at[s], carry_ref, n_ctx, ctx_row)


def _merge(xa, ya, yb, yc, proj, mod_l, w_pa, w_pb, w_pc, w_out, col_gate, norm_w, w_router_t,
           b_router, n_ctx, ctx_row):
    bsz, n, d = xa.shape
    tm = 3 * TOK_BLK if n % (3 * TOK_BLK) == 0 else TOK_BLK
    nt = n // tm
    per = tm // TOK_BLK
    g0 = col_gate // d
    tile = lambda w: pl.BlockSpec((1, tm, w), lambda b, i: (b, i, 0))
    gate = lambda k: pl.BlockSpec((1, tm, d), lambda b, i: (b, i, g0 + k))
    full = lambda a: pl.BlockSpec(a.shape, lambda b, i: (0,) * a.ndim)
    lane_tile = pl.BlockSpec((SUBLANES, tm), lambda b, i: (0, b * nt + i))
    br = b_router.reshape(N_EXPERTS, 1)
    nw = norm_w.reshape(1, d)
    return pl.pallas_call(
        functools.partial(_merge_kernel, n_ctx=n_ctx, ctx_row=ctx_row),
        grid=(bsz, nt),
        in_specs=[tile(d), tile(ya.shape[2]), tile(yb.shape[2]), tile(yc.shape[2]),
                  gate(0), gate(1), gate(2), full(mod_l),
                  full(w_pa), full(w_pb), full(w_pc), full(w_out),
                  full(nw), full(w_router_t), full(br)],
        out_specs=[tile(d), tile(d), lane_tile, lane_tile,
                   pl.BlockSpec((N_EXPERTS, LANES), lambda b, i: (0, 0)),
                   pl.BlockSpec((per, N_EXPERTS, LANES), lambda b, i: (b * nt + i, 0, 0))],
        out_shape=[jax.ShapeDtypeStruct((bsz, n, d), F32),
                   jax.ShapeDtypeStruct((bsz, n, d), BF16),
                   jax.ShapeDtypeStruct((SUBLANES, bsz * n), I32),
                   jax.ShapeDtypeStruct((SUBLANES, bsz * n), F32),
                   jax.ShapeDtypeStruct((N_EXPERTS, LANES), F32),
                   jax.ShapeDtypeStruct((bsz * nt * per, N_EXPERTS, LANES), I32)],
        scratch_shapes=[pltpu.VMEM((N_EXPERTS, LANES), F32)],
        compiler_params=_cparams(2), name="merge",
    )(xa, ya, yb, yc, proj, proj, proj, mod_l, w_pa, w_pb, w_pc, w_out, nw, w_router_t, br)


def _route_tile(x, b, row0, mod_ref, nw_ref, wr_ref, br_ref, h_ref, idx_ref, wt_ref, cnt_ref,
                meta_ref, carry_ref, n_ctx, ctx_row):
    tm, d = x.shape
    h = _rms(x, nw_ref[...])
    shift = _mod_rows(mod_ref, 3, b, row0, n_ctx, ctx_row, tm, d)
    scale = _mod_rows(mod_ref, 4, b, row0, n_ctx, ctx_row, tm, d)
    h = h * (1.0 + scale) + shift
    h_ref[...] = h.astype(h_ref.dtype)

    logits = _dot3_nt(wr_ref[...], h)
    ex = jnp.exp(logits - jnp.max(logits, axis=0, keepdims=True))
    probs = ex / jnp.sum(ex, axis=0, keepdims=True)
    sel = probs + br_ref[...]

    def row(x, r):
        return x[r:r + 1, :]

    best = None
    g_idx = None
    for g in range(N_GROUPS):
        r0 = g * EXPERTS_PER_GROUP
        a0, a1, a2, a3 = (row(sel, r0 + j) for j in range(EXPERTS_PER_GROUP))
        hi1, lo1 = jnp.maximum(a0, a1), jnp.minimum(a0, a1)
        hi2, lo2 = jnp.maximum(a2, a3), jnp.minimum(a2, a3)
        score = jnp.maximum(hi1, hi2) + jnp.maximum(jnp.minimum(hi1, hi2), jnp.maximum(lo1, lo2))
        if g == 0:
            best, g_idx = score, jnp.zeros_like(score, dtype=I32)
        else:
            better = score > best
            best = jnp.where(better, score, best)
            g_idx = jnp.where(better, g, g_idx)

    def pick(x, j):
        out = row(x, j)
        for g in range(1, N_GROUPS):
            out = jnp.where(g_idx == g, row(x, g * EXPERTS_PER_GROUP + j), out)
        return out

    in_grp = [pick(sel, j) for j in range(EXPERTS_PER_GROUP)]
    in_prob = [pick(probs, j) for j in range(EXPERTS_PER_GROUP)]

    def first_argmax(vals, skip):
        bv, bi, bp = None, None, None
        for j in range(EXPERTS_PER_GROUP):
            v = vals[j] if skip is None else jnp.where(skip == j, -jnp.inf, vals[j])
            if bv is None:
                bv, bi, bp = v, jnp.zeros_like(g_idx), in_prob[0]
            else:
                better = v > bv
                bv = jnp.where(better, v, bv)
                bi = jnp.where(better, j, bi)
                bp = jnp.where(better, in_prob[j], bp)
        return bi, bp

    loc0, p0 = first_argmax(in_grp, None)
    loc1, p1 = first_argmax(in_grp, loc0)
    e0 = g_idx * EXPERTS_PER_GROUP + loc0
    e1 = g_idx * EXPERTS_PER_GROUP + loc1
    wsum = p0 + p1
    w0, w1 = p0 / wsum, p1 / wsum

    er = _iota((N_EXPERTS, tm), 0)
    hit0 = er == e0
    hit1 = er == e1
    hot = jnp.where(hit0 | hit1, 1.0, 0.0)
    upper = jnp.where(_iota((tm, tm), 0) < _iota((tm, tm), 1), 1.0, 0.0)
    before = _dot(hot, upper)
    cnt = jnp.sum(hot, axis=1, keepdims=True)
    e_col = _iota((N_EXPERTS, 1), 0)

    def slot_starts(align):
        padded = jnp.floor((cnt + (align - 1)) * (1.0 / align)) * align
        off = jnp.zeros((N_EXPERTS, 1), F32)
        for e in range(N_EXPERTS - 1):
            off = off + jnp.where(e_col > e, padded[e:e + 1, :], 0.0)
        return padded, off

    cnt_pad, off_d = slot_starts(SUBLANES)
    _, off_c = slot_starts(MOE_CHUNK)
    positions = []
    for off in (off_d, off_c):
        for hit in (hit0, hit1):
            positions.append(jnp.sum(jnp.where(hit, before + off, 0.0), axis=0, keepdims=True))

    lane = _iota((N_EXPERTS, LANES), 1)
    carry = carry_ref[...]
    meta = jnp.zeros((N_EXPERTS, LANES), F32)
    for field, val in enumerate((carry, cnt, off_d, off_c)):
        meta = jnp.where(lane == field, val, meta)
    meta_ref[...] = meta.astype(I32)
    carry_ref[...] = carry + cnt_pad
    cnt_ref[...] = carry + cnt_pad

    idx_ref[...] = jnp.zeros_like(idx_ref)
    wt_ref[...] = jnp.zeros_like(wt_ref)
    for r, val in enumerate(positions):
        idx_ref[r:r + 1, :] = val.astype(I32)
    for r, val in enumerate((w0, w1)):
        wt_ref[r:r + 1, :] = val


def _round_up(x, m):
    return -(-x // m) * m


MOE_SORT_ROWS = _round_up(2 * TOK_BLK + N_EXPERTS * (SUBLANES - 1) + MOE_PUT_CHUNK - 1, LANES)
MOE_GATHER_ROWS = _round_up(2 * TOK_BLK + N_EXPERTS * (MOE_CHUNK - 1), LANES)
META_FIELDS = 4


def _moe_blocks(n_assign, n_tiles):
    slack = N_EXPERTS * (n_tiles * (SUBLANES - 1) + MOE_PUT_CHUNK - 1)
    return -(-(n_assign + slack) // MOE_BLK) + N_EXPERTS


def _chunk_copies(meta_ref, start_ref, tile, slot_field, chunk, do, make_copy):
    def expert(e, carry):
        base = (tile * N_EXPERTS + e) * META_FIELDS
        before, rows, slot = meta_ref[base], meta_ref[base + 1], meta_ref[base + slot_field]
        for c in range(TOK_BLK // chunk):
            @pl.when(rows > c * chunk)
            def _():
                do(make_copy(pl.multiple_of(slot + c * chunk, SUBLANES),
                             pl.multiple_of(start_ref[e] + before + c * chunk, SUBLANES)))
        return carry
    lax.fori_loop(0, N_EXPERTS, expert, 0)


def _dispatch_kernel(meta_ref, cnt_ref, pos_ref, h_ref, xs_ref, blk_e_ref, blk_rows_ref,
                     start_ref, sort_ref, zero_ref, sem, *, nblk):
    t = pl.program_id(0)
    nt = pl.num_programs(0)
    tm = h_ref.shape[0]

    @pl.when(t == 0)
    def _():
        def expert(e, end):
            start_ref[e] = end
            return end + ((cnt_ref[e] + MOE_PUT_CHUNK - 1 + MOE_BLK - 1) // MOE_BLK) * MOE_BLK
        start_ref[N_EXPERTS] = lax.fori_loop(0, N_EXPERTS, expert, 0)

        def block(j, carry):
            def count(e, acc):
                return acc + jnp.where(start_ref[e + 1] <= j * MOE_BLK, 1, 0)
            e = jnp.minimum(lax.fori_loop(0, N_EXPERTS, count, 0), N_EXPERTS - 1)
            blk_e_ref[j] = e
            blk_rows_ref[j] = jnp.clip(cnt_ref[e] - (j * MOE_BLK - start_ref[e]), 0, MOE_BLK)
            return carry
        lax.fori_loop(0, nblk, block, 0)

    r = _iota((MOE_SORT_ROWS, tm), 0)
    onehot = jnp.where((r == pos_ref[0:1, :]) | (r == pos_ref[1:2, :]), 1.0, 0.0)
    buf = t % 2
    sort_ref[buf] = _dot(onehot, h_ref[...])

    def copies(which):
        def make_copy(src_row, dst_row):
            return pltpu.make_async_copy(sort_ref.at[which, pl.ds(src_row, MOE_PUT_CHUNK), :],
                                         xs_ref.at[pl.ds(dst_row, MOE_PUT_CHUNK), :], sem.at[which])
        return make_copy

    put = functools.partial(_chunk_copies, meta_ref, start_ref, slot_field=2, chunk=MOE_PUT_CHUNK)

    @pl.when(t > 0)
    def _():
        put(tile=t - 1, do=lambda cp: cp.wait(), make_copy=copies(1 - buf))

    put(tile=t, do=lambda cp: cp.start(), make_copy=copies(buf))

    @pl.when(t == nt - 1)
    def _():
        put(tile=t, do=lambda cp: cp.wait(), make_copy=copies(buf))

        zero_ref[...] = jnp.zeros_like(zero_ref)

        def zero_copy(row, size):
            return pltpu.make_async_copy(zero_ref.at[pl.ds(0, size), :],
                                         xs_ref.at[pl.ds(pl.multiple_of(row, SUBLANES), size), :],
                                         sem.at[buf])

        def fill(do):
            def expert(e, carry):
                lo = start_ref[e] + cnt_ref[e]
                gap = start_ref[e + 1] - lo
                n_big = gap // MOE_CHUNK

                def big(k, c2):
                    do(zero_copy(lo + k * MOE_CHUNK, MOE_CHUNK))
                    return c2
                lax.fori_loop(0, n_big, big, 0)

                def small(k, c2):
                    do(zero_copy(lo + n_big * MOE_CHUNK + k * SUBLANES, SUBLANES))
                    return c2
                lax.fori_loop(0, (gap - n_big * MOE_CHUNK) // SUBLANES, small, 0)
                return carry
            lax.fori_loop(0, N_EXPERTS, expert, 0)

            def unused(k, carry):
                do(zero_copy(start_ref[N_EXPERTS] + k * MOE_ZERO_ROWS, MOE_ZERO_ROWS))
                return carry
            lax.fori_loop(0, (nblk * MOE_BLK - start_ref[N_EXPERTS]) // MOE_ZERO_ROWS, unused, 0)

        fill(lambda cp: cp.start())
        fill(lambda cp: cp.wait())


def _dispatch(meta, counts, pos_rows, h_flat, nblk):
    ntok, d = h_flat.shape
    tm = TOK_BLK
    whole_smem = pl.BlockSpec(memory_space=pltpu.SMEM)
    return pl.pallas_call(
        functools.partial(_dispatch_kernel, nblk=nblk),
        grid=(ntok // tm,),
        in_specs=[whole_smem, whole_smem,
                  pl.BlockSpec((SUBLANES, tm), lambda i: (0, i)),
                  pl.BlockSpec((tm, d), lambda i: (i, 0))],
        out_specs=[pl.BlockSpec(memory_space=pl.ANY), whole_smem, whole_smem, whole_smem],
        out_shape=[jax.ShapeDtypeStruct((nblk * MOE_BLK, d), F32),
                   jax.ShapeDtypeStruct((nblk,), I32),
                   jax.ShapeDtypeStruct((nblk,), I32),
                   jax.ShapeDtypeStruct((N_EXPERTS + 1,), I32)],
        scratch_shapes=[pltpu.VMEM((2, MOE_SORT_ROWS, d), F32), pltpu.VMEM((MOE_ZERO_ROWS, d), F32),
                        pltpu.SemaphoreType.DMA((2,))],
        compiler_params=_cparams(1), name="dispatch",
    )(meta, counts, pos_rows, h_flat)


def _expert_kernel(blk_e_ref, blk_rows_ref, x_ref, wg_ref, wu_ref, wd_ref, o_ref, g_bf, u_bf, d_bf):
    i = pl.program_id(0)
    rows = blk_rows_ref[i]

    @pl.when((i == 0) | (blk_e_ref[i] != blk_e_ref[jnp.maximum(i - 1, 0)]))
    def _():
        g_bf[...] = wg_ref[0, 0].astype(BF16)
        u_bf[...] = wu_ref[0, 0].astype(BF16)
        d_bf[...] = wd_ref[0, 0].astype(BF16)

    @pl.when(rows > 0)
    def _():
        x = x_ref[...].astype(BF16)
        hid = _silu(_dot(x, g_bf[...])) * _dot(x, u_bf[...])
        o_ref[...] = _dot(hid, d_bf[...])

    @pl.when(rows == 0)
    def _():
        o_ref[...] = jnp.zeros_like(o_ref)


def _experts(blk_e, blk_rows, xs, w_gate, w_up, w_down, layer):
    d = xs.shape[1]
    ff = w_gate.shape[3]
    nblk = blk_e.shape[0]
    weight = lambda shape: pl.BlockSpec((1, 1) + shape, lambda i, be, rows: (layer, be[i], 0, 0))
    return pl.pallas_call(
        _expert_kernel,
        grid_spec=pltpu.PrefetchScalarGridSpec(
            num_scalar_prefetch=2, grid=(nblk,),
            in_specs=[pl.BlockSpec((MOE_BLK, d), lambda i, be, rows: (i, 0)),
                      weight((d, ff)), weight((d, ff)), weight((ff, d))],
            out_specs=pl.BlockSpec((MOE_BLK, d), lambda i, be, rows: (i, 0)),
            scratch_shapes=[pltpu.VMEM((d, ff), BF16), pltpu.VMEM((d, ff), BF16),
                            pltpu.VMEM((ff, d), BF16)]),
        out_shape=jax.ShapeDtypeStruct(xs.shape, F32),
        compiler_params=_cparams(1), name="experts",
    )(blk_e, blk_rows, xs, w_gate, w_up, w_down)


def _combine_kernel(meta_ref, start_ref, x_ref, pos_ref, wt_ref, mod_ref, ys_ref, o_ref,
                    gath_ref, sem, *, n_ctx, ctx_row):
    b, i = pl.program_id(0), pl.program_id(1)
    nt = pl.num_programs(1)
    t = b * nt + i
    last = pl.num_programs(0) * nt - 1
    tm, d = x_ref.shape[1:]
    depth = gath_ref.shape[0]
    buf = t % depth

    def copies(which):
        def make_copy(slot_row, ys_row):
            return pltpu.make_async_copy(ys_ref.at[pl.ds(ys_row, MOE_CHUNK), :],
                                         gath_ref.at[which, pl.ds(slot_row, MOE_CHUNK), :],
                                         sem.at[which])
        return make_copy

    get = functools.partial(_chunk_copies, meta_ref, start_ref, slot_field=3, chunk=MOE_CHUNK)

    @pl.when(t == 0)
    def _():
        gath_ref[...] = jnp.zeros_like(gath_ref)
        for ahead in range(depth - 1):
            @pl.when(ahead <= last)
            def _():
                get(tile=ahead, do=lambda cp: cp.start(), make_copy=copies(ahead))

    @pl.when(t + depth - 1 <= last)
    def _():
        get(tile=t + depth - 1, do=lambda cp: cp.start(), make_copy=copies((t + depth - 1) % depth))

    get(tile=t, do=lambda cp: cp.wait(), make_copy=copies(buf))

    col = _iota((tm, MOE_GATHER_ROWS), 1)
    pos = pos_ref[...]
    wt = wt_ref[...]
    sel = (jnp.where(col == pos[:, 0:1], wt[:, 0:1], 0.0)
           + jnp.where(col == pos[:, 1:2], wt[:, 1:2], 0.0))
    hi, lo = _hi_lo(sel)
    rows = gath_ref[buf].astype(BF16)
    y = (jnp.dot(hi, rows, preferred_element_type=F32)
         + jnp.dot(lo, rows, preferred_element_type=F32))
    gate = _mod_rows(mod_ref, 5, b, i * tm, n_ctx, ctx_row, tm, d)
    o_ref[0] = x_ref[0] + gate * y


def _combine(meta, start, x1, pos_cols, wt_cols, mod_l, ys, n_ctx, ctx_row, latent_only):
    bsz, n, d = x1.shape
    tm = TOK_BLK
    nt = n // tm
    whole_smem = pl.BlockSpec(memory_space=pltpu.SMEM)
    cols = pl.BlockSpec((tm, 2), lambda b, i: (b * nt + i, 0))
    if latent_only:
        assert n_ctx == tm
        out_spec = pl.BlockSpec((1, tm, d), lambda b, i: (b, jnp.maximum(i - 1, 0), 0))
        out_rows = n - n_ctx
    else:
        out_spec = pl.BlockSpec((1, tm, d), lambda b, i: (b, i, 0))
        out_rows = n
    return pl.pallas_call(
        functools.partial(_combine_kernel, n_ctx=n_ctx, ctx_row=ctx_row),
        grid=(bsz, nt),
        in_specs=[whole_smem, whole_smem,
                  pl.BlockSpec((1, tm, d), lambda b, i: (b, i, 0)), cols, cols,
                  pl.BlockSpec(mod_l.shape, lambda b, i: (0, 0)),
                  pl.BlockSpec(memory_space=pl.ANY)],
        out_specs=out_spec,
        out_shape=jax.ShapeDtypeStruct((bsz, out_rows, d), F32),
        scratch_shapes=[pltpu.VMEM((MOE_GATHER_DEPTH, MOE_GATHER_ROWS, d), F32),
                        pltpu.SemaphoreType.DMA((MOE_GATHER_DEPTH,))],
        compiler_params=_cparams(2), name="combine",
    )(meta, start, x1, pos_cols, wt_cols, mod_l, ys)


def _layer(xa, mod_l, layer, n_ctx, ctx_row, p, rope):
    bsz, n, d = xa.shape
    na_w = NA_HEADS * HEAD_DIM
    hg_w = HG_HEADS * HG_DK
    wa_qw = WA_HEADS * HEAD_DIM
    wa_kvw = WA_KV_HEADS * HEAD_DIM
    hg0 = 3 * na_w
    wa0 = hg0 + 5 * hg_w
    gate0 = wa0 + wa_qw + 2 * wa_kvw
    n_cols = gate0 + N_BRANCHES * d
    assert p["w_in"].shape[2] == n_cols
    w_att = _take_cols(p["w_in"], layer, [(gate0, n_cols), (0, hg0), (wa0, gate0)])
    w_hg_qig = _take_cols(p["w_in"], layer, [(hg0, hg0 + hg_w), (hg0 + 3 * hg_w, wa0)])
    w_hg_f = _take_cols(p["w_in"], layer, [(hg0 + hg_w, hg0 + 3 * hg_w)])
    col_gate = 0
    col_na = [N_BRANCHES * d + k * na_w for k in range(3)]
    col_wq = N_BRANCHES * d + 3 * na_w
    col_wk = col_wq + wa_qw
    col_wv = col_wk + wa_kvw
    proj, h = _inproj(xa, mod_l, p["norm1"], w_att, n_ctx, ctx_row, 1792, BF16)
    hg_qig = _project(h, w_hg_qig, 768, BF16)
    hg_f = _project(h, w_hg_f, 1024, F32)

    ya = _na_attention(proj, _na_bias(p["na_rpb"]), p["na_q_norm"], p["na_k_norm"], *col_na)
    yc = _wa_attention(proj, p["wa_sink"], p["wa_q_norm"], p["wa_k_norm"], rope[0], rope[1],
                       col_wq, col_wk, col_wv)
    hg_q, hg_i, hg_g = ((hg_qig, k * hg_w) for k in range(3))
    o_f = _hgrn_pass(p["hg_lower"], layer, False, hg_q, (hg_f, 0), hg_i)
    yb = _hgrn_pass(p["hg_lower"], layer, True, hg_q, (hg_f, hg_w), hg_i,
                    final_args=(hg_g, o_f, p["hg_norm"]))

    bf = lambda w: w.astype(BF16)
    x1, h2, idx, wts, counts, meta = _merge(
        xa, ya, yb, yc, proj, mod_l, bf(p["w_pa"]), bf(p["w_pb"]), bf(p["w_pc"]), bf(p["w_out"]),
        col_gate, p["norm2"], p["w_router"].T, p["b_router"], n_ctx, ctx_row)
    ntok = bsz * n
    meta_flat = meta[:, :, :META_FIELDS].reshape(-1)
    xs, blk_e, blk_rows, start = _dispatch(meta_flat, counts[:, 0].astype(I32), idx,
                                           h2.reshape(ntok, d),
                                           _moe_blocks(2 * ntok, ntok // TOK_BLK))
    ys = _experts(blk_e, blk_rows, xs, p["w_gate"], p["w_up"], p["w_down"], layer)
    return _combine(meta_flat, start, x1, idx[2:4].T, wts[:2].T, mod_l, ys, n_ctx, ctx_row,
                    latent_only=p["last"])


def kernel(x, c, ctx, c_ctx, w_ada, b_ada, norm1, norm2, w_in, na_q_norm, na_k_norm, na_rpb, hg_lower,
           hg_norm, wa_q_norm, wa_k_norm, wa_sink, w_pa, w_pb, w_pc, w_out, w_router, b_router,
           w_gate, w_up, w_down):
    bsz, t, d = x.shape
    n_ctx = ctx.shape[1]
    depth = w_ada.shape[0]
    assert n_ctx == TOK_BLK and t % TOK_BLK == 0 and t // TOK_BLK >= 3
    assert t % GRID_W == 0 and bsz + 1 <= SUBLANES

    xa = jnp.concatenate([ctx, x], axis=1)
    cond = jnp.concatenate([c, c_ctx[None], jnp.zeros((SUBLANES - bsz - 1, d), F32)], axis=0)
    mod = _ada(cond, w_ada, b_ada)
    rope = _rope_tables(n_ctx, t)
    for l in range(depth):
        p = dict(norm1=norm1[l], norm2=norm2[l], w_in=w_in, na_q_norm=na_q_norm[l],
                 na_k_norm=na_k_norm[l], na_rpb=na_rpb[l], hg_lower=hg_lower, hg_norm=hg_norm[l],
                 wa_q_norm=wa_q_norm[l], wa_k_norm=wa_k_norm[l], wa_sink=wa_sink[l],
                 w_pa=w_pa[l], w_pb=w_pb[l], w_pc=w_pc[l], w_out=w_out[l],
                 w_router=w_router, b_router=b_router, w_gate=w_gate, w_up=w_up,
                 w_down=w_down, last=l == depth - 1)
        xa = _layer(xa, mod[l], l, n_ctx, bsz, p, rope)
    return xa
```

```python
import functools

import numpy as np
import jax
import jax.numpy as jnp
from jax import lax
from jax.experimental import pallas as pl
from jax.experimental.pallas import tpu as pltpu

F32 = jnp.float32
BF16 = jnp.bfloat16
I32 = jnp.int32

GRID_W = 64
NA_HEADS = 8
NA_ROWS = 8
NA_COLS = 16
HEAD_DIM = 64
HG_HEADS = 4
HG_DK = 128
LOG_FLOOR = 1e-30
WA_HEADS = 8
WA_KV_HEADS = 2
WA_WINDOW = 128
ROPE_THETA = 10000.0
N_EXPERTS = 16
N_GROUPS = 4
EXPERTS_PER_GROUP = N_EXPERTS // N_GROUPS
N_BRANCHES = 3
NEG_INF = -1e30
RMS_EPS = 1e-6

LANES = 128
SUBLANES = 8
VMEM_LIMIT = 56 * 1024 * 1024

TOK_BLK = 256
MOE_BLK = 1024
MOE_CHUNK = 32
MOE_PUT_CHUNK = 32
MOE_GATHER_DEPTH = 2
MOE_ZERO_ROWS = 256


def _cparams(n_axes):
    return pltpu.CompilerParams(
        dimension_semantics=("arbitrary",) * n_axes, vmem_limit_bytes=VMEM_LIMIT)


def _sigmoid(x):
    return 1.0 / (1.0 + jnp.exp(-x))


def _silu(x):
    return x * _sigmoid(x)


def _dot(a, b):
    return jnp.dot(a.astype(BF16), b.astype(BF16), preferred_element_type=F32)


def _dot_nt(a, b):
    return lax.dot_general(a.astype(BF16), b.astype(BF16), (((1,), (1,)), ((), ())),
                           preferred_element_type=F32)


def _dot_tn(a, b):
    return lax.dot_general(a.astype(BF16), b.astype(BF16), (((0,), (0,)), ((), ())),
                           preferred_element_type=F32)


def _hi_lo(a):
    hi = a.astype(BF16)
    lo = (a - hi.astype(F32)).astype(BF16)
    return hi, lo


def _dot3(a, b):
    ah, al = _hi_lo(a)
    bh, bl = _hi_lo(b)
    d = functools.partial(jnp.dot, preferred_element_type=F32)
    return d(ah, bh) + (d(ah, bl) + d(al, bh))


def _dot3_nt(a, b):
    ah, al = _hi_lo(a)
    bh, bl = _hi_lo(b)
    d = functools.partial(lax.dot_general, dimension_numbers=(((1,), (1,)), ((), ())),
                          preferred_element_type=F32)
    return d(ah, bh) + (d(ah, bl) + d(al, bh))


def _rms(x, w):
    return x * lax.rsqrt(jnp.mean(x * x, axis=-1, keepdims=True) + RMS_EPS) * w


def _iota(shape, dim):
    return lax.broadcasted_iota(I32, shape, dim)


def _ada_kernel(cond_ref, w_ref, b_ref, o_ref):
    o_ref[0] = _dot3(_silu(cond_ref[...]), w_ref[0]) + b_ref[0]


def _ada(cond, w_ada, b_ada):
    depth, d, d6 = w_ada.shape
    rows = cond.shape[0]
    tn = d6 // 4
    return pl.pallas_call(
        _ada_kernel,
        grid=(depth, d6 // tn),
        in_specs=[pl.BlockSpec((rows, d), lambda l, j: (0, 0)),
                  pl.BlockSpec((1, d, tn), lambda l, j: (l, 0, j)),
                  pl.BlockSpec((1, 1, tn), lambda l, j: (l, 0, j))],
        out_specs=pl.BlockSpec((1, rows, tn), lambda l, j: (l, 0, j)),
        out_shape=jax.ShapeDtypeStruct((depth, rows, d6), F32),
        compiler_params=_cparams(2), name="ada",
    )(cond, w_ada, b_ada.reshape(depth, 1, d6))


def _mod_rows(mod_ref, which, b, row0, n_ctx, ctx_row, tm, d):
    lat = mod_ref[pl.ds(b, 1), which * d:(which + 1) * d]
    cx = mod_ref[ctx_row:ctx_row + 1, which * d:(which + 1) * d]
    row = row0 + _iota((tm, d), 0)
    return jnp.where(row < n_ctx, cx, lat)


def _inproj_kernel(x_ref, mod_ref, nw_ref, w_ref, o_ref, h_ref, *, n_ctx, ctx_row):
    b, i, j = pl.program_id(0), pl.program_id(1), pl.program_id(2)
    tm, d = h_ref.shape[1:]

    @pl.when(j == 0)
    def _():
        h = _rms(x_ref[0], nw_ref[...])
        shift = _mod_rows(mod_ref, 0, b, i * tm, n_ctx, ctx_row, tm, d)
        scale = _mod_rows(mod_ref, 1, b, i * tm, n_ctx, ctx_row, tm, d)
        h_ref[0] = (h * (1.0 + scale) + shift).astype(BF16)

    o_ref[0] = jnp.dot(h_ref[0], w_ref[...], preferred_element_type=F32).astype(o_ref.dtype)


def _matmul_kernel(h_ref, w_ref, o_ref):
    o_ref[0] = jnp.dot(h_ref[0], w_ref[...], preferred_element_type=F32).astype(o_ref.dtype)


def _project(h, w, tn, out_dtype):
    bsz, n, d = h.shape
    cols = w.shape[1]
    tm = 1408 if n % 1408 == 0 else TOK_BLK
    return pl.pallas_call(
        _matmul_kernel,
        grid=(bsz, n // tm, cols // tn),
        in_specs=[pl.BlockSpec((1, tm, d), lambda b, i, j: (b, i, 0)),
                  pl.BlockSpec((d, tn), lambda b, i, j: (0, j))],
        out_specs=pl.BlockSpec((1, tm, tn), lambda b, i, j: (b, i, j)),
        out_shape=jax.ShapeDtypeStruct((bsz, n, cols), out_dtype),
        compiler_params=_cparams(3), name="project",
    )(h, w)


W_IN_COL_BLK = 256


def _take_cols_kernel(src_ref, w_ref, o_ref):
    del src_ref
    o_ref[...] = w_ref[0].astype(BF16)


def _take_cols(w_in, layer, col_ranges):
    d = w_in.shape[1]
    blk = W_IN_COL_BLK
    src = []
    for start, stop in col_ranges:
        assert start % blk == 0 and stop % blk == 0
        src += list(range(start // blk, stop // blk))
    return pl.pallas_call(
        _take_cols_kernel,
        grid_spec=pltpu.PrefetchScalarGridSpec(
            num_scalar_prefetch=1, grid=(len(src),),
            in_specs=[pl.BlockSpec((1, d, blk), lambda j, src: (layer, 0, src[j]))],
            out_specs=pl.BlockSpec((d, blk), lambda j, src: (0, j))),
        out_shape=jax.ShapeDtypeStruct((d, len(src) * blk), BF16),
        compiler_params=_cparams(1), name="take_cols",
    )(jnp.asarray(src, I32), w_in)


def _inproj(xa, mod_l, norm_w, w, n_ctx, ctx_row, tn, out_dtype):
    bsz, n, d = xa.shape
    cols = w.shape[1]
    tm = 1408 if n % 1408 == 0 else TOK_BLK
    return pl.pallas_call(
        functools.partial(_inproj_kernel, n_ctx=n_ctx, ctx_row=ctx_row),
        grid=(bsz, n // tm, cols // tn),
        in_specs=[pl.BlockSpec((1, tm, d), lambda b, i, j: (b, i, 0)),
                  pl.BlockSpec(mod_l.shape, lambda b, i, j: (0, 0)),
                  pl.BlockSpec((1, d), lambda b, i, j: (0, 0)),
                  pl.BlockSpec((d, tn), lambda b, i, j: (0, j))],
        out_specs=[pl.BlockSpec((1, tm, tn), lambda b, i, j: (b, i, j)),
                   pl.BlockSpec((1, tm, d), lambda b, i, j: (b, i, 0))],
        out_shape=[jax.ShapeDtypeStruct((bsz, n, cols), out_dtype),
                   jax.ShapeDtypeStruct((bsz, n, d), BF16)],
        compiler_params=_cparams(3), name="inproj",
    )(xa, mod_l, norm_w.reshape(1, d), w)


def _pair_norm(x, w):
    lane = _iota(x.shape, 1)
    lo = lane < HEAD_DIM
    sq = x * x
    s_lo = jnp.sum(jnp.where(lo, sq, 0.0), axis=-1, keepdims=True)
    s_hi = jnp.sum(jnp.where(lo, 0.0, sq), axis=-1, keepdims=True)
    ms = jnp.where(lo, s_lo, s_hi) * (1.0 / HEAD_DIM)
    return x * lax.rsqrt(ms + RMS_EPS) * w


LOG2E = 1.4426950408889634


def _ones_beside(v, value_lanes):
    return jnp.where(value_lanes, v, 1.0).astype(BF16)


def _softmax_av(s_loc, s_ctx, v_loc, v_ctx, sink):
    m = jnp.maximum(jnp.max(s_loc, axis=-1, keepdims=True), jnp.max(s_ctx, axis=-1, keepdims=True))
    if sink is not None:
        m = jnp.maximum(m, sink)
    p_loc = jnp.exp2(s_loc - m).astype(BF16)
    p_ctx = jnp.exp2(s_ctx - m).astype(BF16)
    acc = (jnp.dot(p_loc, v_loc, preferred_element_type=F32)
           + jnp.dot(p_ctx, v_ctx, preferred_element_type=F32))
    den = pltpu.roll(acc, LANES // 2, 1)
    if sink is not None:
        den = den + jnp.exp2(sink - m)
    return acc / den


NA_QROWS = TOK_BLK // GRID_W
NA_KROWS = 3 * NA_QROWS
N_RPB_COLS = 2 * NA_COLS - 1
N_RPB_ROWS = 2 * NA_ROWS - 1
NA_VARIANTS = 4


def _na_row_valid(variant, a, k):
    if variant == 0:
        return False
    if variant == 1:
        return NA_QROWS <= k < NA_QROWS + NA_ROWS
    if variant == 3:
        return k < NA_ROWS
    return a <= k < a + NA_ROWS


def _na_bias_kernel(rpb_ref, o_ref):
    h = pl.program_id(0)
    qc = _iota((GRID_W, GRID_W), 0)
    kc = _iota((GRID_W, GRID_W), 1)
    col_lo = jnp.clip(qc - NA_COLS // 2, 0, GRID_W - NA_COLS)
    col_ok = (kc >= col_lo) & (kc < col_lo + NA_COLS)
    d_col = kc - qc + (NA_COLS - 1)
    masked = jnp.full((GRID_W, GRID_W), NEG_INF, F32)
    blocks = []
    for dr in range(N_RPB_ROWS):
        acc = jnp.zeros((GRID_W, GRID_W), F32)
        for dc in range(N_RPB_COLS):
            val = rpb_ref[(h * N_RPB_ROWS + dr) * N_RPB_COLS + dc]
            acc = jnp.where(d_col == dc, val, acc)
        blocks.append(jnp.where(col_ok, acc * LOG2E, NEG_INF))
    for variant in range(NA_VARIANTS):
        for a in range(NA_QROWS):
            for k in range(NA_KROWS):
                dr = k - NA_QROWS - a + NA_ROWS - 1
                ok = _na_row_valid(variant, a, k) and 0 <= dr < N_RPB_ROWS
                o_ref[variant, 0, a * GRID_W:(a + 1) * GRID_W, k * GRID_W:(k + 1) * GRID_W] = (
                    blocks[dr] if ok else masked)


def _na_bias(rpb):
    heads = rpb.shape[0]
    return pl.pallas_call(
        _na_bias_kernel,
        grid=(heads,),
        in_specs=[pl.BlockSpec(memory_space=pltpu.SMEM)],
        out_specs=pl.BlockSpec((NA_VARIANTS, 1, TOK_BLK, 3 * TOK_BLK), lambda h: (0, h, 0, 0)),
        out_shape=jax.ShapeDtypeStruct((NA_VARIANTS, heads, TOK_BLK, 3 * TOK_BLK), F32),
        compiler_params=_cparams(1), name="na_bias",
    )(rpb.reshape(-1))


def _na_kernel(q_ref, k0_ref, kn_ref, vp_ref, vc_ref, vn_ref, vx_ref,
               bias_ref, qn_ref, kn_w_ref, o_ref, ring_ref, kctx_ref):
    i = pl.program_id(0)
    tq = q_ref.shape[1]
    lo = _iota((tq, LANES), 1) < HEAD_DIM
    lo_k = _iota((3 * tq, LANES), 1) < HEAD_DIM
    qn_w = qn_ref[...] * (HEAD_DIM ** -0.5 * LOG2E)
    units = [(b, p) for b in range(q_ref.shape[0]) for p in range(NA_HEADS // 2)]
    pair_of = lambda p: slice(p * LANES, (p + 1) * LANES)

    def normed(ref, b, p):
        return _pair_norm(ref[b, :, pair_of(p)].astype(F32), kn_w_ref[...]).astype(BF16)

    @pl.when(i == 0)
    def _():
        for b, p in units:
            pair = pair_of(p)
            k0 = normed(k0_ref, b, p)
            kctx_ref[b, :, pair] = k0
            ring_ref[0, b, :, pair] = k0
            ring_ref[2, b, :, pair] = k0

    nxt = (i + 1) % 3
    for b, p in units:
        ring_ref[nxt, b, :, pair_of(p)] = normed(kn_ref, b, p)
    prv, cur = (i + 2) % 3, i % 3

    for b, p in units:
        pair = pair_of(p)
        f32 = lambda ref: ref[b, :, pair].astype(F32)
        q = _pair_norm(f32(q_ref), qn_w)
        k_loc = jnp.concatenate([ring_ref[prv, b, :, pair], ring_ref[cur, b, :, pair],
                                 ring_ref[nxt, b, :, pair]], axis=0)
        k_ctx = kctx_ref[b, :, pair]
        v_loc = jnp.concatenate([f32(vp_ref), f32(vc_ref), f32(vn_ref)], axis=0)
        v_ctx = f32(vx_ref)
        q2 = jnp.concatenate([jnp.where(lo, q, 0.0), jnp.where(lo, 0.0, q)], axis=0).astype(BF16)
        s_loc2 = _dot_nt(q2, k_loc)
        s_ctx2 = _dot_nt(q2, k_ctx)
        outs = []
        for h in range(2):
            mine = lo if h == 0 else ~lo
            mine_k = lo_k if h == 0 else ~lo_k
            rows = slice(h * tq, (h + 1) * tq)
            outs.append(_softmax_av(s_loc2[rows] + bias_ref[0, 2 * p + h], s_ctx2[rows],
                                    _ones_beside(v_loc, mine_k), _ones_beside(v_ctx, mine), None))
        o_ref[b, :, pair] = jnp.where(lo, outs[0], outs[1]).astype(o_ref.dtype)


def _na_attention(proj, bias, qn_w, kn_w, col_q, col_k, col_v):
    bsz, n, _ = proj.shape
    nblk = n // TOK_BLK
    last = nblk - 1
    width = NA_HEADS * HEAD_DIM
    cq, ck, cv = col_q // width, col_k // width, col_v // width

    def blk(col, shift):
        return pl.BlockSpec((bsz, TOK_BLK, width),
                            lambda i: (0, jnp.clip(i + shift, 0, last), col))

    def ctx_blk(col):
        return pl.BlockSpec((bsz, TOK_BLK, width), lambda i: (0, 0, col))

    def variant(i):
        return (jnp.where(i == 0, 0, jnp.where(i == 1, 1, jnp.where(i == last, 3, 2))), 0, 0, 0)

    vec = pl.BlockSpec((1, LANES), lambda i: (0, 0))
    return pl.pallas_call(
        _na_kernel,
        grid=(nblk,),
        in_specs=[blk(cq, 0), ctx_blk(ck), blk(ck, 1), blk(cv, -1), blk(cv, 0), blk(cv, 1),
                  ctx_blk(cv),
                  pl.BlockSpec((1, NA_HEADS, TOK_BLK, 3 * TOK_BLK), variant), vec, vec],
        out_specs=pl.BlockSpec((bsz, TOK_BLK, width), lambda i: (0, i, 0)),
        out_shape=jax.ShapeDtypeStruct((bsz, n, width), BF16),
        scratch_shapes=[pltpu.VMEM((3, bsz, TOK_BLK, width), BF16),
                        pltpu.VMEM((bsz, TOK_BLK, width), BF16)],
        compiler_params=_cparams(1), name="na_attn",
    )(proj, proj, proj, proj, proj, proj, proj, bias,
      jnp.tile(qn_w, 2).reshape(1, LANES), jnp.tile(kn_w, 2).reshape(1, LANES))


WA_STACK = 4


def _rope(x, cos, sin_signed):
    lane = _iota(x.shape, 1)
    first = (lane & (HEAD_DIM // 2 - 1)) < HEAD_DIM // 4
    quarter = HEAD_DIM // 4
    partner = jnp.where(first, pltpu.roll(x, LANES - quarter, 1), pltpu.roll(x, quarter, 1))
    return x * cos + partner * sin_signed


def _wa_kernel(sink_ref, q_ref, kp_ref, kc_ref, kn_ref, vp_ref, vc_ref, vn_ref, kx_ref, vx_ref,
               cq_ref, sq_ref, cp_ref, sp_ref, cn_ref, sn_ref, qn_ref, kn_w_ref, o_ref, *, last):
    i = pl.program_id(1)
    tq = q_ref.shape[1]
    tn = kp_ref.shape[1]
    n_loc = tq + 2 * tn
    cos_loc = jnp.concatenate([cp_ref[...], cq_ref[...], cn_ref[...]], axis=0)
    sin_loc = jnp.concatenate([sp_ref[...], sq_ref[...], sn_ref[...]], axis=0)
    qn_w = qn_ref[...] * (HEAD_DIM ** -0.5 * LOG2E)

    qi = _iota((tq, n_loc), 0)
    kj = _iota((tq, n_loc), 1)
    rel = kj - tn - qi
    lo_col = jnp.where(i >= 2, 0, tn)
    hi_col = jnp.where(i >= 1, jnp.where(i < last, n_loc, tn + tq), 0)
    ok = (rel >= -WA_WINDOW) & (rel <= WA_WINDOW) & (kj >= lo_col) & (kj < hi_col)
    mask = jnp.where(ok, 0.0, NEG_INF)

    mask_g = jnp.concatenate([mask] * WA_STACK, axis=0)
    for b in range(q_ref.shape[0]):
        _wa_one_batch(b, sink_ref, q_ref, (kp_ref, kc_ref, kn_ref), (vp_ref, vc_ref, vn_ref),
                      kx_ref, vx_ref, (cos_loc, sin_loc), (cq_ref[...], sq_ref[...]), qn_w,
                      kn_w_ref[...], mask_g, o_ref)


def _wa_one_batch(b, sink_ref, q_ref, k_refs, v_refs, kx_ref, vx_ref, rope_loc, rope_q, qn_w, kn_w,
                  mask_g, o_ref):
    tq = q_ref.shape[1]
    half = LANES // 2
    group = WA_HEADS // WA_KV_HEADS
    f32 = lambda ref: ref[b].astype(F32)
    k_loc = _pair_norm(jnp.concatenate([f32(r) for r in k_refs], axis=0), kn_w)
    k_loc = _rope(k_loc, *rope_loc).astype(BF16)
    k_ctx = _pair_norm(f32(kx_ref), kn_w).astype(BF16)
    v_loc = jnp.concatenate([f32(r) for r in v_refs], axis=0)
    v_ctx = f32(vx_ref)
    lo = _iota((tq, LANES), 1) < HEAD_DIM
    lo_loc = _iota(v_loc.shape, 1) < HEAD_DIM
    q_pairs = [_rope(_pair_norm(q_ref[b, :, p * LANES:(p + 1) * LANES].astype(F32), qn_w), *rope_q)
               for p in range(WA_HEADS // 2)]
    outs = [None] * WA_HEADS
    for first in range(0, WA_HEADS, WA_STACK):
        heads = range(first, first + WA_STACK)
        kv = first // group
        mine, mine_loc = (lo, lo_loc) if kv == 0 else (~lo, ~lo_loc)
        stack = []
        for h in heads:
            q = q_pairs[h // 2]
            q = q if h % 2 == kv else pltpu.roll(q, half, 1)
            stack.append(jnp.where(mine, q, 0.0).astype(BF16))
        q_g = jnp.concatenate(stack, axis=0)
        sink_g = jnp.concatenate([jnp.full((tq, 1), sink_ref[h] * LOG2E, F32) for h in heads], axis=0)
        o_g = _softmax_av(_dot_nt(q_g, k_loc) + mask_g, _dot_nt(q_g, k_ctx),
                          _ones_beside(v_loc, mine_loc), _ones_beside(v_ctx, mine), sink_g)
        for n, h in enumerate(heads):
            o = o_g[n * tq:(n + 1) * tq]
            outs[h] = o if h % 2 == kv else pltpu.roll(o, half, 1)
    for p in range(WA_HEADS // 2):
        o_ref[b, :, p * LANES:(p + 1) * LANES] = jnp.where(
            lo, outs[2 * p], outs[2 * p + 1]).astype(o_ref.dtype)


def _wa_attention(proj, sink, qn_w, kn_w, cos_t, sin_t, col_q, col_k, col_v):
    bsz, n, _ = proj.shape
    nblk = n // TOK_BLK
    last = nblk - 1
    qw = WA_HEADS * HEAD_DIM
    cq, ck, cv = col_q // qw, col_k // LANES, col_v // LANES

    assert TOK_BLK % WA_WINDOW == 0
    per = TOK_BLK // WA_WINDOW

    def near(i, shift):
        if shift == 0:
            return i
        return jnp.clip(per * i - 1 if shift < 0 else per * (i + 1), 0, per * nblk - 1)

    def blk(col, shift):
        rows = TOK_BLK if shift == 0 else WA_WINDOW
        return pl.BlockSpec((1, rows, LANES), lambda b, i: (b, near(i, shift), col))

    def tab(shift):
        rows = TOK_BLK if shift == 0 else WA_WINDOW
        return pl.BlockSpec((rows, LANES), lambda b, i: (near(i, shift), 0))

    def ctx_blk(col):
        return pl.BlockSpec((1, TOK_BLK, LANES), lambda b, i: (b, 0, col))

    vec = pl.BlockSpec((1, LANES), lambda b, i: (0, 0))
    return pl.pallas_call(
        functools.partial(_wa_kernel, last=last),
        grid=(bsz, nblk),
        in_specs=[pl.BlockSpec(memory_space=pltpu.SMEM),
                  pl.BlockSpec((1, TOK_BLK, qw), lambda b, i: (b, i, cq)),
                  blk(ck, -1), blk(ck, 0), blk(ck, 1), blk(cv, -1), blk(cv, 0), blk(cv, 1),
                  ctx_blk(ck), ctx_blk(cv),
                  tab(0), tab(0), tab(-1), tab(-1), tab(1), tab(1), vec, vec],
        out_specs=pl.BlockSpec((1, TOK_BLK, qw), lambda b, i: (b, i, 0)),
        out_shape=jax.ShapeDtypeStruct((bsz, n, qw), BF16),
        compiler_params=_cparams(2), name="wa_attn",
    )(sink, proj, proj, proj, proj, proj, proj, proj, proj, proj,
      cos_t, sin_t, cos_t, sin_t, cos_t, sin_t,
      jnp.tile(qn_w, 2).reshape(1, LANES), jnp.tile(kn_w, 2).reshape(1, LANES))


def _rope_tables(n_ctx, t):
    quarter = HEAD_DIM // 4
    inv_freq = ROPE_THETA ** (-jnp.arange(quarter, dtype=F32) / quarter)
    pos = jnp.arange(t)
    lane = np.arange(LANES)
    in_head = lane % HEAD_DIM
    use_col = in_head >= HEAD_DIM // 2
    second = (in_head % (HEAD_DIM // 2)) >= quarter
    freq = inv_freq[in_head % quarter]
    p = jnp.where(use_col[None, :], (pos % GRID_W)[:, None], (pos // GRID_W)[:, None]).astype(F32)
    ang = p * freq[None, :]
    cos = jnp.cos(ang)
    sin = jnp.where(second[None, :], jnp.sin(ang), -jnp.sin(ang))
    cos = jnp.concatenate([jnp.ones((n_ctx, LANES), F32), cos], axis=0)
    sin = jnp.concatenate([jnp.zeros((n_ctx, LANES), F32), sin], axis=0)
    return cos, sin


HG_CHUNK = 256


def _hgrn_kernel(q_ref, f_ref, v_ref, hl_ref, *rest, layer, reverse, final):
    if final:
        g_ref, prev_ref, nw_ref, o_ref, st_ref = rest
    else:
        o_ref, st_ref = rest
    @pl.when(pl.program_id(0) == 0)
    def _():
        st_ref[...] = jnp.zeros_like(st_ref)

    a = hl_ref[0]
    e = jnp.exp(a - jnp.max(a, axis=0, keepdims=True))
    pr = e / jnp.sum(e, axis=0, keepdims=True)
    lb_all = jnp.zeros((1, a.shape[1]), F32)
    for j in range(1, layer + 1):
        lb_all = lb_all + pr[j:j + 1]

    for b, h in [(b, h) for b in range(q_ref.shape[0]) for h in range(HG_HEADS)]:
        lanes = slice(h * HG_DK, (h + 1) * HG_DK)
        lb = lb_all[:, lanes]
        qs = _silu(q_ref[b, :, lanes].astype(F32))
        f = lb + (1.0 - lb) * _sigmoid(f_ref[b, :, lanes])
        o, st = _hgrn_block(qs, 1.0 - f, jnp.log(jnp.maximum(f, LOG_FLOOR)),
                            v_ref[b, :, lanes].astype(F32), st_ref[b, h], reverse)
        st_ref[b, h] = st
        if final:
            tot = prev_ref[b, :, lanes] + o
            gated = _rms(tot, nw_ref[...]) * _silu(g_ref[b, :, lanes].astype(F32))
            o_ref[b, :, lanes] = gated.astype(o_ref.dtype)
        else:
            o_ref[b, :, lanes] = o


def _hgrn_block(qs, kk, g, v, st, reverse):
    c = qs.shape[0]
    n_sub = c // HG_CHUNK
    row = _iota((c, HG_DK), 0)
    ri = _iota((c, c), 0)
    ci = _iota((c, c), 1)
    block_xor = ri ^ ci

    tri = jnp.where(((ci >= ri) if reverse else (ci <= ri)) & (block_xor < HG_CHUNK), 1.0, 0.0)
    tri = tri.astype(BF16)
    g_hi = g.astype(BF16)
    rest = g - g_hi.astype(F32)
    g_mid = rest.astype(BF16)
    g_lo = (rest - g_mid.astype(F32)).astype(BF16)
    tdot = functools.partial(jnp.dot, preferred_element_type=F32)
    b = tdot(tri, g_hi) + (tdot(tri, g_mid) + tdot(tri, g_lo))

    def ref_rows(hs):
        blk = 2 * hs
        inner = hs if reverse else hs - 1
        if blk >= SUBLANES:
            x = b.reshape(c // blk, blk, HG_DK)
            return jnp.broadcast_to(x[:, inner:inner + 1, :], x.shape).reshape(c, HG_DK)
        x = b.reshape(c // SUBLANES, SUBLANES, HG_DK)
        sub = _iota(x.shape, 1)
        out = None
        for p in range(SUBLANES // blk):
            r = p * blk + inner
            piece = jnp.broadcast_to(x[:, r:r + 1, :], x.shape)
            out = piece if out is None else jnp.where(sub >= p * blk, piece, out)
        return out.reshape(c, HG_DK)

    att = jnp.zeros((c, c), F32)
    hs = HG_CHUNK // 2
    while hs >= 1:
        blk = 2 * hs
        decay = jnp.exp(-jnp.abs(b - ref_rows(hs)))
        in_block = row & (blk - 1)
        later_half = (in_block < hs) if reverse else (in_block >= hs)
        q_l = jnp.where(later_half, qs * decay, 0.0)
        k_l = jnp.where(later_half, 0.0, kk * decay)
        a_l = _dot_nt(q_l, k_l)
        att = att + (a_l if blk == c else jnp.where(block_xor < blk, a_l, 0.0))
        hs //= 2
    o = _dot(att, v) + jnp.sum(qs * kk, axis=-1, keepdims=True) * v

    q_dec = qs * jnp.exp(b)
    b3 = b.reshape(n_sub, HG_CHUNK, HG_DK)
    end = 0 if reverse else HG_CHUNK - 1
    b_end = jnp.broadcast_to(b3[:, end:end + 1, :], b3.shape).reshape(c, HG_DK)
    k_dec = kk * jnp.exp(b_end - b)
    outs = [None] * n_sub
    for s in (reversed(range(n_sub)) if reverse else range(n_sub)):
        rows = slice(s * HG_CHUNK, (s + 1) * HG_CHUNK)
        outs[s] = o[rows] + _dot_nt(q_dec[rows], st)
        st = st * jnp.exp(b_end[s * HG_CHUNK:s * HG_CHUNK + 1, :]) + _dot_tn(v[rows], k_dec[rows])
    return jnp.concatenate(outs, axis=0), st


def _hgrn_pass(hl, layer, reverse, q_src, f_src, v_src, final_args=None):
    bsz, n, _ = q_src[0].shape
    nchunk = n // TOK_BLK
    width = HG_HEADS * HG_DK

    def chunk(t):
        return jnp.where(t == 0, 0, nchunk - t) if reverse else t

    def blk(col):
        return pl.BlockSpec((bsz, TOK_BLK, width), lambda t: (0, chunk(t), col // width))

    in_specs = [blk(q_src[1]), blk(f_src[1]), blk(v_src[1]),
                pl.BlockSpec((1,) + hl.shape[1:], lambda t: (1 if reverse else 0, 0, 0))]
    args = [q_src[0], f_src[0], v_src[0], hl]
    final = final_args is not None
    if final:
        g_src, prev, norm_w = final_args
        in_specs += [blk(g_src[1]), blk(0), pl.BlockSpec((1, HG_DK), lambda t: (0, 0))]
        args += [g_src[0], prev, norm_w.reshape(1, HG_DK)]
    return pl.pallas_call(
        functools.partial(_hgrn_kernel, layer=layer, reverse=reverse, final=final),
        grid=(nchunk,),
        in_specs=in_specs,
        out_specs=blk(0),
        out_shape=jax.ShapeDtypeStruct((bsz, n, width), BF16 if final else F32),
        scratch_shapes=[pltpu.VMEM((bsz, HG_HEADS, HG_DK, HG_DK), F32)],
        compiler_params=_cparams(1), name="hgrn_bwd" if reverse else "hgrn_fwd",
    )(*args)


def _merge_kernel(x_ref, ya_ref, yb_ref, yc_ref, ga_ref, gb_ref, gc_ref, mod_ref,
                  wa_ref, wb_ref, wc_ref, wo_ref, nw_ref, wr_ref, br_ref,
                  o_ref, h_ref, idx_ref, wt_ref, cnt_ref, meta_ref, carry_ref, *, n_ctx, ctx_row):
    b, i = pl.program_id(0), pl.program_id(1)
    tm, d = x_ref.shape[1:]

    @pl.when((b == 0) & (i == 0))
    def _():
        carry_ref[...] = jnp.zeros_like(carry_ref)

    gate_of = lambda ref: _sigmoid(ref[0].astype(F32))
    merged = (gate_of(ga_ref) * _dot(ya_ref[0], wa_ref[...])
              + gate_of(gb_ref) * _dot(yb_ref[0], wb_ref[...])
              + gate_of(gc_ref) * _dot(yc_ref[0], wc_ref[...]))
    mix = _dot(merged, wo_ref[...])
    gate = _mod_rows(mod_ref, 2, b, i * tm, n_ctx, ctx_row, tm, d)
    x1 = x_ref[0] + gate * mix
    o_ref[0] = x1

    for s in range(tm // TOK_BLK):
        rows = pl.ds(s * TOK_BLK, TOK_BLK)
        _route_tile(x1[s * TOK_BLK:(s + 1) * TOK_BLK], b, i * tm + s * TOK_BLK, mod_ref, nw_ref,
                    wr_ref, br_ref, h_ref.at[0, rows, :], idx_ref.at[:, rows], wt_ref.at[:, rows],
                    cnt_ref, meta_ref.at[s], carry_ref, n_ctx, ctx_row)


def _merge(xa, ya, yb, yc, proj, mod_l, w_pa, w_pb, w_pc, w_out, col_gate, norm_w, w_router_t,
           b_router, n_ctx, ctx_row):
    bsz, n, d = xa.shape
    tm = 3 * TOK_BLK if n % (3 * TOK_BLK) == 0 else TOK_BLK
    nt = n // tm
    per = tm // TOK_BLK
    g0 = col_gate // d
    tile = lambda w: pl.BlockSpec((1, tm, w), lambda b, i: (b, i, 0))
    gate = lambda k: pl.BlockSpec((1, tm, d), lambda b, i: (b, i, g0 + k))
    full = lambda a: pl.BlockSpec(a.shape, lambda b, i: (0,) * a.ndim)
    lane_tile = pl.BlockSpec((SUBLANES, tm), lambda b, i: (0, b * nt + i))
    br = b_router.reshape(N_EXPERTS, 1)
    nw = norm_w.reshape(1, d)
    return pl.pallas_call(
        functools.partial(_merge_kernel, n_ctx=n_ctx, ctx_row=ctx_row),
        grid=(bsz, nt),
        in_specs=[tile(d), tile(ya.shape[2]), tile(yb.shape[2]), tile(yc.shape[2]),
                  gate(0), gate(1), gate(2), full(mod_l),
                  full(w_pa), full(w_pb), full(w_pc), full(w_out),
                  full(nw), full(w_router_t), full(br)],
        out_specs=[tile(d), tile(d), lane_tile, lane_tile,
                   pl.BlockSpec((N_EXPERTS, LANES), lambda b, i: (0, 0)),
                   pl.BlockSpec((per, N_EXPERTS, LANES), lambda b, i: (b * nt + i, 0, 0))],
        out_shape=[jax.ShapeDtypeStruct((bsz, n, d), F32),
                   jax.ShapeDtypeStruct((bsz, n, d), BF16),
                   jax.ShapeDtypeStruct((SUBLANES, bsz * n), I32),
                   jax.ShapeDtypeStruct((SUBLANES, bsz * n), F32),
                   jax.ShapeDtypeStruct((N_EXPERTS, LANES), F32),
                   jax.ShapeDtypeStruct((bsz * nt * per, N_EXPERTS, LANES), I32)],
        scratch_shapes=[pltpu.VMEM((N_EXPERTS, LANES), F32)],
        compiler_params=_cparams(2), name="merge",
    )(xa, ya, yb, yc, proj, proj, proj, mod_l, w_pa, w_pb, w_pc, w_out, nw, w_router_t, br)


def _route_tile(x, b, row0, mod_ref, nw_ref, wr_ref, br_ref, h_ref, idx_ref, wt_ref, cnt_ref,
                meta_ref, carry_ref, n_ctx, ctx_row):
    tm, d = x.shape
    h = _rms(x, nw_ref[...])
    shift = _mod_rows(mod_ref, 3, b, row0, n_ctx, ctx_row, tm, d)
    scale = _mod_rows(mod_ref, 4, b, row0, n_ctx, ctx_row, tm, d)
    h = h * (1.0 + scale) + shift
    h_ref[...] = h.astype(h_ref.dtype)

    logits = _dot3_nt(wr_ref[...], h)
    ex = jnp.exp(logits - jnp.max(logits, axis=0, keepdims=True))
    probs = ex / jnp.sum(ex, axis=0, keepdims=True)
    sel = probs + br_ref[...]

    def row(x, r):
        return x[r:r + 1, :]

    best = None
    g_idx = None
    for g in range(N_GROUPS):
        r0 = g * EXPERTS_PER_GROUP
        a0, a1, a2, a3 = (row(sel, r0 + j) for j in range(EXPERTS_PER_GROUP))
        hi1, lo1 = jnp.maximum(a0, a1), jnp.minimum(a0, a1)
        hi2, lo2 = jnp.maximum(a2, a3), jnp.minimum(a2, a3)
        score = jnp.maximum(hi1, hi2) + jnp.maximum(jnp.minimum(hi1, hi2), jnp.maximum(lo1, lo2))
        if g == 0:
            best, g_idx = score, jnp.zeros_like(score, dtype=I32)
        else:
            better = score > best
            best = jnp.where(better, score, best)
            g_idx = jnp.where(better, g, g_idx)

    def pick(x, j):
        out = row(x, j)
        for g in range(1, N_GROUPS):
            out = jnp.where(g_idx == g, row(x, g * EXPERTS_PER_GROUP + j), out)
        return out

    in_grp = [pick(sel, j) for j in range(EXPERTS_PER_GROUP)]
    in_prob = [pick(probs, j) for j in range(EXPERTS_PER_GROUP)]

    def first_argmax(vals, skip):
        bv, bi, bp = None, None, None
        for j in range(EXPERTS_PER_GROUP):
            v = vals[j] if skip is None else jnp.where(skip == j, -jnp.inf, vals[j])
            if bv is None:
                bv, bi, bp = v, jnp.zeros_like(g_idx), in_prob[0]
            else:
                better = v > bv
                bv = jnp.where(better, v, bv)
                bi = jnp.where(better, j, bi)
                bp = jnp.where(better, in_prob[j], bp)
        return bi, bp

    loc0, p0 = first_argmax(in_grp, None)
    loc1, p1 = first_argmax(in_grp, loc0)
    e0 = g_idx * EXPERTS_PER_GROUP + loc0
    e1 = g_idx * EXPERTS_PER_GROUP + loc1
    wsum = p0 + p1
    w0, w1 = p0 / wsum, p1 / wsum

    er = _iota((N_EXPERTS, tm), 0)
    hit0 = er == e0
    hit1 = er == e1
    hot = jnp.where(hit0 | hit1, 1.0, 0.0)
    upper = jnp.where(_iota((tm, tm), 0) < _iota((tm, tm), 1), 1.0, 0.0)
    before = _dot(hot, upper)
    cnt = jnp.sum(hot, axis=1, keepdims=True)
    e_col = _iota((N_EXPERTS, 1), 0)

    def slot_starts(align):
        padded = jnp.floor((cnt + (align - 1)) * (1.0 / align)) * align
        off = jnp.zeros((N_EXPERTS, 1), F32)
        for e in range(N_EXPERTS - 1):
            off = off + jnp.where(e_col > e, padded[e:e + 1, :], 0.0)
        return padded, off

    cnt_pad, off_d = slot_starts(SUBLANES)
    _, off_c = slot_starts(MOE_CHUNK)
    positions = []
    for off in (off_d, off_c):
        for hit in (hit0, hit1):
            positions.append(jnp.sum(jnp.where(hit, before + off, 0.0), axis=0, keepdims=True))

    lane = _iota((N_EXPERTS, LANES), 1)
    carry = carry_ref[...]
    meta = jnp.zeros((N_EXPERTS, LANES), F32)
    for field, val in enumerate((carry, cnt, off_d, off_c)):
        meta = jnp.where(lane == field, val, meta)
    meta_ref[...] = meta.astype(I32)
    carry_ref[...] = carry + cnt_pad
    cnt_ref[...] = carry + cnt_pad

    idx_ref[...] = jnp.zeros_like(idx_ref)
    wt_ref[...] = jnp.zeros_like(wt_ref)
    for r, val in enumerate(positions):
        idx_ref[r:r + 1, :] = val.astype(I32)
    for r, val in enumerate((w0, w1)):
        wt_ref[r:r + 1, :] = val


def _round_up(x, m):
    return -(-x // m) * m


MOE_SORT_ROWS = _round_up(2 * TOK_BLK + N_EXPERTS * (SUBLANES - 1) + MOE_PUT_CHUNK - 1, LANES)
MOE_GATHER_ROWS = _round_up(2 * TOK_BLK + N_EXPERTS * (MOE_CHUNK - 1), LANES)
META_FIELDS = 4


def _moe_blocks(n_assign, n_tiles):
    slack = N_EXPERTS * (n_tiles * (SUBLANES - 1) + MOE_PUT_CHUNK - 1)
    return -(-(n_assign + slack) // MOE_BLK) + N_EXPERTS


def _chunk_copies(meta_ref, start_ref, tile, slot_field, chunk, do, make_copy):
    def expert(e, carry):
        base = (tile * N_EXPERTS + e) * META_FIELDS
        before, rows, slot = meta_ref[base], meta_ref[base + 1], meta_ref[base + slot_field]
        for c in range(TOK_BLK // chunk):
            @pl.when(rows > c * chunk)
            def _():
                do(make_copy(pl.multiple_of(slot + c * chunk, SUBLANES),
                             pl.multiple_of(start_ref[e] + before + c * chunk, SUBLANES)))
        return carry
    lax.fori_loop(0, N_EXPERTS, expert, 0)


def _dispatch_kernel(meta_ref, cnt_ref, pos_ref, h_ref, xs_ref, blk_e_ref, blk_rows_ref,
                     start_ref, sort_ref, zero_ref, sem, *, nblk):
    t = pl.program_id(0)
    nt = pl.num_programs(0)
    tm = h_ref.shape[0]

    @pl.when(t == 0)
    def _():
        def expert(e, end):
            start_ref[e] = end
            return end + ((cnt_ref[e] + MOE_PUT_CHUNK - 1 + MOE_BLK - 1) // MOE_BLK) * MOE_BLK
        start_ref[N_EXPERTS] = lax.fori_loop(0, N_EXPERTS, expert, 0)

        def block(j, carry):
            def count(e, acc):
                return acc + jnp.where(start_ref[e + 1] <= j * MOE_BLK, 1, 0)
            e = jnp.minimum(lax.fori_loop(0, N_EXPERTS, count, 0), N_EXPERTS - 1)
            blk_e_ref[j] = e
            blk_rows_ref[j] = jnp.clip(cnt_ref[e] - (j * MOE_BLK - start_ref[e]), 0, MOE_BLK)
            return carry
        lax.fori_loop(0, nblk, block, 0)

    r = _iota((MOE_SORT_ROWS, tm), 0)
    onehot = jnp.where((r == pos_ref[0:1, :]) | (r == pos_ref[1:2, :]), 1.0, 0.0)
    buf = t % 2
    sort_ref[buf] = _dot(onehot, h_ref[...])

    def copies(which):
        def make_copy(src_row, dst_row):
            return pltpu.make_async_copy(sort_ref.at[which, pl.ds(src_row, MOE_PUT_CHUNK), :],
                                         xs_ref.at[pl.ds(dst_row, MOE_PUT_CHUNK), :], sem.at[which])
        return make_copy

    put = functools.partial(_chunk_copies, meta_ref, start_ref, slot_field=2, chunk=MOE_PUT_CHUNK)

    @pl.when(t > 0)
    def _():
        put(tile=t - 1, do=lambda cp: cp.wait(), make_copy=copies(1 - buf))

    put(tile=t, do=lambda cp: cp.start(), make_copy=copies(buf))

    @pl.when(t == nt - 1)
    def _():
        put(tile=t, do=lambda cp: cp.wait(), make_copy=copies(buf))

        zero_ref[...] = jnp.zeros_like(zero_ref)

        def zero_copy(row, size):
            return pltpu.make_async_copy(zero_ref.at[pl.ds(0, size), :],
                                         xs_ref.at[pl.ds(pl.multiple_of(row, SUBLANES), size), :],
                                         sem.at[buf])

        def fill(do):
            def expert(e, carry):
                lo = start_ref[e] + cnt_ref[e]
                gap = start_ref[e + 1] - lo
                n_big = gap // MOE_CHUNK

                def big(k, c2):
                    do(zero_copy(lo + k * MOE_CHUNK, MOE_CHUNK))
                    return c2
                lax.fori_loop(0, n_big, big, 0)

                def small(k, c2):
                    do(zero_copy(lo + n_big * MOE_CHUNK + k * SUBLANES, SUBLANES))
                    return c2
                lax.fori_loop(0, (gap - n_big * MOE_CHUNK) // SUBLANES, small, 0)
                return carry
            lax.fori_loop(0, N_EXPERTS, expert, 0)

            def unused(k, carry):
                do(zero_copy(start_ref[N_EXPERTS] + k * MOE_ZERO_ROWS, MOE_ZERO_ROWS))
                return carry
            lax.fori_loop(0, (nblk * MOE_BLK - start_ref[N_EXPERTS]) // MOE_ZERO_ROWS, unused, 0)

        fill(lambda cp: cp.start())
        fill(lambda cp: cp.wait())


def _dispatch(meta, counts, pos_rows, h_flat, nblk):
    ntok, d = h_flat.shape
    tm = TOK_BLK
    whole_smem = pl.BlockSpec(memory_space=pltpu.SMEM)
    return pl.pallas_call(
        functools.partial(_dispatch_kernel, nblk=nblk),
        grid=(ntok // tm,),
        in_specs=[whole_smem, whole_smem,
                  pl.BlockSpec((SUBLANES, tm), lambda i: (0, i)),
                  pl.BlockSpec((tm, d), lambda i: (i, 0))],
        out_specs=[pl.BlockSpec(memory_space=pl.ANY), whole_smem, whole_smem, whole_smem],
        out_shape=[jax.ShapeDtypeStruct((nblk * MOE_BLK, d), F32),
                   jax.ShapeDtypeStruct((nblk,), I32),
                   jax.ShapeDtypeStruct((nblk,), I32),
                   jax.ShapeDtypeStruct((N_EXPERTS + 1,), I32)],
        scratch_shapes=[pltpu.VMEM((2, MOE_SORT_ROWS, d), F32), pltpu.VMEM((MOE_ZERO_ROWS, d), F32),
                        pltpu.SemaphoreType.DMA((2,))],
        compiler_params=_cparams(1), name="dispatch",
    )(meta, counts, pos_rows, h_flat)


def _expert_kernel(blk_e_ref, blk_rows_ref, x_ref, wg_ref, wu_ref, wd_ref, o_ref, g_bf, u_bf, d_bf):
    i = pl.program_id(0)
    rows = blk_rows_ref[i]

    @pl.when((i == 0) | (blk_e_ref[i] != blk_e_ref[jnp.maximum(i - 1, 0)]))
    def _():
        g_bf[...] = wg_ref[0, 0].astype(BF16)
        u_bf[...] = wu_ref[0, 0].astype(BF16)
        d_bf[...] = wd_ref[0, 0].astype(BF16)

    @pl.when(rows > 0)
    def _():
        x = x_ref[...].astype(BF16)
        hid = _silu(_dot(x, g_bf[...])) * _dot(x, u_bf[...])
        o_ref[...] = _dot(hid, d_bf[...])

    @pl.when(rows == 0)
    def _():
        o_ref[...] = jnp.zeros_like(o_ref)


def _experts(blk_e, blk_rows, xs, w_gate, w_up, w_down, layer):
    d = xs.shape[1]
    ff = w_gate.shape[3]
    nblk = blk_e.shape[0]
    weight = lambda shape: pl.BlockSpec((1, 1) + shape, lambda i, be, rows: (layer, be[i], 0, 0))
    return pl.pallas_call(
        _expert_kernel,
        grid_spec=pltpu.PrefetchScalarGridSpec(
            num_scalar_prefetch=2, grid=(nblk,),
            in_specs=[pl.BlockSpec((MOE_BLK, d), lambda i, be, rows: (i, 0)),
                      weight((d, ff)), weight((d, ff)), weight((ff, d))],
            out_specs=pl.BlockSpec((MOE_BLK, d), lambda i, be, rows: (i, 0)),
            scratch_shapes=[pltpu.VMEM((d, ff), BF16), pltpu.VMEM((d, ff), BF16),
                            pltpu.VMEM((ff, d), BF16)]),
        out_shape=jax.ShapeDtypeStruct(xs.shape, F32),
        compiler_params=_cparams(1), name="experts",
    )(blk_e, blk_rows, xs, w_gate, w_up, w_down)


def _combine_kernel(meta_ref, start_ref, x_ref, pos_ref, wt_ref, mod_ref, ys_ref, o_ref,
                    gath_ref, sem, *, n_ctx, ctx_row):
    b, i = pl.program_id(0), pl.program_id(1)
    nt = pl.num_programs(1)
    t = b * nt + i
    last = pl.num_programs(0) * nt - 1
    tm, d = x_ref.shape[1:]
    depth = gath_ref.shape[0]
    buf = t % depth

    def copies(which):
        def make_copy(slot_row, ys_row):
            return pltpu.make_async_copy(ys_ref.at[pl.ds(ys_row, MOE_CHUNK), :],
                                         gath_ref.at[which, pl.ds(slot_row, MOE_CHUNK), :],
                                         sem.at[which])
        return make_copy

    get = functools.partial(_chunk_copies, meta_ref, start_ref, slot_field=3, chunk=MOE_CHUNK)

    @pl.when(t == 0)
    def _():
        gath_ref[...] = jnp.zeros_like(gath_ref)
        for ahead in range(depth - 1):
            @pl.when(ahead <= last)
            def _():
                get(tile=ahead, do=lambda cp: cp.start(), make_copy=copies(ahead))

    @pl.when(t + depth - 1 <= last)
    def _():
        get(tile=t + depth - 1, do=lambda cp: cp.start(), make_copy=copies((t + depth - 1) % depth))

    get(tile=t, do=lambda cp: cp.wait(), make_copy=copies(buf))

    col = _iota((tm, MOE_GATHER_ROWS), 1)
    pos = pos_ref[...]
    wt = wt_ref[...]
    sel = (jnp.where(col == pos[:, 0:1], wt[:, 0:1], 0.0)
           + jnp.where(col == pos[:, 1:2], wt[:, 1:2], 0.0))
    hi, lo = _hi_lo(sel)
    rows = gath_ref[buf].astype(BF16)
    y = (jnp.dot(hi, rows, preferred_element_type=F32)
         + jnp.dot(lo, rows, preferred_element_type=F32))
    gate = _mod_rows(mod_ref, 5, b, i * tm, n_ctx, ctx_row, tm, d)
    o_ref[0] = x_ref[0] + gate * y


def _combine(meta, start, x1, pos_cols, wt_cols, mod_l, ys, n_ctx, ctx_row, latent_only):
    bsz, n, d = x1.shape
    tm = TOK_BLK
    nt = n // tm
    whole_smem = pl.BlockSpec(memory_space=pltpu.SMEM)
    cols = pl.BlockSpec((tm, 2), lambda b, i: (b * nt + i, 0))
    if latent_only:
        assert n_ctx == tm
        out_spec = pl.BlockSpec((1, tm, d), lambda b, i: (b, jnp.maximum(i - 1, 0), 0))
        out_rows = n - n_ctx
    else:
        out_spec = pl.BlockSpec((1, tm, d), lambda b, i: (b, i, 0))
        out_rows = n
    return pl.pallas_call(
        functools.partial(_combine_kernel, n_ctx=n_ctx, ctx_row=ctx_row),
        grid=(bsz, nt),
        in_specs=[whole_smem, whole_smem,
                  pl.BlockSpec((1, tm, d), lambda b, i: (b, i, 0)), cols, cols,
                  pl.BlockSpec(mod_l.shape, lambda b, i: (0, 0)),
                  pl.BlockSpec(memory_space=pl.ANY)],
        out_specs=out_spec,
        out_shape=jax.ShapeDtypeStruct((bsz, out_rows, d), F32),
        scratch_shapes=[pltpu.VMEM((MOE_GATHER_DEPTH, MOE_GATHER_ROWS, d), F32),
                        pltpu.SemaphoreType.DMA((MOE_GATHER_DEPTH,))],
        compiler_params=_cparams(2), name="combine",
    )(meta, start, x1, pos_cols, wt_cols, mod_l, ys)


def _layer(xa, mod_l, layer, n_ctx, ctx_row, p, rope):
    bsz, n, d = xa.shape
    na_w = NA_HEADS * HEAD_DIM
    hg_w = HG_HEADS * HG_DK
    wa_qw = WA_HEADS * HEAD_DIM
    wa_kvw = WA_KV_HEADS * HEAD_DIM
    hg0 = 3 * na_w
    wa0 = hg0 + 5 * hg_w
    gate0 = wa0 + wa_qw + 2 * wa_kvw
    n_cols = gate0 + N_BRANCHES * d
    assert p["w_in"].shape[2] == n_cols
    w_att = _take_cols(p["w_in"], layer, [(gate0, n_cols), (0, hg0), (wa0, gate0)])
    w_hg_qig = _take_cols(p["w_in"], layer, [(hg0, hg0 + hg_w), (hg0 + 3 * hg_w, wa0)])
    w_hg_f = _take_cols(p["w_in"], layer, [(hg0 + hg_w, hg0 + 3 * hg_w)])
    col_gate = 0
    col_na = [N_BRANCHES * d + k * na_w for k in range(3)]
    col_wq = N_BRANCHES * d + 3 * na_w
    col_wk = col_wq + wa_qw
    col_wv = col_wk + wa_kvw
    proj, h = _inproj(xa, mod_l, p["norm1"], w_att, n_ctx, ctx_row, 1792, BF16)
    hg_qig = _project(h, w_hg_qig, 768, BF16)
    hg_f = _project(h, w_hg_f, 1024, F32)

    ya = _na_attention(proj, _na_bias(p["na_rpb"]), p["na_q_norm"], p["na_k_norm"], *col_na)
    yc = _wa_attention(proj, p["wa_sink"], p["wa_q_norm"], p["wa_k_norm"], rope[0], rope[1],
                       col_wq, col_wk, col_wv)
    hg_q, hg_i, hg_g = ((hg_qig, k * hg_w) for k in range(3))
    o_f = _hgrn_pass(p["hg_lower"], layer, False, hg_q, (hg_f, 0), hg_i)
    yb = _hgrn_pass(p["hg_lower"], layer, True, hg_q, (hg_f, hg_w), hg_i,
                    final_args=(hg_g, o_f, p["hg_norm"]))

    bf = lambda w: w.astype(BF16)
    x1, h2, idx, wts, counts, meta = _merge(
        xa, ya, yb, yc, proj, mod_l, bf(p["w_pa"]), bf(p["w_pb"]), bf(p["w_pc"]), bf(p["w_out"]),
        col_gate, p["norm2"], p["w_router"].T, p["b_router"], n_ctx, ctx_row)
    ntok = bsz * n
    meta_flat = meta[:, :, :META_FIELDS].reshape(-1)
    xs, blk_e, blk_rows, start = _dispatch(meta_flat, counts[:, 0].astype(I32), idx,
                                           h2.reshape(ntok, d),
                                           _moe_blocks(2 * ntok, ntok // TOK_BLK))
    ys = _experts(blk_e, blk_rows, xs, p["w_gate"], p["w_up"], p["w_down"], layer)
    return _combine(meta_flat, start, x1, idx[2:4].T, wts[:2].T, mod_l, ys, n_ctx, ctx_row,
                    latent_only=p["last"])


def kernel(x, c, ctx, c_ctx, w_ada, b_ada, norm1, norm2, w_in, na_q_norm, na_k_norm, na_rpb, hg_lower,
           hg_norm, wa_q_norm, wa_k_norm, wa_sink, w_pa, w_pb, w_pc, w_out, w_router, b_router,
           w_gate, w_up, w_down):
    bsz, t, d = x.shape
    n_ctx = ctx.shape[1]
    depth = w_ada.shape[0]
    assert n_ctx == TOK_BLK and t % TOK_BLK == 0 and t // TOK_BLK >= 3
    assert t % GRID_W == 0 and bsz + 1 <= SUBLANES

    xa = jnp.concatenate([ctx, x], axis=1)
    cond = jnp.concatenate([c, c_ctx[None], jnp.zeros((SUBLANES - bsz - 1, d), F32)], axis=0)
    mod = _ada(cond, w_ada, b_ada)
    rope = _rope_tables(n_ctx, t)
    for l in range(depth):
        p = dict(norm1=norm1[l], norm2=norm2[l], w_in=w_in, na_q_norm=na_q_norm[l],
                 na_k_norm=na_k_norm[l], na_rpb=na_rpb[l], hg_lower=hg_lower, hg_norm=hg_norm[l],
                 wa_q_norm=wa_q_norm[l], wa_k_norm=wa_k_norm[l], wa_sink=wa_sink[l],
                 w_pa=w_pa[l], w_pb=w_pb[l], w_pc=w_pc[l], w_out=w_out[l],
                 w_router=w_router, b_router=b_router, w_gate=w_gate, w_up=w_up,
                 w_down=w_down, last=l == depth - 1)
        xa = _layer(xa, mod[l], l, n_ctx, bsz, p, rope)
    return xa
```

```python
import functools

import numpy as np
import jax
import jax.numpy as jnp
from jax import lax
from jax.experimental import pallas as pl
from jax.experimental.pallas import tpu as pltpu

F32 = jnp.float32
BF16 = jnp.bfloat16
I32 = jnp.int32

GRID_W = 64
NA_HEADS = 8
NA_ROWS = 8
NA_COLS = 16
HEAD_DIM = 64
HG_HEADS = 4
HG_DK = 128
LOG_FLOOR = 1e-30
WA_HEADS = 8
WA_KV_HEADS = 2
WA_WINDOW = 128
ROPE_THETA = 10000.0
N_EXPERTS = 16
N_GROUPS = 4
EXPERTS_PER_GROUP = N_EXPERTS // N_GROUPS
N_BRANCHES = 3
NEG_INF = -1e30
RMS_EPS = 1e-6

LANES = 128
SUBLANES = 8
VMEM_LIMIT = 56 * 1024 * 1024

TOK_BLK = 256
MOE_BLK = 1024
MOE_CHUNK = 32
MOE_PUT_CHUNK = 32
MOE_GATHER_DEPTH = 2
MOE_ZERO_ROWS = 256


def _cparams(n_axes):
    return pltpu.CompilerParams(
        dimension_semantics=("arbitrary",) * n_axes, vmem_limit_bytes=VMEM_LIMIT)


def _sigmoid(x):
    return 1.0 / (1.0 + jnp.exp(-x))


def _silu(x):
    return x * _sigmoid(x)


def _dot(a, b):
    return jnp.dot(a.astype(BF16), b.astype(BF16), preferred_element_type=F32)


def _dot_nt(a, b):
    return lax.dot_general(a.astype(BF16), b.astype(BF16), (((1,), (1,)), ((), ())),
                           preferred_element_type=F32)


def _dot_tn(a, b):
    return lax.dot_general(a.astype(BF16), b.astype(BF16), (((0,), (0,)), ((), ())),
                           preferred_element_type=F32)


def _hi_lo(a):
    hi = a.astype(BF16)
    lo = (a - hi.astype(F32)).astype(BF16)
    return hi, lo


def _dot3(a, b):
    ah, al = _hi_lo(a)
    bh, bl = _hi_lo(b)
    d = functools.partial(jnp.dot, preferred_element_type=F32)
    return d(ah, bh) + (d(ah, bl) + d(al, bh))


def _dot3_nt(a, b):
    ah, al = _hi_lo(a)
    bh, bl = _hi_lo(b)
    d = functools.partial(lax.dot_general, dimension_numbers=(((1,), (1,)), ((), ())),
                          preferred_element_type=F32)
    return d(ah, bh) + (d(ah, bl) + d(al, bh))


def _rms(x, w):
    return x * lax.rsqrt(jnp.mean(x * x, axis=-1, keepdims=True) + RMS_EPS) * w


def _iota(shape, dim):
    return lax.broadcasted_iota(I32, shape, dim)


def _ada_kernel(cond_ref, w_ref, b_ref, o_ref):
    o_ref[0] = _dot3(_silu(cond_ref[...]), w_ref[0]) + b_ref[0]


def _ada(cond, w_ada, b_ada):
    depth, d, d6 = w_ada.shape
    rows = cond.shape[0]
    tn = d6 // 4
    return pl.pallas_call(
        _ada_kernel,
        grid=(depth, d6 // tn),
        in_specs=[pl.BlockSpec((rows, d), lambda l, j: (0, 0)),
                  pl.BlockSpec((1, d, tn), lambda l, j: (l, 0, j)),
                  pl.BlockSpec((1, 1, tn), lambda l, j: (l, 0, j))],
        out_specs=pl.BlockSpec((1, rows, tn), lambda l, j: (l, 0, j)),
        out_shape=jax.ShapeDtypeStruct((depth, rows, d6), F32),
        compiler_params=_cparams(2), name="ada",
    )(cond, w_ada, b_ada.reshape(depth, 1, d6))


def _mod_rows(mod_ref, which, b, row0, n_ctx, ctx_row, tm, d):
    lat = mod_ref[pl.ds(b, 1), which * d:(which + 1) * d]
    cx = mod_ref[ctx_row:ctx_row + 1, which * d:(which + 1) * d]
    row = row0 + _iota((tm, d), 0)
    return jnp.where(row < n_ctx, cx, lat)


def _inproj_kernel(x_ref, mod_ref, nw_ref, w_ref, o_ref, h_ref, *, n_ctx, ctx_row):
    b, i, j = pl.program_id(0), pl.program_id(1), pl.program_id(2)
    tm, d = h_ref.shape[1:]

    @pl.when(j == 0)
    def _():
        h = _rms(x_ref[0], nw_ref[...])
        shift = _mod_rows(mod_ref, 0, b, i * tm, n_ctx, ctx_row, tm, d)
        scale = _mod_rows(mod_ref, 1, b, i * tm, n_ctx, ctx_row, tm, d)
        h_ref[0] = (h * (1.0 + scale) + shift).astype(BF16)

    o_ref[0] = jnp.dot(h_ref[0], w_ref[...], preferred_element_type=F32).astype(o_ref.dtype)


def _matmul_kernel(h_ref, w_ref, o_ref):
    o_ref[0] = jnp.dot(h_ref[0], w_ref[...], preferred_element_type=F32).astype(o_ref.dtype)


def _project(h, w, tn, out_dtype):
    bsz, n, d = h.shape
    cols = w.shape[1]
    tm = 1408 if n % 1408 == 0 else TOK_BLK
    return pl.pallas_call(
        _matmul_kernel,
        grid=(bsz, n // tm, cols // tn),
        in_specs=[pl.BlockSpec((1, tm, d), lambda b, i, j: (b, i, 0)),
                  pl.BlockSpec((d, tn), lambda b, i, j: (0, j))],
        out_specs=pl.BlockSpec((1, tm, tn), lambda b, i, j: (b, i, j)),
        out_shape=jax.ShapeDtypeStruct((bsz, n, cols), out_dtype),
        compiler_params=_cparams(3), name="project",
    )(h, w)


W_IN_COL_BLK = 256


def _take_cols_kernel(src_ref, w_ref, o_ref):
    del src_ref
    o_ref[...] = w_ref[0].astype(BF16)


def _take_cols(w_in, layer, col_ranges):
    d = w_in.shape[1]
    blk = W_IN_COL_BLK
    src = []
    for start, stop in col_ranges:
        assert start % blk == 0 and stop % blk == 0
        src += list(range(start // blk, stop // blk))
    return pl.pallas_call(
        _take_cols_kernel,
        grid_spec=pltpu.PrefetchScalarGridSpec(
            num_scalar_prefetch=1, grid=(len(src),),
            in_specs=[pl.BlockSpec((1, d, blk), lambda j, src: (layer, 0, src[j]))],
            out_specs=pl.BlockSpec((d, blk), lambda j, src: (0, j))),
        out_shape=jax.ShapeDtypeStruct((d, len(src) * blk), BF16),
        compiler_params=_cparams(1), name="take_cols",
    )(jnp.asarray(src, I32), w_in)


def _inproj(xa, mod_l, norm_w, w, n_ctx, ctx_row, tn, out_dtype):
    bsz, n, d = xa.shape
    cols = w.shape[1]
    tm = 1408 if n % 1408 == 0 else TOK_BLK
    return pl.pallas_call(
        functools.partial(_inproj_kernel, n_ctx=n_ctx, ctx_row=ctx_row),
        grid=(bsz, n // tm, cols // tn),
        in_specs=[pl.BlockSpec((1, tm, d), lambda b, i, j: (b, i, 0)),
                  pl.BlockSpec(mod_l.shape, lambda b, i, j: (0, 0)),
                  pl.BlockSpec((1, d), lambda b, i, j: (0, 0)),
                  pl.BlockSpec((d, tn), lambda b, i, j: (0, j))],
        out_specs=[pl.BlockSpec((1, tm, tn), lambda b, i, j: (b, i, j)),
                   pl.BlockSpec((1, tm, d), lambda b, i, j: (b, i, 0))],
        out_shape=[jax.ShapeDtypeStruct((bsz, n, cols), out_dtype),
                   jax.ShapeDtypeStruct((bsz, n, d), BF16)],
        compiler_params=_cparams(3), name="inproj",
    )(xa, mod_l, norm_w.reshape(1, d), w)


def _pair_norm(x, w):
    lane = _iota(x.shape, 1)
    lo = lane < HEAD_DIM
    sq = x * x
    s_lo = jnp.sum(jnp.where(lo, sq, 0.0), axis=-1, keepdims=True)
    s_hi = jnp.sum(jnp.where(lo, 0.0, sq), axis=-1, keepdims=True)
    ms = jnp.where(lo, s_lo, s_hi) * (1.0 / HEAD_DIM)
    return x * lax.rsqrt(ms + RMS_EPS) * w


LOG2E = 1.4426950408889634


def _ones_beside(v, value_lanes):
    return jnp.where(value_lanes, v, 1.0).astype(BF16)


def _softmax_av(s_loc, s_ctx, v_loc, v_ctx, sink):
    m = jnp.maximum(jnp.max(s_loc, axis=-1, keepdims=True), jnp.max(s_ctx, axis=-1, keepdims=True))
    if sink is not None:
        m = jnp.maximum(m, sink)
    p_loc = jnp.exp2(s_loc - m).astype(BF16)
    p_ctx = jnp.exp2(s_ctx - m).astype(BF16)
    acc = (jnp.dot(p_loc, v_loc, preferred_element_type=F32)
           + jnp.dot(p_ctx, v_ctx, preferred_element_type=F32))
    den = pltpu.roll(acc, LANES // 2, 1)
    if sink is not None:
        den = den + jnp.exp2(sink - m)
    return acc / den


NA_QROWS = TOK_BLK // GRID_W
NA_KROWS = 3 * NA_QROWS
N_RPB_COLS = 2 * NA_COLS - 1
N_RPB_ROWS = 2 * NA_ROWS - 1
NA_VARIANTS = 4


def _na_row_valid(variant, a, k):
    if variant == 0:
        return False
    if variant == 1:
        return NA_QROWS <= k < NA_QROWS + NA_ROWS
    if variant == 3:
        return k < NA_ROWS
    return a <= k < a + NA_ROWS


def _na_bias_kernel(rpb_ref, o_ref):
    h = pl.program_id(0)
    qc = _iota((GRID_W, GRID_W), 0)
    kc = _iota((GRID_W, GRID_W), 1)
    col_lo = jnp.clip(qc - NA_COLS // 2, 0, GRID_W - NA_COLS)
    col_ok = (kc >= col_lo) & (kc < col_lo + NA_COLS)
    d_col = kc - qc + (NA_COLS - 1)
    masked = jnp.full((GRID_W, GRID_W), NEG_INF, F32)
    blocks = []
    for dr in range(N_RPB_ROWS):
        acc = jnp.zeros((GRID_W, GRID_W), F32)
        for dc in range(N_RPB_COLS):
            val = rpb_ref[(h * N_RPB_ROWS + dr) * N_RPB_COLS + dc]
            acc = jnp.where(d_col == dc, val, acc)
        blocks.append(jnp.where(col_ok, acc * LOG2E, NEG_INF))
    for variant in range(NA_VARIANTS):
        for a in range(NA_QROWS):
            for k in range(NA_KROWS):
                dr = k - NA_QROWS - a + NA_ROWS - 1
                ok = _na_row_valid(variant, a, k) and 0 <= dr < N_RPB_ROWS
                o_ref[variant, 0, a * GRID_W:(a + 1) * GRID_W, k * GRID_W:(k + 1) * GRID_W] = (
                    blocks[dr] if ok else masked)


def _na_bias(rpb):
    heads = rpb.shape[0]
    return pl.pallas_call(
        _na_bias_kernel,
        grid=(heads,),
        in_specs=[pl.BlockSpec(memory_space=pltpu.SMEM)],
        out_specs=pl.BlockSpec((NA_VARIANTS, 1, TOK_BLK, 3 * TOK_BLK), lambda h: (0, h, 0, 0)),
        out_shape=jax.ShapeDtypeStruct((NA_VARIANTS, heads, TOK_BLK, 3 * TOK_BLK), F32),
        compiler_params=_cparams(1), name="na_bias",
    )(rpb.reshape(-1))


def _na_kernel(q_ref, k0_ref, kn_ref, v0_ref, vn_ref,
               bias_ref, qn_ref, kn_w_ref, o_ref, ring_ref, kctx_ref, vring_ref, vctx_ref):
    i = pl.program_id(0)
    tq = q_ref.shape[1]
    lo = _iota((tq, LANES), 1) < HEAD_DIM
    qn_w = qn_ref[...] * (HEAD_DIM ** -0.5 * LOG2E)
    units = [(b, p) for b in range(q_ref.shape[0]) for p in range(NA_HEADS // 2)]
    pair_of = lambda p: slice(p * LANES, (p + 1) * LANES)

    def normed(ref, b, p):
        return _pair_norm(ref[b, :, pair_of(p)].astype(F32), kn_w_ref[...]).astype(BF16)

    def values(ref, b, p, h):
        return _ones_beside(ref[b, :, pair_of(p)].astype(F32), lo if h == 0 else ~lo)

    @pl.when(i == 0)
    def _():
        for b, p in units:
            pair = pair_of(p)
            k0 = normed(k0_ref, b, p)
            kctx_ref[b, :, pair] = k0
            ring_ref[0, b, :, pair] = k0
            ring_ref[2, b, :, pair] = k0
            for h in range(2):
                v0 = values(v0_ref, b, p, h)
                vctx_ref[h, b, :, pair] = v0
                vring_ref[h, 0, b, :, pair] = v0
                vring_ref[h, 2, b, :, pair] = v0

    nxt = (i + 1) % 3
    for b, p in units:
        ring_ref[nxt, b, :, pair_of(p)] = normed(kn_ref, b, p)
        for h in range(2):
            vring_ref[h, nxt, b, :, pair_of(p)] = values(vn_ref, b, p, h)
    prv, cur = (i + 2) % 3, i % 3

    for b, p in units:
        pair = pair_of(p)
        f32 = lambda ref: ref[b, :, pair].astype(F32)
        q = _pair_norm(f32(q_ref), qn_w)
        k_loc = jnp.concatenate([ring_ref[prv, b, :, pair], ring_ref[cur, b, :, pair],
                                 ring_ref[nxt, b, :, pair]], axis=0)
        k_ctx = kctx_ref[b, :, pair]
        q2 = jnp.concatenate([jnp.where(lo, q, 0.0), jnp.where(lo, 0.0, q)], axis=0).astype(BF16)
        s_loc2 = _dot_nt(q2, k_loc)
        s_ctx2 = _dot_nt(q2, k_ctx)
        outs = []
        for h in range(2):
            rows = slice(h * tq, (h + 1) * tq)
            v_loc = jnp.concatenate([vring_ref[h, prv, b, :, pair], vring_ref[h, cur, b, :, pair],
                                     vring_ref[h, nxt, b, :, pair]], axis=0)
            outs.append(_softmax_av(s_loc2[rows] + bias_ref[0, 2 * p + h], s_ctx2[rows],
                                    v_loc, vctx_ref[h, b, :, pair], None))
        o_ref[b, :, pair] = jnp.where(lo, outs[0], outs[1]).astype(o_ref.dtype)


def _na_attention(proj, bias, qn_w, kn_w, col_q, col_k, col_v):
    bsz, n, _ = proj.shape
    nblk = n // TOK_BLK
    last = nblk - 1
    width = NA_HEADS * HEAD_DIM
    cq, ck, cv = col_q // width, col_k // width, col_v // width

    def blk(col, shift):
        return pl.BlockSpec((bsz, TOK_BLK, width),
                            lambda i: (0, jnp.clip(i + shift, 0, last), col))

    def ctx_blk(col):
        return pl.BlockSpec((bsz, TOK_BLK, width), lambda i: (0, 0, col))

    def variant(i):
        return (jnp.where(i == 0, 0, jnp.where(i == 1, 1, jnp.where(i == last, 3, 2))), 0, 0, 0)

    vec = pl.BlockSpec((1, LANES), lambda i: (0, 0))
    return pl.pallas_call(
        _na_kernel,
        grid=(nblk,),
        in_specs=[blk(cq, 0), ctx_blk(ck), blk(ck, 1), ctx_blk(cv), blk(cv, 1),
                  pl.BlockSpec((1, NA_HEADS, TOK_BLK, 3 * TOK_BLK), variant), vec, vec],
        out_specs=pl.BlockSpec((bsz, TOK_BLK, width), lambda i: (0, i, 0)),
        out_shape=jax.ShapeDtypeStruct((bsz, n, width), BF16),
        scratch_shapes=[pltpu.VMEM((3, bsz, TOK_BLK, width), BF16),
                        pltpu.VMEM((bsz, TOK_BLK, width), BF16),
                        pltpu.VMEM((2, 3, bsz, TOK_BLK, width), BF16),
                        pltpu.VMEM((2, bsz, TOK_BLK, width), BF16)],
        compiler_params=_cparams(1), name="na_attn",
    )(proj, proj, proj, proj, proj, bias,
      jnp.tile(qn_w, 2).reshape(1, LANES), jnp.tile(kn_w, 2).reshape(1, LANES))


WA_STACK = 4


def _rope(x, cos, sin_signed):
    lane = _iota(x.shape, 1)
    first = (lane & (HEAD_DIM // 2 - 1)) < HEAD_DIM // 4
    quarter = HEAD_DIM // 4
    partner = jnp.where(first, pltpu.roll(x, LANES - quarter, 1), pltpu.roll(x, quarter, 1))
    return x * cos + partner * sin_signed


def _wa_kernel(sink_ref, q_ref, kp_ref, kc_ref, kn_ref, vp_ref, vc_ref, vn_ref, kx_ref, vx_ref,
               cq_ref, sq_ref, cp_ref, sp_ref, cn_ref, sn_ref, qn_ref, kn_w_ref, o_ref, *, last):
    i = pl.program_id(1)
    tq = q_ref.shape[1]
    tn = kp_ref.shape[1]
    n_loc = tq + 2 * tn
    cos_loc = jnp.concatenate([cp_ref[...], cq_ref[...], cn_ref[...]], axis=0)
    sin_loc = jnp.concatenate([sp_ref[...], sq_ref[...], sn_ref[...]], axis=0)
    qn_w = qn_ref[...] * (HEAD_DIM ** -0.5 * LOG2E)

    qi = _iota((tq, n_loc), 0)
    kj = _iota((tq, n_loc), 1)
    rel = kj - tn - qi
    lo_col = jnp.where(i >= 2, 0, tn)
    hi_col = jnp.where(i >= 1, jnp.where(i < last, n_loc, tn + tq), 0)
    ok = (rel >= -WA_WINDOW) & (rel <= WA_WINDOW) & (kj >= lo_col) & (kj < hi_col)
    mask = jnp.where(ok, 0.0, NEG_INF)

    mask_g = jnp.concatenate([mask] * WA_STACK, axis=0)
    for b in range(q_ref.shape[0]):
        _wa_one_batch(b, sink_ref, q_ref, (kp_ref, kc_ref, kn_ref), (vp_ref, vc_ref, vn_ref),
                      kx_ref, vx_ref, (cos_loc, sin_loc), (cq_ref[...], sq_ref[...]), qn_w,
                      kn_w_ref[...], mask_g, o_ref)


def _wa_one_batch(b, sink_ref, q_ref, k_refs, v_refs, kx_ref, vx_ref, rope_loc, rope_q, qn_w, kn_w,
                  mask_g, o_ref):
    tq = q_ref.shape[1]
    half = LANES // 2
    group = WA_HEADS // WA_KV_HEADS
    f32 = lambda ref: ref[b].astype(F32)
    k_loc = _pair_norm(jnp.concatenate([f32(r) for r in k_refs], axis=0), kn_w)
    k_loc = _rope(k_loc, *rope_loc).astype(BF16)
    k_ctx = _pair_norm(f32(kx_ref), kn_w).astype(BF16)
    v_loc = jnp.concatenate([f32(r) for r in v_refs], axis=0)
    v_ctx = f32(vx_ref)
    lo = _iota((tq, LANES), 1) < HEAD_DIM
    lo_loc = _iota(v_loc.shape, 1) < HEAD_DIM
    q_pairs = [_rope(_pair_norm(q_ref[b, :, p * LANES:(p + 1) * LANES].astype(F32), qn_w), *rope_q)
               for p in range(WA_HEADS // 2)]
    outs = [None] * WA_HEADS
    for first in range(0, WA_HEADS, WA_STACK):
        heads = range(first, first + WA_STACK)
        kv = first // group
        mine, mine_loc = (lo, lo_loc) if kv == 0 else (~lo, ~lo_loc)
        stack = []
        for h in heads:
            q = q_pairs[h // 2]
            q = q if h % 2 == kv else pltpu.roll(q, half, 1)
            stack.append(jnp.where(mine, q, 0.0).astype(BF16))
        q_g = jnp.concatenate(stack, axis=0)
        sink_g = jnp.concatenate([jnp.full((tq, 1), sink_ref[h] * LOG2E, F32) for h in heads], axis=0)
        o_g = _softmax_av(_dot_nt(q_g, k_loc) + mask_g, _dot_nt(q_g, k_ctx),
                          _ones_beside(v_loc, mine_loc), _ones_beside(v_ctx, mine), sink_g)
        for n, h in enumerate(heads):
            o = o_g[n * tq:(n + 1) * tq]
            outs[h] = o if h % 2 == kv else pltpu.roll(o, half, 1)
    for p in range(WA_HEADS // 2):
        o_ref[b, :, p * LANES:(p + 1) * LANES] = jnp.where(
            lo, outs[2 * p], outs[2 * p + 1]).astype(o_ref.dtype)


def _wa_attention(proj, sink, qn_w, kn_w, cos_t, sin_t, col_q, col_k, col_v):
    bsz, n, _ = proj.shape
    nblk = n // TOK_BLK
    last = nblk - 1
    qw = WA_HEADS * HEAD_DIM
    cq, ck, cv = col_q // qw, col_k // LANES, col_v // LANES

    assert TOK_BLK % WA_WINDOW == 0
    per = TOK_BLK // WA_WINDOW

    def near(i, shift):
        if shift == 0:
            return i
        return jnp.clip(per * i - 1 if shift < 0 else per * (i + 1), 0, per * nblk - 1)

    def blk(col, shift):
        rows = TOK_BLK if shift == 0 else WA_WINDOW
        return pl.BlockSpec((1, rows, LANES), lambda b, i: (b, near(i, shift), col))

    def tab(shift):
        rows = TOK_BLK if shift == 0 else WA_WINDOW
        return pl.BlockSpec((rows, LANES), lambda b, i: (near(i, shift), 0))

    def ctx_blk(col):
        return pl.BlockSpec((1, TOK_BLK, LANES), lambda b, i: (b, 0, col))

    vec = pl.BlockSpec((1, LANES), lambda b, i: (0, 0))
    return pl.pallas_call(
        functools.partial(_wa_kernel, last=last),
        grid=(bsz, nblk),
        in_specs=[pl.BlockSpec(memory_space=pltpu.SMEM),
                  pl.BlockSpec((1, TOK_BLK, qw), lambda b, i: (b, i, cq)),
                  blk(ck, -1), blk(ck, 0), blk(ck, 1), blk(cv, -1), blk(cv, 0), blk(cv, 1),
                  ctx_blk(ck), ctx_blk(cv),
                  tab(0), tab(0), tab(-1), tab(-1), tab(1), tab(1), vec, vec],
        out_specs=pl.BlockSpec((1, TOK_BLK, qw), lambda b, i: (b, i, 0)),
        out_shape=jax.ShapeDtypeStruct((bsz, n, qw), BF16),
        compiler_params=_cparams(2), name="wa_attn",
    )(sink, proj, proj, proj, proj, proj, proj, proj, proj, proj,
      cos_t, sin_t, cos_t, sin_t, cos_t, sin_t,
      jnp.tile(qn_w, 2).reshape(1, LANES), jnp.tile(kn_w, 2).reshape(1, LANES))


def _rope_tables(n_ctx, t):
    quarter = HEAD_DIM // 4
    inv_freq = ROPE_THETA ** (-jnp.arange(quarter, dtype=F32) / quarter)
    pos = jnp.arange(t)
    lane = np.arange(LANES)
    in_head = lane % HEAD_DIM
    use_col = in_head >= HEAD_DIM // 2
    second = (in_head % (HEAD_DIM // 2)) >= quarter
    freq = inv_freq[in_head % quarter]
    p = jnp.where(use_col[None, :], (pos % GRID_W)[:, None], (pos // GRID_W)[:, None]).astype(F32)
    ang = p * freq[None, :]
    cos = jnp.cos(ang)
    sin = jnp.where(second[None, :], jnp.sin(ang), -jnp.sin(ang))
    cos = jnp.concatenate([jnp.ones((n_ctx, LANES), F32), cos], axis=0)
    sin = jnp.concatenate([jnp.zeros((n_ctx, LANES), F32), sin], axis=0)
    return cos, sin


HG_CHUNK = 256


def _hgrn_kernel(q_ref, f_ref, v_ref, hl_ref, *rest, layer, reverse, final):
    if final:
        g_ref, prev_ref, nw_ref, o_ref, st_ref = rest
    else:
        o_ref, st_ref = rest
    @pl.when(pl.program_id(0) == 0)
    def _():
        st_ref[...] = jnp.zeros_like(st_ref)

    a = hl_ref[0]
    e = jnp.exp(a - jnp.max(a, axis=0, keepdims=True))
    pr = e / jnp.sum(e, axis=0, keepdims=True)
    lb_all = jnp.zeros((1, a.shape[1]), F32)
    for j in range(1, layer + 1):
        lb_all = lb_all + pr[j:j + 1]

    for b, h in [(b, h) for b in range(q_ref.shape[0]) for h in range(HG_HEADS)]:
        lanes = slice(h * HG_DK, (h + 1) * HG_DK)
        lb = lb_all[:, lanes]
        qs = _silu(q_ref[b, :, lanes].astype(F32))
        f = lb + (1.0 - lb) * _sigmoid(f_ref[b, :, lanes])
        o, st = _hgrn_block(qs, 1.0 - f, jnp.log(jnp.maximum(f, LOG_FLOOR)),
                            v_ref[b, :, lanes].astype(F32), st_ref[b, h], reverse)
        st_ref[b, h] = st
        if final:
            tot = prev_ref[b, :, lanes] + o
            gated = _rms(tot, nw_ref[...]) * _silu(g_ref[b, :, lanes].astype(F32))
            o_ref[b, :, lanes] = gated.astype(o_ref.dtype)
        else:
            o_ref[b, :, lanes] = o


def _hgrn_block(qs, kk, g, v, st, reverse):
    c = qs.shape[0]
    n_sub = c // HG_CHUNK
    row = _iota((c, HG_DK), 0)
    ri = _iota((c, c), 0)
    ci = _iota((c, c), 1)
    block_xor = ri ^ ci

    tri = jnp.where(((ci >= ri) if reverse else (ci <= ri)) & (block_xor < HG_CHUNK), 1.0, 0.0)
    tri = tri.astype(BF16)
    g_hi = g.astype(BF16)
    rest = g - g_hi.astype(F32)
    g_mid = rest.astype(BF16)
    g_lo = (rest - g_mid.astype(F32)).astype(BF16)
    tdot = functools.partial(jnp.dot, preferred_element_type=F32)
    b = tdot(tri, g_hi) + (tdot(tri, g_mid) + tdot(tri, g_lo))

    def ref_rows(hs):
        blk = 2 * hs
        inner = hs if reverse else hs - 1
        if blk >= SUBLANES:
            x = b.reshape(c // blk, blk, HG_DK)
            return jnp.broadcast_to(x[:, inner:inner + 1, :], x.shape).reshape(c, HG_DK)
        x = b.reshape(c // SUBLANES, SUBLANES, HG_DK)
        sub = _iota(x.shape, 1)
        out = None
        for p in range(SUBLANES // blk):
            r = p * blk + inner
            piece = jnp.broadcast_to(x[:, r:r + 1, :], x.shape)
            out = piece if out is None else jnp.where(sub >= p * blk, piece, out)
        return out.reshape(c, HG_DK)

    att = jnp.zeros((c, c), F32)
    hs = HG_CHUNK // 2
    while hs >= 1:
        blk = 2 * hs
        decay = jnp.exp(-jnp.abs(b - ref_rows(hs)))
        in_block = row & (blk - 1)
        later_half = (in_block < hs) if reverse else (in_block >= hs)
        q_l = jnp.where(later_half, qs * decay, 0.0)
        k_l = jnp.where(later_half, 0.0, kk * decay)
        a_l = _dot_nt(q_l, k_l)
        att = att + (a_l if blk == c else jnp.where(block_xor < blk, a_l, 0.0))
        hs //= 2
    o = _dot(att, v) + jnp.sum(qs * kk, axis=-1, keepdims=True) * v

    q_dec = qs * jnp.exp(b)
    b3 = b.reshape(n_sub, HG_CHUNK, HG_DK)
    end = 0 if reverse else HG_CHUNK - 1
    b_end = jnp.broadcast_to(b3[:, end:end + 1, :], b3.shape).reshape(c, HG_DK)
    k_dec = kk * jnp.exp(b_end - b)
    outs = [None] * n_sub
    for s in (reversed(range(n_sub)) if reverse else range(n_sub)):
        rows = slice(s * HG_CHUNK, (s + 1) * HG_CHUNK)
        outs[s] = o[rows] + _dot_nt(q_dec[rows], st)
        st = st * jnp.exp(b_end[s * HG_CHUNK:s * HG_CHUNK + 1, :]) + _dot_tn(v[rows], k_dec[rows])
    return jnp.concatenate(outs, axis=0), st


def _hgrn_pass(hl, layer, reverse, q_src, f_src, v_src, final_args=None):
    bsz, n, _ = q_src[0].shape
    nchunk = n // TOK_BLK
    width = HG_HEADS * HG_DK

    def chunk(t):
        return jnp.where(t == 0, 0, nchunk - t) if reverse else t

    def blk(col):
        return pl.BlockSpec((bsz, TOK_BLK, width), lambda t: (0, chunk(t), col // width))

    in_specs = [blk(q_src[1]), blk(f_src[1]), blk(v_src[1]),
                pl.BlockSpec((1,) + hl.shape[1:], lambda t: (1 if reverse else 0, 0, 0))]
    args = [q_src[0], f_src[0], v_src[0], hl]
    final = final_args is not None
    if final:
        g_src, prev, norm_w = final_args
        in_specs += [blk(g_src[1]), blk(0), pl.BlockSpec((1, HG_DK), lambda t: (0, 0))]
        args += [g_src[0], prev, norm_w.reshape(1, HG_DK)]
    return pl.pallas_call(
        functools.partial(_hgrn_kernel, layer=layer, reverse=reverse, final=final),
        grid=(nchunk,),
        in_specs=in_specs,
        out_specs=blk(0),
        out_shape=jax.ShapeDtypeStruct((bsz, n, width), BF16 if final else F32),
        scratch_shapes=[pltpu.VMEM((bsz, HG_HEADS, HG_DK, HG_DK), F32)],
        compiler_params=_cparams(1), name="hgrn_bwd" if reverse else "hgrn_fwd",
    )(*args)


def _merge_kernel(x_ref, ya_ref, yb_ref, yc_ref, ga_ref, gb_ref, gc_ref, mod_ref,
                  wa_ref, wb_ref, wc_ref, wo_ref, nw_ref, wr_ref, br_ref,
                  o_ref, h_ref, idx_ref, wt_ref, cnt_ref, meta_ref, carry_ref, *, n_ctx, ctx_row):
    b, i = pl.program_id(0), pl.program_id(1)
    tm, d = x_ref.shape[1:]

    @pl.when((b == 0) & (i == 0))
    def _():
        carry_ref[...] = jnp.zeros_like(carry_ref)

    gate_of = lambda ref: _sigmoid(ref[0].astype(F32))
    merged = (gate_of(ga_ref) * _dot(ya_ref[0], wa_ref[...])
              + gate_of(gb_ref) * _dot(yb_ref[0], wb_ref[...])
              + gate_of(gc_ref) * _dot(yc_ref[0], wc_ref[...]))
    mix = _dot(merged, wo_ref[...])
    gate = _mod_rows(mod_ref, 2, b, i * tm, n_ctx, ctx_row, tm, d)
    x1 = x_ref[0] + gate * mix
    o_ref[0] = x1

    for s in range(tm // TOK_BLK):
        rows = pl.ds(s * TOK_BLK, TOK_BLK)
        _route_tile(x1[s * TOK_BLK:(s + 1) * TOK_BLK], b, i * tm + s * TOK_BLK, mod_ref, nw_ref,
                    wr_ref, br_ref, h_ref.at[0, rows, :], idx_ref.at[:, rows], wt_ref.at[:, rows],
                    cnt_ref, meta_ref.at[s], carry_ref, n_ctx, ctx_row)


def _merge(xa, ya, yb, yc, proj, mod_l, w_pa, w_pb, w_pc, w_out, col_gate, norm_w, w_router_t,
           b_router, n_ctx, ctx_row):
    bsz, n, d = xa.shape
    tm = 3 * TOK_BLK if n % (3 * TOK_BLK) == 0 else TOK_BLK
    nt = n // tm
    per = tm // TOK_BLK
    g0 = col_gate // d
    tile = lambda w: pl.BlockSpec((1, tm, w), lambda b, i: (b, i, 0))
    gate = lambda k: pl.BlockSpec((1, tm, d), lambda b, i: (b, i, g0 + k))
    full = lambda a: pl.BlockSpec(a.shape, lambda b, i: (0,) * a.ndim)
    lane_tile = pl.BlockSpec((SUBLANES, tm), lambda b, i: (0, b * nt + i))
    br = b_router.reshape(N_EXPERTS, 1)
    nw = norm_w.reshape(1, d)
    return pl.pallas_call(
        functools.partial(_merge_kernel, n_ctx=n_ctx, ctx_row=ctx_row),
        grid=(bsz, nt),
        in_specs=[tile(d), tile(ya.shape[2]), tile(yb.shape[2]), tile(yc.shape[2]),
                  gate(0), gate(1), gate(2), full(mod_l),
                  full(w_pa), full(w_pb), full(w_pc), full(w_out),
                  full(nw), full(w_router_t), full(br)],
        out_specs=[tile(d), tile(d), lane_tile, lane_tile,
                   pl.BlockSpec((N_EXPERTS, LANES), lambda b, i: (0, 0)),
                   pl.BlockSpec((per, N_EXPERTS, LANES), lambda b, i: (b * nt + i, 0, 0))],
        out_shape=[jax.ShapeDtypeStruct((bsz, n, d), F32),
                   jax.ShapeDtypeStruct((bsz, n, d), BF16),
                   jax.ShapeDtypeStruct((SUBLANES, bsz * n), I32),
                   jax.ShapeDtypeStruct((SUBLANES, bsz * n), F32),
                   jax.ShapeDtypeStruct((N_EXPERTS, LANES), F32),
                   jax.ShapeDtypeStruct((bsz * nt * per, N_EXPERTS, LANES), I32)],
        scratch_shapes=[pltpu.VMEM((N_EXPERTS, LANES), F32)],
        compiler_params=_cparams(2), name="merge",
    )(xa, ya, yb, yc, proj, proj, proj, mod_l, w_pa, w_pb, w_pc, w_out, nw, w_router_t, br)


def _route_tile(x, b, row0, mod_ref, nw_ref, wr_ref, br_ref, h_ref, idx_ref, wt_ref, cnt_ref,
                meta_ref, carry_ref, n_ctx, ctx_row):
    tm, d = x.shape
    h = _rms(x, nw_ref[...])
    shift = _mod_rows(mod_ref, 3, b, row0, n_ctx, ctx_row, tm, d)
    scale = _mod_rows(mod_ref, 4, b, row0, n_ctx, ctx_row, tm, d)
    h = h * (1.0 + scale) + shift
    h_ref[...] = h.astype(h_ref.dtype)

    logits = _dot3_nt(wr_ref[...], h)
    ex = jnp.exp(logits - jnp.max(logits, axis=0, keepdims=True))
    probs = ex / jnp.sum(ex, axis=0, keepdims=True)
    sel = probs + br_ref[...]

    def row(x, r):
        return x[r:r + 1, :]

    best = None
    g_idx = None
    for g in range(N_GROUPS):
        r0 = g * EXPERTS_PER_GROUP
        a0, a1, a2, a3 = (row(sel, r0 + j) for j in range(EXPERTS_PER_GROUP))
        hi1, lo1 = jnp.maximum(a0, a1), jnp.minimum(a0, a1)
        hi2, lo2 = jnp.maximum(a2, a3), jnp.minimum(a2, a3)
        score = jnp.maximum(hi1, hi2) + jnp.maximum(jnp.minimum(hi1, hi2), jnp.maximum(lo1, lo2))
        if g == 0:
            best, g_idx = score, jnp.zeros_like(score, dtype=I32)
        else:
            better = score > best
            best = jnp.where(better, score, best)
            g_idx = jnp.where(better, g, g_idx)

    def pick(x, j):
        out = row(x, j)
        for g in range(1, N_GROUPS):
            out = jnp.where(g_idx == g, row(x, g * EXPERTS_PER_GROUP + j), out)
        return out

    in_grp = [pick(sel, j) for j in range(EXPERTS_PER_GROUP)]
    in_prob = [pick(probs, j) for j in range(EXPERTS_PER_GROUP)]

    def first_argmax(vals, skip):
        bv, bi, bp = None, None, None
        for j in range(EXPERTS_PER_GROUP):
            v = vals[j] if skip is None else jnp.where(skip == j, -jnp.inf, vals[j])
            if bv is None:
                bv, bi, bp = v, jnp.zeros_like(g_idx), in_prob[0]
            else:
                better = v > bv
                bv = jnp.where(better, v, bv)
                bi = jnp.where(better, j, bi)
                bp = jnp.where(better, in_prob[j], bp)
        return bi, bp

    loc0, p0 = first_argmax(in_grp, None)
    loc1, p1 = first_argmax(in_grp, loc0)
    e0 = g_idx * EXPERTS_PER_GROUP + loc0
    e1 = g_idx * EXPERTS_PER_GROUP + loc1
    wsum = p0 + p1
    w0, w1 = p0 / wsum, p1 / wsum

    er = _iota((N_EXPERTS, tm), 0)
    hit0 = er == e0
    hit1 = er == e1
    hot = jnp.where(hit0 | hit1, 1.0, 0.0)
    upper = jnp.where(_iota((tm, tm), 0) < _iota((tm, tm), 1), 1.0, 0.0)
    before = _dot(hot, upper)
    cnt = jnp.sum(hot, axis=1, keepdims=True)
    e_col = _iota((N_EXPERTS, 1), 0)

    def slot_starts(align):
        padded = jnp.floor((cnt + (align - 1)) * (1.0 / align)) * align
        off = jnp.zeros((N_EXPERTS, 1), F32)
        for e in range(N_EXPERTS - 1):
            off = off + jnp.where(e_col > e, padded[e:e + 1, :], 0.0)
        return padded, off

    cnt_pad, off_d = slot_starts(SUBLANES)
    _, off_c = slot_starts(MOE_CHUNK)
    positions = []
    for off in (off_d, off_c):
        for hit in (hit0, hit1):
            positions.append(jnp.sum(jnp.where(hit, before + off, 0.0), axis=0, keepdims=True))

    lane = _iota((N_EXPERTS, LANES), 1)
    carry = carry_ref[...]
    meta = jnp.zeros((N_EXPERTS, LANES), F32)
    for field, val in enumerate((carry, cnt, off_d, off_c)):
        meta = jnp.where(lane == field, val, meta)
    meta_ref[...] = meta.astype(I32)
    carry_ref[...] = carry + cnt_pad
    cnt_ref[...] = carry + cnt_pad

    idx_ref[...] = jnp.zeros_like(idx_ref)
    wt_ref[...] = jnp.zeros_like(wt_ref)
    for r, val in enumerate(positions):
        idx_ref[r:r + 1, :] = val.astype(I32)
    for r, val in enumerate((w0, w1)):
        wt_ref[r:r + 1, :] = val


def _round_up(x, m):
    return -(-x // m) * m


MOE_SORT_ROWS = _round_up(2 * TOK_BLK + N_EXPERTS * (SUBLANES - 1) + MOE_PUT_CHUNK - 1, LANES)
MOE_GATHER_ROWS = _round_up(2 * TOK_BLK + N_EXPERTS * (MOE_CHUNK - 1), LANES)
META_FIELDS = 4


def _moe_blocks(n_assign, n_tiles):
    slack = N_EXPERTS * (n_tiles * (SUBLANES - 1) + MOE_PUT_CHUNK - 1)
    return -(-(n_assign + slack) // MOE_BLK) + N_EXPERTS


def _chunk_copies(meta_ref, start_ref, tile, slot_field, chunk, do, make_copy):
    def expert(e, carry):
        base = (tile * N_EXPERTS + e) * META_FIELDS
        before, rows, slot = meta_ref[base], meta_ref[base + 1], meta_ref[base + slot_field]
        for c in range(TOK_BLK // chunk):
            @pl.when(rows > c * chunk)
            def _():
                do(make_copy(pl.multiple_of(slot + c * chunk, SUBLANES),
                             pl.multiple_of(start_ref[e] + before + c * chunk, SUBLANES)))
        return carry
    lax.fori_loop(0, N_EXPERTS, expert, 0)


def _dispatch_kernel(meta_ref, cnt_ref, pos_ref, h_ref, xs_ref, blk_e_ref, blk_rows_ref,
                     start_ref, sort_ref, zero_ref, sem, *, nblk):
    t = pl.program_id(0)
    nt = pl.num_programs(0)
    tm = h_ref.shape[0]

    @pl.when(t == 0)
    def _():
        def expert(e, end):
            start_ref[e] = end
            return end + ((cnt_ref[e] + MOE_PUT_CHUNK - 1 + MOE_BLK - 1) // MOE_BLK) * MOE_BLK
        start_ref[N_EXPERTS] = lax.fori_loop(0, N_EXPERTS, expert, 0)

        def block(j, carry):
            def count(e, acc):
                return acc + jnp.where(start_ref[e + 1] <= j * MOE_BLK, 1, 0)
            e = jnp.minimum(lax.fori_loop(0, N_EXPERTS, count, 0), N_EXPERTS - 1)
            blk_e_ref[j] = e
            blk_rows_ref[j] = jnp.clip(cnt_ref[e] - (j * MOE_BLK - start_ref[e]), 0, MOE_BLK)
            return carry
        lax.fori_loop(0, nblk, block, 0)

    r = _iota((MOE_SORT_ROWS, tm), 0)
    onehot = jnp.where((r == pos_ref[0:1, :]) | (r == pos_ref[1:2, :]), 1.0, 0.0)
    buf = t % 2
    sort_ref[buf] = _dot(onehot, h_ref[...])

    def copies(which):
        def make_copy(src_row, dst_row):
            return pltpu.make_async_copy(sort_ref.at[which, pl.ds(src_row, MOE_PUT_CHUNK), :],
                                         xs_ref.at[pl.ds(dst_row, MOE_PUT_CHUNK), :], sem.at[which])
        return make_copy

    put = functools.partial(_chunk_copies, meta_ref, start_ref, slot_field=2, chunk=MOE_PUT_CHUNK)

    @pl.when(t > 0)
    def _():
        put(tile=t - 1, do=lambda cp: cp.wait(), make_copy=copies(1 - buf))

    put(tile=t, do=lambda cp: cp.start(), make_copy=copies(buf))

    @pl.when(t == nt - 1)
    def _():
        put(tile=t, do=lambda cp: cp.wait(), make_copy=copies(buf))

        zero_ref[...] = jnp.zeros_like(zero_ref)

        def zero_copy(row, size):
            return pltpu.make_async_copy(zero_ref.at[pl.ds(0, size), :],
                                         xs_ref.at[pl.ds(pl.multiple_of(row, SUBLANES), size), :],
                                         sem.at[buf])

        def fill(do):
            def expert(e, carry):
                lo = start_ref[e] + cnt_ref[e]
                gap = start_ref[e + 1] - lo
                n_big = gap // MOE_CHUNK

                def big(k, c2):
                    do(zero_copy(lo + k * MOE_CHUNK, MOE_CHUNK))
                    return c2
                lax.fori_loop(0, n_big, big, 0)

                def small(k, c2):
                    do(zero_copy(lo + n_big * MOE_CHUNK + k * SUBLANES, SUBLANES))
                    return c2
                lax.fori_loop(0, (gap - n_big * MOE_CHUNK) // SUBLANES, small, 0)
                return carry
            lax.fori_loop(0, N_EXPERTS, expert, 0)

            def unused(k, carry):
                do(zero_copy(start_ref[N_EXPERTS] + k * MOE_ZERO_ROWS, MOE_ZERO_ROWS))
                return carry
            lax.fori_loop(0, (nblk * MOE_BLK - start_ref[N_EXPERTS]) // MOE_ZERO_ROWS, unused, 0)

        fill(lambda cp: cp.start())
        fill(lambda cp: cp.wait())


def _dispatch(meta, counts, pos_rows, h_flat, nblk):
    ntok, d = h_flat.shape
    tm = TOK_BLK
    whole_smem = pl.BlockSpec(memory_space=pltpu.SMEM)
    return pl.pallas_call(
        functools.partial(_dispatch_kernel, nblk=nblk),
        grid=(ntok // tm,),
        in_specs=[whole_smem, whole_smem,
                  pl.BlockSpec((SUBLANES, tm), lambda i: (0, i)),
                  pl.BlockSpec((tm, d), lambda i: (i, 0))],
        out_specs=[pl.BlockSpec(memory_space=pl.ANY), whole_smem, whole_smem, whole_smem],
        out_shape=[jax.ShapeDtypeStruct((nblk * MOE_BLK, d), F32),
                   jax.ShapeDtypeStruct((nblk,), I32),
                   jax.ShapeDtypeStruct((nblk,), I32),
                   jax.ShapeDtypeStruct((N_EXPERTS + 1,), I32)],
        scratch_shapes=[pltpu.VMEM((2, MOE_SORT_ROWS, d), F32), pltpu.VMEM((MOE_ZERO_ROWS, d), F32),
                        pltpu.SemaphoreType.DMA((2,))],
        compiler_params=_cparams(1), name="dispatch",
    )(meta, counts, pos_rows, h_flat)


def _expert_kernel(blk_e_ref, blk_rows_ref, x_ref, wg_ref, wu_ref, wd_ref, o_ref, g_bf, u_bf, d_bf):
    i = pl.program_id(0)
    rows = blk_rows_ref[i]

    @pl.when((i == 0) | (blk_e_ref[i] != blk_e_ref[jnp.maximum(i - 1, 0)]))
    def _():
        g_bf[...] = wg_ref[0, 0].astype(BF16)
        u_bf[...] = wu_ref[0, 0].astype(BF16)
        d_bf[...] = wd_ref[0, 0].astype(BF16)

    @pl.when(rows > 0)
    def _():
        x = x_ref[...].astype(BF16)
        hid = _silu(_dot(x, g_bf[...])) * _dot(x, u_bf[...])
        o_ref[...] = _dot(hid, d_bf[...])

    @pl.when(rows == 0)
    def _():
        o_ref[...] = jnp.zeros_like(o_ref)


def _experts(blk_e, blk_rows, xs, w_gate, w_up, w_down, layer):
    d = xs.shape[1]
    ff = w_gate.shape[3]
    nblk = blk_e.shape[0]
    weight = lambda shape: pl.BlockSpec((1, 1) + shape, lambda i, be, rows: (layer, be[i], 0, 0))
    return pl.pallas_call(
        _expert_kernel,
        grid_spec=pltpu.PrefetchScalarGridSpec(
            num_scalar_prefetch=2, grid=(nblk,),
            in_specs=[pl.BlockSpec((MOE_BLK, d), lambda i, be, rows: (i, 0)),
                      weight((d, ff)), weight((d, ff)), weight((ff, d))],
            out_specs=pl.BlockSpec((MOE_BLK, d), lambda i, be, rows: (i, 0)),
            scratch_shapes=[pltpu.VMEM((d, ff), BF16), pltpu.VMEM((d, ff), BF16),
                            pltpu.VMEM((ff, d), BF16)]),
        out_shape=jax.ShapeDtypeStruct(xs.shape, F32),
        compiler_params=_cparams(1), name="experts",
    )(blk_e, blk_rows, xs, w_gate, w_up, w_down)


def _combine_kernel(meta_ref, start_ref, x_ref, pos_ref, wt_ref, mod_ref, ys_ref, o_ref,
                    gath_ref, sem, *, n_ctx, ctx_row):
    b, i = pl.program_id(0), pl.program_id(1)
    nt = pl.num_programs(1)
    t = b * nt + i
    last = pl.num_programs(0) * nt - 1
    tm, d = x_ref.shape[1:]
    depth = gath_ref.shape[0]
    buf = t % depth

    def copies(which):
        def make_copy(slot_row, ys_row):
            return pltpu.make_async_copy(ys_ref.at[pl.ds(ys_row, MOE_CHUNK), :],
                                         gath_ref.at[which, pl.ds(slot_row, MOE_CHUNK), :],
                                         sem.at[which])
        return make_copy

    get = functools.partial(_chunk_copies, meta_ref, start_ref, slot_field=3, chunk=MOE_CHUNK)

    @pl.when(t == 0)
    def _():
        gath_ref[...] = jnp.zeros_like(gath_ref)
        for ahead in range(depth - 1):
            @pl.when(ahead <= last)
            def _():
                get(tile=ahead, do=lambda cp: cp.start(), make_copy=copies(ahead))

    @pl.when(t + depth - 1 <= last)
    def _():
        get(tile=t + depth - 1, do=lambda cp: cp.start(), make_copy=copies((t + depth - 1) % depth))

    get(tile=t, do=lambda cp: cp.wait(), make_copy=copies(buf))

    col = _iota((tm, MOE_GATHER_ROWS), 1)
    pos = pos_ref[...]
    wt = wt_ref[...]
    sel = (jnp.where(col == pos[:, 0:1], wt[:, 0:1], 0.0)
           + jnp.where(col == pos[:, 1:2], wt[:, 1:2], 0.0))
    hi, lo = _hi_lo(sel)
    rows = gath_ref[buf].astype(BF16)
    y = (jnp.dot(hi, rows, preferred_element_type=F32)
         + jnp.dot(lo, rows, preferred_element_type=F32))
    gate = _mod_rows(mod_ref, 5, b, i * tm, n_ctx, ctx_row, tm, d)
    o_ref[0] = x_ref[0] + gate * y


def _combine(meta, start, x1, pos_cols, wt_cols, mod_l, ys, n_ctx, ctx_row, latent_only):
    bsz, n, d = x1.shape
    tm = TOK_BLK
    nt = n // tm
    whole_smem = pl.BlockSpec(memory_space=pltpu.SMEM)
    cols = pl.BlockSpec((tm, 2), lambda b, i: (b * nt + i, 0))
    if latent_only:
        assert n_ctx == tm
        out_spec = pl.BlockSpec((1, tm, d), lambda b, i: (b, jnp.maximum(i - 1, 0), 0))
        out_rows = n - n_ctx
    else:
        out_spec = pl.BlockSpec((1, tm, d), lambda b, i: (b, i, 0))
        out_rows = n
    return pl.pallas_call(
        functools.partial(_combine_kernel, n_ctx=n_ctx, ctx_row=ctx_row),
        grid=(bsz, nt),
        in_specs=[whole_smem, whole_smem,
                  pl.BlockSpec((1, tm, d), lambda b, i: (b, i, 0)), cols, cols,
                  pl.BlockSpec(mod_l.shape, lambda b, i: (0, 0)),
                  pl.BlockSpec(memory_space=pl.ANY)],
        out_specs=out_spec,
        out_shape=jax.ShapeDtypeStruct((bsz, out_rows, d), F32),
        scratch_shapes=[pltpu.VMEM((MOE_GATHER_DEPTH, MOE_GATHER_ROWS, d), F32),
                        pltpu.SemaphoreType.DMA((MOE_GATHER_DEPTH,))],
        compiler_params=_cparams(2), name="combine",
    )(meta, start, x1, pos_cols, wt_cols, mod_l, ys)


def _layer(xa, mod_l, layer, n_ctx, ctx_row, p, rope):
    bsz, n, d = xa.shape
    na_w = NA_HEADS * HEAD_DIM
    hg_w = HG_HEADS * HG_DK
    wa_qw = WA_HEADS * HEAD_DIM
    wa_kvw = WA_KV_HEADS * HEAD_DIM
    hg0 = 3 * na_w
    wa0 = hg0 + 5 * hg_w
    gate0 = wa0 + wa_qw + 2 * wa_kvw
    n_cols = gate0 + N_BRANCHES * d
    assert p["w_in"].shape[2] == n_cols
    w_att = _take_cols(p["w_in"], layer, [(gate0, n_cols), (0, hg0), (wa0, gate0)])
    w_hg_qig = _take_cols(p["w_in"], layer, [(hg0, hg0 + hg_w), (hg0 + 3 * hg_w, wa0)])
    w_hg_f = _take_cols(p["w_in"], layer, [(hg0 + hg_w, hg0 + 3 * hg_w)])
    col_gate = 0
    col_na = [N_BRANCHES * d + k * na_w for k in range(3)]
    col_wq = N_BRANCHES * d + 3 * na_w
    col_wk = col_wq + wa_qw
    col_wv = col_wk + wa_kvw
    proj, h = _inproj(xa, mod_l, p["norm1"], w_att, n_ctx, ctx_row, 1792, BF16)
    hg_qig = _project(h, w_hg_qig, 768, BF16)
    hg_f = _project(h, w_hg_f, 1024, F32)

    ya = _na_attention(proj, _na_bias(p["na_rpb"]), p["na_q_norm"], p["na_k_norm"], *col_na)
    yc = _wa_attention(proj, p["wa_sink"], p["wa_q_norm"], p["wa_k_norm"], rope[0], rope[1],
                       col_wq, col_wk, col_wv)
    hg_q, hg_i, hg_g = ((hg_qig, k * hg_w) for k in range(3))
    o_f = _hgrn_pass(p["hg_lower"], layer, False, hg_q, (hg_f, 0), hg_i)
    yb = _hgrn_pass(p["hg_lower"], layer, True, hg_q, (hg_f, hg_w), hg_i,
                    final_args=(hg_g, o_f, p["hg_norm"]))

    bf = lambda w: w.astype(BF16)
    x1, h2, idx, wts, counts, meta = _merge(
        xa, ya, yb, yc, proj, mod_l, bf(p["w_pa"]), bf(p["w_pb"]), bf(p["w_pc"]), bf(p["w_out"]),
        col_gate, p["norm2"], p["w_router"].T, p["b_router"], n_ctx, ctx_row)
    ntok = bsz * n
    meta_flat = meta[:, :, :META_FIELDS].reshape(-1)
    xs, blk_e, blk_rows, start = _dispatch(meta_flat, counts[:, 0].astype(I32), idx,
                                           h2.reshape(ntok, d),
                                           _moe_blocks(2 * ntok, ntok // TOK_BLK))
    ys = _experts(blk_e, blk_rows, xs, p["w_gate"], p["w_up"], p["w_down"], layer)
    return _combine(meta_flat, start, x1, idx[2:4].T, wts[:2].T, mod_l, ys, n_ctx, ctx_row,
                    latent_only=p["last"])


def kernel(x, c, ctx, c_ctx, w_ada, b_ada, norm1, norm2, w_in, na_q_norm, na_k_norm, na_rpb, hg_lower,
           hg_norm, wa_q_norm, wa_k_norm, wa_sink, w_pa, w_pb, w_pc, w_out, w_router, b_router,
           w_gate, w_up, w_down):
    bsz, t, d = x.shape
    n_ctx = ctx.shape[1]
    depth = w_ada.shape[0]
    assert n_ctx == TOK_BLK and t % TOK_BLK == 0 and t // TOK_BLK >= 3
    assert t % GRID_W == 0 and bsz + 1 <= SUBLANES

    xa = jnp.concatenate([ctx, x], axis=1)
    cond = jnp.concatenate([c, c_ctx[None], jnp.zeros((SUBLANES - bsz - 1, d), F32)], axis=0)
    mod = _ada(cond, w_ada, b_ada)
    rope = _rope_tables(n_ctx, t)
    for l in range(depth):
        p = dict(norm1=norm1[l], norm2=norm2[l], w_in=w_in, na_q_norm=na_q_norm[l],
                 na_k_norm=na_k_norm[l], na_rpb=na_rpb[l], hg_lower=hg_lower, hg_norm=hg_norm[l],
                 wa_q_norm=wa_q_norm[l], wa_k_norm=wa_k_norm[l], wa_sink=wa_sink[l],
                 w_pa=w_pa[l], w_pb=w_pb[l], w_pc=w_pc[l], w_out=w_out[l],
                 w_router=w_router, b_router=b_router, w_gate=w_gate, w_up=w_up,
                 w_down=w_down, last=l == depth - 1)
        xa = _layer(xa, mod[l], l, n_ctx, bsz, p, rope)
    return xa
```
